```python
import jax, jax.numpy as jnp
from jax import lax
import numpy as np

D_MODEL = 2048
BATCH = 8
SEQ = 4096
DEPTH = 1

CHUNK = 64
Q_BLOCK = 2 * CHUNK
POOL_WIDTH = D_MODEL // 2
POOL_WINDOWS = (2, 4, 8, 16)
POOL_GROUPS = len(POOL_WINDOWS)
POOL_GROUP_WIDTH = POOL_WIDTH // POOL_GROUPS
SB_WIDTH = D_MODEL - POOL_WIDTH
SB_HEAD_DIM = 128
SB_HEADS = SB_WIDTH // SB_HEAD_DIM
MIX_WIDTH = POOL_WIDTH + SB_WIDTH
IN_PROJ_WIDTH = POOL_WIDTH + 3 * SB_WIDTH
D_FF = 4 * D_MODEL
DEEPNORM_ALPHA = (2.0 * DEPTH) ** 0.25
DEEPNORM_BETA = (8.0 * DEPTH) ** -0.25
LN_EPS = 1e-5

kernel_name = "hybrid_pool_stickbreaking_deepnorm_block"


def layer_norm(x, g, b):
    xf = x.astype(jnp.float32)
    mu = jnp.mean(xf, axis=-1, keepdims=True)
    var = jnp.mean(jnp.square(xf - mu), axis=-1, keepdims=True)
    y = (xf - mu) * lax.rsqrt(var + LN_EPS)
    return (y * g.astype(jnp.float32) + b.astype(jnp.float32)).astype(x.dtype)


def multi_scale_pool(u, w_pool, pool_scale):
    b, s, _ = u.shape
    ug = u.reshape(b, s, POOL_GROUPS, POOL_GROUP_WIDTH).astype(jnp.float32)
    csum = jnp.concatenate(
        [jnp.zeros((b, 1, POOL_GROUPS, POOL_GROUP_WIDTH), jnp.float32),
         jnp.cumsum(ug, axis=1)], axis=1)
    t = jnp.arange(s, dtype=jnp.int32)
    windows = jnp.asarray(POOL_WINDOWS, dtype=jnp.int32)
    start = jnp.maximum(t[:, None] + 1 - windows[None, :], 0)
    count = (t[:, None] + 1 - start).astype(jnp.float32)
    group_idx = jnp.arange(POOL_GROUPS, dtype=jnp.int32)[None, :]
    c_start = csum[:, start, group_idx]
    mean = (csum[:, 1:] - c_start) / count[None, :, :, None]
    y = mean - ug
    y = jnp.einsum('bsgc,gcd->bsgd', y, w_pool.astype(jnp.float32))
    y = y * pool_scale.astype(jnp.float32)[None, None]
    return y.reshape(b, s, POOL_WIDTH).astype(u.dtype)


def stick_breaking_attention(q, k, v):
    b, s, h, dh = q.shape
    n_blocks = s // Q_BLOCK
    scale = 1.0 / np.sqrt(dh).astype(np.float32)
    qb = q.reshape(b, n_blocks, Q_BLOCK, h, dh).transpose(1, 0, 2, 3, 4)
    key_pos = jnp.arange(s, dtype=jnp.int32)

    def one_block(args):
        qi, i = args
        z = jnp.einsum('bqhd,bkhd->bhqk', qi, k).astype(jnp.float32) * scale
        q_pos = i * Q_BLOCK + jnp.arange(Q_BLOCK, dtype=jnp.int32)
        mask = (key_pos[None, :] < q_pos[:, None])[None, None]
        log_not = jnp.where(mask, jax.nn.log_sigmoid(-z), 0.0)
        after = lax.cumsum(log_not, axis=3, reverse=True) - log_not
        a = jnp.where(mask, jnp.exp(jax.nn.log_sigmoid(z) + after), 0.0)
        return jnp.einsum('bhqk,bkhd->bqhd', a.astype(v.dtype), v)

    out = lax.map(one_block, (qb, jnp.arange(n_blocks, dtype=jnp.int32)))
    return out.transpose(1, 0, 2, 3, 4).reshape(b, s, h * dh)


def _fwd_setup_inputs(seed: int = 0) -> dict:
    key = jax.random.key(seed)
    ks = jax.random.split(key, 16)
    f32 = jnp.float32
    x = jax.random.normal(ks[0], (BATCH, SEQ, D_MODEL), f32)
    ln_in_g = 1.0 + 0.02 * jax.random.normal(ks[1], (D_MODEL,), f32)
    ln_in_b = 0.02 * jax.random.normal(ks[2], (D_MODEL,), f32)
    w_in = jax.random.normal(ks[3], (DEPTH, D_MODEL, IN_PROJ_WIDTH), f32) * D_MODEL ** -0.5
    w_pool = jax.random.normal(ks[4], (DEPTH, POOL_GROUPS, POOL_GROUP_WIDTH, POOL_GROUP_WIDTH), f32) * POOL_GROUP_WIDTH ** -0.5
    pool_scale = 1.0 + 0.02 * jax.random.normal(ks[5], (DEPTH, POOL_GROUPS, POOL_GROUP_WIDTH), f32)
    w_out = jax.random.normal(ks[6], (DEPTH, MIX_WIDTH, D_MODEL), f32) * (MIX_WIDTH ** -0.5 * DEEPNORM_BETA)
    ln1_g = 1.0 + 0.02 * jax.random.normal(ks[7], (DEPTH, D_MODEL), f32)
    ln1_b = 0.02 * jax.random.normal(ks[8], (DEPTH, D_MODEL), f32)
    w_ff1 = jax.random.normal(ks[9], (DEPTH, D_MODEL, D_FF), f32) * D_MODEL ** -0.5
    b_ff1 = 0.02 * jax.random.normal(ks[10], (DEPTH, D_FF), f32)
    w_ff2 = jax.random.normal(ks[11], (DEPTH, D_FF, D_MODEL), f32) * (D_FF ** -0.5 * DEEPNORM_BETA)
    b_ff2 = 0.02 * jax.random.normal(ks[12], (DEPTH, D_MODEL), f32)
    ln2_g = 1.0 + 0.02 * jax.random.normal(ks[13], (DEPTH, D_MODEL), f32)
    ln2_b = 0.02 * jax.random.normal(ks[14], (DEPTH, D_MODEL), f32)
    return {"x": x, "ln_in_g": ln_in_g, "ln_in_b": ln_in_b, "w_in": w_in,
            "w_pool": w_pool, "pool_scale": pool_scale, "w_out": w_out,
            "ln1_g": ln1_g, "ln1_b": ln1_b, "w_ff1": w_ff1, "b_ff1": b_ff1,
            "w_ff2": w_ff2, "b_ff2": b_ff2, "ln2_g": ln2_g, "ln2_b": ln2_b}


def _fwd_reference(x, ln_in_g, ln_in_b, w_in, w_pool, pool_scale, w_out,
              ln1_g, ln1_b, w_ff1, b_ff1, w_ff2, b_ff2, ln2_g, ln2_b):
    b, s, _ = x.shape
    h = layer_norm(x, ln_in_g, ln_in_b)
    for layer in range(DEPTH):
        u = jnp.einsum('bsd,de->bse', h, w_in[layer])
        u_pool = u[..., :POOL_WIDTH]
        q, k, v = jnp.split(u[..., POOL_WIDTH:], 3, axis=-1)
        q = q.reshape(b, s, SB_HEADS, SB_HEAD_DIM)
        k = k.reshape(b, s, SB_HEADS, SB_HEAD_DIM)
        v = v.reshape(b, s, SB_HEADS, SB_HEAD_DIM)
        y_pool = multi_scale_pool(u_pool, w_pool[layer], pool_scale[layer])
        y_sb = stick_breaking_attention(q, k, v)
        mix = jnp.concatenate([y_pool, y_sb], axis=-1)
        mix = jnp.einsum('bse,ed->bsd', mix, w_out[layer])
        h = layer_norm(DEEPNORM_ALPHA * h + mix, ln1_g[layer], ln1_b[layer])
        f = jnp.einsum('bsd,df->bsf', h, w_ff1[layer]) + b_ff1[layer]
        f = jnp.square(jax.nn.relu(f))
        f = jnp.einsum('bsf,fd->bsd', f, w_ff2[layer]) + b_ff2[layer]
        h = layer_norm(DEEPNORM_ALPHA * h + f, ln2_g[layer], ln2_b[layer])
    return h


import jax as _jax
import jax.numpy as _jnp

TWIN_FORMAT = 'train_step'
FWD_PARAMS = ['x', 'ln_in_g', 'ln_in_b', 'w_in', 'w_pool', 'pool_scale', 'w_out', 'ln1_g', 'ln1_b', 'w_ff1', 'b_ff1', 'w_ff2', 'b_ff2', 'ln2_g', 'ln2_b']
TWIN_WEIGHTS = ['ln_in_g', 'ln_in_b', 'w_in', 'w_pool', 'pool_scale', 'w_out', 'ln1_g', 'ln1_b', 'w_ff1', 'b_ff1', 'w_ff2', 'b_ff2', 'ln2_g', 'ln2_b']
TWIN_DIFF_INPUT = 'x'
TWIN_INPUTS = ['x', 'ln_in_g', 'ln_in_b', 'w_in', 'w_pool', 'pool_scale', 'w_out', 'ln1_g', 'ln1_b', 'w_ff1', 'b_ff1', 'w_ff2', 'b_ff2', 'ln2_g', 'ln2_b', 'loss_target', 'm_ln_in_g', 'm_ln_in_b', 'm_w_in', 'm_w_pool', 'm_pool_scale', 'm_w_out', 'm_ln1_g', 'm_ln1_b', 'm_w_ff1', 'm_b_ff1', 'm_w_ff2', 'm_b_ff2', 'm_ln2_g', 'm_ln2_b', 'v_ln_in_g', 'v_ln_in_b', 'v_w_in', 'v_w_pool', 'v_pool_scale', 'v_w_out', 'v_ln1_g', 'v_ln1_b', 'v_w_ff1', 'v_b_ff1', 'v_w_ff2', 'v_b_ff2', 'v_ln2_g', 'v_ln2_b']
TWIN_OUTPUTS = ['loss', 'grad_x', 'grad_ln_in_g', 'grad_ln_in_b', 'grad_w_in', 'grad_w_pool', 'grad_pool_scale', 'grad_w_out', 'grad_ln1_g', 'grad_ln1_b', 'grad_w_ff1', 'grad_b_ff1', 'grad_w_ff2', 'grad_b_ff2', 'grad_ln2_g', 'grad_ln2_b', 'delta_ln_in_g', 'delta_ln_in_b', 'delta_w_in', 'delta_w_pool', 'delta_pool_scale', 'delta_w_out', 'delta_ln1_g', 'delta_ln1_b', 'delta_w_ff1', 'delta_b_ff1', 'delta_w_ff2', 'delta_b_ff2', 'delta_ln2_g', 'delta_ln2_b', 'new_m_ln_in_g', 'new_m_ln_in_b', 'new_m_w_in', 'new_m_w_pool', 'new_m_pool_scale', 'new_m_w_out', 'new_m_ln1_g', 'new_m_ln1_b', 'new_m_w_ff1', 'new_m_b_ff1', 'new_m_w_ff2', 'new_m_b_ff2', 'new_m_ln2_g', 'new_m_ln2_b', 'new_v_ln_in_g', 'new_v_ln_in_b', 'new_v_w_in', 'new_v_w_pool', 'new_v_pool_scale', 'new_v_w_out', 'new_v_ln1_g', 'new_v_ln1_b', 'new_v_w_ff1', 'new_v_b_ff1', 'new_v_w_ff2', 'new_v_b_ff2', 'new_v_ln2_g', 'new_v_ln2_b']
TWIN_LEAF_KINDS = {'loss': 'loss', 'grad_x': 'grad_x', 'grad_ln_in_g': 'grad_w', 'grad_ln_in_b': 'grad_w', 'grad_w_in': 'grad_w', 'grad_w_pool': 'grad_w', 'grad_pool_scale': 'grad_w', 'grad_w_out': 'grad_w', 'grad_ln1_g': 'grad_w', 'grad_ln1_b': 'grad_w', 'grad_w_ff1': 'grad_w', 'grad_b_ff1': 'grad_w', 'grad_w_ff2': 'grad_w', 'grad_b_ff2': 'grad_w', 'grad_ln2_g': 'grad_w', 'grad_ln2_b': 'grad_w', 'delta_ln_in_g': 'delta_w', 'delta_ln_in_b': 'delta_w', 'delta_w_in': 'delta_w', 'delta_w_pool': 'delta_w', 'delta_pool_scale': 'delta_w', 'delta_w_out': 'delta_w', 'delta_ln1_g': 'delta_w', 'delta_ln1_b': 'delta_w', 'delta_w_ff1': 'delta_w', 'delta_b_ff1': 'delta_w', 'delta_w_ff2': 'delta_w', 'delta_b_ff2': 'delta_w', 'delta_ln2_g': 'delta_w', 'delta_ln2_b': 'delta_w', 'new_m_ln_in_g': 'new_m', 'new_m_ln_in_b': 'new_m', 'new_m_w_in': 'new_m', 'new_m_w_pool': 'new_m', 'new_m_pool_scale': 'new_m', 'new_m_w_out': 'new_m', 'new_m_ln1_g': 'new_m', 'new_m_ln1_b': 'new_m', 'new_m_w_ff1': 'new_m', 'new_m_b_ff1': 'new_m', 'new_m_w_ff2': 'new_m', 'new_m_b_ff2': 'new_m', 'new_m_ln2_g': 'new_m', 'new_m_ln2_b': 'new_m', 'new_v_ln_in_g': 'new_v', 'new_v_ln_in_b': 'new_v', 'new_v_w_in': 'new_v', 'new_v_w_pool': 'new_v', 'new_v_pool_scale': 'new_v', 'new_v_w_out': 'new_v', 'new_v_ln1_g': 'new_v', 'new_v_ln1_b': 'new_v', 'new_v_w_ff1': 'new_v', 'new_v_b_ff1': 'new_v', 'new_v_w_ff2': 'new_v', 'new_v_b_ff2': 'new_v', 'new_v_ln2_g': 'new_v', 'new_v_ln2_b': 'new_v'}


def _forward(args):
    return _fwd_reference(*[args[k] for k in FWD_PARAMS])


def _output_shape():
    def fwd():
        inp = _fwd_setup_inputs(0)
        return _fwd_reference(*[inp[k] for k in FWD_PARAMS])
    out = _jax.eval_shape(fwd)
    return out.shape, out.dtype

N_MICROBATCH = 1
ADAM_LR = 0.001
ADAM_B1 = 0.9
ADAM_B2 = 0.999
ADAM_EPS = 1e-08
ADAM_WD = 0.01
ADAM_STEP = 10
PER_EXAMPLE_BATCH_AXIS = {'x': 0, 'loss_target': 0}
SHARED_INPUTS = []
_WEIGHT_DTYPES = {'ln_in_g': _jnp.float32, 'ln_in_b': _jnp.float32, 'w_in': _jnp.float32, 'w_pool': _jnp.float32, 'pool_scale': _jnp.float32, 'w_out': _jnp.float32, 'ln1_g': _jnp.float32, 'ln1_b': _jnp.float32, 'w_ff1': _jnp.float32, 'b_ff1': _jnp.float32, 'w_ff2': _jnp.float32, 'b_ff2': _jnp.float32, 'ln2_g': _jnp.float32, 'ln2_b': _jnp.float32}
MOMENT_SCALE = {'ln_in_g': 3.584429e-01, 'ln_in_b': 2.908829e-01, 'w_in': 2.525642e-02, 'w_pool': 3.790664e-02, 'pool_scale': 4.192974e-02, 'w_out': 5.610587e-02, 'ln1_g': 4.257008e-01, 'ln1_b': 2.971036e-01, 'w_ff1': 2.678011e-02, 'b_ff1': 5.350909e-02, 'w_ff2': 1.217477e-01, 'b_ff2': 2.305564e-01, 'ln2_g': 1.604485e+01, 'ln2_b': 3.467140e+00}


def _to_microbatches(a, axis):
    t = _jnp.moveaxis(a, axis, 0)
    t = t.reshape((N_MICROBATCH, t.shape[0] // N_MICROBATCH) + t.shape[1:])
    return _jnp.moveaxis(t, 1, axis + 1)


def setup_inputs(seed: int = 0) -> dict:
    inp = _fwd_setup_inputs(seed)
    key = _jax.random.fold_in(_jax.random.key(seed), 7919)
    shape, _ = _output_shape()
    out = dict(inp)
    out["loss_target"] = _jax.random.normal(_jax.random.fold_in(key, 0), shape, _jnp.float32)
    for i, name in enumerate(TWIN_WEIGHTS):
        w = inp[name].astype(_jnp.float32)
        if MOMENT_SCALE is None:
            s = _jnp.sqrt(_jnp.mean(_jnp.square(w)) + 1e-30)
        else:
            s = MOMENT_SCALE[name]
        km, kv = _jax.random.split(_jax.random.fold_in(key, i + 1))
        out[name] = w
        out["m_" + name] = s * _jax.random.normal(km, w.shape, _jnp.float32)
        out["v_" + name] = (s * s) * _jax.random.uniform(kv, w.shape, _jnp.float32, 0.5, 1.5)
    if N_MICROBATCH > 1:
        for name, axis in PER_EXAMPLE_BATCH_AXIS.items():
            out[name] = _to_microbatches(out[name], axis)
    return {'x': out['x'], 'ln_in_g': out['ln_in_g'], 'ln_in_b': out['ln_in_b'], 'w_in': out['w_in'], 'w_pool': out['w_pool'], 'pool_scale': out['pool_scale'], 'w_out': out['w_out'], 'ln1_g': out['ln1_g'], 'ln1_b': out['ln1_b'], 'w_ff1': out['w_ff1'], 'b_ff1': out['b_ff1'], 'w_ff2': out['w_ff2'], 'b_ff2': out['b_ff2'], 'ln2_g': out['ln2_g'], 'ln2_b': out['ln2_b'], 'loss_target': out['loss_target'], 'm_ln_in_g': out['m_ln_in_g'], 'm_ln_in_b': out['m_ln_in_b'], 'm_w_in': out['m_w_in'], 'm_w_pool': out['m_w_pool'], 'm_pool_scale': out['m_pool_scale'], 'm_w_out': out['m_w_out'], 'm_ln1_g': out['m_ln1_g'], 'm_ln1_b': out['m_ln1_b'], 'm_w_ff1': out['m_w_ff1'], 'm_b_ff1': out['m_b_ff1'], 'm_w_ff2': out['m_w_ff2'], 'm_b_ff2': out['m_b_ff2'], 'm_ln2_g': out['m_ln2_g'], 'm_ln2_b': out['m_ln2_b'], 'v_ln_in_g': out['v_ln_in_g'], 'v_ln_in_b': out['v_ln_in_b'], 'v_w_in': out['v_w_in'], 'v_w_pool': out['v_w_pool'], 'v_pool_scale': out['v_pool_scale'], 'v_w_out': out['v_w_out'], 'v_ln1_g': out['v_ln1_g'], 'v_ln1_b': out['v_ln1_b'], 'v_w_ff1': out['v_w_ff1'], 'v_b_ff1': out['v_b_ff1'], 'v_w_ff2': out['v_w_ff2'], 'v_b_ff2': out['v_b_ff2'], 'v_ln2_g': out['v_ln2_g'], 'v_ln2_b': out['v_ln2_b']}


def _loss(weights, diff, rest, loss_target):
    with _jax.named_scope("forward"):
        args = {**rest, TWIN_DIFF_INPUT: diff, **{k: w.astype(_WEIGHT_DTYPES[k]) for k, w in weights.items()}}
        y = _forward(args)
    with _jax.named_scope("loss_head"):
        err = _jnp.square(y.astype(_jnp.float32) - loss_target)
        return 0.5 * _jnp.sum(_jnp.mean(err, axis=-1)) if err.ndim else 0.5 * err


def _adamw(w, g, m, v):
    m = ADAM_B1 * m + (1.0 - ADAM_B1) * g
    v = ADAM_B2 * v + (1.0 - ADAM_B2) * _jnp.square(g)
    m_hat = m / (1.0 - ADAM_B1 ** ADAM_STEP)
    v_hat = v / (1.0 - ADAM_B2 ** ADAM_STEP)
    delta = -ADAM_LR * (m_hat / (_jnp.sqrt(v_hat) + ADAM_EPS) + ADAM_WD * w)
    return delta, m, v


def reference(x, ln_in_g, ln_in_b, w_in, w_pool, pool_scale, w_out, ln1_g, ln1_b, w_ff1, b_ff1, w_ff2, b_ff2, ln2_g, ln2_b, loss_target, m_ln_in_g, m_ln_in_b, m_w_in, m_w_pool, m_pool_scale, m_w_out, m_ln1_g, m_ln1_b, m_w_ff1, m_b_ff1, m_w_ff2, m_b_ff2, m_ln2_g, m_ln2_b, v_ln_in_g, v_ln_in_b, v_w_in, v_w_pool, v_pool_scale, v_w_out, v_ln1_g, v_ln1_b, v_w_ff1, v_b_ff1, v_w_ff2, v_b_ff2, v_ln2_g, v_ln2_b):
    given = dict(x=x, ln_in_g=ln_in_g, ln_in_b=ln_in_b, w_in=w_in, w_pool=w_pool, pool_scale=pool_scale, w_out=w_out, ln1_g=ln1_g, ln1_b=ln1_b, w_ff1=w_ff1, b_ff1=b_ff1, w_ff2=w_ff2, b_ff2=b_ff2, ln2_g=ln2_g, ln2_b=ln2_b, loss_target=loss_target, m_ln_in_g=m_ln_in_g, m_ln_in_b=m_ln_in_b, m_w_in=m_w_in, m_w_pool=m_w_pool, m_pool_scale=m_pool_scale, m_w_out=m_w_out, m_ln1_g=m_ln1_g, m_ln1_b=m_ln1_b, m_w_ff1=m_w_ff1, m_b_ff1=m_b_ff1, m_w_ff2=m_w_ff2, m_b_ff2=m_b_ff2, m_ln2_g=m_ln2_g, m_ln2_b=m_ln2_b, v_ln_in_g=v_ln_in_g, v_ln_in_b=v_ln_in_b, v_w_in=v_w_in, v_w_pool=v_w_pool, v_pool_scale=v_pool_scale, v_w_out=v_w_out, v_ln1_g=v_ln1_g, v_ln1_b=v_ln1_b, v_w_ff1=v_w_ff1, v_b_ff1=v_b_ff1, v_w_ff2=v_w_ff2, v_b_ff2=v_b_ff2, v_ln2_g=v_ln2_g, v_ln2_b=v_ln2_b)
    weights = {n: given[n] for n in TWIN_WEIGHTS}
    shared = {n: given[n] for n in SHARED_INPUTS}
    per_example = {n: given[n] for n in ['x']}
    grad_fn = _jax.value_and_grad(_loss, argnums=(0, 1))

    def one_microbatch(ex, loss_target):
        ex = dict(ex)
        diff = ex.pop(TWIN_DIFF_INPUT)
        return grad_fn(weights, diff, {**shared, **ex}, loss_target)

    if N_MICROBATCH == 1:
        loss, (grad_w, grad_x) = one_microbatch(per_example, given["loss_target"])
    else:
        def body(carry, xs):
            loss_sum, grad_sum = carry
            l_k, (gw_k, gx_k) = one_microbatch(xs[0], xs[1])
            with _jax.named_scope("update"):
                return (loss_sum + l_k, _jax.tree.map(_jnp.add, grad_sum, gw_k)), gx_k

        init = (_jnp.zeros((), _jnp.float32), _jax.tree.map(_jnp.zeros_like, weights))
        (loss, grad_w), grad_x = _jax.lax.scan(body, init, (per_example, given["loss_target"]))
    with _jax.named_scope("update"):
        delta_w, new_m, new_v = {}, {}, {}
        for n in TWIN_WEIGHTS:
            delta_w[n], new_m[n], new_v[n] = _adamw(weights[n], grad_w[n], given["m_" + n], given["v_" + n])
    return (loss, grad_x, *[grad_w[n] for n in TWIN_WEIGHTS], *[delta_w[n] for n in TWIN_WEIGHTS],
            *[new_m[n] for n in TWIN_WEIGHTS], *[new_v[n] for n in TWIN_WEIGHTS])
```

```python
import functools

import jax
import jax.numpy as jnp
from jax import lax
from jax.experimental import pallas as pl
from jax.experimental.pallas import tpu as pltpu

F32 = jnp.float32
BF16 = jnp.bfloat16
MESH = pl.DeviceIdType.MESH

HEAD_DIM = 128
POOL_WINDOWS = (2, 4, 8, 16)
POOL_HALO = 16
LN_EPS = 1e-5
ALPHA = 2.0 ** 0.25
ADAM_LR, ADAM_B1, ADAM_B2, ADAM_EPS, ADAM_WD, ADAM_STEP = 0.001, 0.9, 0.999, 1e-08, 0.01, 10

QB = 128
KB = 128
VMEM_LIMIT = 56 * 1024 * 1024
N_CHIPS = 4
N_DEV = 8


def _params(sem=None):
    return pltpu.CompilerParams(dimension_semantics=sem, vmem_limit_bytes=VMEM_LIMIT)


def _tile(dim, pref):
    return pref if dim % pref == 0 else dim


def _pos():
    return lax.axis_index("x"), lax.axis_index("y"), lax.axis_index("c")


def _other_chips(x, y):
    return [(1 - x, y), (x, 1 - y), (1 - x, 1 - y)]


def _matmul(name, a, b, grid, a_spec, b_spec, contract, acc_shape, extras, extra_specs, out_shape, out_specs,
            epilogue):
    n_extra, n_out, gk = len(extras), len(out_shape), grid[2]

    def body(*refs):
        a_ref, b_ref = refs[0], refs[1]
        extra_refs = refs[2:2 + n_extra]
        out_refs = refs[2 + n_extra:2 + n_extra + n_out]
        acc_ref = refs[-1]
        kk = pl.program_id(2)

        @pl.when(kk == 0)
        def _():
            acc_ref[...] = jnp.zeros_like(acc_ref)

        acc_ref[...] += lax.dot_general(a_ref[...], b_ref[...], (contract, ((), ())),
                                        preferred_element_type=F32)

        @pl.when(kk == gk - 1)
        def _():
            epilogue(acc_ref[...], extra_refs, out_refs)

    return pl.pallas_call(
        body, name=name, grid=grid, in_specs=[a_spec, b_spec, *extra_specs], out_specs=out_specs,
        out_shape=out_shape, scratch_shapes=[pltpu.VMEM(acc_shape, F32)],
        compiler_params=_params(("parallel", "parallel", "arbitrary")),
    )(a, b, *extras)


def _mm_nn(name, a, b, extras, extra_specs, out_shape, out_specs, epilogue, b_chips=False, tm=1024, tn=1024,
           tk=1024):
    m, k = a.shape
    n = b.shape[1] if not b_chips else b.shape[2] * N_CHIPS
    tm, tk = _tile(m, tm), _tile(k, tk)
    if b_chips:
        tn = _tile(b.shape[2], tn)
        nb = b.shape[2] // tn
        b_spec = pl.BlockSpec((None, tk, tn), lambda i, j, kk: (j // nb, kk, j % nb))
    else:
        tn = _tile(n, tn)
        b_spec = pl.BlockSpec((tk, tn), lambda i, j, kk: (kk, j))
    a_spec = pl.BlockSpec((tm, tk), lambda i, j, kk: (i, kk))
    return _matmul(name, a, b, (m // tm, n // tn, k // tk), a_spec, b_spec, ((1,), (0,)), (tm, tn), extras,
                   extra_specs(tm, tn), out_shape, out_specs(tm, tn), epilogue)


def _mm_nt(name, a, b, extras, extra_specs, out_shape, out_specs, epilogue, b_chips=False, tm=1024, tn=1024,
           tk=1024):
    m, k = a.shape
    n = b.shape[0] if not b_chips else b.shape[1]
    tm, tn = _tile(m, tm), _tile(n, tn)
    if b_chips:
        tk = _tile(b.shape[2], tk)
        nb = b.shape[2] // tk
        b_spec = pl.BlockSpec((None, tn, tk), lambda i, j, kk: (kk // nb, j, kk % nb))
    else:
        tk = _tile(k, tk)
        b_spec = pl.BlockSpec((tn, tk), lambda i, j, kk: (j, kk))
    a_spec = pl.BlockSpec((tm, tk), lambda i, j, kk: (i, kk))
    return _matmul(name, a, b, (m // tm, n // tn, k // tk), a_spec, b_spec, ((1,), (1,)), (tm, tn), extras,
                   extra_specs(tm, tn), out_shape, out_specs(tm, tn), epilogue)


def _mm_tn(name, a, b, out_chips=False, tm=1024, tn=1024, tk=1024):
    k, m = a.shape
    n = b.shape[1]
    tm, tk = _tile(m, tm), _tile(k, tk)
    a_spec = pl.BlockSpec((tk, tm), lambda i, j, kk: (kk, i))
    if out_chips:
        nc = n // N_CHIPS
        tn = _tile(nc, tn)
        nb = nc // tn
        out_shape = [jax.ShapeDtypeStruct((N_CHIPS, m, nc), F32)]
        out_specs = [pl.BlockSpec((None, tm, tn), lambda i, j, kk: (j // nb, i, j % nb))]
    else:
        tn = _tile(n, tn)
        out_shape = [jax.ShapeDtypeStruct((m, n), F32)]
        out_specs = [pl.BlockSpec((tm, tn), lambda i, j, kk: (i, j))]
    b_spec = pl.BlockSpec((tk, tn), lambda i, j, kk: (kk, j))

    def epilogue(acc, extra_refs, out_refs):
        out_refs[0][...] = acc

    return _matmul(name, a, b, (m // tm, n // tn, k // tk), a_spec, b_spec, ((0,), (0,)), (tm, tn), (), [],
                   out_shape, out_specs, epilogue)[0]


def _tile_spec(tm, tn):
    return pl.BlockSpec((tm, tn), lambda i, j, kk: (i, j))


def _row_spec(tn):
    return pl.BlockSpec((1, tn), lambda i, j, kk: (0, j))


def _ln_stats(r):
    mu = jnp.mean(r, axis=-1, keepdims=True)
    var = jnp.mean(jnp.square(r - mu), axis=-1, keepdims=True)
    rstd = lax.rsqrt(var + LN_EPS)
    return (r - mu) * rstd, rstd


def _ln_fwd(name, r, g, b, tr=256):
    t, d = r.shape
    tr = _tile(t, tr)

    def body(r_ref, g_ref, b_ref, y_ref, yb_ref, xhat_ref, rstd_ref):
        xhat, rstd = _ln_stats(r_ref[...])
        y = xhat * g_ref[...] + b_ref[...]
        y_ref[...] = y
        yb_ref[...] = y.astype(BF16)
        xhat_ref[...] = xhat
        rstd_ref[...] = rstd

    row = pl.BlockSpec((tr, d), lambda i: (i, 0))
    vec = pl.BlockSpec((1, d), lambda i: (0, 0))
    return pl.pallas_call(
        body, name=name, grid=(t // tr,), in_specs=[row, vec, vec],
        out_specs=[row, row, row, pl.BlockSpec((tr, 1), lambda i: (i, 0))],
        out_shape=[jax.ShapeDtypeStruct((t, d), F32), jax.ShapeDtypeStruct((t, d), BF16),
                   jax.ShapeDtypeStruct((t, d), F32), jax.ShapeDtypeStruct((t, 1), F32)],
        compiler_params=_params(("parallel",)),
    )(r, g, b)


def _ln_bwd_rows(dy, xhat, rstd, g):
    dxhat = dy * g
    m1 = jnp.mean(dxhat, axis=-1, keepdims=True)
    m2 = jnp.mean(dxhat * xhat, axis=-1, keepdims=True)
    return rstd * (dxhat - m1 - xhat * m2)


def _ln_bwd(name, dy, xhat, rstd, g, tr=256):
    t, d = dy.shape
    tr = _tile(t, tr)

    def body(dy_ref, xhat_ref, rstd_ref, g_ref, dr_ref, drb_ref, dg_ref, db_ref):
        @pl.when(pl.program_id(0) == 0)
        def _():
            dg_ref[...] = jnp.zeros_like(dg_ref)
            db_ref[...] = jnp.zeros_like(db_ref)

        dy_t, xhat_t = dy_ref[...], xhat_ref[...]
        dr = _ln_bwd_rows(dy_t, xhat_t, rstd_ref[...], g_ref[...])
        dr_ref[...] = dr
        drb_ref[...] = dr.astype(BF16)
        dg_ref[...] += jnp.sum(dy_t * xhat_t, axis=0, keepdims=True)
        db_ref[...] += jnp.sum(dy_t, axis=0, keepdims=True)

    row = pl.BlockSpec((tr, d), lambda i: (i, 0))
    vec = pl.BlockSpec((1, d), lambda i: (0, 0))
    return pl.pallas_call(
        body, name=name, grid=(t // tr,), in_specs=[row, row, pl.BlockSpec((tr, 1), lambda i: (i, 0)), vec],
        out_specs=[row, row, vec, vec],
        out_shape=[jax.ShapeDtypeStruct((t, d), F32), jax.ShapeDtypeStruct((t, d), BF16),
                   jax.ShapeDtypeStruct((1, d), F32), jax.ShapeDtypeStruct((1, d), F32)],
        compiler_params=_params(("arbitrary",)),
    )(dy, xhat, rstd, g)


def _ln2_loss_bwd(r2, target, g, b, tr=256):
    t, d = r2.shape
    tr = _tile(t, tr)

    def body(r_ref, t_ref, g_ref, b_ref, dr_ref, drb_ref, loss_ref, dg_ref, db_ref, dsum_ref):
        @pl.when(pl.program_id(0) == 0)
        def _():
            loss_ref[...] = jnp.zeros_like(loss_ref)
            dg_ref[...] = jnp.zeros_like(dg_ref)
            db_ref[...] = jnp.zeros_like(db_ref)
            dsum_ref[...] = jnp.zeros_like(dsum_ref)

        xhat, rstd = _ln_stats(r_ref[...])
        g_t = g_ref[...]
        err = xhat * g_t + b_ref[...] - t_ref[...]
        loss_ref[...] += 0.5 * jnp.sum(jnp.mean(jnp.square(err), axis=-1, keepdims=True), axis=0, keepdims=True)
        dy = err * (1.0 / d)
        dr = _ln_bwd_rows(dy, xhat, rstd, g_t)
        dr_ref[...] = dr
        drb_ref[...] = dr.astype(BF16)
        dg_ref[...] += jnp.sum(dy * xhat, axis=0, keepdims=True)
        db_ref[...] += jnp.sum(dy, axis=0, keepdims=True)
        dsum_ref[...] += jnp.sum(dr, axis=0, keepdims=True)

    row = pl.BlockSpec((tr, d), lambda i: (i, 0))
    vec = pl.BlockSpec((1, d), lambda i: (0, 0))
    return pl.pallas_call(
        body, name="ln2_loss_bwd", grid=(t // tr,), in_specs=[row, row, vec, vec],
        out_specs=[row, row, pl.BlockSpec((8, 128), lambda i: (0, 0)), vec, vec, vec],
        out_shape=[jax.ShapeDtypeStruct((t, d), F32), jax.ShapeDtypeStruct((t, d), BF16),
                   jax.ShapeDtypeStruct((8, 128), F32), jax.ShapeDtypeStruct((1, d), F32),
                   jax.ShapeDtypeStruct((1, d), F32), jax.ShapeDtypeStruct((1, d), F32)],
        compiler_params=_params(("arbitrary",)),
    )(r2, target, g, b)


POOL_ROWS = 512


def _pool_mean_minus_token(u_ref, r0, rows, grp, first):
    width = u_ref.shape[1]
    body = u_ref[pl.ds(r0, rows), :]
    halo = u_ref[pl.ds(pl.multiple_of(jnp.maximum(r0 - POOL_HALO, 0), POOL_HALO), POOL_HALO), :]
    halo = jnp.where(first, 0.0, halo)
    full = jnp.concatenate([halo, body], axis=0)
    s = full
    for step in range(len(POOL_WINDOWS)):
        shifted = pltpu.roll(s, 1 << step, axis=0)
        s = s + jnp.where(grp >= step, shifted, 0.0)
    s = s[POOL_HALO:, :]
    tpos = r0 + lax.broadcasted_iota(jnp.int32, (rows, width), 0)
    count = jnp.minimum(tpos + 1, 2 << grp).astype(F32)
    return s / count - body, count


def _pool_fwd(u, w_pool, pool_scale, t, pw):
    gw = pw // len(POOL_WINDOWS)
    rows = _tile(t, POOL_ROWS)

    def body(u_ref, w_ref, s_ref, o_ref):
        grp = pl.program_id(0)

        def chunk(ci, carry):
            r0 = pl.multiple_of(ci * rows, rows)
            y, _ = _pool_mean_minus_token(u_ref, r0, rows, grp, ci == 0)
            yw = jnp.dot(y.astype(BF16), w_ref[...], preferred_element_type=F32)
            o_ref[pl.ds(r0, rows), :] = (yw * s_ref[...]).astype(BF16)
            return carry

        lax.fori_loop(0, t // rows, chunk, 0)

    return pl.pallas_call(
        body, name="pool_fwd", grid=(len(POOL_WINDOWS),),
        in_specs=[pl.BlockSpec((t, gw), lambda g: (0, g)), pl.BlockSpec((None, gw, gw), lambda g: (g, 0, 0)),
                  pl.BlockSpec((None, 1, gw), lambda g: (g, 0, 0))],
        out_specs=pl.BlockSpec((t, gw), lambda g: (0, g)),
        out_shape=jax.ShapeDtypeStruct((t, pw), BF16),
        compiler_params=_params(("parallel",)),
    )(u, w_pool, pool_scale)


def _pool_bwd(u, dmix, w_pool, pool_scale, t, pw):
    n_grp = len(POOL_WINDOWS)
    gw = pw // n_grp
    rows = _tile(t, POOL_ROWS)

    def body(u_ref, dm_ref, w_ref, s_ref, du_ref, dw_ref, ds_ref, e_ref):
        grp = pl.program_id(0)
        dw_ref[...] = jnp.zeros_like(dw_ref)
        ds_ref[...] = jnp.zeros_like(ds_ref)
        e_ref[pl.ds(t, POOL_HALO), :] = jnp.zeros((POOL_HALO, gw), F32)

        def chunk(ci, carry):
            r0 = pl.multiple_of(ci * rows, rows)
            y, count = _pool_mean_minus_token(u_ref, r0, rows, grp, ci == 0)
            yb = y.astype(BF16)
            yw = jnp.dot(yb, w_ref[...], preferred_element_type=F32)
            dy2 = dm_ref[pl.ds(r0, rows), :]
            ds_ref[...] += jnp.sum(dy2 * yw, axis=0, keepdims=True)
            dyw = (dy2 * s_ref[...]).astype(BF16)
            dw_ref[...] += lax.dot_general(yb, dyw, (((0,), (0,)), ((), ())), preferred_element_type=F32)
            dy = lax.dot_general(dyw, w_ref[...], (((1,), (1,)), ((), ())), preferred_element_type=F32)
            e_ref[pl.ds(r0, rows), :] = dy / count
            return carry

        lax.fori_loop(0, t // rows, chunk, 0)

        def chunk2(ci, carry):
            r0 = pl.multiple_of(ci * rows, rows)
            full = e_ref[pl.ds(r0, rows + POOL_HALO), :]
            s = full
            for step in range(n_grp):
                shifted = pltpu.roll(s, rows + POOL_HALO - (1 << step), axis=0)
                s = s + jnp.where(grp >= step, shifted, 0.0)
            e = full[:rows, :]
            tpos = r0 + lax.broadcasted_iota(jnp.int32, (rows, gw), 0)
            count = jnp.minimum(tpos + 1, 2 << grp).astype(F32)
            du_ref[pl.ds(r0, rows), :] = (s[:rows, :] - e * count).astype(BF16)
            return carry

        lax.fori_loop(0, t // rows, chunk2, 0)

    return pl.pallas_call(
        body, name="pool_bwd", grid=(n_grp,),
        in_specs=[pl.BlockSpec((t, gw), lambda g: (0, g)), pl.BlockSpec((t, gw), lambda g: (0, g)),
                  pl.BlockSpec((None, gw, gw), lambda g: (g, 0, 0)),
                  pl.BlockSpec((None, 1, gw), lambda g: (g, 0, 0))],
        out_specs=[pl.BlockSpec((t, gw), lambda g: (0, g)), pl.BlockSpec((None, gw, gw), lambda g: (g, 0, 0)),
                   pl.BlockSpec((None, 1, gw), lambda g: (g, 0, 0))],
        out_shape=[jax.ShapeDtypeStruct((t, pw), BF16), jax.ShapeDtypeStruct((n_grp, gw, gw), F32),
                   jax.ShapeDtypeStruct((n_grp, 1, gw), F32)],
        scratch_shapes=[pltpu.VMEM((t + POOL_HALO, gw), F32)],
        compiler_params=_params(("parallel",)),
    )(u, dmix, w_pool, pool_scale)


def _sb_scores(q, k_blk, scale, mask):
    z = lax.dot_general(q, k_blk, (((1,), (1,)), ((), ())), preferred_element_type=F32) * scale
    tneg = jnp.exp(-jnp.abs(z))
    log_not = -(jnp.maximum(z, 0.0) + jnp.log1p(tneg))
    return z, jnp.where(mask, log_not, 0.0), tneg


def _split_dot(v, tri):
    hi = v.astype(BF16)
    lo = (v - hi.astype(F32)).astype(BF16)
    return (jnp.dot(hi, tri, preferred_element_type=F32) + jnp.dot(lo, tri, preferred_element_type=F32))


def _attn_fwd(u, t, nh):
    scale = float(1.0 / (HEAD_DIM ** 0.5))

    def body(q_ref, k_ref, v_ref, o_ref):
        i = pl.program_id(1)
        q = q_ref[...].astype(BF16)
        row = lax.broadcasted_iota(jnp.int32, (QB, KB), 0)
        col = lax.broadcasted_iota(jnp.int32, (QB, KB), 1)
        suffix = (row >= col).astype(BF16)

        def step(n, carry):
            acc, after = carry
            ks = pl.multiple_of((i - n) * KB, KB)
            k_blk = k_ref[pl.ds(ks, KB), :].astype(BF16)
            v_blk = v_ref[pl.ds(ks, KB), :].astype(BF16)
            mask = jnp.logical_or(n > 0, col < row)
            z, log_not, _ = _sb_scores(q, k_blk, scale, mask)
            within = _split_dot(log_not, suffix)
            a = jnp.where(mask, jnp.exp(z + within + after), 0.0)
            acc = acc + jnp.dot(a.astype(BF16), v_blk, preferred_element_type=F32)
            return acc, after + jnp.sum(log_not, axis=1, keepdims=True)

        acc, _ = lax.fori_loop(0, i + 1, step, (jnp.zeros((QB, HEAD_DIM), F32), jnp.zeros((QB, 1), F32)))
        o_ref[...] = acc.astype(BF16)

    return pl.pallas_call(
        body, name="attn_fwd", grid=(nh, t // QB),
        in_specs=[pl.BlockSpec((QB, HEAD_DIM), lambda h, i: (i, nh + h)),
                  pl.BlockSpec((t, HEAD_DIM), lambda h, i: (0, 2 * nh + h)),
                  pl.BlockSpec((t, HEAD_DIM), lambda h, i: (0, 3 * nh + h))],
        out_specs=pl.BlockSpec((QB, HEAD_DIM), lambda h, i: (i, h)),
        out_shape=jax.ShapeDtypeStruct((t, nh * HEAD_DIM), BF16),
        compiler_params=_params(("parallel", "arbitrary")),
    )(u, u, u)


def _attn_bwd(u, dmix, t, nh):
    scale = float(1.0 / (HEAD_DIM ** 0.5))
    width = nh * HEAD_DIM

    def body(q_ref, k_ref, v_ref, do_ref, dq_ref, dk_ref, dv_ref, g_ref, z_ref):
        i = pl.program_id(1)

        @pl.when(i == 0)
        def _():
            dk_ref[...] = jnp.zeros_like(dk_ref)
            dv_ref[...] = jnp.zeros_like(dv_ref)

        q = q_ref[...].astype(BF16)
        do = do_ref[...].astype(BF16)
        row = lax.broadcasted_iota(jnp.int32, (QB, KB), 0)
        col = lax.broadcasted_iota(jnp.int32, (QB, KB), 1)
        suffix = (row >= col).astype(BF16)
        prefix = (row <= col).astype(BF16)

        def down(n, after):
            ks = pl.multiple_of((i - n) * KB, KB)
            k_blk = k_ref[pl.ds(ks, KB), :].astype(BF16)
            v_blk = v_ref[pl.ds(ks, KB), :].astype(BF16)
            mask = jnp.logical_or(n > 0, col < row)
            z, log_not, _ = _sb_scores(q, k_blk, scale, mask)
            within = _split_dot(log_not, suffix)
            a = jnp.where(mask, jnp.exp(z + within + after), 0.0)
            da = lax.dot_general(do, v_blk, (((1,), (1,)), ((), ())), preferred_element_type=F32)
            g_ref[:, pl.ds(ks, KB)] = a * da
            z_ref[:, pl.ds(ks, KB)] = z
            dv_ref[pl.ds(ks, KB), :] += lax.dot_general(a.astype(BF16), do, (((0,), (0,)), ((), ())),
                                                        preferred_element_type=F32)
            return after + jnp.sum(log_not, axis=1, keepdims=True)

        lax.fori_loop(0, i + 1, down, jnp.zeros((QB, 1), F32))

        def up(kb, carry):
            dq, before = carry
            ks = pl.multiple_of(kb * KB, KB)
            k_blk = k_ref[pl.ds(ks, KB), :].astype(BF16)
            g = g_ref[:, pl.ds(ks, KB)]
            z = z_ref[:, pl.ds(ks, KB)]
            mask = jnp.logical_or(kb < i, col < row)
            g_upto = _split_dot(g, prefix) + before
            dz = jnp.where(mask, g - jax.nn.sigmoid(z) * g_upto, 0.0)
            dzs = (dz * scale).astype(BF16)
            dq = dq + jnp.dot(dzs, k_blk, preferred_element_type=F32)
            dk_ref[pl.ds(ks, KB), :] += lax.dot_general(dzs, q, (((0,), (0,)), ((), ())),
                                                        preferred_element_type=F32)
            return dq, before + jnp.sum(g, axis=1, keepdims=True)

        dq, _ = lax.fori_loop(0, i + 1, up, (jnp.zeros((QB, HEAD_DIM), F32), jnp.zeros((QB, 1), F32)))
        dq_ref[...] = dq.astype(BF16)

    strip = lambda off: pl.BlockSpec((t, HEAD_DIM), lambda h, i: (0, off + h))
    return pl.pallas_call(
        body, name="attn_bwd", grid=(nh, t // QB),
        in_specs=[pl.BlockSpec((QB, HEAD_DIM), lambda h, i: (i, nh + h)), strip(2 * nh), strip(3 * nh),
                  pl.BlockSpec((QB, HEAD_DIM), lambda h, i: (i, nh + h))],
        out_specs=[pl.BlockSpec((QB, HEAD_DIM), lambda h, i: (i, h)), strip(0), strip(0)],
        out_shape=[jax.ShapeDtypeStruct((t, width), BF16), jax.ShapeDtypeStruct((t, width), F32),
                   jax.ShapeDtypeStruct((t, width), F32)],
        scratch_shapes=[pltpu.VMEM((QB, t), F32), pltpu.VMEM((QB, t), F32)],
        compiler_params=_params(("parallel", "arbitrary")),
    )(u, u, u, dmix)


def _row_tile(rows, cols, pref_bytes=2 * 1024 * 1024):
    tr = max(8, pref_bytes // (4 * cols))
    while rows % tr:
        tr //= 2
    return max(tr, 1)


def _pair_sum(name, g, s, c_idx):
    _, _, r2, cols = g.shape
    tr = _row_tile(r2, cols)

    def body(c_ref, g_ref, s_ref, o_ref):
        o_ref[...] = g_ref[...] + s_ref[...]

    return pl.pallas_call(
        body, name=name,
        grid_spec=pltpu.PrefetchScalarGridSpec(
            num_scalar_prefetch=1, grid=(N_CHIPS, r2 // tr),
            in_specs=[pl.BlockSpec((None, None, tr, cols), lambda p, i, c: (p, c[0], i, 0)),
                      pl.BlockSpec((None, tr, cols), lambda p, i, c: (p, i, 0))],
            out_specs=pl.BlockSpec((None, tr, cols), lambda p, i, c: (p, i, 0))),
        out_shape=jax.ShapeDtypeStruct((N_CHIPS, r2, cols), F32),
        compiler_params=_params(("parallel", "parallel")),
    )(c_idx, g, s)


def _chip_sum(name, p, r, chip_idx):
    _, r2, cols = p.shape
    tr = _row_tile(r2, cols)

    def body(c_ref, p_ref, r_ref, o_ref):
        o_ref[...] = ((p_ref[...] + r_ref[0]) + r_ref[1]) + r_ref[2]

    return pl.pallas_call(
        body, name=name,
        grid_spec=pltpu.PrefetchScalarGridSpec(
            num_scalar_prefetch=1, grid=(r2 // tr,),
            in_specs=[pl.BlockSpec((None, tr, cols), lambda i, c: (c[0], i, 0)),
                      pl.BlockSpec((3, tr, cols), lambda i, c: (0, i, 0))],
            out_specs=pl.BlockSpec((tr, cols), lambda i, c: (i, 0))),
        out_shape=jax.ShapeDtypeStruct((r2, cols), F32),
        compiler_params=_params(("parallel",)),
    )(chip_idx, p, r)


def _colsum(name, a):
    def body(a_ref, o_ref):
        o_ref[...] = jnp.sum(a_ref[...], axis=0, keepdims=True)

    whole = lambda shape: pl.BlockSpec(shape, lambda i: (0, 0))
    return pl.pallas_call(
        body, name=name, grid=(1,), in_specs=[whole(a.shape)], out_specs=whole((1, a.shape[1])),
        out_shape=jax.ShapeDtypeStruct((1, a.shape[1]), F32), compiler_params=_params(("arbitrary",)),
    )(a)


def _adamw(name, w, g, m, v):
    rows, cols = w.shape
    tr = _row_tile(rows, cols, 1024 * 1024)

    def body(w_ref, g_ref, m_ref, v_ref, d_ref, nm_ref, nv_ref):
        g_t = g_ref[...]
        m_t = ADAM_B1 * m_ref[...] + (1.0 - ADAM_B1) * g_t
        v_t = ADAM_B2 * v_ref[...] + (1.0 - ADAM_B2) * jnp.square(g_t)
        m_hat = m_t / (1.0 - ADAM_B1 ** ADAM_STEP)
        v_hat = v_t / (1.0 - ADAM_B2 ** ADAM_STEP)
        d_ref[...] = -ADAM_LR * (m_hat / (jnp.sqrt(v_hat) + ADAM_EPS) + ADAM_WD * w_ref[...])
        nm_ref[...] = m_t
        nv_ref[...] = v_t

    spec = pl.BlockSpec((tr, cols), lambda i: (i, 0))
    shape = jax.ShapeDtypeStruct((rows, cols), F32)
    return pl.pallas_call(
        body, name=name, grid=(rows // tr,), in_specs=[spec] * 4, out_specs=[spec] * 3, out_shape=[shape] * 3,
        compiler_params=_params(("parallel",)),
    )(w, g, m, v)


ANY = pl.BlockSpec(memory_space=pl.ANY)


def _gather_weights(shards, whole):
    ns, nw = len(shards), len(whole)

    def body(*refs):
        ins, wins = refs[:ns], refs[ns:ns + nw]
        outs, wouts = refs[ns + nw:2 * ns + nw], refs[2 * ns + nw:2 * (ns + nw)]
        send_sems, recv_sems, fsend_sems, frecv_sems, wsend_sems, wrecv_sems, local_sems = refs[2 * (ns + nw):]
        x, y, c = _pos()
        me = 2 * x + y
        chips = _other_chips(x, y)
        sibling = (x, y, 1 - c)

        local = [pltpu.make_async_copy(ins[a], outs[a].at[me], local_sems.at[a]) for a in range(ns)]
        local += [pltpu.make_async_copy(wins[a], wouts[a].at[me], local_sems.at[ns + a]) for a in range(nw)]
        for cp in local:
            cp.start()

        def half_copy(a, j, sems_s, sems_r, chip_idx, half, to):
            piece = outs[a].at[chip_idx, half]
            return pltpu.make_async_remote_copy(src_ref=piece, dst_ref=piece, send_sem=sems_s.at[3 * a + j],
                                                recv_sem=sems_r.at[3 * a + j], device_id=to, device_id_type=MESH)

        sends = []
        for a in range(ns):
            for j, chip in enumerate(chips):
                cp = pltpu.make_async_remote_copy(
                    src_ref=ins[a].at[c], dst_ref=outs[a].at[me, c], send_sem=send_sems.at[3 * a + j],
                    recv_sem=recv_sems.at[3 * a + j], device_id=(*chip, c), device_id_type=MESH)
                cp.start()
                sends.append(cp)
        for a in range(nw):
            for j, chip in enumerate(chips):
                cp = pltpu.make_async_remote_copy(
                    src_ref=wins[a], dst_ref=wouts[a].at[me], send_sem=wsend_sems.at[3 * a + j],
                    recv_sem=wrecv_sems.at[3 * a + j], device_id=(*chip, c), device_id_type=MESH)
                cp.start()
                sends.append(cp)
        for a in range(ns):
            for j, chip in enumerate(chips):
                idx = 2 * chip[0] + chip[1]
                half_copy(a, j, send_sems, recv_sems, idx, c, (x, y, c)).wait_recv()
                fw = half_copy(a, j, fsend_sems, frecv_sems, idx, c, sibling)
                fw.start()
                sends.append(fw)
        for a in range(ns):
            for j, chip in enumerate(chips):
                half_copy(a, j, fsend_sems, frecv_sems, 2 * chip[0] + chip[1], 1 - c, (x, y, c)).wait_recv()
        for a in range(nw):
            for j, chip in enumerate(chips):
                land = wouts[a].at[2 * chip[0] + chip[1]]
                pltpu.make_async_remote_copy(
                    src_ref=land, dst_ref=land, send_sem=wsend_sems.at[3 * a + j],
                    recv_sem=wrecv_sems.at[3 * a + j], device_id=(x, y, c), device_id_type=MESH).wait_recv()
        for cp in sends:
            cp.wait_send()
        for cp in local:
            cp.wait()

    halves = [s.reshape(2, s.shape[0] // 2, s.shape[1]) for s in shards]
    out_shape = [jax.ShapeDtypeStruct((N_CHIPS, *h.shape), h.dtype) for h in halves]
    out_shape += [jax.ShapeDtypeStruct((N_CHIPS, *w.shape), w.dtype) for w in whole]
    outs = pl.pallas_call(
        body, name="gather_weights", in_specs=[ANY] * (ns + nw), out_specs=[ANY] * (ns + nw), out_shape=out_shape,
        scratch_shapes=[pltpu.SemaphoreType.DMA((3 * ns,)), pltpu.SemaphoreType.DMA((3 * ns,)),
                        pltpu.SemaphoreType.DMA((3 * ns,)), pltpu.SemaphoreType.DMA((3 * ns,)),
                        pltpu.SemaphoreType.DMA((3 * nw,)), pltpu.SemaphoreType.DMA((3 * nw,)),
                        pltpu.SemaphoreType.DMA((ns + nw,))],
    )(*halves, *whole)
    gathered = [o.reshape(N_CHIPS, s.shape[0], s.shape[1]) for o, s in zip(outs[:ns], shards)]
    return gathered, list(outs[ns:])


def _sibling_swap_halves(gs):
    n = len(gs)

    def body(*refs):
        ins, outs = refs[:n], refs[n:2 * n]
        send_sems, recv_sems = refs[2 * n:]
        x, y, c = _pos()
        cps = []
        for a in range(n):
            cp = pltpu.make_async_remote_copy(
                src_ref=ins[a].at[:, 1 - c], dst_ref=outs[a], send_sem=send_sems.at[a], recv_sem=recv_sems.at[a],
                device_id=(x, y, 1 - c), device_id_type=MESH)
            cp.start()
            cps.append(cp)
        for cp in cps:
            cp.wait()

    return pl.pallas_call(
        body, name="rs_sibling_swap", in_specs=[ANY] * n, out_specs=[ANY] * n,
        out_shape=[jax.ShapeDtypeStruct((N_CHIPS, g.shape[2], g.shape[3]), g.dtype) for g in gs],
        scratch_shapes=[pltpu.SemaphoreType.DMA((n,)), pltpu.SemaphoreType.DMA((n,))],
    )(*gs)


def _chip_scatter(ps):
    n = len(ps)

    def body(*refs):
        ins, outs = refs[:n], refs[n:2 * n]
        send_sems, recv_sems = refs[2 * n:]
        x, y, c = _pos()
        cps = []
        for a in range(n):
            for j, chip in enumerate(_other_chips(x, y)):
                cp = pltpu.make_async_remote_copy(
                    src_ref=ins[a].at[2 * chip[0] + chip[1]], dst_ref=outs[a].at[j],
                    send_sem=send_sems.at[3 * a + j], recv_sem=recv_sems.at[3 * a + j],
                    device_id=(*chip, c), device_id_type=MESH)
                cp.start()
                cps.append(cp)
        for cp in cps:
            cp.wait()

    return pl.pallas_call(
        body, name="rs_chip_scatter", in_specs=[ANY] * n, out_specs=[ANY] * n,
        out_shape=[jax.ShapeDtypeStruct((3, p.shape[1], p.shape[2]), p.dtype) for p in ps],
        scratch_shapes=[pltpu.SemaphoreType.DMA((3 * n,)), pltpu.SemaphoreType.DMA((3 * n,))],
    )(*ps)


def _sibling_assemble(qs):
    n = len(qs)

    def body(*refs):
        ins, outs = refs[:n], refs[n:2 * n]
        send_sems, recv_sems, local_sems = refs[2 * n:]
        x, y, c = _pos()
        cps = []
        for a in range(n):
            lc = pltpu.make_async_copy(ins[a], outs[a].at[c], local_sems.at[a])
            lc.start()
            cp = pltpu.make_async_remote_copy(
                src_ref=ins[a], dst_ref=outs[a].at[c], send_sem=send_sems.at[a], recv_sem=recv_sems.at[a],
                device_id=(x, y, 1 - c), device_id_type=MESH)
            cp.start()
            cps.append((lc, cp))
        for a, (lc, cp) in enumerate(cps):
            lc.wait()
            cp.wait_send()
            pltpu.make_async_remote_copy(
                src_ref=ins[a], dst_ref=outs[a].at[1 - c], send_sem=send_sems.at[a], recv_sem=recv_sems.at[a],
                device_id=(x, y, c), device_id_type=MESH).wait_recv()

    return pl.pallas_call(
        body, name="rs_sibling_assemble", in_specs=[ANY] * n, out_specs=[ANY] * n,
        out_shape=[jax.ShapeDtypeStruct((2, *q.shape), q.dtype) for q in qs],
        scratch_shapes=[pltpu.SemaphoreType.DMA((n,)), pltpu.SemaphoreType.DMA((n,)),
                        pltpu.SemaphoreType.DMA((n,))],
    )(*qs)


def _all_reduce_small(packed):
    rows, cols = packed.shape

    def body(in_ref, out_ref, all_ref, send_sems, recv_sems):
        x, y, c = _pos()
        me = 4 * x + 2 * y + c
        all_ref[me] = in_ref[...]
        cps = []
        for r in range(1, N_DEV):
            bx, by, bc = (r >> 2) & 1, (r >> 1) & 1, r & 1
            peer = (1 - x if bx else x, 1 - y if by else y, 1 - c if bc else c)
            cp = pltpu.make_async_remote_copy(
                src_ref=in_ref, dst_ref=all_ref.at[me], send_sem=send_sems.at[r - 1], recv_sem=recv_sems.at[r - 1],
                device_id=peer, device_id_type=MESH)
            cp.start()
            cps.append(cp)
        for cp in cps:
            cp.wait()
        total = all_ref[0]
        for d in range(1, N_DEV):
            total = total + all_ref[d]
        out_ref[...] = total

    vmem = pl.BlockSpec(memory_space=pltpu.VMEM)
    return pl.pallas_call(
        body, name="all_reduce_small", in_specs=[vmem], out_specs=vmem,
        out_shape=jax.ShapeDtypeStruct((rows, cols), F32),
        scratch_shapes=[pltpu.VMEM((N_DEV, rows, cols), F32), pltpu.SemaphoreType.DMA((N_DEV - 1,)),
                        pltpu.SemaphoreType.DMA((N_DEV - 1,))],
        compiler_params=pltpu.CompilerParams(vmem_limit_bytes=VMEM_LIMIT),
    )(packed)


def kernel(x, ln_in_g, ln_in_b, w_in, w_pool, pool_scale, w_out, ln1_g, ln1_b, w_ff1, b_ff1, w_ff2, b_ff2, ln2_g, ln2_b, loss_target, m_ln_in_g, m_ln_in_b, m_w_in, m_w_pool, m_pool_scale, m_w_out, m_ln1_g, m_ln1_b, m_w_ff1, m_b_ff1, m_w_ff2, m_b_ff2, m_ln2_g, m_ln2_b, v_ln_in_g, v_ln_in_b, v_w_in, v_w_pool, v_pool_scale, v_w_out, v_ln1_g, v_ln1_b, v_w_ff1, v_b_ff1, v_w_ff2, v_b_ff2, v_ln2_g, v_ln2_b):
    t, d = x.shape[1], x.shape[2]
    pw = d // 2
    n_grp = len(POOL_WINDOWS)
    gw = pw // n_grp
    gwc = gw // N_CHIPS
    nh = pw // HEAD_DIM
    ff = w_ff1.shape[2] * N_CHIPS
    assert w_in.shape[0] == 1 and w_in.shape[2] * N_CHIPS == 2 * d and gwc <= 128

    x_idx, y_idx, c_idx = _pos()
    chip_arr = jnp.reshape(2 * x_idx + y_idx, (1,)).astype(jnp.int32)
    c_arr = jnp.reshape(c_idx, (1,)).astype(jnp.int32)

    xs = x.reshape(t, d)
    target = loss_target.reshape(t, d)
    row = lambda vec: vec.reshape(1, -1)

    scale_tile = jnp.zeros((8, 128), F32).at[:n_grp, :gwc].set(pool_scale[0])
    (win_g, wout_g, wff1_g, wff2_g, wpool_g), (scale_g,) = _gather_weights(
        [w_in[0].astype(BF16), w_out[0].astype(BF16), w_ff1[0].astype(BF16), w_ff2[0].astype(BF16),
         w_pool[0].reshape(gw, gw).astype(BF16)], [scale_tile])
    wout_full = wout_g.reshape(d, d)
    wff2_full = wff2_g.reshape(ff, d)
    wpool_full = wpool_g.reshape(N_CHIPS, n_grp, gwc, gw).transpose(1, 0, 2, 3).reshape(n_grp, gw, gw)
    scale_full = scale_g[:, :n_grp, :gwc].transpose(1, 0, 2).reshape(n_grp, 1, gw)

    h0, h0b, xhat0, rstd0 = _ln_fwd("ln_in_fwd", xs, row(ln_in_g), row(ln_in_b))

    def store_f32(acc, extra_refs, out_refs):
        out_refs[0][...] = acc

    u = _mm_nn("in_proj", h0b, win_g, (), lambda tm, tn: [], [jax.ShapeDtypeStruct((t, 2 * d), F32)],
               lambda tm, tn: [_tile_spec(tm, tn)], store_f32, b_chips=True)[0]
    y_pool = _pool_fwd(u, wpool_full, scale_full, t, pw)
    y_sb = _attn_fwd(u, t, nh)
    mix_in = jnp.concatenate([y_pool, y_sb], axis=1)

    def residual(acc, extra_refs, out_refs):
        out_refs[0][...] = ALPHA * extra_refs[0][...] + acc

    r1 = _mm_nn("out_proj", mix_in, wout_full, (h0,), lambda tm, tn: [_tile_spec(tm, tn)],
                [jax.ShapeDtypeStruct((t, d), F32)], lambda tm, tn: [_tile_spec(tm, tn)], residual)[0]
    h1, h1b, xhat1, rstd1 = _ln_fwd("ln1_fwd", r1, ln1_g, ln1_b)

    def relu_sq(acc, extra_refs, out_refs):
        p = jnp.maximum(acc + extra_refs[0][...], 0.0)
        out_refs[0][...] = p
        out_refs[1][...] = jnp.square(p).astype(BF16)

    relu_z, act_b = _mm_nn("ff1", h1b, wff1_g, (b_ff1,), lambda tm, tn: [_row_spec(tn)],
                           [jax.ShapeDtypeStruct((t, ff), F32), jax.ShapeDtypeStruct((t, ff), BF16)],
                           lambda tm, tn: [_tile_spec(tm, tn)] * 2, relu_sq, b_chips=True)

    def residual_bias(acc, extra_refs, out_refs):
        out_refs[0][...] = ALPHA * extra_refs[0][...] + (acc + extra_refs[1][...])

    r2 = _mm_nn("ff2", act_b, wff2_full, (h1, b_ff2), lambda tm, tn: [_tile_spec(tm, tn), _row_spec(tn)],
                [jax.ShapeDtypeStruct((t, d), F32)], lambda tm, tn: [_tile_spec(tm, tn)], residual_bias)[0]

    dr2, dr2b, loss_tile, g_ln2_g, g_ln2_b, g_b_ff2 = _ln2_loss_bwd(r2, target, ln2_g, ln2_b)
    loss = lax.psum(loss_tile[0, 0], ("x", "y", "c"))

    g_w_ff2 = _mm_tn("grad_w_ff2", act_b, dr2b)

    def relu_sq_bwd(acc, extra_refs, out_refs):
        dz = acc * (2.0 * extra_refs[0][...])
        out_refs[0][...] = dz.astype(BF16)
        rows = lax.broadcasted_iota(jnp.int32, out_refs[1].shape, 0)
        out_refs[1][...] = jnp.where(rows == 0, jnp.sum(dz, axis=0, keepdims=True), 0.0)

    tm_ff = _tile(t, 1024)
    dz1b, g_b_ff1_parts = _mm_nt(
        "ff2_bwd", dr2b, wff2_full, (relu_z,), lambda tm, tn: [_tile_spec(tm, tn)],
        [jax.ShapeDtypeStruct((t, ff), BF16), jax.ShapeDtypeStruct((8 * (t // tm_ff), ff), F32)],
        lambda tm, tn: [_tile_spec(tm, tn), pl.BlockSpec((8, tn), lambda i, j, kk: (i, j))], relu_sq_bwd)
    g_w_ff1 = _mm_tn("grad_w_ff1", h1b, dz1b, out_chips=True)

    def plus_alpha(acc, extra_refs, out_refs):
        out_refs[0][...] = ALPHA * extra_refs[0][...] + acc

    dh1 = _mm_nt("ff1_bwd", dz1b, wff1_g, (dr2,), lambda tm, tn: [_tile_spec(tm, tn)],
                 [jax.ShapeDtypeStruct((t, d), F32)], lambda tm, tn: [_tile_spec(tm, tn)], plus_alpha,
                 b_chips=True)[0]
    dr1, dr1b, g_ln1_g, g_ln1_b = _ln_bwd("ln1_bwd", dh1, xhat1, rstd1, ln1_g)

    g_w_out = _mm_tn("grad_w_out", mix_in, dr1b)
    dmix = _mm_nt("out_proj_bwd", dr1b, wout_full, (), lambda tm, tn: [], [jax.ShapeDtypeStruct((t, d), F32)],
                  lambda tm, tn: [_tile_spec(tm, tn)], store_f32)[0]
    du_pool, g_w_pool_full, g_scale_full = _pool_bwd(u, dmix, wpool_full, scale_full, t, pw)
    dq, dk, dv = _attn_bwd(u, dmix, t, nh)
    du = jnp.concatenate([du_pool, dq, dk.astype(BF16), dv.astype(BF16)], axis=1)
    g_w_in = _mm_tn("grad_w_in", h0b, du, out_chips=True)
    dh0 = _mm_nt("in_proj_bwd", du, win_g, (dr1,), lambda tm, tn: [_tile_spec(tm, tn)],
                 [jax.ShapeDtypeStruct((t, d), F32)], lambda tm, tn: [_tile_spec(tm, tn)], plus_alpha,
                 b_chips=True)[0]
    dx, _, g_ln_in_g, g_ln_in_b = _ln_bwd("ln_in_bwd", dh0, xhat0, rstd0, row(ln_in_g))

    g_w_pool_chips = g_w_pool_full.reshape(n_grp, N_CHIPS, gwc, gw).transpose(1, 0, 2, 3).reshape(N_CHIPS, gw, gw)
    by_chip = [g_w_in, g_w_out.reshape(N_CHIPS, d // N_CHIPS, d), g_w_ff1, g_w_ff2.reshape(N_CHIPS, ff // N_CHIPS, d),
               g_w_pool_chips]
    names = ["w_in", "w_out", "w_ff1", "w_ff2", "w_pool"]
    halves = [g.reshape(N_CHIPS, 2, g.shape[1] // 2, g.shape[2]) for g in by_chip]
    swapped = _sibling_swap_halves(halves)
    pair = [_pair_sum("pair_sum_" + nm, g, s, c_arr) for nm, g, s in zip(names, halves, swapped)]
    received = _chip_scatter(pair)
    mine = [_chip_sum("chip_sum_" + nm, p, r, chip_arr) for nm, p, r in zip(names, pair, received)]
    reduced = [f.reshape(f.shape[0] * f.shape[1], f.shape[2]) for f in _sibling_assemble(mine)]

    big = {}
    for nm, g, w, m, v in zip(names, reduced, [w_in, w_out, w_ff1, w_ff2, w_pool], [m_w_in, m_w_out, m_w_ff1, m_w_ff2, m_w_pool],
                              [v_w_in, v_w_out, v_w_ff1, v_w_ff2, v_w_pool]):
        flat = lambda arr: arr.reshape(g.shape)
        delta, new_m, new_v = _adamw("adamw_" + nm, flat(w), g, flat(m), flat(v))
        big[nm] = tuple(arr.reshape(w.shape) for arr in (g, delta, new_m, new_v))

    lane = 2048 if d % 2048 == 0 else d
    small_names = ["ln_in_g", "ln_in_b", "ln1_g", "ln1_b", "b_ff1", "b_ff2", "ln2_g", "ln2_b"]
    small_w = dict(ln_in_g=ln_in_g, ln_in_b=ln_in_b, ln1_g=ln1_g, ln1_b=ln1_b, b_ff1=b_ff1, b_ff2=b_ff2, ln2_g=ln2_g,
                   ln2_b=ln2_b)
    small_m = dict(ln_in_g=m_ln_in_g, ln_in_b=m_ln_in_b, ln1_g=m_ln1_g, ln1_b=m_ln1_b, b_ff1=m_b_ff1, b_ff2=m_b_ff2,
                   ln2_g=m_ln2_g, ln2_b=m_ln2_b)
    small_v = dict(ln_in_g=v_ln_in_g, ln_in_b=v_ln_in_b, ln1_g=v_ln1_g, ln1_b=v_ln1_b, b_ff1=v_b_ff1, b_ff2=v_b_ff2,
                   ln2_g=v_ln2_g, ln2_b=v_ln2_b)
    small_g = dict(ln_in_g=g_ln_in_g, ln_in_b=g_ln_in_b, ln1_g=g_ln1_g, ln1_b=g_ln1_b, b_ff2=g_b_ff2, ln2_g=g_ln2_g,
                   ln2_b=g_ln2_b)

    def pack(parts):
        flat = jnp.concatenate([p.reshape(-1) for p in parts])
        n_rows = -(-flat.shape[0] // lane)
        n_rows = -(-n_rows // 8) * 8
        return jnp.pad(flat, (0, n_rows * lane - flat.shape[0])).reshape(n_rows, lane)

    small_g["b_ff1"] = _colsum("b_ff1_colsum", g_b_ff1_parts)
    summed = _all_reduce_small(pack([small_g[nm] for nm in small_names] + [g_scale_full])).reshape(-1)

    g_small, off = {}, 0
    for nm in small_names:
        g_small[nm] = summed[off:off + small_w[nm].size]
        off += small_w[nm].size
    g_scale_all = summed[off:off + n_grp * gw].reshape(n_grp, N_CHIPS, gwc)
    g_scale = lax.dynamic_index_in_dim(g_scale_all, chip_arr[0], axis=1, keepdims=False)

    order = small_names + ["pool_scale"]
    small_w["pool_scale"], small_m["pool_scale"], small_v["pool_scale"] = pool_scale, m_pool_scale, v_pool_scale
    g_small["pool_scale"] = g_scale
    delta_s, new_m_s, new_v_s = _adamw("adamw_small", pack([small_w[nm] for nm in order]),
                                       pack([g_small[nm] for nm in order]), pack([small_m[nm] for nm in order]),
                                       pack([small_v[nm] for nm in order]))
    small = {}
    off = 0
    for nm in order:
        size, shape = small_w[nm].size, small_w[nm].shape
        cut = lambda arr: arr.reshape(-1)[off:off + size].reshape(shape)
        small[nm] = (g_small[nm].reshape(shape), cut(delta_s), cut(new_m_s), cut(new_v_s))
        off += size

    every = {**big, **small}
    weight_order = ["ln_in_g", "ln_in_b", "w_in", "w_pool", "pool_scale", "w_out", "ln1_g", "ln1_b", "w_ff1", "b_ff1",
                    "w_ff2", "b_ff2", "ln2_g", "ln2_b"]
    grads = [every[nm][0] for nm in weight_order]
    deltas = [every[nm][1] for nm in weight_order]
    new_ms = [every[nm][2] for nm in weight_order]
    new_vs = [every[nm][3] for nm in weight_order]
    return (loss, dx.reshape(x.shape), *grads, *deltas, *new_ms, *new_vs)
```

```python
import functools

import jax
import jax.numpy as jnp
from jax import lax
from jax.experimental import pallas as pl
from jax.experimental.pallas import tpu as pltpu

F32 = jnp.float32
BF16 = jnp.bfloat16
MESH = pl.DeviceIdType.MESH

HEAD_DIM = 128
POOL_WINDOWS = (2, 4, 8, 16)
POOL_HALO = 16
LN_EPS = 1e-5
ALPHA = 2.0 ** 0.25
ADAM_LR, ADAM_B1, ADAM_B2, ADAM_EPS, ADAM_WD, ADAM_STEP = 0.001, 0.9, 0.999, 1e-08, 0.01, 10

QB = 128
KB = 128
VMEM_LIMIT = 56 * 1024 * 1024
N_CHIPS = 4
N_DEV = 8


def _params(sem=None):
    return pltpu.CompilerParams(dimension_semantics=sem, vmem_limit_bytes=VMEM_LIMIT)


def _tile(dim, pref):
    return pref if dim % pref == 0 else dim


def _pos():
    return lax.axis_index("x"), lax.axis_index("y"), lax.axis_index("c")


def _other_chips(x, y):
    return [(1 - x, y), (x, 1 - y), (1 - x, 1 - y)]


def _matmul(name, a, b, grid, a_spec, b_spec, contract, acc_shape, extras, extra_specs, out_shape, out_specs,
            epilogue):
    n_extra, n_out, gk = len(extras), len(out_shape), grid[2]

    def product(a_ref, b_ref):
        return lax.dot_general(a_ref[...], b_ref[...], (contract, ((), ())), preferred_element_type=F32)

    def body_one_step(*refs):
        epilogue(product(refs[0], refs[1]), refs[2:2 + n_extra], refs[2 + n_extra:])

    def body(*refs):
        a_ref, b_ref = refs[0], refs[1]
        extra_refs = refs[2:2 + n_extra]
        out_refs = refs[2 + n_extra:2 + n_extra + n_out]
        acc_ref = refs[-1]
        kk = pl.program_id(2)

        @pl.when(kk == 0)
        def _():
            acc_ref[...] = product(a_ref, b_ref)

        @pl.when(kk > 0)
        def _():
            acc_ref[...] += product(a_ref, b_ref)

        @pl.when(kk == gk - 1)
        def _():
            epilogue(acc_ref[...], extra_refs, out_refs)

    return pl.pallas_call(
        body_one_step if gk == 1 else body, name=name, grid=grid, in_specs=[a_spec, b_spec, *extra_specs],
        out_specs=out_specs, out_shape=out_shape,
        scratch_shapes=[] if gk == 1 else [pltpu.VMEM(acc_shape, F32)],
        compiler_params=_params(("parallel", "parallel", "arbitrary")),
    )(a, b, *extras)


def _mm_nn(name, a, b, extras, extra_specs, out_shape, out_specs, epilogue, b_chips=False, tm=1024, tn=1024,
           tk=2048):
    m, k = a.shape
    n = b.shape[1] if not b_chips else b.shape[2] * N_CHIPS
    tm, tk = _tile(m, tm), _tile(k, tk)
    if b_chips:
        tn = _tile(b.shape[2], tn)
        nb = b.shape[2] // tn
        b_spec = pl.BlockSpec((None, tk, tn), lambda i, j, kk: (j // nb, kk, j % nb))
    else:
        tn = _tile(n, tn)
        b_spec = pl.BlockSpec((tk, tn), lambda i, j, kk: (kk, j))
    a_spec = pl.BlockSpec((tm, tk), lambda i, j, kk: (i, kk))
    return _matmul(name, a, b, (m // tm, n // tn, k // tk), a_spec, b_spec, ((1,), (0,)), (tm, tn), extras,
                   extra_specs(tm, tn), out_shape, out_specs(tm, tn), epilogue)


def _mm_nt(name, a, b, extras, extra_specs, out_shape, out_specs, epilogue, b_chips=False, tm=1024, tn=1024,
           tk=2048):
    m, k = a.shape
    n = b.shape[0] if not b_chips else b.shape[1]
    tm, tn = _tile(m, tm), _tile(n, tn)
    if b_chips:
        tk = _tile(b.shape[2], tk)
        nb = b.shape[2] // tk
        b_spec = pl.BlockSpec((None, tn, tk), lambda i, j, kk: (kk // nb, j, kk % nb))
    else:
        tk = _tile(k, tk)
        b_spec = pl.BlockSpec((tn, tk), lambda i, j, kk: (j, kk))
    a_spec = pl.BlockSpec((tm, tk), lambda i, j, kk: (i, kk))
    return _matmul(name, a, b, (m // tm, n // tn, k // tk), a_spec, b_spec, ((1,), (1,)), (tm, tn), extras,
                   extra_specs(tm, tn), out_shape, out_specs(tm, tn), epilogue)


def _mm_tn(name, a, b, out_chips=False, tm=1024, tn=1024, tk=2048):
    k, m = a.shape
    n = b.shape[1]
    tm, tk = _tile(m, tm), _tile(k, tk)
    a_spec = pl.BlockSpec((tk, tm), lambda i, j, kk: (kk, i))
    if out_chips:
        nc = n // N_CHIPS
        tn = _tile(nc, tn)
        nb = nc // tn
        out_shape = [jax.ShapeDtypeStruct((N_CHIPS, m, nc), F32)]
        out_specs = [pl.BlockSpec((None, tm, tn), lambda i, j, kk: (j // nb, i, j % nb))]
    else:
        tn = _tile(n, tn)
        out_shape = [jax.ShapeDtypeStruct((m, n), F32)]
        out_specs = [pl.BlockSpec((tm, tn), lambda i, j, kk: (i, j))]
    b_spec = pl.BlockSpec((tk, tn), lambda i, j, kk: (kk, j))

    def epilogue(acc, extra_refs, out_refs):
        out_refs[0][...] = acc

    return _matmul(name, a, b, (m // tm, n // tn, k // tk), a_spec, b_spec, ((0,), (0,)), (tm, tn), (), [],
                   out_shape, out_specs, epilogue)[0]


def _tile_spec(tm, tn):
    return pl.BlockSpec((tm, tn), lambda i, j, kk: (i, j))


def _row_spec(tn):
    return pl.BlockSpec((1, tn), lambda i, j, kk: (0, j))


def _ln_stats(r):
    mu = jnp.mean(r, axis=-1, keepdims=True)
    var = jnp.mean(jnp.square(r - mu), axis=-1, keepdims=True)
    rstd = lax.rsqrt(var + LN_EPS)
    return (r - mu) * rstd, rstd


def _ln_fwd(name, r, g, b, tr=256):
    t, d = r.shape
    tr = _tile(t, tr)

    def body(r_ref, g_ref, b_ref, y_ref, yb_ref, xhat_ref, rstd_ref):
        xhat, rstd = _ln_stats(r_ref[...])
        y = xhat * g_ref[...] + b_ref[...]
        y_ref[...] = y
        yb_ref[...] = y.astype(BF16)
        xhat_ref[...] = xhat
        rstd_ref[...] = rstd

    row = pl.BlockSpec((tr, d), lambda i: (i, 0))
    vec = pl.BlockSpec((1, d), lambda i: (0, 0))
    return pl.pallas_call(
        body, name=name, grid=(t // tr,), in_specs=[row, vec, vec],
        out_specs=[row, row, row, pl.BlockSpec((tr, 1), lambda i: (i, 0))],
        out_shape=[jax.ShapeDtypeStruct((t, d), F32), jax.ShapeDtypeStruct((t, d), BF16),
                   jax.ShapeDtypeStruct((t, d), F32), jax.ShapeDtypeStruct((t, 1), F32)],
        compiler_params=_params(("parallel",)),
    )(r, g, b)


def _ln_bwd_rows(dy, xhat, rstd, g):
    dxhat = dy * g
    m1 = jnp.mean(dxhat, axis=-1, keepdims=True)
    m2 = jnp.mean(dxhat * xhat, axis=-1, keepdims=True)
    return rstd * (dxhat - m1 - xhat * m2)


def _ln_bwd(name, dy, xhat, rstd, g, tr=256):
    t, d = dy.shape
    tr = _tile(t, tr)

    def body(dy_ref, xhat_ref, rstd_ref, g_ref, dr_ref, drb_ref, dg_ref, db_ref):
        @pl.when(pl.program_id(0) == 0)
        def _():
            dg_ref[...] = jnp.zeros_like(dg_ref)
            db_ref[...] = jnp.zeros_like(db_ref)

        dy_t, xhat_t = dy_ref[...], xhat_ref[...]
        dr = _ln_bwd_rows(dy_t, xhat_t, rstd_ref[...], g_ref[...])
        dr_ref[...] = dr
        drb_ref[...] = dr.astype(BF16)
        dg_ref[...] += jnp.sum(dy_t * xhat_t, axis=0, keepdims=True)
        db_ref[...] += jnp.sum(dy_t, axis=0, keepdims=True)

    row = pl.BlockSpec((tr, d), lambda i: (i, 0))
    vec = pl.BlockSpec((1, d), lambda i: (0, 0))
    return pl.pallas_call(
        body, name=name, grid=(t // tr,), in_specs=[row, row, pl.BlockSpec((tr, 1), lambda i: (i, 0)), vec],
        out_specs=[row, row, vec, vec],
        out_shape=[jax.ShapeDtypeStruct((t, d), F32), jax.ShapeDtypeStruct((t, d), BF16),
                   jax.ShapeDtypeStruct((1, d), F32), jax.ShapeDtypeStruct((1, d), F32)],
        compiler_params=_params(("arbitrary",)),
    )(dy, xhat, rstd, g)


def _ln2_loss_bwd(r2, target, g, b, tr=256):
    t, d = r2.shape
    tr = _tile(t, tr)

    def body(r_ref, t_ref, g_ref, b_ref, dr_ref, drb_ref, loss_ref, dg_ref, db_ref, dsum_ref):
        @pl.when(pl.program_id(0) == 0)
        def _():
            loss_ref[...] = jnp.zeros_like(loss_ref)
            dg_ref[...] = jnp.zeros_like(dg_ref)
            db_ref[...] = jnp.zeros_like(db_ref)
            dsum_ref[...] = jnp.zeros_like(dsum_ref)

        xhat, rstd = _ln_stats(r_ref[...])
        g_t = g_ref[...]
        err = xhat * g_t + b_ref[...] - t_ref[...]
        loss_ref[...] += 0.5 * jnp.sum(jnp.mean(jnp.square(err), axis=-1, keepdims=True), axis=0, keepdims=True)
        dy = err * (1.0 / d)
        dr = _ln_bwd_rows(dy, xhat, rstd, g_t)
        dr_ref[...] = dr
        drb_ref[...] = dr.astype(BF16)
        dg_ref[...] += jnp.sum(dy * xhat, axis=0, keepdims=True)
        db_ref[...] += jnp.sum(dy, axis=0, keepdims=True)
        dsum_ref[...] += jnp.sum(dr, axis=0, keepdims=True)

    row = pl.BlockSpec((tr, d), lambda i: (i, 0))
    vec = pl.BlockSpec((1, d), lambda i: (0, 0))
    return pl.pallas_call(
        body, name="ln2_loss_bwd", grid=(t // tr,), in_specs=[row, row, vec, vec],
        out_specs=[row, row, pl.BlockSpec((8, 128), lambda i: (0, 0)), vec, vec, vec],
        out_shape=[jax.ShapeDtypeStruct((t, d), F32), jax.ShapeDtypeStruct((t, d), BF16),
                   jax.ShapeDtypeStruct((8, 128), F32), jax.ShapeDtypeStruct((1, d), F32),
                   jax.ShapeDtypeStruct((1, d), F32), jax.ShapeDtypeStruct((1, d), F32)],
        compiler_params=_params(("arbitrary",)),
    )(r2, target, g, b)


POOL_ROWS = 512


def _pool_mean_minus_token(u_ref, r0, rows, grp, first):
    width = u_ref.shape[1]
    body = u_ref[pl.ds(r0, rows), :]
    halo = u_ref[pl.ds(pl.multiple_of(jnp.maximum(r0 - POOL_HALO, 0), POOL_HALO), POOL_HALO), :]
    halo = jnp.where(first, 0.0, halo)
    full = jnp.concatenate([halo, body], axis=0)
    s = full
    for step in range(len(POOL_WINDOWS)):
        shifted = pltpu.roll(s, 1 << step, axis=0)
        s = s + jnp.where(grp >= step, shifted, 0.0)
    s = s[POOL_HALO:, :]
    tpos = r0 + lax.broadcasted_iota(jnp.int32, (rows, width), 0)
    count = jnp.minimum(tpos + 1, 2 << grp).astype(F32)
    return s / count - body, count


def _pool_fwd(u, w_pool, pool_scale, t, pw):
    gw = pw // len(POOL_WINDOWS)
    rows = _tile(t, POOL_ROWS)

    def body(u_ref, w_ref, s_ref, o_ref):
        grp = pl.program_id(0)

        def chunk(ci, carry):
            r0 = pl.multiple_of(ci * rows, rows)
            y, _ = _pool_mean_minus_token(u_ref, r0, rows, grp, ci == 0)
            yw = jnp.dot(y.astype(BF16), w_ref[...], preferred_element_type=F32)
            o_ref[pl.ds(r0, rows), :] = (yw * s_ref[...]).astype(BF16)
            return carry

        lax.fori_loop(0, t // rows, chunk, 0)

    return pl.pallas_call(
        body, name="pool_fwd", grid=(len(POOL_WINDOWS),),
        in_specs=[pl.BlockSpec((t, gw), lambda g: (0, g)), pl.BlockSpec((None, gw, gw), lambda g: (g, 0, 0)),
                  pl.BlockSpec((None, 1, gw), lambda g: (g, 0, 0))],
        out_specs=pl.BlockSpec((t, gw), lambda g: (0, g)),
        out_shape=jax.ShapeDtypeStruct((t, pw), BF16),
        compiler_params=_params(("parallel",)),
    )(u, w_pool, pool_scale)


def _pool_bwd(u, dmix, w_pool, pool_scale, t, pw):
    n_grp = len(POOL_WINDOWS)
    gw = pw // n_grp
    rows = _tile(t, POOL_ROWS)

    def body(u_ref, dm_ref, w_ref, s_ref, du_ref, dw_ref, ds_ref, e_ref):
        grp = pl.program_id(0)
        dw_ref[...] = jnp.zeros_like(dw_ref)
        ds_ref[...] = jnp.zeros_like(ds_ref)
        e_ref[pl.ds(t, POOL_HALO), :] = jnp.zeros((POOL_HALO, gw), F32)

        def chunk(ci, carry):
            r0 = pl.multiple_of(ci * rows, rows)
            y, count = _pool_mean_minus_token(u_ref, r0, rows, grp, ci == 0)
            yb = y.astype(BF16)
            yw = jnp.dot(yb, w_ref[...], preferred_element_type=F32)
            dy2 = dm_ref[pl.ds(r0, rows), :]
            ds_ref[...] += jnp.sum(dy2 * yw, axis=0, keepdims=True)
            dyw = (dy2 * s_ref[...]).astype(BF16)
            dw_ref[...] += lax.dot_general(yb, dyw, (((0,), (0,)), ((), ())), preferred_element_type=F32)
            dy = lax.dot_general(dyw, w_ref[...], (((1,), (1,)), ((), ())), preferred_element_type=F32)
            e_ref[pl.ds(r0, rows), :] = dy / count
            return carry

        lax.fori_loop(0, t // rows, chunk, 0)

        def chunk2(ci, carry):
            r0 = pl.multiple_of(ci * rows, rows)
            full = e_ref[pl.ds(r0, rows + POOL_HALO), :]
            s = full
            for step in range(n_grp):
                shifted = pltpu.roll(s, rows + POOL_HALO - (1 << step), axis=0)
                s = s + jnp.where(grp >= step, shifted, 0.0)
            e = full[:rows, :]
            tpos = r0 + lax.broadcasted_iota(jnp.int32, (rows, gw), 0)
            count = jnp.minimum(tpos + 1, 2 << grp).astype(F32)
            du_ref[pl.ds(r0, rows), :] = (s[:rows, :] - e * count).astype(BF16)
            return carry

        lax.fori_loop(0, t // rows, chunk2, 0)

    return pl.pallas_call(
        body, name="pool_bwd", grid=(n_grp,),
        in_specs=[pl.BlockSpec((t, gw), lambda g: (0, g)), pl.BlockSpec((t, gw), lambda g: (0, g)),
                  pl.BlockSpec((None, gw, gw), lambda g: (g, 0, 0)),
                  pl.BlockSpec((None, 1, gw), lambda g: (g, 0, 0))],
        out_specs=[pl.BlockSpec((t, gw), lambda g: (0, g)), pl.BlockSpec((None, gw, gw), lambda g: (g, 0, 0)),
                   pl.BlockSpec((None, 1, gw), lambda g: (g, 0, 0))],
        out_shape=[jax.ShapeDtypeStruct((t, pw), BF16), jax.ShapeDtypeStruct((n_grp, gw, gw), F32),
                   jax.ShapeDtypeStruct((n_grp, 1, gw), F32)],
        scratch_shapes=[pltpu.VMEM((t + POOL_HALO, gw), F32)],
        compiler_params=_params(("parallel",)),
    )(u, dmix, w_pool, pool_scale)


def _sb_scores(q, k_blk, scale, mask):
    z = lax.dot_general(q, k_blk, (((1,), (1,)), ((), ())), preferred_element_type=F32) * scale
    tneg = jnp.exp(-jnp.abs(z))
    log_not = -(jnp.maximum(z, 0.0) + jnp.log1p(tneg))
    return z, jnp.where(mask, log_not, 0.0), tneg


EXP_IS_ZERO_BELOW = -104.0


def _weights_alive(after):
    return (jnp.max(after) >= EXP_IS_ZERO_BELOW).astype(jnp.int32)


def _split_dot(v, tri):
    hi = v.astype(BF16)
    lo = (v - hi.astype(F32)).astype(BF16)
    return (jnp.dot(hi, tri, preferred_element_type=F32) + jnp.dot(lo, tri, preferred_element_type=F32))


def _attn_fwd(u, t, nh):
    scale = float(1.0 / (HEAD_DIM ** 0.5))

    def body(q_ref, k_ref, v_ref, o_ref):
        i = pl.program_id(1)
        q = q_ref[...].astype(BF16)
        row = lax.broadcasted_iota(jnp.int32, (QB, KB), 0)
        col = lax.broadcasted_iota(jnp.int32, (QB, KB), 1)
        suffix = (row >= col).astype(BF16)

        def more(carry):
            return jnp.logical_and(carry[0] <= i, carry[3] > 0)

        def step(carry):
            n, acc, after, _ = carry
            ks = pl.multiple_of((i - n) * KB, KB)
            k_blk = k_ref[pl.ds(ks, KB), :].astype(BF16)
            v_blk = v_ref[pl.ds(ks, KB), :].astype(BF16)
            mask = jnp.logical_or(n > 0, col < row)
            z, log_not, _ = _sb_scores(q, k_blk, scale, mask)
            within = _split_dot(log_not, suffix)
            a = jnp.where(mask, jnp.exp(z + within + after), 0.0)
            acc = acc + jnp.dot(a.astype(BF16), v_blk, preferred_element_type=F32)
            after = after + jnp.sum(log_not, axis=1, keepdims=True)
            return n + 1, acc, after, _weights_alive(after)

        _, acc, _, _ = lax.while_loop(more, step, (jnp.int32(0), jnp.zeros((QB, HEAD_DIM), F32),
                                                   jnp.zeros((QB, 1), F32), jnp.int32(1)))
        o_ref[...] = acc.astype(BF16)

    return pl.pallas_call(
        body, name="attn_fwd", grid=(nh, t // QB),
        in_specs=[pl.BlockSpec((QB, HEAD_DIM), lambda h, i: (i, nh + h)),
                  pl.BlockSpec((t, HEAD_DIM), lambda h, i: (0, 2 * nh + h)),
                  pl.BlockSpec((t, HEAD_DIM), lambda h, i: (0, 3 * nh + h))],
        out_specs=pl.BlockSpec((QB, HEAD_DIM), lambda h, i: (i, h)),
        out_shape=jax.ShapeDtypeStruct((t, nh * HEAD_DIM), BF16),
        compiler_params=_params(("parallel", "arbitrary")),
    )(u, u, u)


def _attn_bwd(u, dmix, t, nh):
    scale = float(1.0 / (HEAD_DIM ** 0.5))
    width = nh * HEAD_DIM

    def body(q_ref, k_ref, v_ref, do_ref, dq_ref, dk_ref, dv_ref, g_ref, z_ref):
        i = pl.program_id(1)

        @pl.when(i == 0)
        def _():
            dk_ref[...] = jnp.zeros_like(dk_ref)
            dv_ref[...] = jnp.zeros_like(dv_ref)

        q = q_ref[...].astype(BF16)
        do = do_ref[...].astype(BF16)
        row = lax.broadcasted_iota(jnp.int32, (QB, KB), 0)
        col = lax.broadcasted_iota(jnp.int32, (QB, KB), 1)
        suffix = (row >= col).astype(BF16)
        prefix = (row <= col).astype(BF16)

        def more(carry):
            return jnp.logical_and(carry[0] <= i, carry[2] > 0)

        def down(carry):
            n, after, _ = carry
            ks = pl.multiple_of((i - n) * KB, KB)
            k_blk = k_ref[pl.ds(ks, KB), :].astype(BF16)
            v_blk = v_ref[pl.ds(ks, KB), :].astype(BF16)
            mask = jnp.logical_or(n > 0, col < row)
            z, log_not, _ = _sb_scores(q, k_blk, scale, mask)
            within = _split_dot(log_not, suffix)
            a = jnp.where(mask, jnp.exp(z + within + after), 0.0)
            da = lax.dot_general(do, v_blk, (((1,), (1,)), ((), ())), preferred_element_type=F32)
            g_ref[:, pl.ds(ks, KB)] = a * da
            z_ref[:, pl.ds(ks, KB)] = z
            dv_ref[pl.ds(ks, KB), :] += lax.dot_general(a.astype(BF16), do, (((0,), (0,)), ((), ())),
                                                        preferred_element_type=F32)
            after = after + jnp.sum(log_not, axis=1, keepdims=True)
            return n + 1, after, _weights_alive(after)

        visited, _, _ = lax.while_loop(more, down, (jnp.int32(0), jnp.zeros((QB, 1), F32), jnp.int32(1)))

        def up(kb, carry):
            dq, before = carry
            ks = pl.multiple_of(kb * KB, KB)
            k_blk = k_ref[pl.ds(ks, KB), :].astype(BF16)
            g = g_ref[:, pl.ds(ks, KB)]
            z = z_ref[:, pl.ds(ks, KB)]
            mask = jnp.logical_or(kb < i, col < row)
            g_upto = _split_dot(g, prefix) + before
            dz = jnp.where(mask, g - jax.nn.sigmoid(z) * g_upto, 0.0)
            dzs = (dz * scale).astype(BF16)
            dq = dq + jnp.dot(dzs, k_blk, preferred_element_type=F32)
            dk_ref[pl.ds(ks, KB), :] += lax.dot_general(dzs, q, (((0,), (0,)), ((), ())),
                                                        preferred_element_type=F32)
            return dq, before + jnp.sum(g, axis=1, keepdims=True)

        dq, _ = lax.fori_loop(i + 1 - visited, i + 1, up,
                              (jnp.zeros((QB, HEAD_DIM), F32), jnp.zeros((QB, 1), F32)))
        dq_ref[...] = dq.astype(BF16)

    strip = lambda off: pl.BlockSpec((t, HEAD_DIM), lambda h, i: (0, off + h))
    return pl.pallas_call(
        body, name="attn_bwd", grid=(nh, t // QB),
        in_specs=[pl.BlockSpec((QB, HEAD_DIM), lambda h, i: (i, nh + h)), strip(2 * nh), strip(3 * nh),
                  pl.BlockSpec((QB, HEAD_DIM), lambda h, i: (i, nh + h))],
        out_specs=[pl.BlockSpec((QB, HEAD_DIM), lambda h, i: (i, h)), strip(0), strip(0)],
        out_shape=[jax.ShapeDtypeStruct((t, width), BF16), jax.ShapeDtypeStruct((t, width), F32),
                   jax.ShapeDtypeStruct((t, width), F32)],
        scratch_shapes=[pltpu.VMEM((QB, t), F32), pltpu.VMEM((QB, t), F32)],
        compiler_params=_params(("parallel", "arbitrary")),
    )(u, u, u, dmix)


def _row_tile(rows, cols, pref_bytes=2 * 1024 * 1024):
    tr = max(8, pref_bytes // (4 * cols))
    while rows % tr:
        tr //= 2
    return max(tr, 1)


def _pair_sum(name, g, s, c_idx):
    _, _, r2, cols = g.shape
    tr = _row_tile(r2, cols)

    def body(c_ref, g_ref, s_ref, o_ref):
        o_ref[...] = g_ref[...] + s_ref[...]

    return pl.pallas_call(
        body, name=name,
        grid_spec=pltpu.PrefetchScalarGridSpec(
            num_scalar_prefetch=1, grid=(N_CHIPS, r2 // tr),
            in_specs=[pl.BlockSpec((None, None, tr, cols), lambda p, i, c: (p, c[0], i, 0)),
                      pl.BlockSpec((None, tr, cols), lambda p, i, c: (p, i, 0))],
            out_specs=pl.BlockSpec((None, tr, cols), lambda p, i, c: (p, i, 0))),
        out_shape=jax.ShapeDtypeStruct((N_CHIPS, r2, cols), F32),
        compiler_params=_params(("parallel", "parallel")),
    )(c_idx, g, s)


def _chip_sum(name, p, r, place):
    _, r2, cols = p.shape
    tr = _row_tile(r2, cols)

    def body(place_ref, p_ref, r_ref, o_ref):
        o_ref[...] = ((p_ref[...] + r_ref[0]) + r_ref[1]) + r_ref[2]

    return pl.pallas_call(
        body, name=name,
        grid_spec=pltpu.PrefetchScalarGridSpec(
            num_scalar_prefetch=1, grid=(r2 // tr,),
            in_specs=[pl.BlockSpec((None, tr, cols), lambda i, s: (s[0], i, 0)),
                      pl.BlockSpec((3, tr, cols), lambda i, s: (0, i, 0))],
            out_specs=pl.BlockSpec((None, tr, cols), lambda i, s: (s[1], i, 0))),
        out_shape=jax.ShapeDtypeStruct((2, r2, cols), F32),
        compiler_params=_params(("parallel",)),
    )(place, p, r)


def _cast_into_slot(name, w, place):
    rows, cols = w.shape
    r2 = rows // 2
    tr = _row_tile(r2, cols)
    nb = r2 // tr

    def body(place_ref, w_ref, o_ref):
        o_ref[...] = w_ref[...].astype(BF16)

    return pl.pallas_call(
        body, name=name,
        grid_spec=pltpu.PrefetchScalarGridSpec(
            num_scalar_prefetch=1, grid=(2, nb),
            in_specs=[pl.BlockSpec((tr, cols), lambda h, i, s: (h * nb + i, 0))],
            out_specs=pl.BlockSpec((None, None, tr, cols), lambda h, i, s: (s[0], h, i, 0))),
        out_shape=jax.ShapeDtypeStruct((N_CHIPS, 2, r2, cols), BF16),
        compiler_params=_params(("parallel", "parallel")),
    )(place, w)


def _colsum(name, a):
    def body(a_ref, o_ref):
        o_ref[...] = jnp.sum(a_ref[...], axis=0, keepdims=True)

    whole = lambda shape: pl.BlockSpec(shape, lambda i: (0, 0))
    return pl.pallas_call(
        body, name=name, grid=(1,), in_specs=[whole(a.shape)], out_specs=whole((1, a.shape[1])),
        out_shape=jax.ShapeDtypeStruct((1, a.shape[1]), F32), compiler_params=_params(("arbitrary",)),
    )(a)


def _adamw(name, w, g, m, v):
    rows, cols = w.shape
    tr = _row_tile(rows, cols, 1024 * 1024)

    def body(w_ref, g_ref, m_ref, v_ref, d_ref, nm_ref, nv_ref):
        g_t = g_ref[...]
        m_t = ADAM_B1 * m_ref[...] + (1.0 - ADAM_B1) * g_t
        v_t = ADAM_B2 * v_ref[...] + (1.0 - ADAM_B2) * jnp.square(g_t)
        m_hat = m_t / (1.0 - ADAM_B1 ** ADAM_STEP)
        v_hat = v_t / (1.0 - ADAM_B2 ** ADAM_STEP)
        d_ref[...] = -ADAM_LR * (m_hat / (jnp.sqrt(v_hat) + ADAM_EPS) + ADAM_WD * w_ref[...])
        nm_ref[...] = m_t
        nv_ref[...] = v_t

    spec = pl.BlockSpec((tr, cols), lambda i: (i, 0))
    shape = jax.ShapeDtypeStruct((rows, cols), F32)
    return pl.pallas_call(
        body, name=name, grid=(rows // tr,), in_specs=[spec] * 4, out_specs=[spec] * 3, out_shape=[shape] * 3,
        compiler_params=_params(("parallel",)),
    )(w, g, m, v)


ANY = pl.BlockSpec(memory_space=pl.ANY)


def _gather_weights(slots, wslots):
    ns, nw = len(slots), len(wslots)

    def body(*refs):
        outs, wouts = refs[ns + nw:2 * ns + nw], refs[2 * ns + nw:2 * (ns + nw)]
        send_sems, recv_sems, fsend_sems, frecv_sems, wsend_sems, wrecv_sems = refs[2 * (ns + nw):]
        x, y, c = _pos()
        me = 2 * x + y
        chips = _other_chips(x, y)
        sibling = (x, y, 1 - c)

        def half_copy(a, j, sems_s, sems_r, chip_idx, half, to):
            piece = outs[a].at[chip_idx, half]
            return pltpu.make_async_remote_copy(src_ref=piece, dst_ref=piece, send_sem=sems_s.at[3 * a + j],
                                                recv_sem=sems_r.at[3 * a + j], device_id=to, device_id_type=MESH)

        def whole_copy(a, j, chip_idx, to):
            piece = wouts[a].at[chip_idx]
            return pltpu.make_async_remote_copy(src_ref=piece, dst_ref=piece, send_sem=wsend_sems.at[3 * a + j],
                                                recv_sem=wrecv_sems.at[3 * a + j], device_id=to, device_id_type=MESH)

        sends = []
        for a in range(ns):
            for j, chip in enumerate(chips):
                sends.append(half_copy(a, j, send_sems, recv_sems, me, c, (*chip, c)))
        for a in range(nw):
            for j, chip in enumerate(chips):
                sends.append(whole_copy(a, j, me, (*chip, c)))
        for cp in sends:
            cp.start()
        for a in range(ns):
            for j, chip in enumerate(chips):
                idx = 2 * chip[0] + chip[1]
                half_copy(a, j, send_sems, recv_sems, idx, c, (x, y, c)).wait_recv()
                fw = half_copy(a, j, fsend_sems, frecv_sems, idx, c, sibling)
                fw.start()
                sends.append(fw)
        for a in range(ns):
            for j, chip in enumerate(chips):
                half_copy(a, j, fsend_sems, frecv_sems, 2 * chip[0] + chip[1], 1 - c, (x, y, c)).wait_recv()
        for a in range(nw):
            for j, chip in enumerate(chips):
                whole_copy(a, j, 2 * chip[0] + chip[1], (x, y, c)).wait_recv()
        for cp in sends:
            cp.wait_send()

    n = ns + nw
    outs = pl.pallas_call(
        body, name="gather_weights", in_specs=[ANY] * n, out_specs=[ANY] * n,
        out_shape=[jax.ShapeDtypeStruct(s.shape, s.dtype) for s in (*slots, *wslots)],
        input_output_aliases={i: i for i in range(n)},
        scratch_shapes=[pltpu.SemaphoreType.DMA((3 * ns,)), pltpu.SemaphoreType.DMA((3 * ns,)),
                        pltpu.SemaphoreType.DMA((3 * ns,)), pltpu.SemaphoreType.DMA((3 * ns,)),
                        pltpu.SemaphoreType.DMA((3 * nw,)), pltpu.SemaphoreType.DMA((3 * nw,))],
    )(*slots, *wslots)
    gathered = [o.reshape(N_CHIPS, 2 * o.shape[2], o.shape[3]) for o in outs[:ns]]
    return gathered, list(outs[ns:])


def _sibling_swap_halves(gs):
    n = len(gs)

    def body(*refs):
        ins, outs = refs[:n], refs[n:2 * n]
        send_sems, recv_sems = refs[2 * n:]
        x, y, c = _pos()
        cps = []
        for a in range(n):
            cp = pltpu.make_async_remote_copy(
                src_ref=ins[a].at[:, 1 - c], dst_ref=outs[a], send_sem=send_sems.at[a], recv_sem=recv_sems.at[a],
                device_id=(x, y, 1 - c), device_id_type=MESH)
            cp.start()
            cps.append(cp)
        for cp in cps:
            cp.wait()

    return pl.pallas_call(
        body, name="rs_sibling_swap", in_specs=[ANY] * n, out_specs=[ANY] * n,
        out_shape=[jax.ShapeDtypeStruct((N_CHIPS, g.shape[2], g.shape[3]), g.dtype) for g in gs],
        scratch_shapes=[pltpu.SemaphoreType.DMA((n,)), pltpu.SemaphoreType.DMA((n,))],
    )(*gs)


def _chip_scatter(ps):
    n = len(ps)

    def body(*refs):
        ins, outs = refs[:n], refs[n:2 * n]
        send_sems, recv_sems = refs[2 * n:]
        x, y, c = _pos()
        cps = []
        for a in range(n):
            for j, chip in enumerate(_other_chips(x, y)):
                cp = pltpu.make_async_remote_copy(
                    src_ref=ins[a].at[2 * chip[0] + chip[1]], dst_ref=outs[a].at[j],
                    send_sem=send_sems.at[3 * a + j], recv_sem=recv_sems.at[3 * a + j],
                    device_id=(*chip, c), device_id_type=MESH)
                cp.start()
                cps.append(cp)
        for cp in cps:
            cp.wait()

    return pl.pallas_call(
        body, name="rs_chip_scatter", in_specs=[ANY] * n, out_specs=[ANY] * n,
        out_shape=[jax.ShapeDtypeStruct((3, p.shape[1], p.shape[2]), p.dtype) for p in ps],
        scratch_shapes=[pltpu.SemaphoreType.DMA((3 * n,)), pltpu.SemaphoreType.DMA((3 * n,))],
    )(*ps)


def _sibling_assemble(qs):
    n = len(qs)

    def body(*refs):
        outs = refs[n:2 * n]
        send_sems, recv_sems = refs[2 * n:]
        x, y, c = _pos()

        def half_copy(a, half, to):
            piece = outs[a].at[half]
            return pltpu.make_async_remote_copy(src_ref=piece, dst_ref=piece, send_sem=send_sems.at[a],
                                                recv_sem=recv_sems.at[a], device_id=to, device_id_type=MESH)

        sends = [half_copy(a, c, (x, y, 1 - c)) for a in range(n)]
        for cp in sends:
            cp.start()
        for a in range(n):
            half_copy(a, 1 - c, (x, y, c)).wait_recv()
        for cp in sends:
            cp.wait_send()

    return pl.pallas_call(
        body, name="rs_sibling_assemble", in_specs=[ANY] * n, out_specs=[ANY] * n,
        out_shape=[jax.ShapeDtypeStruct(q.shape, q.dtype) for q in qs],
        input_output_aliases={i: i for i in range(n)},
        scratch_shapes=[pltpu.SemaphoreType.DMA((n,)), pltpu.SemaphoreType.DMA((n,))],
    )(*qs)


def _all_reduce_small(packed):
    rows, cols = packed.shape

    def body(in_ref, out_ref, all_ref, send_sems, recv_sems):
        x, y, c = _pos()
        me = 4 * x + 2 * y + c
        all_ref[me] = in_ref[...]
        cps = []
        for r in range(1, N_DEV):
            bx, by, bc = (r >> 2) & 1, (r >> 1) & 1, r & 1
            peer = (1 - x if bx else x, 1 - y if by else y, 1 - c if bc else c)
            cp = pltpu.make_async_remote_copy(
                src_ref=in_ref, dst_ref=all_ref.at[me], send_sem=send_sems.at[r - 1], recv_sem=recv_sems.at[r - 1],
                device_id=peer, device_id_type=MESH)
            cp.start()
            cps.append(cp)
        for cp in cps:
            cp.wait()
        total = all_ref[0]
        for d in range(1, N_DEV):
            total = total + all_ref[d]
        out_ref[...] = total

    vmem = pl.BlockSpec(memory_space=pltpu.VMEM)
    return pl.pallas_call(
        body, name="all_reduce_small", in_specs=[vmem], out_specs=vmem,
        out_shape=jax.ShapeDtypeStruct((rows, cols), F32),
        scratch_shapes=[pltpu.VMEM((N_DEV, rows, cols), F32), pltpu.SemaphoreType.DMA((N_DEV - 1,)),
                        pltpu.SemaphoreType.DMA((N_DEV - 1,))],
        compiler_params=pltpu.CompilerParams(vmem_limit_bytes=VMEM_LIMIT),
    )(packed)


def kernel(x, ln_in_g, ln_in_b, w_in, w_pool, pool_scale, w_out, ln1_g, ln1_b, w_ff1, b_ff1, w_ff2, b_ff2, ln2_g, ln2_b, loss_target, m_ln_in_g, m_ln_in_b, m_w_in, m_w_pool, m_pool_scale, m_w_out, m_ln1_g, m_ln1_b, m_w_ff1, m_b_ff1, m_w_ff2, m_b_ff2, m_ln2_g, m_ln2_b, v_ln_in_g, v_ln_in_b, v_w_in, v_w_pool, v_pool_scale, v_w_out, v_ln1_g, v_ln1_b, v_w_ff1, v_b_ff1, v_w_ff2, v_b_ff2, v_ln2_g, v_ln2_b):
    t, d = x.shape[1], x.shape[2]
    pw = d // 2
    n_grp = len(POOL_WINDOWS)
    gw = pw // n_grp
    gwc = gw // N_CHIPS
    nh = pw // HEAD_DIM
    ff = w_ff1.shape[2] * N_CHIPS
    assert w_in.shape[0] == 1 and w_in.shape[2] * N_CHIPS == 2 * d and gwc <= 128

    x_idx, y_idx, c_idx = _pos()
    chip_arr = jnp.reshape(2 * x_idx + y_idx, (1,)).astype(jnp.int32)
    c_arr = jnp.reshape(c_idx, (1,)).astype(jnp.int32)
    place = jnp.concatenate([chip_arr, c_arr])

    xs = x.reshape(t, d)
    target = loss_target.reshape(t, d)
    row = lambda vec: vec.reshape(1, -1)

    scale_tile = jnp.zeros((1, 8, 128), F32).at[0, :n_grp, :gwc].set(pool_scale[0])
    scale_slots = lax.dynamic_update_slice(jnp.zeros((N_CHIPS, 8, 128), F32), scale_tile, (chip_arr[0], 0, 0))
    shards = dict(w_in=w_in[0], w_out=w_out[0], w_ff1=w_ff1[0], w_ff2=w_ff2[0], w_pool=w_pool[0].reshape(gw, gw))
    (win_g, wout_g, wff1_g, wff2_g, wpool_g), (scale_g,) = _gather_weights(
        [_cast_into_slot("cast_" + nm, w, place) for nm, w in shards.items()], [scale_slots])
    wout_full = wout_g.reshape(d, d)
    wff2_full = wff2_g.reshape(ff, d)
    wpool_full = wpool_g.reshape(N_CHIPS, n_grp, gwc, gw).transpose(1, 0, 2, 3).reshape(n_grp, gw, gw)
    scale_full = scale_g[:, :n_grp, :gwc].transpose(1, 0, 2).reshape(n_grp, 1, gw)

    h0, h0b, xhat0, rstd0 = _ln_fwd("ln_in_fwd", xs, row(ln_in_g), row(ln_in_b))

    def store_f32(acc, extra_refs, out_refs):
        out_refs[0][...] = acc

    u = _mm_nn("in_proj", h0b, win_g, (), lambda tm, tn: [], [jax.ShapeDtypeStruct((t, 2 * d), F32)],
               lambda tm, tn: [_tile_spec(tm, tn)], store_f32, b_chips=True)[0]
    y_pool = _pool_fwd(u, wpool_full, scale_full, t, pw)
    y_sb = _attn_fwd(u, t, nh)
    mix_in = jnp.concatenate([y_pool, y_sb], axis=1)

    def residual(acc, extra_refs, out_refs):
        out_refs[0][...] = ALPHA * extra_refs[0][...] + acc

    r1 = _mm_nn("out_proj", mix_in, wout_full, (h0,), lambda tm, tn: [_tile_spec(tm, tn)],
                [jax.ShapeDtypeStruct((t, d), F32)], lambda tm, tn: [_tile_spec(tm, tn)], residual)[0]
    h1, h1b, xhat1, rstd1 = _ln_fwd("ln1_fwd", r1, ln1_g, ln1_b)

    def relu_sq(acc, extra_refs, out_refs):
        p = jnp.maximum(acc + extra_refs[0][...], 0.0)
        out_refs[0][...] = p
        out_refs[1][...] = jnp.square(p).astype(BF16)

    relu_z, act_b = _mm_nn("ff1", h1b, wff1_g, (b_ff1,), lambda tm, tn: [_row_spec(tn)],
                           [jax.ShapeDtypeStruct((t, ff), F32), jax.ShapeDtypeStruct((t, ff), BF16)],
                           lambda tm, tn: [_tile_spec(tm, tn)] * 2, relu_sq, b_chips=True)

    def residual_bias(acc, extra_refs, out_refs):
        out_refs[0][...] = ALPHA * extra_refs[0][...] + (acc + extra_refs[1][...])

    r2 = _mm_nn("ff2", act_b, wff2_full, (h1, b_ff2), lambda tm, tn: [_tile_spec(tm, tn), _row_spec(tn)],
                [jax.ShapeDtypeStruct((t, d), F32)], lambda tm, tn: [_tile_spec(tm, tn)], residual_bias)[0]

    dr2, dr2b, loss_tile, g_ln2_g, g_ln2_b, g_b_ff2 = _ln2_loss_bwd(r2, target, ln2_g, ln2_b)
    loss = lax.psum(loss_tile[0, 0], ("x", "y", "c"))

    g_w_ff2 = _mm_tn("grad_w_ff2", act_b, dr2b)

    def relu_sq_bwd(acc, extra_refs, out_refs):
        dz = acc * (2.0 * extra_refs[0][...])
        out_refs[0][...] = dz.astype(BF16)
        rows = lax.broadcasted_iota(jnp.int32, out_refs[1].shape, 0)
        out_refs[1][...] = jnp.where(rows == 0, jnp.sum(dz, axis=0, keepdims=True), 0.0)

    tm_ff = _tile(t, 1024)
    dz1b, g_b_ff1_parts = _mm_nt(
        "ff2_bwd", dr2b, wff2_full, (relu_z,), lambda tm, tn: [_tile_spec(tm, tn)],
        [jax.ShapeDtypeStruct((t, ff), BF16), jax.ShapeDtypeStruct((8 * (t // tm_ff), ff), F32)],
        lambda tm, tn: [_tile_spec(tm, tn), pl.BlockSpec((8, tn), lambda i, j, kk: (i, j))], relu_sq_bwd)
    g_w_ff1 = _mm_tn("grad_w_ff1", h1b, dz1b, out_chips=True)

    def plus_alpha(acc, extra_refs, out_refs):
        out_refs[0][...] = ALPHA * extra_refs[0][...] + acc

    dh1 = _mm_nt("ff1_bwd", dz1b, wff1_g, (dr2,), lambda tm, tn: [_tile_spec(tm, tn)],
                 [jax.ShapeDtypeStruct((t, d), F32)], lambda tm, tn: [_tile_spec(tm, tn)], plus_alpha,
                 b_chips=True)[0]
    dr1, dr1b, g_ln1_g, g_ln1_b = _ln_bwd("ln1_bwd", dh1, xhat1, rstd1, ln1_g)

    g_w_out = _mm_tn("grad_w_out", mix_in, dr1b)
    dmix = _mm_nt("out_proj_bwd", dr1b, wout_full, (), lambda tm, tn: [], [jax.ShapeDtypeStruct((t, d), F32)],
                  lambda tm, tn: [_tile_spec(tm, tn)], store_f32)[0]
    du_pool, g_w_pool_full, g_scale_full = _pool_bwd(u, dmix, wpool_full, scale_full, t, pw)
    dq, dk, dv = _attn_bwd(u, dmix, t, nh)
    du = jnp.concatenate([du_pool, dq, dk.astype(BF16), dv.astype(BF16)], axis=1)
    g_w_in = _mm_tn("grad_w_in", h0b, du, out_chips=True)
    dh0 = _mm_nt("in_proj_bwd", du, win_g, (dr1,), lambda tm, tn: [_tile_spec(tm, tn)],
                 [jax.ShapeDtypeStruct((t, d), F32)], lambda tm, tn: [_tile_spec(tm, tn)], plus_alpha,
                 b_chips=True)[0]
    dx, _, g_ln_in_g, g_ln_in_b = _ln_bwd("ln_in_bwd", dh0, xhat0, rstd0, row(ln_in_g))

    g_w_pool_chips = g_w_pool_full.reshape(n_grp, N_CHIPS, gwc, gw).transpose(1, 0, 2, 3).reshape(N_CHIPS, gw, gw)
    by_chip = [g_w_in, g_w_out.reshape(N_CHIPS, d // N_CHIPS, d), g_w_ff1, g_w_ff2.reshape(N_CHIPS, ff // N_CHIPS, d),
               g_w_pool_chips]
    names = ["w_in", "w_out", "w_ff1", "w_ff2", "w_pool"]
    halves = [g.reshape(N_CHIPS, 2, g.shape[1] // 2, g.shape[2]) for g in by_chip]
    swapped = _sibling_swap_halves(halves)
    pair = [_pair_sum("pair_sum_" + nm, g, s, c_arr) for nm, g, s in zip(names, halves, swapped)]
    received = _chip_scatter(pair)
    mine = [_chip_sum("chip_sum_" + nm, p, r, place) for nm, p, r in zip(names, pair, received)]
    reduced = [f.reshape(f.shape[0] * f.shape[1], f.shape[2]) for f in _sibling_assemble(mine)]

    big = {}
    for nm, g, w, m, v in zip(names, reduced, [w_in, w_out, w_ff1, w_ff2, w_pool], [m_w_in, m_w_out, m_w_ff1, m_w_ff2, m_w_pool],
                              [v_w_in, v_w_out, v_w_ff1, v_w_ff2, v_w_pool]):
        flat = lambda arr: arr.reshape(g.shape)
        delta, new_m, new_v = _adamw("adamw_" + nm, flat(w), g, flat(m), flat(v))
        big[nm] = tuple(arr.reshape(w.shape) for arr in (g, delta, new_m, new_v))

    lane = 2048 if d % 2048 == 0 else d
    small_names = ["ln_in_g", "ln_in_b", "ln1_g", "ln1_b", "b_ff1", "b_ff2", "ln2_g", "ln2_b"]
    small_w = dict(ln_in_g=ln_in_g, ln_in_b=ln_in_b, ln1_g=ln1_g, ln1_b=ln1_b, b_ff1=b_ff1, b_ff2=b_ff2, ln2_g=ln2_g,
                   ln2_b=ln2_b)
    small_m = dict(ln_in_g=m_ln_in_g, ln_in_b=m_ln_in_b, ln1_g=m_ln1_g, ln1_b=m_ln1_b, b_ff1=m_b_ff1, b_ff2=m_b_ff2,
                   ln2_g=m_ln2_g, ln2_b=m_ln2_b)
    small_v = dict(ln_in_g=v_ln_in_g, ln_in_b=v_ln_in_b, ln1_g=v_ln1_g, ln1_b=v_ln1_b, b_ff1=v_b_ff1, b_ff2=v_b_ff2,
                   ln2_g=v_ln2_g, ln2_b=v_ln2_b)
    small_g = dict(ln_in_g=g_ln_in_g, ln_in_b=g_ln_in_b, ln1_g=g_ln1_g, ln1_b=g_ln1_b, b_ff2=g_b_ff2, ln2_g=g_ln2_g,
                   ln2_b=g_ln2_b)

    def pack(parts):
        flat = jnp.concatenate([p.reshape(-1) for p in parts])
        n_rows = -(-flat.shape[0] // lane)
        n_rows = -(-n_rows // 8) * 8
        return jnp.pad(flat, (0, n_rows * lane - flat.shape[0])).reshape(n_rows, lane)

    small_g["b_ff1"] = _colsum("b_ff1_colsum", g_b_ff1_parts)
    summed = _all_reduce_small(pack([small_g[nm] for nm in small_names] + [g_scale_full])).reshape(-1)

    g_small, off = {}, 0
    for nm in small_names:
        g_small[nm] = summed[off:off + small_w[nm].size]
        off += small_w[nm].size
    g_scale_all = summed[off:off + n_grp * gw].reshape(n_grp, N_CHIPS, gwc)
    g_scale = lax.dynamic_index_in_dim(g_scale_all, chip_arr[0], axis=1, keepdims=False)

    order = small_names + ["pool_scale"]
    small_w["pool_scale"], small_m["pool_scale"], small_v["pool_scale"] = pool_scale, m_pool_scale, v_pool_scale
    g_small["pool_scale"] = g_scale
    delta_s, new_m_s, new_v_s = _adamw("adamw_small", pack([small_w[nm] for nm in order]),
                                       pack([g_small[nm] for nm in order]), pack([small_m[nm] for nm in order]),
                                       pack([small_v[nm] for nm in order]))
    small = {}
    off = 0
    for nm in order:
        size, shape = small_w[nm].size, small_w[nm].shape
        cut = lambda arr: arr.reshape(-1)[off:off + size].reshape(shape)
        small[nm] = (g_small[nm].reshape(shape), cut(delta_s), cut(new_m_s), cut(new_v_s))
        off += size

    every = {**big, **small}
    weight_order = ["ln_in_g", "ln_in_b", "w_in", "w_pool", "pool_scale", "w_out", "ln1_g", "ln1_b", "w_ff1", "b_ff1",
                    "w_ff2", "b_ff2", "ln2_g", "ln2_b"]
    grads = [every[nm][0] for nm in weight_order]
    deltas = [every[nm][1] for nm in weight_order]
    new_ms = [every[nm][2] for nm in weight_order]
    new_vs = [every[nm][3] for nm in weight_order]
    return (loss, dx.reshape(x.shape), *grads, *deltas, *new_ms, *new_vs)
```

```python
import functools

import jax
import jax.numpy as jnp
from jax import lax
from jax.experimental import pallas as pl
from jax.experimental.pallas import tpu as pltpu

F32 = jnp.float32
BF16 = jnp.bfloat16
MESH = pl.DeviceIdType.MESH

HEAD_DIM = 128
POOL_WINDOWS = (2, 4, 8, 16)
POOL_HALO = 16
LN_EPS = 1e-5
ALPHA = 2.0 ** 0.25
ADAM_LR, ADAM_B1, ADAM_B2, ADAM_EPS, ADAM_WD, ADAM_STEP = 0.001, 0.9, 0.999, 1e-08, 0.01, 10

QB = 128
KB = 128
VMEM_LIMIT = 56 * 1024 * 1024
N_CHIPS = 4
N_DEV = 8


def _params(sem=None):
    return pltpu.CompilerParams(dimension_semantics=sem, vmem_limit_bytes=VMEM_LIMIT)


def _tile(dim, pref):
    return pref if dim % pref == 0 else dim


def _pos():
    return lax.axis_index("x"), lax.axis_index("y"), lax.axis_index("c")


def _other_chips(x, y):
    return [(1 - x, y), (x, 1 - y), (1 - x, 1 - y)]


def _matmul(name, a, b, grid, a_spec, b_spec, contract, acc_shape, extras, extra_specs, out_shape, out_specs,
            epilogue):
    n_extra, n_out, gk = len(extras), len(out_shape), grid[2]

    def product(a_ref, b_ref):
        return lax.dot_general(a_ref[...], b_ref[...], (contract, ((), ())), preferred_element_type=F32)

    def body_one_step(*refs):
        epilogue(product(refs[0], refs[1]), refs[2:2 + n_extra], refs[2 + n_extra:])

    def body(*refs):
        a_ref, b_ref = refs[0], refs[1]
        extra_refs = refs[2:2 + n_extra]
        out_refs = refs[2 + n_extra:2 + n_extra + n_out]
        acc_ref = refs[-1]
        kk = pl.program_id(2)

        @pl.when(kk == 0)
        def _():
            acc_ref[...] = product(a_ref, b_ref)

        @pl.when(kk > 0)
        def _():
            acc_ref[...] += product(a_ref, b_ref)

        @pl.when(kk == gk - 1)
        def _():
            epilogue(acc_ref[...], extra_refs, out_refs)

    return pl.pallas_call(
        body_one_step if gk == 1 else body, name=name, grid=grid, in_specs=[a_spec, b_spec, *extra_specs],
        out_specs=out_specs, out_shape=out_shape,
        scratch_shapes=[] if gk == 1 else [pltpu.VMEM(acc_shape, F32)],
        compiler_params=_params(("parallel", "arbitrary", "arbitrary")),
    )(a, b, *extras)


def _mm_nn(name, a, b, extras, extra_specs, out_shape, out_specs, epilogue, b_chips=False, tm=1024, tn=1024,
           tk=2048):
    m, k = a.shape
    n = b.shape[1] if not b_chips else b.shape[2] * N_CHIPS
    tm, tk = _tile(m, tm), _tile(k, tk)
    if b_chips:
        tn = _tile(b.shape[2], tn)
        nb = b.shape[2] // tn
        b_spec = pl.BlockSpec((None, tk, tn), lambda i, j, kk: (j // nb, kk, j % nb))
    else:
        tn = _tile(n, tn)
        b_spec = pl.BlockSpec((tk, tn), lambda i, j, kk: (kk, j))
    a_spec = pl.BlockSpec((tm, tk), lambda i, j, kk: (i, kk))
    return _matmul(name, a, b, (m // tm, n // tn, k // tk), a_spec, b_spec, ((1,), (0,)), (tm, tn), extras,
                   extra_specs(tm, tn), out_shape, out_specs(tm, tn), epilogue)


def _mm_nt(name, a, b, extras, extra_specs, out_shape, out_specs, epilogue, b_chips=False, tm=1024, tn=1024,
           tk=2048):
    m, k = a.shape
    n = b.shape[0] if not b_chips else b.shape[1]
    tm, tn = _tile(m, tm), _tile(n, tn)
    if b_chips:
        tk = _tile(b.shape[2], tk)
        nb = b.shape[2] // tk
        b_spec = pl.BlockSpec((None, tn, tk), lambda i, j, kk: (kk // nb, j, kk % nb))
    else:
        tk = _tile(k, tk)
        b_spec = pl.BlockSpec((tn, tk), lambda i, j, kk: (j, kk))
    a_spec = pl.BlockSpec((tm, tk), lambda i, j, kk: (i, kk))
    return _matmul(name, a, b, (m // tm, n // tn, k // tk), a_spec, b_spec, ((1,), (1,)), (tm, tn), extras,
                   extra_specs(tm, tn), out_shape, out_specs(tm, tn), epilogue)


def _mm_tn(name, a, b, out_chips=False, tm=1024, tn=1024, tk=2048):
    k, m = a.shape
    n = b.shape[1]
    tm, tk = _tile(m, tm), _tile(k, tk)
    a_spec = pl.BlockSpec((tk, tm), lambda i, j, kk: (kk, i))
    if out_chips:
        nc = n // N_CHIPS
        tn = _tile(nc, tn)
        nb = nc // tn
        out_shape = [jax.ShapeDtypeStruct((N_CHIPS, m, nc), F32)]
        out_specs = [pl.BlockSpec((None, tm, tn), lambda i, j, kk: (j // nb, i, j % nb))]
    else:
        tn = _tile(n, tn)
        out_shape = [jax.ShapeDtypeStruct((m, n), F32)]
        out_specs = [pl.BlockSpec((tm, tn), lambda i, j, kk: (i, j))]
    b_spec = pl.BlockSpec((tk, tn), lambda i, j, kk: (kk, j))

    def epilogue(acc, extra_refs, out_refs):
        out_refs[0][...] = acc

    return _matmul(name, a, b, (m // tm, n // tn, k // tk), a_spec, b_spec, ((0,), (0,)), (tm, tn), (), [],
                   out_shape, out_specs, epilogue)[0]


def _tile_spec(tm, tn):
    return pl.BlockSpec((tm, tn), lambda i, j, kk: (i, j))


def _row_spec(tn):
    return pl.BlockSpec((1, tn), lambda i, j, kk: (0, j))


def _ln_stats(r):
    mu = jnp.mean(r, axis=-1, keepdims=True)
    var = jnp.mean(jnp.square(r - mu), axis=-1, keepdims=True)
    rstd = lax.rsqrt(var + LN_EPS)
    return (r - mu) * rstd, rstd


def _ln_fwd(name, r, g, b, tr=256):
    t, d = r.shape
    tr = _tile(t, tr)

    def body(r_ref, g_ref, b_ref, y_ref, yb_ref, xhat_ref, rstd_ref):
        xhat, rstd = _ln_stats(r_ref[...])
        y = xhat * g_ref[...] + b_ref[...]
        y_ref[...] = y
        yb_ref[...] = y.astype(BF16)
        xhat_ref[...] = xhat
        rstd_ref[...] = rstd

    row = pl.BlockSpec((tr, d), lambda i: (i, 0))
    vec = pl.BlockSpec((1, d), lambda i: (0, 0))
    return pl.pallas_call(
        body, name=name, grid=(t // tr,), in_specs=[row, vec, vec],
        out_specs=[row, row, row, pl.BlockSpec((tr, 1), lambda i: (i, 0))],
        out_shape=[jax.ShapeDtypeStruct((t, d), F32), jax.ShapeDtypeStruct((t, d), BF16),
                   jax.ShapeDtypeStruct((t, d), F32), jax.ShapeDtypeStruct((t, 1), F32)],
        compiler_params=_params(("parallel",)),
    )(r, g, b)


def _ln_bwd_rows(dy, xhat, rstd, g):
    dxhat = dy * g
    m1 = jnp.mean(dxhat, axis=-1, keepdims=True)
    m2 = jnp.mean(dxhat * xhat, axis=-1, keepdims=True)
    return rstd * (dxhat - m1 - xhat * m2)


def _ln_bwd(name, dy, xhat, rstd, g, tr=256):
    t, d = dy.shape
    tr = _tile(t, tr)

    def body(dy_ref, xhat_ref, rstd_ref, g_ref, dr_ref, drb_ref, dg_ref, db_ref):
        @pl.when(pl.program_id(0) == 0)
        def _():
            dg_ref[...] = jnp.zeros_like(dg_ref)
            db_ref[...] = jnp.zeros_like(db_ref)

        dy_t, xhat_t = dy_ref[...], xhat_ref[...]
        dr = _ln_bwd_rows(dy_t, xhat_t, rstd_ref[...], g_ref[...])
        dr_ref[...] = dr
        drb_ref[...] = dr.astype(BF16)
        dg_ref[...] += jnp.sum(dy_t * xhat_t, axis=0, keepdims=True)
        db_ref[...] += jnp.sum(dy_t, axis=0, keepdims=True)

    row = pl.BlockSpec((tr, d), lambda i: (i, 0))
    vec = pl.BlockSpec((1, d), lambda i: (0, 0))
    return pl.pallas_call(
        body, name=name, grid=(t // tr,), in_specs=[row, row, pl.BlockSpec((tr, 1), lambda i: (i, 0)), vec],
        out_specs=[row, row, vec, vec],
        out_shape=[jax.ShapeDtypeStruct((t, d), F32), jax.ShapeDtypeStruct((t, d), BF16),
                   jax.ShapeDtypeStruct((1, d), F32), jax.ShapeDtypeStruct((1, d), F32)],
        compiler_params=_params(("arbitrary",)),
    )(dy, xhat, rstd, g)


def _ln2_loss_bwd(r2, target, g, b, tr=256):
    t, d = r2.shape
    tr = _tile(t, tr)

    def body(r_ref, t_ref, g_ref, b_ref, dr_ref, drb_ref, loss_ref, dg_ref, db_ref, dsum_ref):
        @pl.when(pl.program_id(0) == 0)
        def _():
            loss_ref[...] = jnp.zeros_like(loss_ref)
            dg_ref[...] = jnp.zeros_like(dg_ref)
            db_ref[...] = jnp.zeros_like(db_ref)
            dsum_ref[...] = jnp.zeros_like(dsum_ref)

        xhat, rstd = _ln_stats(r_ref[...])
        g_t = g_ref[...]
        err = xhat * g_t + b_ref[...] - t_ref[...]
        loss_ref[...] += 0.5 * jnp.sum(jnp.mean(jnp.square(err), axis=-1, keepdims=True), axis=0, keepdims=True)
        dy = err * (1.0 / d)
        dr = _ln_bwd_rows(dy, xhat, rstd, g_t)
        dr_ref[...] = dr
        drb_ref[...] = dr.astype(BF16)
        dg_ref[...] += jnp.sum(dy * xhat, axis=0, keepdims=True)
        db_ref[...] += jnp.sum(dy, axis=0, keepdims=True)
        dsum_ref[...] += jnp.sum(dr, axis=0, keepdims=True)

    row = pl.BlockSpec((tr, d), lambda i: (i, 0))
    vec = pl.BlockSpec((1, d), lambda i: (0, 0))
    return pl.pallas_call(
        body, name="ln2_loss_bwd", grid=(t // tr,), in_specs=[row, row, vec, vec],
        out_specs=[row, row, pl.BlockSpec((8, 128), lambda i: (0, 0)), vec, vec, vec],
        out_shape=[jax.ShapeDtypeStruct((t, d), F32), jax.ShapeDtypeStruct((t, d), BF16),
                   jax.ShapeDtypeStruct((8, 128), F32), jax.ShapeDtypeStruct((1, d), F32),
                   jax.ShapeDtypeStruct((1, d), F32), jax.ShapeDtypeStruct((1, d), F32)],
        compiler_params=_params(("arbitrary",)),
    )(r2, target, g, b)


POOL_ROWS = 512


def _pool_mean_minus_token(u_ref, r0, rows, grp, first):
    width = u_ref.shape[1]
    body = u_ref[pl.ds(r0, rows), :]
    halo = u_ref[pl.ds(pl.multiple_of(jnp.maximum(r0 - POOL_HALO, 0), POOL_HALO), POOL_HALO), :]
    halo = jnp.where(first, 0.0, halo)
    full = jnp.concatenate([halo, body], axis=0)
    s = full
    for step in range(len(POOL_WINDOWS)):
        shifted = pltpu.roll(s, 1 << step, axis=0)
        s = s + jnp.where(grp >= step, shifted, 0.0)
    s = s[POOL_HALO:, :]
    tpos = r0 + lax.broadcasted_iota(jnp.int32, (rows, width), 0)
    count = jnp.minimum(tpos + 1, 2 << grp).astype(F32)
    return s / count - body, count


def _pool_fwd(u, w_pool, pool_scale, t, pw):
    gw = pw // len(POOL_WINDOWS)
    rows = _tile(t, POOL_ROWS)

    def body(u_ref, w_ref, s_ref, o_ref):
        grp = pl.program_id(0)

        def chunk(ci, carry):
            r0 = pl.multiple_of(ci * rows, rows)
            y, _ = _pool_mean_minus_token(u_ref, r0, rows, grp, ci == 0)
            yw = jnp.dot(y.astype(BF16), w_ref[...], preferred_element_type=F32)
            o_ref[pl.ds(r0, rows), :] = (yw * s_ref[...]).astype(BF16)
            return carry

        lax.fori_loop(0, t // rows, chunk, 0)

    return pl.pallas_call(
        body, name="pool_fwd", grid=(len(POOL_WINDOWS),),
        in_specs=[pl.BlockSpec((t, gw), lambda g: (0, g)), pl.BlockSpec((None, gw, gw), lambda g: (g, 0, 0)),
                  pl.BlockSpec((None, 1, gw), lambda g: (g, 0, 0))],
        out_specs=pl.BlockSpec((t, gw), lambda g: (0, g)),
        out_shape=jax.ShapeDtypeStruct((t, pw), BF16),
        compiler_params=_params(("parallel",)),
    )(u, w_pool, pool_scale)


def _pool_bwd(u, dmix, w_pool, pool_scale, t, pw):
    n_grp = len(POOL_WINDOWS)
    gw = pw // n_grp
    rows = _tile(t, POOL_ROWS)

    def body(u_ref, dm_ref, w_ref, s_ref, du_ref, dw_ref, ds_ref, e_ref):
        grp = pl.program_id(0)
        dw_ref[...] = jnp.zeros_like(dw_ref)
        ds_ref[...] = jnp.zeros_like(ds_ref)
        e_ref[pl.ds(t, POOL_HALO), :] = jnp.zeros((POOL_HALO, gw), F32)

        def chunk(ci, carry):
            r0 = pl.multiple_of(ci * rows, rows)
            y, count = _pool_mean_minus_token(u_ref, r0, rows, grp, ci == 0)
            yb = y.astype(BF16)
            yw = jnp.dot(yb, w_ref[...], preferred_element_type=F32)
            dy2 = dm_ref[pl.ds(r0, rows), :]
            ds_ref[...] += jnp.sum(dy2 * yw, axis=0, keepdims=True)
            dyw = (dy2 * s_ref[...]).astype(BF16)
            dw_ref[...] += lax.dot_general(yb, dyw, (((0,), (0,)), ((), ())), preferred_element_type=F32)
            dy = lax.dot_general(dyw, w_ref[...], (((1,), (1,)), ((), ())), preferred_element_type=F32)
            e_ref[pl.ds(r0, rows), :] = dy / count
            return carry

        lax.fori_loop(0, t // rows, chunk, 0)

        def chunk2(ci, carry):
            r0 = pl.multiple_of(ci * rows, rows)
            full = e_ref[pl.ds(r0, rows + POOL_HALO), :]
            s = full
            for step in range(n_grp):
                shifted = pltpu.roll(s, rows + POOL_HALO - (1 << step), axis=0)
                s = s + jnp.where(grp >= step, shifted, 0.0)
            e = full[:rows, :]
            tpos = r0 + lax.broadcasted_iota(jnp.int32, (rows, gw), 0)
            count = jnp.minimum(tpos + 1, 2 << grp).astype(F32)
            du_ref[pl.ds(r0, rows), :] = (s[:rows, :] - e * count).astype(BF16)
            return carry

        lax.fori_loop(0, t // rows, chunk2, 0)

    return pl.pallas_call(
        body, name="pool_bwd", grid=(n_grp,),
        in_specs=[pl.BlockSpec((t, gw), lambda g: (0, g)), pl.BlockSpec((t, gw), lambda g: (0, g)),
                  pl.BlockSpec((None, gw, gw), lambda g: (g, 0, 0)),
                  pl.BlockSpec((None, 1, gw), lambda g: (g, 0, 0))],
        out_specs=[pl.BlockSpec((t, gw), lambda g: (0, g)), pl.BlockSpec((None, gw, gw), lambda g: (g, 0, 0)),
                   pl.BlockSpec((None, 1, gw), lambda g: (g, 0, 0))],
        out_shape=[jax.ShapeDtypeStruct((t, pw), BF16), jax.ShapeDtypeStruct((n_grp, gw, gw), F32),
                   jax.ShapeDtypeStruct((n_grp, 1, gw), F32)],
        scratch_shapes=[pltpu.VMEM((t + POOL_HALO, gw), F32)],
        compiler_params=_params(("parallel",)),
    )(u, dmix, w_pool, pool_scale)


def _sb_scores(q, k_blk, scale, mask):
    z = lax.dot_general(q, k_blk, (((1,), (1,)), ((), ())), preferred_element_type=F32) * scale
    tneg = jnp.exp(-jnp.abs(z))
    log_not = -(jnp.maximum(z, 0.0) + jnp.log1p(tneg))
    return z, jnp.where(mask, log_not, 0.0), tneg


EXP_IS_ZERO_BELOW = -104.0


def _weights_alive(after):
    return (jnp.max(after) >= EXP_IS_ZERO_BELOW).astype(jnp.int32)


def _split_dot(v, tri):
    hi = v.astype(BF16)
    lo = (v - hi.astype(F32)).astype(BF16)
    return (jnp.dot(hi, tri, preferred_element_type=F32) + jnp.dot(lo, tri, preferred_element_type=F32))


def _head(ref, h, rows=None):
    cols = slice(h * HEAD_DIM, (h + 1) * HEAD_DIM)
    return ref[:, cols] if rows is None else ref[rows, cols]


def _attn_fwd(ub, t, nh, hg):
    scale = float(1.0 / (HEAD_DIM ** 0.5))
    ng = nh // hg

    def body(q_ref, k_ref, v_ref, o_ref):
        i = pl.program_id(1)
        row = lax.broadcasted_iota(jnp.int32, (QB, KB), 0)
        col = lax.broadcasted_iota(jnp.int32, (QB, KB), 1)
        suffix = (row >= col).astype(BF16)

        def more(carry):
            return jnp.logical_and(carry[0] <= i, carry[3] > 0)

        def step(carry):
            n, accs, afters, _ = carry
            rows = pl.ds(pl.multiple_of((i - n) * KB, KB), KB)
            mask = jnp.logical_or(n > 0, col < row)
            new_accs, new_afters = [], []
            for h in range(hg):
                z, log_not, _ = _sb_scores(_head(q_ref, h), _head(k_ref, h, rows), scale, mask)
                within = _split_dot(log_not, suffix)
                a = jnp.where(mask, jnp.exp(z + within + afters[h]), 0.0)
                new_accs.append(accs[h] + jnp.dot(a.astype(BF16), _head(v_ref, h, rows),
                                                  preferred_element_type=F32))
                new_afters.append(afters[h] + jnp.sum(log_not, axis=1, keepdims=True))
            return n + 1, tuple(new_accs), tuple(new_afters), _weights_alive(functools.reduce(jnp.maximum, new_afters))

        init = (jnp.int32(0), tuple(jnp.zeros((QB, HEAD_DIM), F32) for _ in range(hg)),
                tuple(jnp.zeros((QB, 1), F32) for _ in range(hg)), jnp.int32(1))
        _, accs, _, _ = lax.while_loop(more, step, init)
        for h in range(hg):
            o_ref[:, h * HEAD_DIM:(h + 1) * HEAD_DIM] = accs[h].astype(BF16)

    wide = hg * HEAD_DIM
    return pl.pallas_call(
        body, name="attn_fwd", grid=(ng, t // QB),
        in_specs=[pl.BlockSpec((QB, wide), lambda g, i: (i, ng + g)),
                  pl.BlockSpec((t, wide), lambda g, i: (0, 2 * ng + g)),
                  pl.BlockSpec((t, wide), lambda g, i: (0, 3 * ng + g))],
        out_specs=pl.BlockSpec((QB, wide), lambda g, i: (i, g)),
        out_shape=jax.ShapeDtypeStruct((t, nh * HEAD_DIM), BF16),
        compiler_params=_params(("parallel", "arbitrary")),
    )(ub, ub, ub)


def _attn_bwd(ub, dmix, t, nh, hg):
    scale = float(1.0 / (HEAD_DIM ** 0.5))
    width = nh * HEAD_DIM
    ng = nh // hg

    def body(q_ref, k_ref, v_ref, do_ref, dq_ref, dk_ref, dv_ref, g_ref, z_ref):
        i = pl.program_id(1)

        @pl.when(i == 0)
        def _():
            dk_ref[...] = jnp.zeros_like(dk_ref)
            dv_ref[...] = jnp.zeros_like(dv_ref)

        row = lax.broadcasted_iota(jnp.int32, (QB, KB), 0)
        col = lax.broadcasted_iota(jnp.int32, (QB, KB), 1)
        suffix = (row >= col).astype(BF16)
        prefix = (row <= col).astype(BF16)

        def more(carry):
            return jnp.logical_and(carry[0] <= i, carry[2] > 0)

        def down(carry):
            n, afters, _ = carry
            ks = pl.multiple_of((i - n) * KB, KB)
            rows = pl.ds(ks, KB)
            mask = jnp.logical_or(n > 0, col < row)
            new_afters = []
            for h in range(hg):
                do = _head(do_ref, h).astype(BF16)
                z, log_not, _ = _sb_scores(_head(q_ref, h), _head(k_ref, h, rows), scale, mask)
                within = _split_dot(log_not, suffix)
                a = jnp.where(mask, jnp.exp(z + within + afters[h]), 0.0)
                da = lax.dot_general(do, _head(v_ref, h, rows), (((1,), (1,)), ((), ())),
                                     preferred_element_type=F32)
                g_ref[h, :, pl.ds(ks, KB)] = a * da
                z_ref[h, :, pl.ds(ks, KB)] = z
                dv_ref[rows, h * HEAD_DIM:(h + 1) * HEAD_DIM] += lax.dot_general(
                    a.astype(BF16), do, (((0,), (0,)), ((), ())), preferred_element_type=F32)
                new_afters.append(afters[h] + jnp.sum(log_not, axis=1, keepdims=True))
            return n + 1, tuple(new_afters), _weights_alive(functools.reduce(jnp.maximum, new_afters))

        visited, _, _ = lax.while_loop(
            more, down, (jnp.int32(0), tuple(jnp.zeros((QB, 1), F32) for _ in range(hg)), jnp.int32(1)))

        def up(kb, carry):
            dqs, befores = carry
            ks = pl.multiple_of(kb * KB, KB)
            rows = pl.ds(ks, KB)
            mask = jnp.logical_or(kb < i, col < row)
            new_dqs, new_befores = [], []
            for h in range(hg):
                g = g_ref[h, :, pl.ds(ks, KB)]
                z = z_ref[h, :, pl.ds(ks, KB)]
                g_upto = _split_dot(g, prefix) + befores[h]
                dz = jnp.where(mask, g - jax.nn.sigmoid(z) * g_upto, 0.0)
                dzs = (dz * scale).astype(BF16)
                new_dqs.append(dqs[h] + jnp.dot(dzs, _head(k_ref, h, rows), preferred_element_type=F32))
                dk_ref[rows, h * HEAD_DIM:(h + 1) * HEAD_DIM] += lax.dot_general(
                    dzs, _head(q_ref, h), (((0,), (0,)), ((), ())), preferred_element_type=F32)
                new_befores.append(befores[h] + jnp.sum(g, axis=1, keepdims=True))
            return tuple(new_dqs), tuple(new_befores)

        dqs, _ = lax.fori_loop(i + 1 - visited, i + 1, up,
                               (tuple(jnp.zeros((QB, HEAD_DIM), F32) for _ in range(hg)),
                                tuple(jnp.zeros((QB, 1), F32) for _ in range(hg))))
        for h in range(hg):
            dq_ref[:, h * HEAD_DIM:(h + 1) * HEAD_DIM] = dqs[h].astype(BF16)

    wide = hg * HEAD_DIM
    tile = lambda off: pl.BlockSpec((QB, wide), lambda g, i: (i, off + g))
    strip = lambda off: pl.BlockSpec((t, wide), lambda g, i: (0, off + g))
    return pl.pallas_call(
        body, name="attn_bwd", grid=(ng, t // QB),
        in_specs=[tile(ng), strip(2 * ng), strip(3 * ng), tile(ng)],
        out_specs=[tile(0), strip(0), strip(0)],
        out_shape=[jax.ShapeDtypeStruct((t, width), BF16), jax.ShapeDtypeStruct((t, width), F32),
                   jax.ShapeDtypeStruct((t, width), F32)],
        scratch_shapes=[pltpu.VMEM((hg, QB, t), F32), pltpu.VMEM((hg, QB, t), F32)],
        compiler_params=_params(("parallel", "arbitrary")),
    )(ub, ub, ub, dmix)


def _row_tile(rows, cols, pref_bytes=2 * 1024 * 1024):
    tr = max(8, pref_bytes // (4 * cols))
    while rows % tr:
        tr //= 2
    return max(tr, 1)


def _pair_sum(name, g, s, c_idx):
    _, _, r2, cols = g.shape
    tr = _row_tile(r2, cols)

    def body(c_ref, g_ref, s_ref, o_ref):
        o_ref[...] = (g_ref[...] + s_ref[...]).astype(BF16)

    return pl.pallas_call(
        body, name=name,
        grid_spec=pltpu.PrefetchScalarGridSpec(
            num_scalar_prefetch=1, grid=(N_CHIPS, r2 // tr),
            in_specs=[pl.BlockSpec((None, None, tr, cols), lambda p, i, c: (p, c[0], i, 0)),
                      pl.BlockSpec((None, tr, cols), lambda p, i, c: (p, i, 0))],
            out_specs=pl.BlockSpec((None, tr, cols), lambda p, i, c: (p, i, 0))),
        out_shape=jax.ShapeDtypeStruct((N_CHIPS, r2, cols), BF16),
        compiler_params=_params(("parallel", "parallel")),
    )(c_idx, g, s)


def _chip_sum(name, g, s, r, place):
    _, _, r2, cols = g.shape
    tr = _row_tile(r2, cols)

    def body(place_ref, g_ref, s_ref, r_ref, o_ref):
        own = g_ref[...] + s_ref[...]
        o_ref[...] = ((own + r_ref[0].astype(F32)) + r_ref[1].astype(F32)) + r_ref[2].astype(F32)

    return pl.pallas_call(
        body, name=name,
        grid_spec=pltpu.PrefetchScalarGridSpec(
            num_scalar_prefetch=1, grid=(r2 // tr,),
            in_specs=[pl.BlockSpec((None, None, tr, cols), lambda i, p: (p[0], p[1], i, 0)),
                      pl.BlockSpec((None, tr, cols), lambda i, p: (p[0], i, 0)),
                      pl.BlockSpec((3, tr, cols), lambda i, p: (0, i, 0))],
            out_specs=pl.BlockSpec((None, tr, cols), lambda i, p: (p[1], i, 0))),
        out_shape=jax.ShapeDtypeStruct((2, r2, cols), F32),
        compiler_params=_params(("parallel",)),
    )(place, g, s, r)


def _cast_into_slot(name, w, place):
    rows, cols = w.shape
    r2 = rows // 2
    tr = _row_tile(r2, cols)
    nb = r2 // tr

    def body(place_ref, w_ref, o_ref):
        o_ref[...] = w_ref[...].astype(BF16)

    return pl.pallas_call(
        body, name=name,
        grid_spec=pltpu.PrefetchScalarGridSpec(
            num_scalar_prefetch=1, grid=(2, nb),
            in_specs=[pl.BlockSpec((tr, cols), lambda h, i, s: (h * nb + i, 0))],
            out_specs=pl.BlockSpec((None, None, tr, cols), lambda h, i, s: (s[0], h, i, 0))),
        out_shape=jax.ShapeDtypeStruct((N_CHIPS, 2, r2, cols), BF16),
        compiler_params=_params(("parallel", "parallel")),
    )(place, w)


def _colsum(name, a):
    def body(a_ref, o_ref):
        o_ref[...] = jnp.sum(a_ref[...], axis=0, keepdims=True)

    whole = lambda shape: pl.BlockSpec(shape, lambda i: (0, 0))
    return pl.pallas_call(
        body, name=name, grid=(1,), in_specs=[whole(a.shape)], out_specs=whole((1, a.shape[1])),
        out_shape=jax.ShapeDtypeStruct((1, a.shape[1]), F32), compiler_params=_params(("arbitrary",)),
    )(a)


def _adamw(name, w, g, m, v):
    rows, cols = w.shape
    tr = _row_tile(rows, cols, 1024 * 1024)

    def body(w_ref, g_ref, m_ref, v_ref, d_ref, nm_ref, nv_ref):
        g_t = g_ref[...]
        m_t = ADAM_B1 * m_ref[...] + (1.0 - ADAM_B1) * g_t
        v_t = ADAM_B2 * v_ref[...] + (1.0 - ADAM_B2) * jnp.square(g_t)
        m_hat = m_t / (1.0 - ADAM_B1 ** ADAM_STEP)
        v_hat = v_t / (1.0 - ADAM_B2 ** ADAM_STEP)
        d_ref[...] = -ADAM_LR * (m_hat / (jnp.sqrt(v_hat) + ADAM_EPS) + ADAM_WD * w_ref[...])
        nm_ref[...] = m_t
        nv_ref[...] = v_t

    spec = pl.BlockSpec((tr, cols), lambda i: (i, 0))
    shape = jax.ShapeDtypeStruct((rows, cols), F32)
    return pl.pallas_call(
        body, name=name, grid=(rows // tr,), in_specs=[spec] * 4, out_specs=[spec] * 3, out_shape=[shape] * 3,
        compiler_params=_params(("parallel",)),
    )(w, g, m, v)


ANY = pl.BlockSpec(memory_space=pl.ANY)


def _gather_weights(slots, wslots):
    ns, nw = len(slots), len(wslots)

    def body(*refs):
        outs, wouts = refs[ns + nw:2 * ns + nw], refs[2 * ns + nw:2 * (ns + nw)]
        send_sems, recv_sems, fsend_sems, frecv_sems, wsend_sems, wrecv_sems = refs[2 * (ns + nw):]
        x, y, c = _pos()
        me = 2 * x + y
        chips = _other_chips(x, y)
        sibling = (x, y, 1 - c)

        def half_copy(a, j, sems_s, sems_r, chip_idx, half, to):
            piece = outs[a].at[chip_idx, half]
            return pltpu.make_async_remote_copy(src_ref=piece, dst_ref=piece, send_sem=sems_s.at[3 * a + j],
                                                recv_sem=sems_r.at[3 * a + j], device_id=to, device_id_type=MESH)

        def whole_copy(a, j, chip_idx, to):
            piece = wouts[a].at[chip_idx]
            return pltpu.make_async_remote_copy(src_ref=piece, dst_ref=piece, send_sem=wsend_sems.at[3 * a + j],
                                                recv_sem=wrecv_sems.at[3 * a + j], device_id=to, device_id_type=MESH)

        sends = []
        for a in range(ns):
            for j, chip in enumerate(chips):
                sends.append(half_copy(a, j, send_sems, recv_sems, me, c, (*chip, c)))
        for a in range(nw):
            for j, chip in enumerate(chips):
                sends.append(whole_copy(a, j, me, (*chip, c)))
        for cp in sends:
            cp.start()
        for a in range(ns):
            for j, chip in enumerate(chips):
                idx = 2 * chip[0] + chip[1]
                half_copy(a, j, send_sems, recv_sems, idx, c, (x, y, c)).wait_recv()
                fw = half_copy(a, j, fsend_sems, frecv_sems, idx, c, sibling)
                fw.start()
                sends.append(fw)
        for a in range(ns):
            for j, chip in enumerate(chips):
                half_copy(a, j, fsend_sems, frecv_sems, 2 * chip[0] + chip[1], 1 - c, (x, y, c)).wait_recv()
        for a in range(nw):
            for j, chip in enumerate(chips):
                whole_copy(a, j, 2 * chip[0] + chip[1], (x, y, c)).wait_recv()
        for cp in sends:
            cp.wait_send()

    n = ns + nw
    outs = pl.pallas_call(
        body, name="gather_weights", in_specs=[ANY] * n, out_specs=[ANY] * n,
        out_shape=[jax.ShapeDtypeStruct(s.shape, s.dtype) for s in (*slots, *wslots)],
        input_output_aliases={i: i for i in range(n)},
        scratch_shapes=[pltpu.SemaphoreType.DMA((3 * ns,)), pltpu.SemaphoreType.DMA((3 * ns,)),
                        pltpu.SemaphoreType.DMA((3 * ns,)), pltpu.SemaphoreType.DMA((3 * ns,)),
                        pltpu.SemaphoreType.DMA((3 * nw,)), pltpu.SemaphoreType.DMA((3 * nw,))],
    )(*slots, *wslots)
    gathered = [o.reshape(N_CHIPS, 2 * o.shape[2], o.shape[3]) for o in outs[:ns]]
    return gathered, list(outs[ns:])


def _sibling_swap_halves(gs):
    n = len(gs)

    def body(*refs):
        ins, outs = refs[:n], refs[n:2 * n]
        send_sems, recv_sems = refs[2 * n:]
        x, y, c = _pos()
        cps = []
        for a in range(n):
            cp = pltpu.make_async_remote_copy(
                src_ref=ins[a].at[:, 1 - c], dst_ref=outs[a], send_sem=send_sems.at[a], recv_sem=recv_sems.at[a],
                device_id=(x, y, 1 - c), device_id_type=MESH)
            cp.start()
            cps.append(cp)
        for cp in cps:
            cp.wait()

    return pl.pallas_call(
        body, name="rs_sibling_swap", in_specs=[ANY] * n, out_specs=[ANY] * n,
        out_shape=[jax.ShapeDtypeStruct((N_CHIPS, g.shape[2], g.shape[3]), g.dtype) for g in gs],
        scratch_shapes=[pltpu.SemaphoreType.DMA((n,)), pltpu.SemaphoreType.DMA((n,))],
    )(*gs)


def _chip_scatter(ps):
    n = len(ps)

    def body(*refs):
        ins, outs = refs[:n], refs[n:2 * n]
        send_sems, recv_sems = refs[2 * n:]
        x, y, c = _pos()
        cps = []
        for a in range(n):
            for j, chip in enumerate(_other_chips(x, y)):
                cp = pltpu.make_async_remote_copy(
                    src_ref=ins[a].at[2 * chip[0] + chip[1]], dst_ref=outs[a].at[j],
                    send_sem=send_sems.at[3 * a + j], recv_sem=recv_sems.at[3 * a + j],
                    device_id=(*chip, c), device_id_type=MESH)
                cp.start()
                cps.append(cp)
        for cp in cps:
            cp.wait()

    return pl.pallas_call(
        body, name="rs_chip_scatter", in_specs=[ANY] * n, out_specs=[ANY] * n,
        out_shape=[jax.ShapeDtypeStruct((3, p.shape[1], p.shape[2]), p.dtype) for p in ps],
        scratch_shapes=[pltpu.SemaphoreType.DMA((3 * n,)), pltpu.SemaphoreType.DMA((3 * n,))],
    )(*ps)


def _sibling_assemble(qs):
    n = len(qs)

    def body(*refs):
        outs = refs[n:2 * n]
        send_sems, recv_sems = refs[2 * n:]
        x, y, c = _pos()

        def half_copy(a, half, to):
            piece = outs[a].at[half]
            return pltpu.make_async_remote_copy(src_ref=piece, dst_ref=piece, send_sem=send_sems.at[a],
                                                recv_sem=recv_sems.at[a], device_id=to, device_id_type=MESH)

        sends = [half_copy(a, c, (x, y, 1 - c)) for a in range(n)]
        for cp in sends:
            cp.start()
        for a in range(n):
            half_copy(a, 1 - c, (x, y, c)).wait_recv()
        for cp in sends:
            cp.wait_send()

    return pl.pallas_call(
        body, name="rs_sibling_assemble", in_specs=[ANY] * n, out_specs=[ANY] * n,
        out_shape=[jax.ShapeDtypeStruct(q.shape, q.dtype) for q in qs],
        input_output_aliases={i: i for i in range(n)},
        scratch_shapes=[pltpu.SemaphoreType.DMA((n,)), pltpu.SemaphoreType.DMA((n,))],
    )(*qs)


def _all_reduce_small(packed):
    rows, cols = packed.shape

    def body(in_ref, out_ref, all_ref, send_sems, recv_sems):
        x, y, c = _pos()
        me = 4 * x + 2 * y + c
        all_ref[me] = in_ref[...]
        cps = []
        for r in range(1, N_DEV):
            bx, by, bc = (r >> 2) & 1, (r >> 1) & 1, r & 1
            peer = (1 - x if bx else x, 1 - y if by else y, 1 - c if bc else c)
            cp = pltpu.make_async_remote_copy(
                src_ref=in_ref, dst_ref=all_ref.at[me], send_sem=send_sems.at[r - 1], recv_sem=recv_sems.at[r - 1],
                device_id=peer, device_id_type=MESH)
            cp.start()
            cps.append(cp)
        for cp in cps:
            cp.wait()
        total = all_ref[0]
        for d in range(1, N_DEV):
            total = total + all_ref[d]
        out_ref[...] = total

    vmem = pl.BlockSpec(memory_space=pltpu.VMEM)
    return pl.pallas_call(
        body, name="all_reduce_small", in_specs=[vmem], out_specs=vmem,
        out_shape=jax.ShapeDtypeStruct((rows, cols), F32),
        scratch_shapes=[pltpu.VMEM((N_DEV, rows, cols), F32), pltpu.SemaphoreType.DMA((N_DEV - 1,)),
                        pltpu.SemaphoreType.DMA((N_DEV - 1,))],
        compiler_params=pltpu.CompilerParams(vmem_limit_bytes=VMEM_LIMIT),
    )(packed)


def kernel(x, ln_in_g, ln_in_b, w_in, w_pool, pool_scale, w_out, ln1_g, ln1_b, w_ff1, b_ff1, w_ff2, b_ff2, ln2_g, ln2_b, loss_target, m_ln_in_g, m_ln_in_b, m_w_in, m_w_pool, m_pool_scale, m_w_out, m_ln1_g, m_ln1_b, m_w_ff1, m_b_ff1, m_w_ff2, m_b_ff2, m_ln2_g, m_ln2_b, v_ln_in_g, v_ln_in_b, v_w_in, v_w_pool, v_pool_scale, v_w_out, v_ln1_g, v_ln1_b, v_w_ff1, v_b_ff1, v_w_ff2, v_b_ff2, v_ln2_g, v_ln2_b):
    t, d = x.shape[1], x.shape[2]
    pw = d // 2
    n_grp = len(POOL_WINDOWS)
    gw = pw // n_grp
    gwc = gw // N_CHIPS
    nh = pw // HEAD_DIM
    ff = w_ff1.shape[2] * N_CHIPS
    assert w_in.shape[0] == 1 and w_in.shape[2] * N_CHIPS == 2 * d and gwc <= 128

    x_idx, y_idx, c_idx = _pos()
    chip_arr = jnp.reshape(2 * x_idx + y_idx, (1,)).astype(jnp.int32)
    c_arr = jnp.reshape(c_idx, (1,)).astype(jnp.int32)
    place = jnp.concatenate([chip_arr, c_arr])

    xs = x.reshape(t, d)
    target = loss_target.reshape(t, d)
    row = lambda vec: vec.reshape(1, -1)

    scale_tile = jnp.zeros((1, 8, 128), F32).at[0, :n_grp, :gwc].set(pool_scale[0])
    scale_slots = lax.dynamic_update_slice(jnp.zeros((N_CHIPS, 8, 128), F32), scale_tile, (chip_arr[0], 0, 0))
    shards = dict(w_in=w_in[0], w_out=w_out[0], w_ff1=w_ff1[0], w_ff2=w_ff2[0], w_pool=w_pool[0].reshape(gw, gw))
    (win_g, wout_g, wff1_g, wff2_g, wpool_g), (scale_g,) = _gather_weights(
        [_cast_into_slot("cast_" + nm, w, place) for nm, w in shards.items()], [scale_slots])
    wout_full = wout_g.reshape(d, d)
    wff2_full = wff2_g.reshape(ff, d)
    wpool_full = wpool_g.reshape(N_CHIPS, n_grp, gwc, gw).transpose(1, 0, 2, 3).reshape(n_grp, gw, gw)
    scale_full = scale_g[:, :n_grp, :gwc].transpose(1, 0, 2).reshape(n_grp, 1, gw)

    h0, h0b, xhat0, rstd0 = _ln_fwd("ln_in_fwd", xs, row(ln_in_g), row(ln_in_b))

    def store_f32(acc, extra_refs, out_refs):
        out_refs[0][...] = acc

    def pool_f32_all_bf16(acc, extra_refs, out_refs):
        @pl.when(pl.program_id(1) == 0)
        def _():
            out_refs[0][...] = acc

        out_refs[1][...] = acc.astype(BF16)

    assert w_in.shape[2] == pw
    u, ub = _mm_nn("in_proj", h0b, win_g, (), lambda tm, tn: [],
                   [jax.ShapeDtypeStruct((t, pw), F32), jax.ShapeDtypeStruct((t, 2 * d), BF16)],
                   lambda tm, tn: [pl.BlockSpec((tm, tn), lambda i, j, kk: (i, 0)), _tile_spec(tm, tn)],
                   pool_f32_all_bf16, b_chips=True, tn=pw)
    y_pool = _pool_fwd(u, wpool_full, scale_full, t, pw)
    y_sb = _attn_fwd(ub, t, nh, min(nh, 4))
    mix_in = jnp.concatenate([y_pool, y_sb], axis=1)

    def residual(acc, extra_refs, out_refs):
        out_refs[0][...] = ALPHA * extra_refs[0][...] + acc

    r1 = _mm_nn("out_proj", mix_in, wout_full, (h0,), lambda tm, tn: [_tile_spec(tm, tn)],
                [jax.ShapeDtypeStruct((t, d), F32)], lambda tm, tn: [_tile_spec(tm, tn)], residual)[0]
    h1, h1b, xhat1, rstd1 = _ln_fwd("ln1_fwd", r1, ln1_g, ln1_b)

    def relu_sq(acc, extra_refs, out_refs):
        p = jnp.maximum(acc + extra_refs[0][...], 0.0)
        out_refs[0][...] = p
        out_refs[1][...] = jnp.square(p).astype(BF16)

    relu_z, act_b = _mm_nn("ff1", h1b, wff1_g, (b_ff1,), lambda tm, tn: [_row_spec(tn)],
                           [jax.ShapeDtypeStruct((t, ff), F32), jax.ShapeDtypeStruct((t, ff), BF16)],
                           lambda tm, tn: [_tile_spec(tm, tn)] * 2, relu_sq, b_chips=True)

    def residual_bias(acc, extra_refs, out_refs):
        out_refs[0][...] = ALPHA * extra_refs[0][...] + (acc + extra_refs[1][...])

    r2 = _mm_nn("ff2", act_b, wff2_full, (h1, b_ff2), lambda tm, tn: [_tile_spec(tm, tn), _row_spec(tn)],
                [jax.ShapeDtypeStruct((t, d), F32)], lambda tm, tn: [_tile_spec(tm, tn)], residual_bias)[0]

    dr2, dr2b, loss_tile, g_ln2_g, g_ln2_b, g_b_ff2 = _ln2_loss_bwd(r2, target, ln2_g, ln2_b)
    loss = lax.psum(loss_tile[0, 0], ("x", "y", "c"))

    g_w_ff2 = _mm_tn("grad_w_ff2", act_b, dr2b)

    def relu_sq_bwd(acc, extra_refs, out_refs):
        dz = acc * (2.0 * extra_refs[0][...])
        out_refs[0][...] = dz.astype(BF16)
        rows = lax.broadcasted_iota(jnp.int32, out_refs[1].shape, 0)
        out_refs[1][...] = jnp.where(rows == 0, jnp.sum(dz, axis=0, keepdims=True), 0.0)

    tm_ff = _tile(t, 1024)
    dz1b, g_b_ff1_parts = _mm_nt(
        "ff2_bwd", dr2b, wff2_full, (relu_z,), lambda tm, tn: [_tile_spec(tm, tn)],
        [jax.ShapeDtypeStruct((t, ff), BF16), jax.ShapeDtypeStruct((8 * (t // tm_ff), ff), F32)],
        lambda tm, tn: [_tile_spec(tm, tn), pl.BlockSpec((8, tn), lambda i, j, kk: (i, j))], relu_sq_bwd)
    g_w_ff1 = _mm_tn("grad_w_ff1", h1b, dz1b, out_chips=True)

    def plus_alpha(acc, extra_refs, out_refs):
        out_refs[0][...] = ALPHA * extra_refs[0][...] + acc

    dh1 = _mm_nt("ff1_bwd", dz1b, wff1_g, (dr2,), lambda tm, tn: [_tile_spec(tm, tn)],
                 [jax.ShapeDtypeStruct((t, d), F32)], lambda tm, tn: [_tile_spec(tm, tn)], plus_alpha,
                 b_chips=True)[0]
    dr1, dr1b, g_ln1_g, g_ln1_b = _ln_bwd("ln1_bwd", dh1, xhat1, rstd1, ln1_g)

    g_w_out = _mm_tn("grad_w_out", mix_in, dr1b)
    dmix = _mm_nt("out_proj_bwd", dr1b, wout_full, (), lambda tm, tn: [], [jax.ShapeDtypeStruct((t, d), F32)],
                  lambda tm, tn: [_tile_spec(tm, tn)], store_f32)[0]
    du_pool, g_w_pool_full, g_scale_full = _pool_bwd(u, dmix, wpool_full, scale_full, t, pw)
    dq, dk, dv = _attn_bwd(ub, dmix, t, nh, min(nh, 2))
    du = jnp.concatenate([du_pool, dq, dk.astype(BF16), dv.astype(BF16)], axis=1)
    g_w_in = _mm_tn("grad_w_in", h0b, du, out_chips=True)
    dh0 = _mm_nt("in_proj_bwd", du, win_g, (dr1,), lambda tm, tn: [_tile_spec(tm, tn)],
                 [jax.ShapeDtypeStruct((t, d), F32)], lambda tm, tn: [_tile_spec(tm, tn)], plus_alpha,
                 b_chips=True)[0]
    dx, _, g_ln_in_g, g_ln_in_b = _ln_bwd("ln_in_bwd", dh0, xhat0, rstd0, row(ln_in_g))

    g_w_pool_chips = g_w_pool_full.reshape(n_grp, N_CHIPS, gwc, gw).transpose(1, 0, 2, 3).reshape(N_CHIPS, gw, gw)
    by_chip = [g_w_in, g_w_out.reshape(N_CHIPS, d // N_CHIPS, d), g_w_ff1, g_w_ff2.reshape(N_CHIPS, ff // N_CHIPS, d),
               g_w_pool_chips]
    names = ["w_in", "w_out", "w_ff1", "w_ff2", "w_pool"]
    halves = [g.reshape(N_CHIPS, 2, g.shape[1] // 2, g.shape[2]) for g in by_chip]
    swapped = _sibling_swap_halves(halves)
    pair = [_pair_sum("pair_sum_" + nm, g, s, c_arr) for nm, g, s in zip(names, halves, swapped)]
    received = _chip_scatter(pair)
    mine = [_chip_sum("chip_sum_" + nm, g, s, r, place) for nm, g, s, r in zip(names, halves, swapped, received)]
    reduced = [f.reshape(f.shape[0] * f.shape[1], f.shape[2]) for f in _sibling_assemble(mine)]

    big = {}
    for nm, g, w, m, v in zip(names, reduced, [w_in, w_out, w_ff1, w_ff2, w_pool], [m_w_in, m_w_out, m_w_ff1, m_w_ff2, m_w_pool],
                              [v_w_in, v_w_out, v_w_ff1, v_w_ff2, v_w_pool]):
        flat = lambda arr: arr.reshape(g.shape)
        delta, new_m, new_v = _adamw("adamw_" + nm, flat(w), g, flat(m), flat(v))
        big[nm] = tuple(arr.reshape(w.shape) for arr in (g, delta, new_m, new_v))

    lane = 2048 if d % 2048 == 0 else d
    small_names = ["ln_in_g", "ln_in_b", "ln1_g", "ln1_b", "b_ff1", "b_ff2", "ln2_g", "ln2_b"]
    small_w = dict(ln_in_g=ln_in_g, ln_in_b=ln_in_b, ln1_g=ln1_g, ln1_b=ln1_b, b_ff1=b_ff1, b_ff2=b_ff2, ln2_g=ln2_g,
                   ln2_b=ln2_b)
    small_m = dict(ln_in_g=m_ln_in_g, ln_in_b=m_ln_in_b, ln1_g=m_ln1_g, ln1_b=m_ln1_b, b_ff1=m_b_ff1, b_ff2=m_b_ff2,
                   ln2_g=m_ln2_g, ln2_b=m_ln2_b)
    small_v = dict(ln_in_g=v_ln_in_g, ln_in_b=v_ln_in_b, ln1_g=v_ln1_g, ln1_b=v_ln1_b, b_ff1=v_b_ff1, b_ff2=v_b_ff2,
                   ln2_g=v_ln2_g, ln2_b=v_ln2_b)
    small_g = dict(ln_in_g=g_ln_in_g, ln_in_b=g_ln_in_b, ln1_g=g_ln1_g, ln1_b=g_ln1_b, b_ff2=g_b_ff2, ln2_g=g_ln2_g,
                   ln2_b=g_ln2_b)

    def pack(parts):
        flat = jnp.concatenate([p.reshape(-1) for p in parts])
        n_rows = -(-flat.shape[0] // lane)
        n_rows = -(-n_rows // 8) * 8
        return jnp.pad(flat, (0, n_rows * lane - flat.shape[0])).reshape(n_rows, lane)

    small_g["b_ff1"] = _colsum("b_ff1_colsum", g_b_ff1_parts)
    summed = _all_reduce_small(pack([small_g[nm] for nm in small_names] + [g_scale_full])).reshape(-1)

    g_small, off = {}, 0
    for nm in small_names:
        g_small[nm] = summed[off:off + small_w[nm].size]
        off += small_w[nm].size
    g_scale_all = summed[off:off + n_grp * gw].reshape(n_grp, N_CHIPS, gwc)
    g_scale = lax.dynamic_index_in_dim(g_scale_all, chip_arr[0], axis=1, keepdims=False)

    order = small_names + ["pool_scale"]
    small_w["pool_scale"], small_m["pool_scale"], small_v["pool_scale"] = pool_scale, m_pool_scale, v_pool_scale
    g_small["pool_scale"] = g_scale
    delta_s, new_m_s, new_v_s = _adamw("adamw_small", pack([small_w[nm] for nm in order]),
                                       pack([g_small[nm] for nm in order]), pack([small_m[nm] for nm in order]),
                                       pack([small_v[nm] for nm in order]))
    small = {}
    off = 0
    for nm in order:
        size, shape = small_w[nm].size, small_w[nm].shape
        cut = lambda arr: arr.reshape(-1)[off:off + size].reshape(shape)
        small[nm] = (g_small[nm].reshape(shape), cut(delta_s), cut(new_m_s), cut(new_v_s))
        off += size

    every = {**big, **small}
    weight_order = ["ln_in_g", "ln_in_b", "w_in", "w_pool", "pool_scale", "w_out", "ln1_g", "ln1_b", "w_ff1", "b_ff1",
                    "w_ff2", "b_ff2", "ln2_g", "ln2_b"]
    grads = [every[nm][0] for nm in weight_order]
    deltas = [every[nm][1] for nm in weight_order]
    new_ms = [every[nm][2] for nm in weight_order]
    new_vs = [every[nm][3] for nm in weight_order]
    return (loss, dx.reshape(x.shape), *grads, *deltas, *new_ms, *new_vs)
```

```python
import functools

import jax
import jax.numpy as jnp
from jax import lax
from jax.experimental import pallas as pl
from jax.experimental.pallas import tpu as pltpu

F32 = jnp.float32
BF16 = jnp.bfloat16
MESH = pl.DeviceIdType.MESH

HEAD_DIM = 128
POOL_WINDOWS = (2, 4, 8, 16)
POOL_HALO = 16
LN_EPS = 1e-5
ALPHA = 2.0 ** 0.25
ADAM_LR, ADAM_B1, ADAM_B2, ADAM_EPS, ADAM_WD, ADAM_STEP = 0.001, 0.9, 0.999, 1e-08, 0.01, 10

QB = 128
KB = 128
VMEM_LIMIT = 56 * 1024 * 1024
N_CHIPS = 4
N_DEV = 8


def _params(sem=None):
    return pltpu.CompilerParams(dimension_semantics=sem, vmem_limit_bytes=VMEM_LIMIT)


def _tile(dim, pref):
    return pref if dim % pref == 0 else dim


def _pos():
    return lax.axis_index("x"), lax.axis_index("y"), lax.axis_index("c")


def _other_chips(x, y):
    return [(1 - x, y), (x, 1 - y), (1 - x, 1 - y)]


ANY = pl.BlockSpec(memory_space=pl.ANY)


class _Phase:
    def __init__(self, ins, out_shapes, aliases, n_sems, build):
        self.ins, self.out_shapes, self.aliases, self.n_sems, self.build = ins, out_shapes, aliases, n_sems, build


def _remote(src, dst, send_sems, recv_sems, k, to):
    return pltpu.make_async_remote_copy(src_ref=src, dst_ref=dst, send_sem=send_sems.at[k], recv_sem=recv_sems.at[k],
                                        device_id=to, device_id_type=MESH)


def _swap_phase(g):
    def build(ins, outs, ss, rs):
        x, y, c = _pos()
        cp = _remote(ins[0].at[:, 1 - c], outs[0], ss, rs, 0, (x, y, 1 - c))
        return [cp], [cp]

    return _Phase([g], [jax.ShapeDtypeStruct((N_CHIPS, g.shape[2], g.shape[3]), g.dtype)], {}, 1, build)


def _scatter_phase(p):
    def build(ins, outs, ss, rs):
        x, y, c = _pos()
        cps = [_remote(ins[0].at[2 * chip[0] + chip[1]], outs[0].at[j], ss, rs, j, (*chip, c))
               for j, chip in enumerate(_other_chips(x, y))]
        return cps, cps

    return _Phase([p], [jax.ShapeDtypeStruct((3, p.shape[1], p.shape[2]), p.dtype)], {}, 3, build)


def _assemble_phase(q):
    def build(ins, outs, ss, rs):
        x, y, c = _pos()
        mine, other = outs[0].at[c], outs[0].at[1 - c]
        return [_remote(mine, mine, ss, rs, 0, (x, y, 1 - c))], [_remote(other, other, ss, rs, 0, (x, y, c))]

    return _Phase([q], [jax.ShapeDtypeStruct(q.shape, q.dtype)], {0: 0}, 1, build)


def _gather_ici_phase(slot):
    def build(ins, outs, ss, rs):
        x, y, c = _pos()
        mine = outs[0].at[2 * x + y, c]
        sends, recvs = [], []
        for j, chip in enumerate(_other_chips(x, y)):
            theirs = outs[0].at[2 * chip[0] + chip[1], c]
            sends.append(_remote(mine, mine, ss, rs, j, (*chip, c)))
            recvs.append(_remote(theirs, theirs, ss, rs, j, (x, y, c)))
        return sends, recvs

    return _Phase([slot], [jax.ShapeDtypeStruct(slot.shape, slot.dtype)], {0: 0}, 3, build)


def _gather_d2d_phase(slot):
    def build(ins, outs, ss, rs):
        x, y, c = _pos()
        sends, recvs = [], []
        for j, chip in enumerate(_other_chips(x, y)):
            landed = outs[0].at[2 * chip[0] + chip[1], c]
            coming = outs[0].at[2 * chip[0] + chip[1], 1 - c]
            sends.append(_remote(landed, landed, ss, rs, j, (x, y, 1 - c)))
            recvs.append(_remote(coming, coming, ss, rs, j, (x, y, c)))
        return sends, recvs

    return _Phase([slot], [jax.ShapeDtypeStruct(slot.shape, slot.dtype)], {0: 0}, 3, build)


def _gather_whole_phase(slots):
    def build(ins, outs, ss, rs):
        x, y, c = _pos()
        mine = outs[0].at[2 * x + y]
        sends, recvs = [], []
        for j, chip in enumerate(_other_chips(x, y)):
            theirs = outs[0].at[2 * chip[0] + chip[1]]
            sends.append(_remote(mine, mine, ss, rs, j, (*chip, c)))
            recvs.append(_remote(theirs, theirs, ss, rs, j, (x, y, c)))
        return sends, recvs

    return _Phase([slots], [jax.ShapeDtypeStruct(slots.shape, slots.dtype)], {0: 0}, 3, build)


def _split_refs(refs, n_in, n_out, n_scratch, phases):
    n_pin = sum(len(ph.ins) for ph in phases)
    n_pout = sum(len(ph.out_shapes) for ph in phases)
    cuts = [n_in, n_pin, n_out, n_pout, n_scratch]
    parts, at = [], 0
    for n in cuts:
        parts.append(refs[at:at + n])
        at += n
    parts.append(refs[at:])
    return parts


def _build_phases(phases, pin, pout, sems):
    built, i, o = [], 0, 0
    for k, ph in enumerate(phases):
        built.append(ph.build(pin[i:i + len(ph.ins)], pout[o:o + len(ph.out_shapes)], sems[2 * k], sems[2 * k + 1]))
        i += len(ph.ins)
        o += len(ph.out_shapes)
    return built


def _finish_phases(built):
    for _, recvs in built:
        for cp in recvs:
            cp.wait_recv()
    for sends, _ in built:
        for cp in sends:
            cp.wait_send()


def _call(body, name, grid, in_specs, out_specs, out_shape, scratch_shapes, semantics, args, phases=()):
    n_in, n_out, n_scratch = len(args), len(out_shape), len(scratch_shapes)
    aliases, in_at, out_at = {}, n_in, n_out
    for ph in phases:
        aliases.update({in_at + i: out_at + o for i, o in ph.aliases.items()})
        in_at += len(ph.ins)
        out_at += len(ph.out_shapes)

    def hosted(*refs):
        ins, pin, outs, pout, scratch, sems = _split_refs(refs, n_in, n_out, n_scratch, phases)
        ids = [pl.program_id(a) for a in range(len(grid))]
        first = functools.reduce(jnp.logical_and, [i == 0 for i in ids])
        last = functools.reduce(jnp.logical_and, [i == g - 1 for i, g in zip(ids, grid)])

        @pl.when(first)
        def _():
            for sends, _ in _build_phases(phases, pin, pout, sems):
                for cp in sends:
                    cp.start()

        body(*ins, *outs, *scratch)

        @pl.when(last)
        def _():
            _finish_phases(_build_phases(phases, pin, pout, sems))

    p_args = [a for ph in phases for a in ph.ins]
    p_shapes = [s for ph in phases for s in ph.out_shapes]
    sem_shapes = [pltpu.SemaphoreType.DMA((ph.n_sems,)) for ph in phases for _ in range(2)]
    outs = pl.pallas_call(
        hosted if phases else body, name=name, grid=grid, in_specs=[*in_specs, *[ANY] * len(p_args)],
        out_specs=[*out_specs, *[ANY] * len(p_shapes)], out_shape=[*out_shape, *p_shapes],
        input_output_aliases=aliases, scratch_shapes=[*scratch_shapes, *sem_shapes],
        compiler_params=_params(("arbitrary",) * len(grid) if phases else semantics),
    )(*args, *p_args)
    phase_outs, at = [], n_out
    for ph in phases:
        phase_outs.append(list(outs[at:at + len(ph.out_shapes)]))
        at += len(ph.out_shapes)
    return list(outs[:n_out]), phase_outs


def _comm_call(name, phases):
    def body(*refs):
        _, pin, _, pout, _, sems = _split_refs(refs, 0, 0, 0, phases)
        built = _build_phases(phases, pin, pout, sems)
        for sends, _ in built:
            for cp in sends:
                cp.start()
        _finish_phases(built)

    aliases, in_at, out_at = {}, 0, 0
    for ph in phases:
        aliases.update({in_at + i: out_at + o for i, o in ph.aliases.items()})
        in_at += len(ph.ins)
        out_at += len(ph.out_shapes)
    p_args = [a for ph in phases for a in ph.ins]
    p_shapes = [s for ph in phases for s in ph.out_shapes]
    outs = pl.pallas_call(
        body, name=name, in_specs=[ANY] * len(p_args), out_specs=[ANY] * len(p_shapes), out_shape=p_shapes,
        input_output_aliases=aliases,
        scratch_shapes=[pltpu.SemaphoreType.DMA((ph.n_sems,)) for ph in phases for _ in range(2)],
    )(*p_args)
    phase_outs, at = [], 0
    for ph in phases:
        phase_outs.append(list(outs[at:at + len(ph.out_shapes)]))
        at += len(ph.out_shapes)
    return phase_outs


def _matmul(name, a, b, grid, a_spec, b_spec, contract, acc_shape, extras, extra_specs, out_shape, out_specs,
            epilogue, phases=()):
    n_extra, n_out, gk = len(extras), len(out_shape), grid[2]

    def product(a_ref, b_ref):
        return lax.dot_general(a_ref[...], b_ref[...], (contract, ((), ())), preferred_element_type=F32)

    def body_one_step(*refs):
        epilogue(product(refs[0], refs[1]), refs[2:2 + n_extra], refs[2 + n_extra:])

    def body(*refs):
        a_ref, b_ref = refs[0], refs[1]
        extra_refs = refs[2:2 + n_extra]
        out_refs = refs[2 + n_extra:2 + n_extra + n_out]
        acc_ref = refs[-1]
        kk = pl.program_id(2)

        @pl.when(kk == 0)
        def _():
            acc_ref[...] = product(a_ref, b_ref)

        @pl.when(kk > 0)
        def _():
            acc_ref[...] += product(a_ref, b_ref)

        @pl.when(kk == gk - 1)
        def _():
            epilogue(acc_ref[...], extra_refs, out_refs)

    outs, phase_outs = _call(
        body_one_step if gk == 1 else body, name, grid, [a_spec, b_spec, *extra_specs], out_specs, out_shape,
        [] if gk == 1 else [pltpu.VMEM(acc_shape, F32)], ("parallel", "arbitrary", "arbitrary"), (a, b, *extras),
        phases)
    return (outs, phase_outs) if phases else outs


def _mm_nn(name, a, b, extras, extra_specs, out_shape, out_specs, epilogue, b_chips=False, tm=1024, tn=1024,
           tk=2048, phases=()):
    m, k = a.shape
    n = b.shape[1] if not b_chips else b.shape[2] * N_CHIPS
    tm, tk = _tile(m, tm), _tile(k, tk)
    if b_chips:
        tn = _tile(b.shape[2], tn)
        nb = b.shape[2] // tn
        b_spec = pl.BlockSpec((None, tk, tn), lambda i, j, kk: (j // nb, kk, j % nb))
    else:
        tn = _tile(n, tn)
        b_spec = pl.BlockSpec((tk, tn), lambda i, j, kk: (kk, j))
    a_spec = pl.BlockSpec((tm, tk), lambda i, j, kk: (i, kk))
    return _matmul(name, a, b, (m // tm, n // tn, k // tk), a_spec, b_spec, ((1,), (0,)), (tm, tn), extras,
                   extra_specs(tm, tn), out_shape, out_specs(tm, tn), epilogue, phases)


def _mm_nt(name, a, b, extras, extra_specs, out_shape, out_specs, epilogue, b_chips=False, tm=1024, tn=1024,
           tk=2048, phases=()):
    m, k = a.shape
    n = b.shape[0] if not b_chips else b.shape[1]
    tm, tn = _tile(m, tm), _tile(n, tn)
    if b_chips:
        tk = _tile(b.shape[2], tk)
        nb = b.shape[2] // tk
        b_spec = pl.BlockSpec((None, tn, tk), lambda i, j, kk: (kk // nb, j, kk % nb))
    else:
        tk = _tile(k, tk)
        b_spec = pl.BlockSpec((tn, tk), lambda i, j, kk: (j, kk))
    a_spec = pl.BlockSpec((tm, tk), lambda i, j, kk: (i, kk))
    return _matmul(name, a, b, (m // tm, n // tn, k // tk), a_spec, b_spec, ((1,), (1,)), (tm, tn), extras,
                   extra_specs(tm, tn), out_shape, out_specs(tm, tn), epilogue, phases)


def _mm_tn(name, a, b, out_chips=False, tm=1024, tn=1024, tk=2048, phases=()):
    k, m = a.shape
    n = b.shape[1]
    tm, tk = _tile(m, tm), _tile(k, tk)
    a_spec = pl.BlockSpec((tk, tm), lambda i, j, kk: (kk, i))
    if out_chips:
        nc = n // N_CHIPS
        tn = _tile(nc, tn)
        nb = nc // tn
        out_shape = [jax.ShapeDtypeStruct((N_CHIPS, m, nc), F32)]
        out_specs = [pl.BlockSpec((None, tm, tn), lambda i, j, kk: (j // nb, i, j % nb))]
    else:
        tn = _tile(n, tn)
        out_shape = [jax.ShapeDtypeStruct((m, n), F32)]
        out_specs = [pl.BlockSpec((tm, tn), lambda i, j, kk: (i, j))]
    b_spec = pl.BlockSpec((tk, tn), lambda i, j, kk: (kk, j))

    def epilogue(acc, extra_refs, out_refs):
        out_refs[0][...] = acc

    res = _matmul(name, a, b, (m // tm, n // tn, k // tk), a_spec, b_spec, ((0,), (0,)), (tm, tn), (), [],
                  out_shape, out_specs, epilogue, phases)
    return (res[0][0], res[1]) if phases else res[0]


def _tile_spec(tm, tn):
    return pl.BlockSpec((tm, tn), lambda i, j, kk: (i, j))


def _row_spec(tn):
    return pl.BlockSpec((1, tn), lambda i, j, kk: (0, j))


def _ln_stats(r):
    mu = jnp.mean(r, axis=-1, keepdims=True)
    var = jnp.mean(jnp.square(r - mu), axis=-1, keepdims=True)
    rstd = lax.rsqrt(var + LN_EPS)
    return (r - mu) * rstd, rstd


def _ln_fwd(name, r, g, b, tr=256, phases=()):
    t, d = r.shape
    tr = _tile(t, tr)

    def body(r_ref, g_ref, b_ref, y_ref, yb_ref, xhat_ref, rstd_ref):
        xhat, rstd = _ln_stats(r_ref[...])
        y = xhat * g_ref[...] + b_ref[...]
        y_ref[...] = y
        yb_ref[...] = y.astype(BF16)
        xhat_ref[...] = xhat
        rstd_ref[...] = rstd

    row = pl.BlockSpec((tr, d), lambda i: (i, 0))
    vec = pl.BlockSpec((1, d), lambda i: (0, 0))
    outs, phase_outs = _call(
        body, name, (t // tr,), [row, vec, vec], [row, row, row, pl.BlockSpec((tr, 1), lambda i: (i, 0))],
        [jax.ShapeDtypeStruct((t, d), F32), jax.ShapeDtypeStruct((t, d), BF16),
         jax.ShapeDtypeStruct((t, d), F32), jax.ShapeDtypeStruct((t, 1), F32)], [], ("parallel",), (r, g, b), phases)
    return (outs, phase_outs) if phases else outs


def _ln_bwd_rows(dy, xhat, rstd, g):
    dxhat = dy * g
    m1 = jnp.mean(dxhat, axis=-1, keepdims=True)
    m2 = jnp.mean(dxhat * xhat, axis=-1, keepdims=True)
    return rstd * (dxhat - m1 - xhat * m2)


def _ln_bwd(name, dy, xhat, rstd, g, tr=256, phases=()):
    t, d = dy.shape
    tr = _tile(t, tr)

    def body(dy_ref, xhat_ref, rstd_ref, g_ref, dr_ref, drb_ref, dg_ref, db_ref):
        @pl.when(pl.program_id(0) == 0)
        def _():
            dg_ref[...] = jnp.zeros_like(dg_ref)
            db_ref[...] = jnp.zeros_like(db_ref)

        dy_t, xhat_t = dy_ref[...], xhat_ref[...]
        dr = _ln_bwd_rows(dy_t, xhat_t, rstd_ref[...], g_ref[...])
        dr_ref[...] = dr
        drb_ref[...] = dr.astype(BF16)
        dg_ref[...] += jnp.sum(dy_t * xhat_t, axis=0, keepdims=True)
        db_ref[...] += jnp.sum(dy_t, axis=0, keepdims=True)

    row = pl.BlockSpec((tr, d), lambda i: (i, 0))
    vec = pl.BlockSpec((1, d), lambda i: (0, 0))
    outs, phase_outs = _call(
        body, name, (t // tr,), [row, row, pl.BlockSpec((tr, 1), lambda i: (i, 0)), vec], [row, row, vec, vec],
        [jax.ShapeDtypeStruct((t, d), F32), jax.ShapeDtypeStruct((t, d), BF16),
         jax.ShapeDtypeStruct((1, d), F32), jax.ShapeDtypeStruct((1, d), F32)], [], ("arbitrary",),
        (dy, xhat, rstd, g), phases)
    return (outs, phase_outs) if phases else outs


def _ln2_loss_bwd(r2, target, g, b, tr=256):
    t, d = r2.shape
    tr = _tile(t, tr)

    def body(r_ref, t_ref, g_ref, b_ref, dr_ref, drb_ref, loss_ref, dg_ref, db_ref, dsum_ref):
        @pl.when(pl.program_id(0) == 0)
        def _():
            loss_ref[...] = jnp.zeros_like(loss_ref)
            dg_ref[...] = jnp.zeros_like(dg_ref)
            db_ref[...] = jnp.zeros_like(db_ref)
            dsum_ref[...] = jnp.zeros_like(dsum_ref)

        xhat, rstd = _ln_stats(r_ref[...])
        g_t = g_ref[...]
        err = xhat * g_t + b_ref[...] - t_ref[...]
        loss_ref[...] += 0.5 * jnp.sum(jnp.mean(jnp.square(err), axis=-1, keepdims=True), axis=0, keepdims=True)
        dy = err * (1.0 / d)
        dr = _ln_bwd_rows(dy, xhat, rstd, g_t)
        dr_ref[...] = dr
        drb_ref[...] = dr.astype(BF16)
        dg_ref[...] += jnp.sum(dy * xhat, axis=0, keepdims=True)
        db_ref[...] += jnp.sum(dy, axis=0, keepdims=True)
        dsum_ref[...] += jnp.sum(dr, axis=0, keepdims=True)

    row = pl.BlockSpec((tr, d), lambda i: (i, 0))
    vec = pl.BlockSpec((1, d), lambda i: (0, 0))
    return pl.pallas_call(
        body, name="ln2_loss_bwd", grid=(t // tr,), in_specs=[row, row, vec, vec],
        out_specs=[row, row, pl.BlockSpec((8, 128), lambda i: (0, 0)), vec, vec, vec],
        out_shape=[jax.ShapeDtypeStruct((t, d), F32), jax.ShapeDtypeStruct((t, d), BF16),
                   jax.ShapeDtypeStruct((8, 128), F32), jax.ShapeDtypeStruct((1, d), F32),
                   jax.ShapeDtypeStruct((1, d), F32), jax.ShapeDtypeStruct((1, d), F32)],
        compiler_params=_params(("arbitrary",)),
    )(r2, target, g, b)


POOL_ROWS = 512


def _pool_mean_minus_token(u_ref, r0, rows, grp, first):
    width = u_ref.shape[1]
    body = u_ref[pl.ds(r0, rows), :]
    halo = u_ref[pl.ds(pl.multiple_of(jnp.maximum(r0 - POOL_HALO, 0), POOL_HALO), POOL_HALO), :]
    halo = jnp.where(first, 0.0, halo)
    full = jnp.concatenate([halo, body], axis=0)
    s = full
    for step in range(len(POOL_WINDOWS)):
        shifted = pltpu.roll(s, 1 << step, axis=0)
        s = s + jnp.where(grp >= step, shifted, 0.0)
    s = s[POOL_HALO:, :]
    tpos = r0 + lax.broadcasted_iota(jnp.int32, (rows, width), 0)
    count = jnp.minimum(tpos + 1, 2 << grp).astype(F32)
    return s / count - body, count


def _pool_fwd(u, w_pool, pool_scale, t, pw):
    gw = pw // len(POOL_WINDOWS)
    rows = _tile(t, POOL_ROWS)

    def body(u_ref, w_ref, s_ref, o_ref):
        grp = pl.program_id(0)

        def chunk(ci, carry):
            r0 = pl.multiple_of(ci * rows, rows)
            y, _ = _pool_mean_minus_token(u_ref, r0, rows, grp, ci == 0)
            yw = jnp.dot(y.astype(BF16), w_ref[...], preferred_element_type=F32)
            o_ref[pl.ds(r0, rows), :] = (yw * s_ref[...]).astype(BF16)
            return carry

        lax.fori_loop(0, t // rows, chunk, 0)

    return pl.pallas_call(
        body, name="pool_fwd", grid=(len(POOL_WINDOWS),),
        in_specs=[pl.BlockSpec((t, gw), lambda g: (0, g)), pl.BlockSpec((None, gw, gw), lambda g: (g, 0, 0)),
                  pl.BlockSpec((None, 1, gw), lambda g: (g, 0, 0))],
        out_specs=pl.BlockSpec((t, gw), lambda g: (0, g)),
        out_shape=jax.ShapeDtypeStruct((t, pw), BF16),
        compiler_params=_params(("parallel",)),
    )(u, w_pool, pool_scale)


def _pool_bwd(u, dmix, w_pool, pool_scale, t, pw):
    n_grp = len(POOL_WINDOWS)
    gw = pw // n_grp
    rows = _tile(t, POOL_ROWS)

    def body(u_ref, dm_ref, w_ref, s_ref, du_ref, dw_ref, ds_ref, e_ref):
        grp = pl.program_id(0)
        dw_ref[...] = jnp.zeros_like(dw_ref)
        ds_ref[...] = jnp.zeros_like(ds_ref)
        e_ref[pl.ds(t, POOL_HALO), :] = jnp.zeros((POOL_HALO, gw), F32)

        def chunk(ci, carry):
            r0 = pl.multiple_of(ci * rows, rows)
            y, count = _pool_mean_minus_token(u_ref, r0, rows, grp, ci == 0)
            yb = y.astype(BF16)
            yw = jnp.dot(yb, w_ref[...], preferred_element_type=F32)
            dy2 = dm_ref[pl.ds(r0, rows), :]
            ds_ref[...] += jnp.sum(dy2 * yw, axis=0, keepdims=True)
            dyw = (dy2 * s_ref[...]).astype(BF16)
            dw_ref[...] += lax.dot_general(yb, dyw, (((0,), (0,)), ((), ())), preferred_element_type=F32)
            dy = lax.dot_general(dyw, w_ref[...], (((1,), (1,)), ((), ())), preferred_element_type=F32)
            e_ref[pl.ds(r0, rows), :] = dy / count
            return carry

        lax.fori_loop(0, t // rows, chunk, 0)

        def chunk2(ci, carry):
            r0 = pl.multiple_of(ci * rows, rows)
            full = e_ref[pl.ds(r0, rows + POOL_HALO), :]
            s = full
            for step in range(n_grp):
                shifted = pltpu.roll(s, rows + POOL_HALO - (1 << step), axis=0)
                s = s + jnp.where(grp >= step, shifted, 0.0)
            e = full[:rows, :]
            tpos = r0 + lax.broadcasted_iota(jnp.int32, (rows, gw), 0)
            count = jnp.minimum(tpos + 1, 2 << grp).astype(F32)
            du_ref[pl.ds(r0, rows), :] = (s[:rows, :] - e * count).astype(BF16)
            return carry

        lax.fori_loop(0, t // rows, chunk2, 0)

    return pl.pallas_call(
        body, name="pool_bwd", grid=(n_grp,),
        in_specs=[pl.BlockSpec((t, gw), lambda g: (0, g)), pl.BlockSpec((t, gw), lambda g: (0, g)),
                  pl.BlockSpec((None, gw, gw), lambda g: (g, 0, 0)),
                  pl.BlockSpec((None, 1, gw), lambda g: (g, 0, 0))],
        out_specs=[pl.BlockSpec((t, gw), lambda g: (0, g)), pl.BlockSpec((None, gw, gw), lambda g: (g, 0, 0)),
                   pl.BlockSpec((None, 1, gw), lambda g: (g, 0, 0))],
        out_shape=[jax.ShapeDtypeStruct((t, pw), BF16), jax.ShapeDtypeStruct((n_grp, gw, gw), F32),
                   jax.ShapeDtypeStruct((n_grp, 1, gw), F32)],
        scratch_shapes=[pltpu.VMEM((t + POOL_HALO, gw), F32)],
        compiler_params=_params(("parallel",)),
    )(u, dmix, w_pool, pool_scale)


def _sb_scores(q, k_blk, scale, mask):
    z = lax.dot_general(q, k_blk, (((1,), (1,)), ((), ())), preferred_element_type=F32) * scale
    tneg = jnp.exp(-jnp.abs(z))
    log_not = -(jnp.maximum(z, 0.0) + jnp.log1p(tneg))
    return z, jnp.where(mask, log_not, 0.0), tneg


EXP_IS_ZERO_BELOW = -104.0


def _weights_alive(after):
    return (jnp.max(after) >= EXP_IS_ZERO_BELOW).astype(jnp.int32)


def _split_dot(v, tri):
    hi = v.astype(BF16)
    lo = (v - hi.astype(F32)).astype(BF16)
    return (jnp.dot(hi, tri, preferred_element_type=F32) + jnp.dot(lo, tri, preferred_element_type=F32))


def _head(ref, h, rows=None):
    cols = slice(h * HEAD_DIM, (h + 1) * HEAD_DIM)
    return ref[:, cols] if rows is None else ref[rows, cols]


def _attn_fwd(ub, t, nh, hg, phases=()):
    scale = float(1.0 / (HEAD_DIM ** 0.5))
    ng = nh // hg

    def body(q_ref, k_ref, v_ref, o_ref):
        i = pl.program_id(1)
        row = lax.broadcasted_iota(jnp.int32, (QB, KB), 0)
        col = lax.broadcasted_iota(jnp.int32, (QB, KB), 1)
        suffix = (row >= col).astype(BF16)

        def more(carry):
            return jnp.logical_and(carry[0] <= i, carry[3] > 0)

        def step(carry):
            n, accs, afters, _ = carry
            rows = pl.ds(pl.multiple_of((i - n) * KB, KB), KB)
            mask = jnp.logical_or(n > 0, col < row)
            new_accs, new_afters = [], []
            for h in range(hg):
                z, log_not, _ = _sb_scores(_head(q_ref, h), _head(k_ref, h, rows), scale, mask)
                within = _split_dot(log_not, suffix)
                a = jnp.where(mask, jnp.exp(z + within + afters[h]), 0.0)
                new_accs.append(accs[h] + jnp.dot(a.astype(BF16), _head(v_ref, h, rows),
                                                  preferred_element_type=F32))
                new_afters.append(afters[h] + jnp.sum(log_not, axis=1, keepdims=True))
            return n + 1, tuple(new_accs), tuple(new_afters), _weights_alive(functools.reduce(jnp.maximum, new_afters))

        init = (jnp.int32(0), tuple(jnp.zeros((QB, HEAD_DIM), F32) for _ in range(hg)),
                tuple(jnp.zeros((QB, 1), F32) for _ in range(hg)), jnp.int32(1))
        _, accs, _, _ = lax.while_loop(more, step, init)
        for h in range(hg):
            o_ref[:, h * HEAD_DIM:(h + 1) * HEAD_DIM] = accs[h].astype(BF16)

    wide = hg * HEAD_DIM
    outs, phase_outs = _call(
        body, "attn_fwd", (ng, t // QB),
        [pl.BlockSpec((QB, wide), lambda g, i: (i, ng + g)), pl.BlockSpec((t, wide), lambda g, i: (0, 2 * ng + g)),
         pl.BlockSpec((t, wide), lambda g, i: (0, 3 * ng + g))],
        [pl.BlockSpec((QB, wide), lambda g, i: (i, g))], [jax.ShapeDtypeStruct((t, nh * HEAD_DIM), BF16)], [],
        ("parallel", "arbitrary"), (ub, ub, ub), phases)
    return outs[0], phase_outs


def _attn_bwd(ub, dmix, t, nh, hg, phases=()):
    scale = float(1.0 / (HEAD_DIM ** 0.5))
    width = nh * HEAD_DIM
    ng = nh // hg

    def body(q_ref, k_ref, v_ref, do_ref, dq_ref, dk_ref, dv_ref, g_ref, z_ref):
        i = pl.program_id(1)

        @pl.when(i == 0)
        def _():
            dk_ref[...] = jnp.zeros_like(dk_ref)
            dv_ref[...] = jnp.zeros_like(dv_ref)

        row = lax.broadcasted_iota(jnp.int32, (QB, KB), 0)
        col = lax.broadcasted_iota(jnp.int32, (QB, KB), 1)
        suffix = (row >= col).astype(BF16)
        prefix = (row <= col).astype(BF16)

        def more(carry):
            return jnp.logical_and(carry[0] <= i, carry[2] > 0)

        def down(carry):
            n, afters, _ = carry
            ks = pl.multiple_of((i - n) * KB, KB)
            rows = pl.ds(ks, KB)
            mask = jnp.logical_or(n > 0, col < row)
            new_afters = []
            for h in range(hg):
                do = _head(do_ref, h).astype(BF16)
                z, log_not, _ = _sb_scores(_head(q_ref, h), _head(k_ref, h, rows), scale, mask)
                within = _split_dot(log_not, suffix)
                a = jnp.where(mask, jnp.exp(z + within + afters[h]), 0.0)
                da = lax.dot_general(do, _head(v_ref, h, rows), (((1,), (1,)), ((), ())),
                                     preferred_element_type=F32)
                g_ref[h, :, pl.ds(ks, KB)] = a * da
                z_ref[h, :, pl.ds(ks, KB)] = z
                dv_ref[rows, h * HEAD_DIM:(h + 1) * HEAD_DIM] += lax.dot_general(
                    a.astype(BF16), do, (((0,), (0,)), ((), ())), preferred_element_type=F32)
                new_afters.append(afters[h] + jnp.sum(log_not, axis=1, keepdims=True))
            return n + 1, tuple(new_afters), _weights_alive(functools.reduce(jnp.maximum, new_afters))

        visited, _, _ = lax.while_loop(
            more, down, (jnp.int32(0), tuple(jnp.zeros((QB, 1), F32) for _ in range(hg)), jnp.int32(1)))

        def up(kb, carry):
            dqs, befores = carry
            ks = pl.multiple_of(kb * KB, KB)
            rows = pl.ds(ks, KB)
            mask = jnp.logical_or(kb < i, col < row)
            new_dqs, new_befores = [], []
            for h in range(hg):
                g = g_ref[h, :, pl.ds(ks, KB)]
                z = z_ref[h, :, pl.ds(ks, KB)]
                g_upto = _split_dot(g, prefix) + befores[h]
                dz = jnp.where(mask, g - jax.nn.sigmoid(z) * g_upto, 0.0)
                dzs = (dz * scale).astype(BF16)
                new_dqs.append(dqs[h] + jnp.dot(dzs, _head(k_ref, h, rows), preferred_element_type=F32))
                dk_ref[rows, h * HEAD_DIM:(h + 1) * HEAD_DIM] += lax.dot_general(
                    dzs, _head(q_ref, h), (((0,), (0,)), ((), ())), preferred_element_type=F32)
                new_befores.append(befores[h] + jnp.sum(g, axis=1, keepdims=True))
            return tuple(new_dqs), tuple(new_befores)

        dqs, _ = lax.fori_loop(i + 1 - visited, i + 1, up,
                               (tuple(jnp.zeros((QB, HEAD_DIM), F32) for _ in range(hg)),
                                tuple(jnp.zeros((QB, 1), F32) for _ in range(hg))))
        for h in range(hg):
            dq_ref[:, h * HEAD_DIM:(h + 1) * HEAD_DIM] = dqs[h].astype(BF16)

    wide = hg * HEAD_DIM
    tile = lambda off: pl.BlockSpec((QB, wide), lambda g, i: (i, off + g))
    strip = lambda off: pl.BlockSpec((t, wide), lambda g, i: (0, off + g))
    return _call(
        body, "attn_bwd", (ng, t // QB), [tile(ng), strip(2 * ng), strip(3 * ng), tile(ng)],
        [tile(0), strip(0), strip(0)],
        [jax.ShapeDtypeStruct((t, width), BF16), jax.ShapeDtypeStruct((t, width), F32),
         jax.ShapeDtypeStruct((t, width), F32)],
        [pltpu.VMEM((hg, QB, t), F32), pltpu.VMEM((hg, QB, t), F32)], ("parallel", "arbitrary"),
        (ub, ub, ub, dmix), phases)


def _row_tile(rows, cols, pref_bytes=2 * 1024 * 1024):
    tr = max(8, pref_bytes // (4 * cols))
    while rows % tr:
        tr //= 2
    return max(tr, 1)


def _pair_sum(name, g, s, c_idx):
    _, _, r2, cols = g.shape
    tr = _row_tile(r2, cols)

    def body(c_ref, g_ref, s_ref, o_ref):
        o_ref[...] = (g_ref[...] + s_ref[...]).astype(BF16)

    return pl.pallas_call(
        body, name=name,
        grid_spec=pltpu.PrefetchScalarGridSpec(
            num_scalar_prefetch=1, grid=(N_CHIPS, r2 // tr),
            in_specs=[pl.BlockSpec((None, None, tr, cols), lambda p, i, c: (p, c[0], i, 0)),
                      pl.BlockSpec((None, tr, cols), lambda p, i, c: (p, i, 0))],
            out_specs=pl.BlockSpec((None, tr, cols), lambda p, i, c: (p, i, 0))),
        out_shape=jax.ShapeDtypeStruct((N_CHIPS, r2, cols), BF16),
        compiler_params=_params(("parallel", "parallel")),
    )(c_idx, g, s)


def _chip_sum(name, g, s, r, place):
    _, _, r2, cols = g.shape
    tr = _row_tile(r2, cols)

    def body(place_ref, g_ref, s_ref, r_ref, o_ref):
        own = g_ref[...] + s_ref[...]
        o_ref[...] = ((own + r_ref[0].astype(F32)) + r_ref[1].astype(F32)) + r_ref[2].astype(F32)

    return pl.pallas_call(
        body, name=name,
        grid_spec=pltpu.PrefetchScalarGridSpec(
            num_scalar_prefetch=1, grid=(r2 // tr,),
            in_specs=[pl.BlockSpec((None, None, tr, cols), lambda i, p: (p[0], p[1], i, 0)),
                      pl.BlockSpec((None, tr, cols), lambda i, p: (p[0], i, 0)),
                      pl.BlockSpec((3, tr, cols), lambda i, p: (0, i, 0))],
            out_specs=pl.BlockSpec((None, tr, cols), lambda i, p: (p[1], i, 0))),
        out_shape=jax.ShapeDtypeStruct((2, r2, cols), F32),
        compiler_params=_params(("parallel",)),
    )(place, g, s, r)


def _cast_into_slot(name, w, place):
    rows, cols = w.shape
    r2 = rows // 2
    tr = _row_tile(r2, cols)
    nb = r2 // tr

    def body(place_ref, w_ref, o_ref):
        o_ref[...] = w_ref[...].astype(BF16)

    return pl.pallas_call(
        body, name=name,
        grid_spec=pltpu.PrefetchScalarGridSpec(
            num_scalar_prefetch=1, grid=(2, nb),
            in_specs=[pl.BlockSpec((tr, cols), lambda h, i, s: (h * nb + i, 0))],
            out_specs=pl.BlockSpec((None, None, tr, cols), lambda h, i, s: (s[0], h, i, 0))),
        out_shape=jax.ShapeDtypeStruct((N_CHIPS, 2, r2, cols), BF16),
        compiler_params=_params(("parallel", "parallel")),
    )(place, w)


def _colsum(name, a):
    def body(a_ref, o_ref):
        o_ref[...] = jnp.sum(a_ref[...], axis=0, keepdims=True)

    whole = lambda shape: pl.BlockSpec(shape, lambda i: (0, 0))
    return pl.pallas_call(
        body, name=name, grid=(1,), in_specs=[whole(a.shape)], out_specs=whole((1, a.shape[1])),
        out_shape=jax.ShapeDtypeStruct((1, a.shape[1]), F32), compiler_params=_params(("arbitrary",)),
    )(a)


def _adamw(name, w, g, m, v):
    rows, cols = w.shape
    tr = _row_tile(rows, cols, 1024 * 1024)

    def body(w_ref, g_ref, m_ref, v_ref, d_ref, nm_ref, nv_ref):
        g_t = g_ref[...]
        m_t = ADAM_B1 * m_ref[...] + (1.0 - ADAM_B1) * g_t
        v_t = ADAM_B2 * v_ref[...] + (1.0 - ADAM_B2) * jnp.square(g_t)
        m_hat = m_t / (1.0 - ADAM_B1 ** ADAM_STEP)
        v_hat = v_t / (1.0 - ADAM_B2 ** ADAM_STEP)
        d_ref[...] = -ADAM_LR * (m_hat / (jnp.sqrt(v_hat) + ADAM_EPS) + ADAM_WD * w_ref[...])
        nm_ref[...] = m_t
        nv_ref[...] = v_t

    spec = pl.BlockSpec((tr, cols), lambda i: (i, 0))
    shape = jax.ShapeDtypeStruct((rows, cols), F32)
    return pl.pallas_call(
        body, name=name, grid=(rows // tr,), in_specs=[spec] * 4, out_specs=[spec] * 3, out_shape=[shape] * 3,
        compiler_params=_params(("parallel",)),
    )(w, g, m, v)


def _all_reduce_small(packed):
    rows, cols = packed.shape

    def body(in_ref, out_ref, all_ref, send_sems, recv_sems):
        x, y, c = _pos()
        me = 4 * x + 2 * y + c
        all_ref[me] = in_ref[...]
        cps = []
        for r in range(1, N_DEV):
            bx, by, bc = (r >> 2) & 1, (r >> 1) & 1, r & 1
            peer = (1 - x if bx else x, 1 - y if by else y, 1 - c if bc else c)
            cp = pltpu.make_async_remote_copy(
                src_ref=in_ref, dst_ref=all_ref.at[me], send_sem=send_sems.at[r - 1], recv_sem=recv_sems.at[r - 1],
                device_id=peer, device_id_type=MESH)
            cp.start()
            cps.append(cp)
        for cp in cps:
            cp.wait()
        total = all_ref[0]
        for d in range(1, N_DEV):
            total = total + all_ref[d]
        out_ref[...] = total

    vmem = pl.BlockSpec(memory_space=pltpu.VMEM)
    return pl.pallas_call(
        body, name="all_reduce_small", in_specs=[vmem], out_specs=vmem,
        out_shape=jax.ShapeDtypeStruct((rows, cols), F32),
        scratch_shapes=[pltpu.VMEM((N_DEV, rows, cols), F32), pltpu.SemaphoreType.DMA((N_DEV - 1,)),
                        pltpu.SemaphoreType.DMA((N_DEV - 1,))],
        compiler_params=pltpu.CompilerParams(vmem_limit_bytes=VMEM_LIMIT),
    )(packed)


def kernel(x, ln_in_g, ln_in_b, w_in, w_pool, pool_scale, w_out, ln1_g, ln1_b, w_ff1, b_ff1, w_ff2, b_ff2, ln2_g, ln2_b, loss_target, m_ln_in_g, m_ln_in_b, m_w_in, m_w_pool, m_pool_scale, m_w_out, m_ln1_g, m_ln1_b, m_w_ff1, m_b_ff1, m_w_ff2, m_b_ff2, m_ln2_g, m_ln2_b, v_ln_in_g, v_ln_in_b, v_w_in, v_w_pool, v_pool_scale, v_w_out, v_ln1_g, v_ln1_b, v_w_ff1, v_b_ff1, v_w_ff2, v_b_ff2, v_ln2_g, v_ln2_b):
    t, d = x.shape[1], x.shape[2]
    pw = d // 2
    n_grp = len(POOL_WINDOWS)
    gw = pw // n_grp
    gwc = gw // N_CHIPS
    nh = pw // HEAD_DIM
    ff = w_ff1.shape[2] * N_CHIPS
    assert w_in.shape[0] == 1 and w_in.shape[2] * N_CHIPS == 2 * d and gwc <= 128

    x_idx, y_idx, c_idx = _pos()
    chip_arr = jnp.reshape(2 * x_idx + y_idx, (1,)).astype(jnp.int32)
    c_arr = jnp.reshape(c_idx, (1,)).astype(jnp.int32)
    place = jnp.concatenate([chip_arr, c_arr])

    xs = x.reshape(t, d)
    target = loss_target.reshape(t, d)
    row = lambda vec: vec.reshape(1, -1)

    scale_tile = jnp.zeros((1, 8, 128), F32).at[0, :n_grp, :gwc].set(pool_scale[0])
    scale_slots = lax.dynamic_update_slice(jnp.zeros((N_CHIPS, 8, 128), F32), scale_tile, (chip_arr[0], 0, 0))
    shards = dict(w_in=w_in[0], w_out=w_out[0], w_ff1=w_ff1[0], w_ff2=w_ff2[0], w_pool=w_pool[0].reshape(gw, gw))
    slot = {nm: _cast_into_slot("cast_" + nm, w, place) for nm, w in shards.items()}
    unsplit = lambda s: s.reshape(N_CHIPS, 2 * s.shape[2], s.shape[3])

    (h0, h0b, xhat0, rstd0), ((win_s,), (wpool_s,), (scale_g,)) = _ln_fwd(
        "ln_in_fwd", xs, row(ln_in_g), row(ln_in_b),
        phases=[_gather_ici_phase(slot["w_in"]), _gather_ici_phase(slot["w_pool"]), _gather_whole_phase(scale_slots)])
    (win_s,), (wpool_s,) = _comm_call("gather_d2d_first", [_gather_d2d_phase(win_s), _gather_d2d_phase(wpool_s)])
    win_g = unsplit(win_s)
    wpool_full = unsplit(wpool_s).reshape(N_CHIPS, n_grp, gwc, gw).transpose(1, 0, 2, 3).reshape(n_grp, gw, gw)
    scale_full = scale_g[:, :n_grp, :gwc].transpose(1, 0, 2).reshape(n_grp, 1, gw)

    def store_f32(acc, extra_refs, out_refs):
        out_refs[0][...] = acc

    def pool_f32_all_bf16(acc, extra_refs, out_refs):
        @pl.when(pl.program_id(1) == 0)
        def _():
            out_refs[0][...] = acc

        out_refs[1][...] = acc.astype(BF16)

    assert w_in.shape[2] == pw
    (u, ub), ((wout_s,),) = _mm_nn(
        "in_proj", h0b, win_g, (), lambda tm, tn: [],
        [jax.ShapeDtypeStruct((t, pw), F32), jax.ShapeDtypeStruct((t, 2 * d), BF16)],
        lambda tm, tn: [pl.BlockSpec((tm, tn), lambda i, j, kk: (i, 0)), _tile_spec(tm, tn)],
        pool_f32_all_bf16, b_chips=True, tn=pw, phases=[_gather_ici_phase(slot["w_out"])])
    y_pool = _pool_fwd(u, wpool_full, scale_full, t, pw)
    y_sb, ((wff1_s,), (wout_s,)) = _attn_fwd(
        ub, t, nh, min(nh, 4), phases=[_gather_ici_phase(slot["w_ff1"]), _gather_d2d_phase(wout_s)])
    wout_full = unsplit(wout_s).reshape(d, d)
    mix_in = jnp.concatenate([y_pool, y_sb], axis=1)

    def residual(acc, extra_refs, out_refs):
        out_refs[0][...] = ALPHA * extra_refs[0][...] + acc

    (r1,), ((wff1_s,),) = _mm_nn(
        "out_proj", mix_in, wout_full, (h0,), lambda tm, tn: [_tile_spec(tm, tn)],
        [jax.ShapeDtypeStruct((t, d), F32)], lambda tm, tn: [_tile_spec(tm, tn)], residual,
        phases=[_gather_d2d_phase(wff1_s)])
    wff1_g = unsplit(wff1_s)
    h1, h1b, xhat1, rstd1 = _ln_fwd("ln1_fwd", r1, ln1_g, ln1_b)

    def relu_sq(acc, extra_refs, out_refs):
        p = jnp.maximum(acc + extra_refs[0][...], 0.0)
        out_refs[0][...] = p
        out_refs[1][...] = jnp.square(p).astype(BF16)

    (relu_z, act_b), ((wff2_s,),) = _mm_nn(
        "ff1", h1b, wff1_g, (b_ff1,), lambda tm, tn: [_row_spec(tn)],
        [jax.ShapeDtypeStruct((t, ff), F32), jax.ShapeDtypeStruct((t, ff), BF16)],
        lambda tm, tn: [_tile_spec(tm, tn)] * 2, relu_sq, b_chips=True, phases=[_gather_ici_phase(slot["w_ff2"])])
    ((wff2_s,),) = _comm_call("gather_d2d_last", [_gather_d2d_phase(wff2_s)])
    wff2_full = unsplit(wff2_s).reshape(ff, d)

    def residual_bias(acc, extra_refs, out_refs):
        out_refs[0][...] = ALPHA * extra_refs[0][...] + (acc + extra_refs[1][...])

    r2 = _mm_nn("ff2", act_b, wff2_full, (h1, b_ff2), lambda tm, tn: [_tile_spec(tm, tn), _row_spec(tn)],
                [jax.ShapeDtypeStruct((t, d), F32)], lambda tm, tn: [_tile_spec(tm, tn)], residual_bias)[0]

    dr2, dr2b, loss_tile, g_ln2_g, g_ln2_b, g_b_ff2 = _ln2_loss_bwd(r2, target, ln2_g, ln2_b)
    loss = lax.psum(loss_tile[0, 0], ("x", "y", "c"))

    halves = lambda g: g.reshape(N_CHIPS, 2, g.shape[1] // 2, g.shape[2])
    g_ff2 = halves(_mm_tn("grad_w_ff2", act_b, dr2b).reshape(N_CHIPS, ff // N_CHIPS, d))

    def relu_sq_bwd(acc, extra_refs, out_refs):
        dz = acc * (2.0 * extra_refs[0][...])
        out_refs[0][...] = dz.astype(BF16)
        rows = lax.broadcasted_iota(jnp.int32, out_refs[1].shape, 0)
        out_refs[1][...] = jnp.where(rows == 0, jnp.sum(dz, axis=0, keepdims=True), 0.0)

    tm_ff = _tile(t, 1024)
    (dz1b, g_b_ff1_parts), ((s_ff2,),) = _mm_nt(
        "ff2_bwd", dr2b, wff2_full, (relu_z,), lambda tm, tn: [_tile_spec(tm, tn)],
        [jax.ShapeDtypeStruct((t, ff), BF16), jax.ShapeDtypeStruct((8 * (t // tm_ff), ff), F32)],
        lambda tm, tn: [_tile_spec(tm, tn), pl.BlockSpec((8, tn), lambda i, j, kk: (i, j))], relu_sq_bwd,
        phases=[_swap_phase(g_ff2)])
    p_ff2 = _pair_sum("pair_sum_w_ff2", g_ff2, s_ff2, c_arr)
    g_ff1, ((r_ff2,),) = _mm_tn("grad_w_ff1", h1b, dz1b, out_chips=True, phases=[_scatter_phase(p_ff2)])
    g_ff1 = halves(g_ff1)

    def plus_alpha(acc, extra_refs, out_refs):
        out_refs[0][...] = ALPHA * extra_refs[0][...] + acc

    (dh1,), ((s_ff1,),) = _mm_nt(
        "ff1_bwd", dz1b, wff1_g, (dr2,), lambda tm, tn: [_tile_spec(tm, tn)],
        [jax.ShapeDtypeStruct((t, d), F32)], lambda tm, tn: [_tile_spec(tm, tn)], plus_alpha, b_chips=True,
        phases=[_swap_phase(g_ff1)])
    q_ff2 = _chip_sum("chip_sum_w_ff2", g_ff2, s_ff2, r_ff2, place)
    p_ff1 = _pair_sum("pair_sum_w_ff1", g_ff1, s_ff1, c_arr)
    (dr1, dr1b, g_ln1_g, g_ln1_b), ((q_ff2,),) = _ln_bwd("ln1_bwd", dh1, xhat1, rstd1, ln1_g,
                                                         phases=[_assemble_phase(q_ff2)])

    g_out = halves(_mm_tn("grad_w_out", mix_in, dr1b).reshape(N_CHIPS, d // N_CHIPS, d))
    (dmix,), ((s_out,),) = _mm_nt(
        "out_proj_bwd", dr1b, wout_full, (), lambda tm, tn: [], [jax.ShapeDtypeStruct((t, d), F32)],
        lambda tm, tn: [_tile_spec(tm, tn)], store_f32, phases=[_swap_phase(g_out)])
    p_out = _pair_sum("pair_sum_w_out", g_out, s_out, c_arr)
    du_pool, g_w_pool_full, g_scale_full = _pool_bwd(u, dmix, wpool_full, scale_full, t, pw)
    (dq, dk, dv), ((r_ff1,), (r_out,)) = _attn_bwd(ub, dmix, t, nh, min(nh, 2),
                                                   phases=[_scatter_phase(p_ff1), _scatter_phase(p_out)])
    q_ff1 = _chip_sum("chip_sum_w_ff1", g_ff1, s_ff1, r_ff1, place)
    q_out = _chip_sum("chip_sum_w_out", g_out, s_out, r_out, place)
    du = jnp.concatenate([du_pool, dq, dk.astype(BF16), dv.astype(BF16)], axis=1)
    g_in, ((q_ff1,), (q_out,)) = _mm_tn("grad_w_in", h0b, du, out_chips=True,
                                        phases=[_assemble_phase(q_ff1), _assemble_phase(q_out)])
    g_in = halves(g_in)
    g_pool = halves(g_w_pool_full.reshape(n_grp, N_CHIPS, gwc, gw).transpose(1, 0, 2, 3).reshape(N_CHIPS, gw, gw))
    (dh0,), ((s_in,), (s_pool,)) = _mm_nt(
        "in_proj_bwd", du, win_g, (dr1,), lambda tm, tn: [_tile_spec(tm, tn)],
        [jax.ShapeDtypeStruct((t, d), F32)], lambda tm, tn: [_tile_spec(tm, tn)], plus_alpha, b_chips=True,
        phases=[_swap_phase(g_in), _swap_phase(g_pool)])
    p_in = _pair_sum("pair_sum_w_in", g_in, s_in, c_arr)
    p_pool = _pair_sum("pair_sum_w_pool", g_pool, s_pool, c_arr)
    (dx, _, g_ln_in_g, g_ln_in_b), ((r_in,), (r_pool,)) = _ln_bwd(
        "ln_in_bwd", dh0, xhat0, rstd0, row(ln_in_g), phases=[_scatter_phase(p_in), _scatter_phase(p_pool)])
    q_in = _chip_sum("chip_sum_w_in", g_in, s_in, r_in, place)
    q_pool = _chip_sum("chip_sum_w_pool", g_pool, s_pool, r_pool, place)
    (q_in,), (q_pool,) = _comm_call("rs_assemble_last", [_assemble_phase(q_in), _assemble_phase(q_pool)])

    big = {}
    for nm, q, w, m, v in [("w_in", q_in, w_in, m_w_in, v_w_in), ("w_out", q_out, w_out, m_w_out, v_w_out),
                           ("w_ff1", q_ff1, w_ff1, m_w_ff1, v_w_ff1), ("w_ff2", q_ff2, w_ff2, m_w_ff2, v_w_ff2),
                           ("w_pool", q_pool, w_pool, m_w_pool, v_w_pool)]:
        g = q.reshape(2 * q.shape[1], q.shape[2])
        flat = lambda arr: arr.reshape(g.shape)
        delta, new_m, new_v = _adamw("adamw_" + nm, flat(w), g, flat(m), flat(v))
        big[nm] = tuple(arr.reshape(w.shape) for arr in (g, delta, new_m, new_v))

    lane = 2048 if d % 2048 == 0 else d
    small_names = ["ln_in_g", "ln_in_b", "ln1_g", "ln1_b", "b_ff1", "b_ff2", "ln2_g", "ln2_b"]
    small_w = dict(ln_in_g=ln_in_g, ln_in_b=ln_in_b, ln1_g=ln1_g, ln1_b=ln1_b, b_ff1=b_ff1, b_ff2=b_ff2, ln2_g=ln2_g,
                   ln2_b=ln2_b)
    small_m = dict(ln_in_g=m_ln_in_g, ln_in_b=m_ln_in_b, ln1_g=m_ln1_g, ln1_b=m_ln1_b, b_ff1=m_b_ff1, b_ff2=m_b_ff2,
                   ln2_g=m_ln2_g, ln2_b=m_ln2_b)
    small_v = dict(ln_in_g=v_ln_in_g, ln_in_b=v_ln_in_b, ln1_g=v_ln1_g, ln1_b=v_ln1_b, b_ff1=v_b_ff1, b_ff2=v_b_ff2,
                   ln2_g=v_ln2_g, ln2_b=v_ln2_b)
    small_g = dict(ln_in_g=g_ln_in_g, ln_in_b=g_ln_in_b, ln1_g=g_ln1_g, ln1_b=g_ln1_b, b_ff2=g_b_ff2, ln2_g=g_ln2_g,
                   ln2_b=g_ln2_b)

    def pack(parts):
        flat = jnp.concatenate([p.reshape(-1) for p in parts])
        n_rows = -(-flat.shape[0] // lane)
        n_rows = -(-n_rows // 8) * 8
        return jnp.pad(flat, (0, n_rows * lane - flat.shape[0])).reshape(n_rows, lane)

    small_g["b_ff1"] = _colsum("b_ff1_colsum", g_b_ff1_parts)
    summed = _all_reduce_small(pack([small_g[nm] for nm in small_names] + [g_scale_full])).reshape(-1)

    g_small, off = {}, 0
    for nm in small_names:
        g_small[nm] = summed[off:off + small_w[nm].size]
        off += small_w[nm].size
    g_scale_all = summed[off:off + n_grp * gw].reshape(n_grp, N_CHIPS, gwc)
    g_scale = lax.dynamic_index_in_dim(g_scale_all, chip_arr[0], axis=1, keepdims=False)

    order = small_names + ["pool_scale"]
    small_w["pool_scale"], small_m["pool_scale"], small_v["pool_scale"] = pool_scale, m_pool_scale, v_pool_scale
    g_small["pool_scale"] = g_scale
    delta_s, new_m_s, new_v_s = _adamw("adamw_small", pack([small_w[nm] for nm in order]),
                                       pack([g_small[nm] for nm in order]), pack([small_m[nm] for nm in order]),
                                       pack([small_v[nm] for nm in order]))
    small = {}
    off = 0
    for nm in order:
        size, shape = small_w[nm].size, small_w[nm].shape
        cut = lambda arr: arr.reshape(-1)[off:off + size].reshape(shape)
        small[nm] = (g_small[nm].reshape(shape), cut(delta_s), cut(new_m_s), cut(new_v_s))
        off += size

    every = {**big, **small}
    weight_order = ["ln_in_g", "ln_in_b", "w_in", "w_pool", "pool_scale", "w_out", "ln1_g", "ln1_b", "w_ff1", "b_ff1",
                    "w_ff2", "b_ff2", "ln2_g", "ln2_b"]
    grads = [every[nm][0] for nm in weight_order]
    deltas = [every[nm][1] for nm in weight_order]
    new_ms = [every[nm][2] for nm in weight_order]
    new_vs = [every[nm][3] for nm in weight_order]
    return (loss, dx.reshape(x.shape), *grads, *deltas, *new_ms, *new_vs)
```

```python
import functools

import jax
import jax.numpy as jnp
from jax import lax
from jax.experimental import pallas as pl
from jax.experimental.pallas import tpu as pltpu

F32 = jnp.float32
BF16 = jnp.bfloat16
MESH = pl.DeviceIdType.MESH

HEAD_DIM = 128
POOL_WINDOWS = (2, 4, 8, 16)
POOL_HALO = 16
LN_EPS = 1e-5
ALPHA = 2.0 ** 0.25
ADAM_LR, ADAM_B1, ADAM_B2, ADAM_EPS, ADAM_WD, ADAM_STEP = 0.001, 0.9, 0.999, 1e-08, 0.01, 10

QB = 256
KB = 256
VMEM_LIMIT = 56 * 1024 * 1024
N_CHIPS = 4
N_DEV = 8


def _params(sem=None):
    return pltpu.CompilerParams(dimension_semantics=sem, vmem_limit_bytes=VMEM_LIMIT)


def _tile(dim, pref):
    return pref if dim % pref == 0 else dim


def _pos():
    return lax.axis_index("x"), lax.axis_index("y"), lax.axis_index("c")


def _other_chips(x, y):
    return [(1 - x, y), (x, 1 - y), (1 - x, 1 - y)]


ANY = pl.BlockSpec(memory_space=pl.ANY)


class _Phase:
    def __init__(self, ins, out_shapes, aliases, n_sems, build):
        self.ins, self.out_shapes, self.aliases, self.n_sems, self.build = ins, out_shapes, aliases, n_sems, build


def _remote(src, dst, send_sems, recv_sems, k, to):
    return pltpu.make_async_remote_copy(src_ref=src, dst_ref=dst, send_sem=send_sems.at[k], recv_sem=recv_sems.at[k],
                                        device_id=to, device_id_type=MESH)


def _swap_phase(g):
    def build(ins, outs, ss, rs):
        x, y, c = _pos()
        cp = _remote(ins[0].at[:, 1 - c], outs[0], ss, rs, 0, (x, y, 1 - c))
        return [cp], [cp]

    return _Phase([g], [jax.ShapeDtypeStruct((N_CHIPS, g.shape[2], g.shape[3]), g.dtype)], {}, 1, build)


def _scatter_phase(p):
    def build(ins, outs, ss, rs):
        x, y, c = _pos()
        cps = [_remote(ins[0].at[2 * chip[0] + chip[1]], outs[0].at[j], ss, rs, j, (*chip, c))
               for j, chip in enumerate(_other_chips(x, y))]
        return cps, cps

    return _Phase([p], [jax.ShapeDtypeStruct((3, p.shape[1], p.shape[2]), p.dtype)], {}, 3, build)


def _assemble_phase(q):
    def build(ins, outs, ss, rs):
        x, y, c = _pos()
        mine, other = outs[0].at[c], outs[0].at[1 - c]
        return [_remote(mine, mine, ss, rs, 0, (x, y, 1 - c))], [_remote(other, other, ss, rs, 0, (x, y, c))]

    return _Phase([q], [jax.ShapeDtypeStruct(q.shape, q.dtype)], {0: 0}, 1, build)


def _gather_ici_phase(slot):
    def build(ins, outs, ss, rs):
        x, y, c = _pos()
        mine = outs[0].at[2 * x + y, c]
        sends, recvs = [], []
        for j, chip in enumerate(_other_chips(x, y)):
            theirs = outs[0].at[2 * chip[0] + chip[1], c]
            sends.append(_remote(mine, mine, ss, rs, j, (*chip, c)))
            recvs.append(_remote(theirs, theirs, ss, rs, j, (x, y, c)))
        return sends, recvs

    return _Phase([slot], [jax.ShapeDtypeStruct(slot.shape, slot.dtype)], {0: 0}, 3, build)


def _gather_d2d_phase(slot):
    def build(ins, outs, ss, rs):
        x, y, c = _pos()
        sends, recvs = [], []
        for j, chip in enumerate(_other_chips(x, y)):
            landed = outs[0].at[2 * chip[0] + chip[1], c]
            coming = outs[0].at[2 * chip[0] + chip[1], 1 - c]
            sends.append(_remote(landed, landed, ss, rs, j, (x, y, 1 - c)))
            recvs.append(_remote(coming, coming, ss, rs, j, (x, y, c)))
        return sends, recvs

    return _Phase([slot], [jax.ShapeDtypeStruct(slot.shape, slot.dtype)], {0: 0}, 3, build)


def _gather_whole_phase(slots):
    def build(ins, outs, ss, rs):
        x, y, c = _pos()
        mine = outs[0].at[2 * x + y]
        sends, recvs = [], []
        for j, chip in enumerate(_other_chips(x, y)):
            theirs = outs[0].at[2 * chip[0] + chip[1]]
            sends.append(_remote(mine, mine, ss, rs, j, (*chip, c)))
            recvs.append(_remote(theirs, theirs, ss, rs, j, (x, y, c)))
        return sends, recvs

    return _Phase([slots], [jax.ShapeDtypeStruct(slots.shape, slots.dtype)], {0: 0}, 3, build)


def _split_refs(refs, n_in, n_out, n_scratch, phases):
    n_pin = sum(len(ph.ins) for ph in phases)
    n_pout = sum(len(ph.out_shapes) for ph in phases)
    cuts = [n_in, n_pin, n_out, n_pout, n_scratch]
    parts, at = [], 0
    for n in cuts:
        parts.append(refs[at:at + n])
        at += n
    parts.append(refs[at:])
    return parts


def _build_phases(phases, pin, pout, sems):
    built, i, o = [], 0, 0
    for k, ph in enumerate(phases):
        built.append(ph.build(pin[i:i + len(ph.ins)], pout[o:o + len(ph.out_shapes)], sems[2 * k], sems[2 * k + 1]))
        i += len(ph.ins)
        o += len(ph.out_shapes)
    return built


def _finish_phases(built):
    for _, recvs in built:
        for cp in recvs:
            cp.wait_recv()
    for sends, _ in built:
        for cp in sends:
            cp.wait_send()


def _call(body, name, grid, in_specs, out_specs, out_shape, scratch_shapes, semantics, args, phases=()):
    n_in, n_out, n_scratch = len(args), len(out_shape), len(scratch_shapes)
    aliases, in_at, out_at = {}, n_in, n_out
    for ph in phases:
        aliases.update({in_at + i: out_at + o for i, o in ph.aliases.items()})
        in_at += len(ph.ins)
        out_at += len(ph.out_shapes)

    def hosted(*refs):
        ins, pin, outs, pout, scratch, sems = _split_refs(refs, n_in, n_out, n_scratch, phases)
        ids = [pl.program_id(a) for a in range(len(grid))]
        first = functools.reduce(jnp.logical_and, [i == 0 for i in ids])
        last = functools.reduce(jnp.logical_and, [i == g - 1 for i, g in zip(ids, grid)])

        @pl.when(first)
        def _():
            for sends, _ in _build_phases(phases, pin, pout, sems):
                for cp in sends:
                    cp.start()

        body(*ins, *outs, *scratch)

        @pl.when(last)
        def _():
            _finish_phases(_build_phases(phases, pin, pout, sems))

    p_args = [a for ph in phases for a in ph.ins]
    p_shapes = [s for ph in phases for s in ph.out_shapes]
    sem_shapes = [pltpu.SemaphoreType.DMA((ph.n_sems,)) for ph in phases for _ in range(2)]
    outs = pl.pallas_call(
        hosted if phases else body, name=name, grid=grid, in_specs=[*in_specs, *[ANY] * len(p_args)],
        out_specs=[*out_specs, *[ANY] * len(p_shapes)], out_shape=[*out_shape, *p_shapes],
        input_output_aliases=aliases, scratch_shapes=[*scratch_shapes, *sem_shapes],
        compiler_params=_params(("arbitrary",) * len(grid) if phases else semantics),
    )(*args, *p_args)
    phase_outs, at = [], n_out
    for ph in phases:
        phase_outs.append(list(outs[at:at + len(ph.out_shapes)]))
        at += len(ph.out_shapes)
    return list(outs[:n_out]), phase_outs


def _comm_call(name, phases):
    def body(*refs):
        _, pin, _, pout, _, sems = _split_refs(refs, 0, 0, 0, phases)
        built = _build_phases(phases, pin, pout, sems)
        for sends, _ in built:
            for cp in sends:
                cp.start()
        _finish_phases(built)

    aliases, in_at, out_at = {}, 0, 0
    for ph in phases:
        aliases.update({in_at + i: out_at + o for i, o in ph.aliases.items()})
        in_at += len(ph.ins)
        out_at += len(ph.out_shapes)
    p_args = [a for ph in phases for a in ph.ins]
    p_shapes = [s for ph in phases for s in ph.out_shapes]
    outs = pl.pallas_call(
        body, name=name, in_specs=[ANY] * len(p_args), out_specs=[ANY] * len(p_shapes), out_shape=p_shapes,
        input_output_aliases=aliases,
        scratch_shapes=[pltpu.SemaphoreType.DMA((ph.n_sems,)) for ph in phases for _ in range(2)],
    )(*p_args)
    phase_outs, at = [], 0
    for ph in phases:
        phase_outs.append(list(outs[at:at + len(ph.out_shapes)]))
        at += len(ph.out_shapes)
    return phase_outs


def _matmul(name, a, b, grid, a_spec, b_spec, contract, acc_shape, extras, extra_specs, out_shape, out_specs,
            epilogue, phases=()):
    n_extra, n_out, gk = len(extras), len(out_shape), grid[2]

    def product(a_ref, b_ref):
        return lax.dot_general(a_ref[...], b_ref[...], (contract, ((), ())), preferred_element_type=F32)

    def body_one_step(*refs):
        epilogue(product(refs[0], refs[1]), refs[2:2 + n_extra], refs[2 + n_extra:])

    def body(*refs):
        a_ref, b_ref = refs[0], refs[1]
        extra_refs = refs[2:2 + n_extra]
        out_refs = refs[2 + n_extra:2 + n_extra + n_out]
        acc_ref = refs[-1]
        kk = pl.program_id(2)

        @pl.when(kk == 0)
        def _():
            acc_ref[...] = product(a_ref, b_ref)

        @pl.when(kk > 0)
        def _():
            acc_ref[...] += product(a_ref, b_ref)

        @pl.when(kk == gk - 1)
        def _():
            epilogue(acc_ref[...], extra_refs, out_refs)

    outs, phase_outs = _call(
        body_one_step if gk == 1 else body, name, grid, [a_spec, b_spec, *extra_specs], out_specs, out_shape,
        [] if gk == 1 else [pltpu.VMEM(acc_shape, F32)], ("parallel", "arbitrary", "arbitrary"), (a, b, *extras),
        phases)
    return (outs, phase_outs) if phases else outs


def _mm_nn(name, a, b, extras, extra_specs, out_shape, out_specs, epilogue, b_chips=False, tm=1024, tn=1024,
           tk=2048, phases=()):
    m, k = a.shape
    n = b.shape[1] if not b_chips else b.shape[2] * N_CHIPS
    tm, tk = _tile(m, tm), _tile(k, tk)
    if b_chips:
        tn = _tile(b.shape[2], tn)
        nb = b.shape[2] // tn
        b_spec = pl.BlockSpec((None, tk, tn), lambda i, j, kk: (j // nb, kk, j % nb))
    else:
        tn = _tile(n, tn)
        b_spec = pl.BlockSpec((tk, tn), lambda i, j, kk: (kk, j))
    a_spec = pl.BlockSpec((tm, tk), lambda i, j, kk: (i, kk))
    return _matmul(name, a, b, (m // tm, n // tn, k // tk), a_spec, b_spec, ((1,), (0,)), (tm, tn), extras,
                   extra_specs(tm, tn), out_shape, out_specs(tm, tn), epilogue, phases)


def _mm_nt(name, a, b, extras, extra_specs, out_shape, out_specs, epilogue, b_chips=False, tm=1024, tn=1024,
           tk=2048, phases=()):
    m, k = a.shape
    n = b.shape[0] if not b_chips else b.shape[1]
    tm, tn = _tile(m, tm), _tile(n, tn)
    if b_chips:
        tk = _tile(b.shape[2], tk)
        nb = b.shape[2] // tk
        b_spec = pl.BlockSpec((None, tn, tk), lambda i, j, kk: (kk // nb, j, kk % nb))
    else:
        tk = _tile(k, tk)
        b_spec = pl.BlockSpec((tn, tk), lambda i, j, kk: (j, kk))
    a_spec = pl.BlockSpec((tm, tk), lambda i, j, kk: (i, kk))
    return _matmul(name, a, b, (m // tm, n // tn, k // tk), a_spec, b_spec, ((1,), (1,)), (tm, tn), extras,
                   extra_specs(tm, tn), out_shape, out_specs(tm, tn), epilogue, phases)


def _mm_tn(name, a, b, out_chips=False, tm=1024, tn=1024, tk=2048, phases=()):
    k, m = a.shape
    n = b.shape[1]
    tm, tk = _tile(m, tm), _tile(k, tk)
    a_spec = pl.BlockSpec((tk, tm), lambda i, j, kk: (kk, i))
    if out_chips:
        nc = n // N_CHIPS
        tn = _tile(nc, tn)
        nb = nc // tn
        out_shape = [jax.ShapeDtypeStruct((N_CHIPS, m, nc), F32)]
        out_specs = [pl.BlockSpec((None, tm, tn), lambda i, j, kk: (j // nb, i, j % nb))]
    else:
        tn = _tile(n, tn)
        out_shape = [jax.ShapeDtypeStruct((m, n), F32)]
        out_specs = [pl.BlockSpec((tm, tn), lambda i, j, kk: (i, j))]
    b_spec = pl.BlockSpec((tk, tn), lambda i, j, kk: (kk, j))

    def epilogue(acc, extra_refs, out_refs):
        out_refs[0][...] = acc

    res = _matmul(name, a, b, (m // tm, n // tn, k // tk), a_spec, b_spec, ((0,), (0,)), (tm, tn), (), [],
                  out_shape, out_specs, epilogue, phases)
    return (res[0][0], res[1]) if phases else res[0]


def _tile_spec(tm, tn):
    return pl.BlockSpec((tm, tn), lambda i, j, kk: (i, j))


def _row_spec(tn):
    return pl.BlockSpec((1, tn), lambda i, j, kk: (0, j))


def _ln_stats(r):
    mu = jnp.mean(r, axis=-1, keepdims=True)
    var = jnp.mean(jnp.square(r - mu), axis=-1, keepdims=True)
    rstd = lax.rsqrt(var + LN_EPS)
    return (r - mu) * rstd, rstd


def _ln_fwd(name, r, g, b, tr=256, phases=()):
    t, d = r.shape
    tr = _tile(t, tr)

    def body(r_ref, g_ref, b_ref, y_ref, yb_ref, xhat_ref, rstd_ref):
        xhat, rstd = _ln_stats(r_ref[...])
        y = xhat * g_ref[...] + b_ref[...]
        y_ref[...] = y
        yb_ref[...] = y.astype(BF16)
        xhat_ref[...] = xhat
        rstd_ref[...] = rstd

    row = pl.BlockSpec((tr, d), lambda i: (i, 0))
    vec = pl.BlockSpec((1, d), lambda i: (0, 0))
    outs, phase_outs = _call(
        body, name, (t // tr,), [row, vec, vec], [row, row, row, pl.BlockSpec((tr, 1), lambda i: (i, 0))],
        [jax.ShapeDtypeStruct((t, d), F32), jax.ShapeDtypeStruct((t, d), BF16),
         jax.ShapeDtypeStruct((t, d), F32), jax.ShapeDtypeStruct((t, 1), F32)], [], ("parallel",), (r, g, b), phases)
    return (outs, phase_outs) if phases else outs


def _ln_bwd_rows(dy, xhat, rstd, g):
    dxhat = dy * g
    m1 = jnp.mean(dxhat, axis=-1, keepdims=True)
    m2 = jnp.mean(dxhat * xhat, axis=-1, keepdims=True)
    return rstd * (dxhat - m1 - xhat * m2)


def _ln_bwd(name, dy, xhat, rstd, g, tr=256, phases=()):
    t, d = dy.shape
    tr = _tile(t, tr)

    def body(dy_ref, xhat_ref, rstd_ref, g_ref, dr_ref, drb_ref, dg_ref, db_ref):
        @pl.when(pl.program_id(0) == 0)
        def _():
            dg_ref[...] = jnp.zeros_like(dg_ref)
            db_ref[...] = jnp.zeros_like(db_ref)

        dy_t, xhat_t = dy_ref[...], xhat_ref[...]
        dr = _ln_bwd_rows(dy_t, xhat_t, rstd_ref[...], g_ref[...])
        dr_ref[...] = dr
        drb_ref[...] = dr.astype(BF16)
        dg_ref[...] += jnp.sum(dy_t * xhat_t, axis=0, keepdims=True)
        db_ref[...] += jnp.sum(dy_t, axis=0, keepdims=True)

    row = pl.BlockSpec((tr, d), lambda i: (i, 0))
    vec = pl.BlockSpec((1, d), lambda i: (0, 0))
    outs, phase_outs = _call(
        body, name, (t // tr,), [row, row, pl.BlockSpec((tr, 1), lambda i: (i, 0)), vec], [row, row, vec, vec],
        [jax.ShapeDtypeStruct((t, d), F32), jax.ShapeDtypeStruct((t, d), BF16),
         jax.ShapeDtypeStruct((1, d), F32), jax.ShapeDtypeStruct((1, d), F32)], [], ("arbitrary",),
        (dy, xhat, rstd, g), phases)
    return (outs, phase_outs) if phases else outs


def _ln2_loss_bwd(r2, target, g, b, tr=256):
    t, d = r2.shape
    tr = _tile(t, tr)

    def body(r_ref, t_ref, g_ref, b_ref, dr_ref, drb_ref, loss_ref, dg_ref, db_ref, dsum_ref):
        @pl.when(pl.program_id(0) == 0)
        def _():
            loss_ref[...] = jnp.zeros_like(loss_ref)
            dg_ref[...] = jnp.zeros_like(dg_ref)
            db_ref[...] = jnp.zeros_like(db_ref)
            dsum_ref[...] = jnp.zeros_like(dsum_ref)

        xhat, rstd = _ln_stats(r_ref[...])
        g_t = g_ref[...]
        err = xhat * g_t + b_ref[...] - t_ref[...]
        loss_ref[...] += 0.5 * jnp.sum(jnp.mean(jnp.square(err), axis=-1, keepdims=True), axis=0, keepdims=True)
        dy = err * (1.0 / d)
        dr = _ln_bwd_rows(dy, xhat, rstd, g_t)
        dr_ref[...] = dr
        drb_ref[...] = dr.astype(BF16)
        dg_ref[...] += jnp.sum(dy * xhat, axis=0, keepdims=True)
        db_ref[...] += jnp.sum(dy, axis=0, keepdims=True)
        dsum_ref[...] += jnp.sum(dr, axis=0, keepdims=True)

    row = pl.BlockSpec((tr, d), lambda i: (i, 0))
    vec = pl.BlockSpec((1, d), lambda i: (0, 0))
    return pl.pallas_call(
        body, name="ln2_loss_bwd", grid=(t // tr,), in_specs=[row, row, vec, vec],
        out_specs=[row, row, pl.BlockSpec((8, 128), lambda i: (0, 0)), vec, vec, vec],
        out_shape=[jax.ShapeDtypeStruct((t, d), F32), jax.ShapeDtypeStruct((t, d), BF16),
                   jax.ShapeDtypeStruct((8, 128), F32), jax.ShapeDtypeStruct((1, d), F32),
                   jax.ShapeDtypeStruct((1, d), F32), jax.ShapeDtypeStruct((1, d), F32)],
        compiler_params=_params(("arbitrary",)),
    )(r2, target, g, b)


POOL_ROWS = 512


def _pool_mean_minus_token(u_ref, r0, rows, grp, first):
    width = u_ref.shape[1]
    body = u_ref[pl.ds(r0, rows), :]
    halo = u_ref[pl.ds(pl.multiple_of(jnp.maximum(r0 - POOL_HALO, 0), POOL_HALO), POOL_HALO), :]
    halo = jnp.where(first, 0.0, halo)
    full = jnp.concatenate([halo, body], axis=0)
    s = full
    for step in range(len(POOL_WINDOWS)):
        shifted = pltpu.roll(s, 1 << step, axis=0)
        s = s + jnp.where(grp >= step, shifted, 0.0)
    s = s[POOL_HALO:, :]
    tpos = r0 + lax.broadcasted_iota(jnp.int32, (rows, width), 0)
    count = jnp.minimum(tpos + 1, 2 << grp).astype(F32)
    return s / count - body, count


def _pool_fwd(u, w_pool, pool_scale, t, pw):
    gw = pw // len(POOL_WINDOWS)
    rows = _tile(t, POOL_ROWS)

    def body(u_ref, w_ref, s_ref, o_ref):
        grp = pl.program_id(0)

        def chunk(ci, carry):
            r0 = pl.multiple_of(ci * rows, rows)
            y, _ = _pool_mean_minus_token(u_ref, r0, rows, grp, ci == 0)
            yw = jnp.dot(y.astype(BF16), w_ref[...], preferred_element_type=F32)
            o_ref[pl.ds(r0, rows), :] = (yw * s_ref[...]).astype(BF16)
            return carry

        lax.fori_loop(0, t // rows, chunk, 0)

    return pl.pallas_call(
        body, name="pool_fwd", grid=(len(POOL_WINDOWS),),
        in_specs=[pl.BlockSpec((t, gw), lambda g: (0, g)), pl.BlockSpec((None, gw, gw), lambda g: (g, 0, 0)),
                  pl.BlockSpec((None, 1, gw), lambda g: (g, 0, 0))],
        out_specs=pl.BlockSpec((t, gw), lambda g: (0, g)),
        out_shape=jax.ShapeDtypeStruct((t, pw), BF16),
        compiler_params=_params(("parallel",)),
    )(u, w_pool, pool_scale)


def _pool_bwd(u, dmix, w_pool, pool_scale, t, pw):
    n_grp = len(POOL_WINDOWS)
    gw = pw // n_grp
    rows = _tile(t, POOL_ROWS)

    def body(u_ref, dm_ref, w_ref, s_ref, du_ref, dw_ref, ds_ref, e_ref):
        grp = pl.program_id(0)
        dw_ref[...] = jnp.zeros_like(dw_ref)
        ds_ref[...] = jnp.zeros_like(ds_ref)
        e_ref[pl.ds(t, POOL_HALO), :] = jnp.zeros((POOL_HALO, gw), F32)

        def chunk(ci, carry):
            r0 = pl.multiple_of(ci * rows, rows)
            y, count = _pool_mean_minus_token(u_ref, r0, rows, grp, ci == 0)
            yb = y.astype(BF16)
            yw = jnp.dot(yb, w_ref[...], preferred_element_type=F32)
            dy2 = dm_ref[pl.ds(r0, rows), :]
            ds_ref[...] += jnp.sum(dy2 * yw, axis=0, keepdims=True)
            dyw = (dy2 * s_ref[...]).astype(BF16)
            dw_ref[...] += lax.dot_general(yb, dyw, (((0,), (0,)), ((), ())), preferred_element_type=F32)
            dy = lax.dot_general(dyw, w_ref[...], (((1,), (1,)), ((), ())), preferred_element_type=F32)
            e_ref[pl.ds(r0, rows), :] = dy / count
            return carry

        lax.fori_loop(0, t // rows, chunk, 0)

        def chunk2(ci, carry):
            r0 = pl.multiple_of(ci * rows, rows)
            full = e_ref[pl.ds(r0, rows + POOL_HALO), :]
            s = full
            for step in range(n_grp):
                shifted = pltpu.roll(s, rows + POOL_HALO - (1 << step), axis=0)
                s = s + jnp.where(grp >= step, shifted, 0.0)
            e = full[:rows, :]
            tpos = r0 + lax.broadcasted_iota(jnp.int32, (rows, gw), 0)
            count = jnp.minimum(tpos + 1, 2 << grp).astype(F32)
            du_ref[pl.ds(r0, rows), :] = (s[:rows, :] - e * count).astype(BF16)
            return carry

        lax.fori_loop(0, t // rows, chunk2, 0)

    return pl.pallas_call(
        body, name="pool_bwd", grid=(n_grp,),
        in_specs=[pl.BlockSpec((t, gw), lambda g: (0, g)), pl.BlockSpec((t, gw), lambda g: (0, g)),
                  pl.BlockSpec((None, gw, gw), lambda g: (g, 0, 0)),
                  pl.BlockSpec((None, 1, gw), lambda g: (g, 0, 0))],
        out_specs=[pl.BlockSpec((t, gw), lambda g: (0, g)), pl.BlockSpec((None, gw, gw), lambda g: (g, 0, 0)),
                   pl.BlockSpec((None, 1, gw), lambda g: (g, 0, 0))],
        out_shape=[jax.ShapeDtypeStruct((t, pw), BF16), jax.ShapeDtypeStruct((n_grp, gw, gw), F32),
                   jax.ShapeDtypeStruct((n_grp, 1, gw), F32)],
        scratch_shapes=[pltpu.VMEM((t + POOL_HALO, gw), F32)],
        compiler_params=_params(("parallel",)),
    )(u, dmix, w_pool, pool_scale)


def _sb_scores(q, k_blk, scale, mask):
    z = lax.dot_general(q, k_blk, (((1,), (1,)), ((), ())), preferred_element_type=F32) * scale
    tneg = jnp.exp(-jnp.abs(z))
    log_not = -(jnp.maximum(z, 0.0) + jnp.log1p(tneg))
    return z, jnp.where(mask, log_not, 0.0), tneg


EXP_IS_ZERO_BELOW = -104.0


def _weights_alive(after):
    return (jnp.max(after) >= EXP_IS_ZERO_BELOW).astype(jnp.int32)


def _split_dot(vs, tri):
    parts = []
    for v in vs:
        hi = v.astype(BF16)
        parts += [hi, (v - hi.astype(F32)).astype(BF16)]
    prod = jnp.dot(jnp.concatenate(parts, axis=0), tri, preferred_element_type=F32)
    m = vs[0].shape[0]
    return [prod[2 * k * m:(2 * k + 1) * m] + prod[(2 * k + 1) * m:(2 * k + 2) * m] for k in range(len(vs))]


def _head(ref, h, rows=None):
    cols = slice(h * HEAD_DIM, (h + 1) * HEAD_DIM)
    return ref[:, cols] if rows is None else ref[rows, cols]


def _attn_fwd(ub, t, nh, hg, phases=()):
    scale = float(1.0 / (HEAD_DIM ** 0.5))
    ng = nh // hg

    def body(q_ref, k_ref, v_ref, o_ref):
        i = pl.program_id(1)
        row = lax.broadcasted_iota(jnp.int32, (QB, KB), 0)
        col = lax.broadcasted_iota(jnp.int32, (QB, KB), 1)
        suffix = (row >= col).astype(BF16)

        def more(carry):
            return jnp.logical_and(carry[0] <= i, carry[3] > 0)

        def step(carry):
            n, accs, afters, _ = carry
            rows = pl.ds(pl.multiple_of((i - n) * KB, KB), KB)
            mask = jnp.logical_or(n > 0, col < row)
            new_accs, new_afters = [], []
            scores = [_sb_scores(_head(q_ref, h), _head(k_ref, h, rows), scale, mask) for h in range(hg)]
            withins = _split_dot([log_not for _, log_not, _ in scores], suffix)
            for h in range(hg):
                z, log_not, _ = scores[h]
                a = jnp.where(mask, jnp.exp(z + withins[h] + afters[h]), 0.0)
                new_accs.append(accs[h] + jnp.dot(a.astype(BF16), _head(v_ref, h, rows),
                                                  preferred_element_type=F32))
                new_afters.append(afters[h] + jnp.sum(log_not, axis=1, keepdims=True))
            return n + 1, tuple(new_accs), tuple(new_afters), _weights_alive(functools.reduce(jnp.maximum, new_afters))

        init = (jnp.int32(0), tuple(jnp.zeros((QB, HEAD_DIM), F32) for _ in range(hg)),
                tuple(jnp.zeros((QB, 1), F32) for _ in range(hg)), jnp.int32(1))
        _, accs, _, _ = lax.while_loop(more, step, init)
        for h in range(hg):
            o_ref[:, h * HEAD_DIM:(h + 1) * HEAD_DIM] = accs[h].astype(BF16)

    wide = hg * HEAD_DIM
    outs, phase_outs = _call(
        body, "attn_fwd", (ng, t // QB),
        [pl.BlockSpec((QB, wide), lambda g, i: (i, ng + g)), pl.BlockSpec((t, wide), lambda g, i: (0, 2 * ng + g)),
         pl.BlockSpec((t, wide), lambda g, i: (0, 3 * ng + g))],
        [pl.BlockSpec((QB, wide), lambda g, i: (i, g))], [jax.ShapeDtypeStruct((t, nh * HEAD_DIM), BF16)], [],
        ("parallel", "arbitrary"), (ub, ub, ub), phases)
    return outs[0], phase_outs


def _attn_bwd(ub, dmix, t, nh, hg, phases=()):
    scale = float(1.0 / (HEAD_DIM ** 0.5))
    width = nh * HEAD_DIM
    ng = nh // hg

    def body(q_ref, k_ref, v_ref, do_ref, dq_ref, dk_ref, dv_ref, g_ref, z_ref):
        i = pl.program_id(1)

        @pl.when(i == 0)
        def _():
            dk_ref[...] = jnp.zeros_like(dk_ref)
            dv_ref[...] = jnp.zeros_like(dv_ref)

        row = lax.broadcasted_iota(jnp.int32, (QB, KB), 0)
        col = lax.broadcasted_iota(jnp.int32, (QB, KB), 1)
        suffix = (row >= col).astype(BF16)
        prefix = (row <= col).astype(BF16)

        def more(carry):
            return jnp.logical_and(carry[0] <= i, carry[2] > 0)

        def down(carry):
            n, afters, _ = carry
            ks = pl.multiple_of((i - n) * KB, KB)
            rows = pl.ds(ks, KB)
            mask = jnp.logical_or(n > 0, col < row)
            new_afters = []
            scores = [_sb_scores(_head(q_ref, h), _head(k_ref, h, rows), scale, mask) for h in range(hg)]
            withins = _split_dot([log_not for _, log_not, _ in scores], suffix)
            for h in range(hg):
                do = _head(do_ref, h).astype(BF16)
                z, log_not, _ = scores[h]
                a = jnp.where(mask, jnp.exp(z + withins[h] + afters[h]), 0.0)
                da = lax.dot_general(do, _head(v_ref, h, rows), (((1,), (1,)), ((), ())),
                                     preferred_element_type=F32)
                g_ref[h, :, pl.ds(ks, KB)] = a * da
                z_ref[h, :, pl.ds(ks, KB)] = z
                dv_ref[rows, h * HEAD_DIM:(h + 1) * HEAD_DIM] += lax.dot_general(
                    a.astype(BF16), do, (((0,), (0,)), ((), ())), preferred_element_type=F32)
                new_afters.append(afters[h] + jnp.sum(log_not, axis=1, keepdims=True))
            return n + 1, tuple(new_afters), _weights_alive(functools.reduce(jnp.maximum, new_afters))

        visited, _, _ = lax.while_loop(
            more, down, (jnp.int32(0), tuple(jnp.zeros((QB, 1), F32) for _ in range(hg)), jnp.int32(1)))

        def up(kb, carry):
            dqs, befores = carry
            ks = pl.multiple_of(kb * KB, KB)
            rows = pl.ds(ks, KB)
            mask = jnp.logical_or(kb < i, col < row)
            new_dqs, new_befores = [], []
            gs = [g_ref[h, :, pl.ds(ks, KB)] for h in range(hg)]
            g_withins = _split_dot(gs, prefix)
            for h in range(hg):
                g = gs[h]
                z = z_ref[h, :, pl.ds(ks, KB)]
                g_upto = g_withins[h] + befores[h]
                dz = jnp.where(mask, g - jax.nn.sigmoid(z) * g_upto, 0.0)
                dzs = (dz * scale).astype(BF16)
                new_dqs.append(dqs[h] + jnp.dot(dzs, _head(k_ref, h, rows), preferred_element_type=F32))
                dk_ref[rows, h * HEAD_DIM:(h + 1) * HEAD_DIM] += lax.dot_general(
                    dzs, _head(q_ref, h), (((0,), (0,)), ((), ())), preferred_element_type=F32)
                new_befores.append(befores[h] + jnp.sum(g, axis=1, keepdims=True))
            return tuple(new_dqs), tuple(new_befores)

        dqs, _ = lax.fori_loop(i + 1 - visited, i + 1, up,
                               (tuple(jnp.zeros((QB, HEAD_DIM), F32) for _ in range(hg)),
                                tuple(jnp.zeros((QB, 1), F32) for _ in range(hg))))
        for h in range(hg):
            dq_ref[:, h * HEAD_DIM:(h + 1) * HEAD_DIM] = dqs[h].astype(BF16)

    wide = hg * HEAD_DIM
    tile = lambda off: pl.BlockSpec((QB, wide), lambda g, i: (i, off + g))
    strip = lambda off: pl.BlockSpec((t, wide), lambda g, i: (0, off + g))
    return _call(
        body, "attn_bwd", (ng, t // QB), [tile(ng), strip(2 * ng), strip(3 * ng), tile(ng)],
        [tile(0), strip(0), strip(0)],
        [jax.ShapeDtypeStruct((t, width), BF16), jax.ShapeDtypeStruct((t, width), F32),
         jax.ShapeDtypeStruct((t, width), F32)],
        [pltpu.VMEM((hg, QB, t), F32), pltpu.VMEM((hg, QB, t), F32)], ("parallel", "arbitrary"),
        (ub, ub, ub, dmix), phases)


def _row_tile(rows, cols, pref_bytes=2 * 1024 * 1024):
    tr = max(8, pref_bytes // (4 * cols))
    while rows % tr:
        tr //= 2
    return max(tr, 1)


def _pair_sum(name, g, s, c_idx):
    _, _, r2, cols = g.shape
    tr = _row_tile(r2, cols)

    def body(c_ref, g_ref, s_ref, o_ref):
        o_ref[...] = (g_ref[...] + s_ref[...]).astype(BF16)

    return pl.pallas_call(
        body, name=name,
        grid_spec=pltpu.PrefetchScalarGridSpec(
            num_scalar_prefetch=1, grid=(N_CHIPS, r2 // tr),
            in_specs=[pl.BlockSpec((None, None, tr, cols), lambda p, i, c: (p, c[0], i, 0)),
                      pl.BlockSpec((None, tr, cols), lambda p, i, c: (p, i, 0))],
            out_specs=pl.BlockSpec((None, tr, cols), lambda p, i, c: (p, i, 0))),
        out_shape=jax.ShapeDtypeStruct((N_CHIPS, r2, cols), BF16),
        compiler_params=_params(("parallel", "parallel")),
    )(c_idx, g, s)


def _chip_sum(name, g, s, r, place):
    _, _, r2, cols = g.shape
    tr = _row_tile(r2, cols)

    def body(place_ref, g_ref, s_ref, r_ref, o_ref):
        own = g_ref[...] + s_ref[...]
        o_ref[...] = ((own + r_ref[0].astype(F32)) + r_ref[1].astype(F32)) + r_ref[2].astype(F32)

    return pl.pallas_call(
        body, name=name,
        grid_spec=pltpu.PrefetchScalarGridSpec(
            num_scalar_prefetch=1, grid=(r2 // tr,),
            in_specs=[pl.BlockSpec((None, None, tr, cols), lambda i, p: (p[0], p[1], i, 0)),
                      pl.BlockSpec((None, tr, cols), lambda i, p: (p[0], i, 0)),
                      pl.BlockSpec((3, tr, cols), lambda i, p: (0, i, 0))],
            out_specs=pl.BlockSpec((None, tr, cols), lambda i, p: (p[1], i, 0))),
        out_shape=jax.ShapeDtypeStruct((2, r2, cols), F32),
        compiler_params=_params(("parallel",)),
    )(place, g, s, r)


def _cast_into_slot(name, w, place):
    rows, cols = w.shape
    r2 = rows // 2
    tr = _row_tile(r2, cols)
    nb = r2 // tr

    def body(place_ref, w_ref, o_ref):
        o_ref[...] = w_ref[...].astype(BF16)

    return pl.pallas_call(
        body, name=name,
        grid_spec=pltpu.PrefetchScalarGridSpec(
            num_scalar_prefetch=1, grid=(2, nb),
            in_specs=[pl.BlockSpec((tr, cols), lambda h, i, s: (h * nb + i, 0))],
            out_specs=pl.BlockSpec((None, None, tr, cols), lambda h, i, s: (s[0], h, i, 0))),
        out_shape=jax.ShapeDtypeStruct((N_CHIPS, 2, r2, cols), BF16),
        compiler_params=_params(("parallel", "parallel")),
    )(place, w)


def _colsum(name, a):
    def body(a_ref, o_ref):
        o_ref[...] = jnp.sum(a_ref[...], axis=0, keepdims=True)

    whole = lambda shape: pl.BlockSpec(shape, lambda i: (0, 0))
    return pl.pallas_call(
        body, name=name, grid=(1,), in_specs=[whole(a.shape)], out_specs=whole((1, a.shape[1])),
        out_shape=jax.ShapeDtypeStruct((1, a.shape[1]), F32), compiler_params=_params(("arbitrary",)),
    )(a)


def _adamw(name, w, g, m, v):
    rows, cols = w.shape
    tr = _row_tile(rows, cols, 1024 * 1024)

    def body(w_ref, g_ref, m_ref, v_ref, g_out_ref, d_ref, nm_ref, nv_ref):
        g_t = g_ref[...]
        m_t = ADAM_B1 * m_ref[...] + (1.0 - ADAM_B1) * g_t
        v_t = ADAM_B2 * v_ref[...] + (1.0 - ADAM_B2) * jnp.square(g_t)
        m_hat = m_t / (1.0 - ADAM_B1 ** ADAM_STEP)
        v_hat = v_t / (1.0 - ADAM_B2 ** ADAM_STEP)
        g_out_ref[...] = g_t
        d_ref[...] = -ADAM_LR * (m_hat / (jnp.sqrt(v_hat) + ADAM_EPS) + ADAM_WD * w_ref[...])
        nm_ref[...] = m_t
        nv_ref[...] = v_t

    spec = pl.BlockSpec((tr, cols), lambda i: (i, 0))
    shape = jax.ShapeDtypeStruct((rows, cols), F32)
    return pl.pallas_call(
        body, name=name, grid=(rows // tr,), in_specs=[spec] * 4, out_specs=[spec] * 4, out_shape=[shape] * 4,
        compiler_params=_params(("parallel",)),
    )(w, g, m, v)


def _all_reduce_small(packed):
    rows, cols = packed.shape

    def body(in_ref, out_ref, all_ref, send_sems, recv_sems):
        x, y, c = _pos()
        me = 4 * x + 2 * y + c
        all_ref[me] = in_ref[...]
        cps = []
        for r in range(1, N_DEV):
            bx, by, bc = (r >> 2) & 1, (r >> 1) & 1, r & 1
            peer = (1 - x if bx else x, 1 - y if by else y, 1 - c if bc else c)
            cp = pltpu.make_async_remote_copy(
                src_ref=in_ref, dst_ref=all_ref.at[me], send_sem=send_sems.at[r - 1], recv_sem=recv_sems.at[r - 1],
                device_id=peer, device_id_type=MESH)
            cp.start()
            cps.append(cp)
        for cp in cps:
            cp.wait()
        total = all_ref[0]
        for d in range(1, N_DEV):
            total = total + all_ref[d]
        out_ref[...] = total

    vmem = pl.BlockSpec(memory_space=pltpu.VMEM)
    return pl.pallas_call(
        body, name="all_reduce_small", in_specs=[vmem], out_specs=vmem,
        out_shape=jax.ShapeDtypeStruct((rows, cols), F32),
        scratch_shapes=[pltpu.VMEM((N_DEV, rows, cols), F32), pltpu.SemaphoreType.DMA((N_DEV - 1,)),
                        pltpu.SemaphoreType.DMA((N_DEV - 1,))],
        compiler_params=pltpu.CompilerParams(vmem_limit_bytes=VMEM_LIMIT),
    )(packed)


def kernel(x, ln_in_g, ln_in_b, w_in, w_pool, pool_scale, w_out, ln1_g, ln1_b, w_ff1, b_ff1, w_ff2, b_ff2, ln2_g, ln2_b, loss_target, m_ln_in_g, m_ln_in_b, m_w_in, m_w_pool, m_pool_scale, m_w_out, m_ln1_g, m_ln1_b, m_w_ff1, m_b_ff1, m_w_ff2, m_b_ff2, m_ln2_g, m_ln2_b, v_ln_in_g, v_ln_in_b, v_w_in, v_w_pool, v_pool_scale, v_w_out, v_ln1_g, v_ln1_b, v_w_ff1, v_b_ff1, v_w_ff2, v_b_ff2, v_ln2_g, v_ln2_b):
    t, d = x.shape[1], x.shape[2]
    pw = d // 2
    n_grp = len(POOL_WINDOWS)
    gw = pw // n_grp
    gwc = gw // N_CHIPS
    nh = pw // HEAD_DIM
    ff = w_ff1.shape[2] * N_CHIPS
    assert w_in.shape[0] == 1 and w_in.shape[2] * N_CHIPS == 2 * d and gwc <= 128

    x_idx, y_idx, c_idx = _pos()
    chip_arr = jnp.reshape(2 * x_idx + y_idx, (1,)).astype(jnp.int32)
    c_arr = jnp.reshape(c_idx, (1,)).astype(jnp.int32)
    place = jnp.concatenate([chip_arr, c_arr])

    xs = x.reshape(t, d)
    target = loss_target.reshape(t, d)
    row = lambda vec: vec.reshape(1, -1)

    scale_tile = jnp.zeros((1, 8, 128), F32).at[0, :n_grp, :gwc].set(pool_scale[0])
    scale_slots = lax.dynamic_update_slice(jnp.zeros((N_CHIPS, 8, 128), F32), scale_tile, (chip_arr[0], 0, 0))
    shards = dict(w_in=w_in[0], w_out=w_out[0], w_ff1=w_ff1[0], w_ff2=w_ff2[0], w_pool=w_pool[0].reshape(gw, gw))
    slot = {nm: _cast_into_slot("cast_" + nm, w, place) for nm, w in shards.items()}
    unsplit = lambda s: s.reshape(N_CHIPS, 2 * s.shape[2], s.shape[3])

    (h0, h0b, xhat0, rstd0), ((win_s,), (wpool_s,), (scale_g,)) = _ln_fwd(
        "ln_in_fwd", xs, row(ln_in_g), row(ln_in_b),
        phases=[_gather_ici_phase(slot["w_in"]), _gather_ici_phase(slot["w_pool"]), _gather_whole_phase(scale_slots)])
    (win_s,), (wpool_s,) = _comm_call("gather_d2d_first", [_gather_d2d_phase(win_s), _gather_d2d_phase(wpool_s)])
    win_g = unsplit(win_s)
    wpool_full = unsplit(wpool_s).reshape(N_CHIPS, n_grp, gwc, gw).transpose(1, 0, 2, 3).reshape(n_grp, gw, gw)
    scale_full = scale_g[:, :n_grp, :gwc].transpose(1, 0, 2).reshape(n_grp, 1, gw)

    def store_f32(acc, extra_refs, out_refs):
        out_refs[0][...] = acc

    def pool_f32_all_bf16(acc, extra_refs, out_refs):
        @pl.when(pl.program_id(1) == 0)
        def _():
            out_refs[0][...] = acc

        out_refs[1][...] = acc.astype(BF16)

    assert w_in.shape[2] == pw
    (u, ub), ((wout_s,),) = _mm_nn(
        "in_proj", h0b, win_g, (), lambda tm, tn: [],
        [jax.ShapeDtypeStruct((t, pw), F32), jax.ShapeDtypeStruct((t, 2 * d), BF16)],
        lambda tm, tn: [pl.BlockSpec((tm, tn), lambda i, j, kk: (i, 0)), _tile_spec(tm, tn)],
        pool_f32_all_bf16, b_chips=True, tn=pw, phases=[_gather_ici_phase(slot["w_out"])])
    y_pool = _pool_fwd(u, wpool_full, scale_full, t, pw)
    y_sb, ((wff1_s,), (wout_s,)) = _attn_fwd(
        ub, t, nh, min(nh, 4), phases=[_gather_ici_phase(slot["w_ff1"]), _gather_d2d_phase(wout_s)])
    wout_full = unsplit(wout_s).reshape(d, d)
    mix_in = jnp.concatenate([y_pool, y_sb], axis=1)

    def residual(acc, extra_refs, out_refs):
        out_refs[0][...] = ALPHA * extra_refs[0][...] + acc

    (r1,), ((wff1_s,),) = _mm_nn(
        "out_proj", mix_in, wout_full, (h0,), lambda tm, tn: [_tile_spec(tm, tn)],
        [jax.ShapeDtypeStruct((t, d), F32)], lambda tm, tn: [_tile_spec(tm, tn)], residual,
        phases=[_gather_d2d_phase(wff1_s)])
    wff1_g = unsplit(wff1_s)
    h1, h1b, xhat1, rstd1 = _ln_fwd("ln1_fwd", r1, ln1_g, ln1_b)

    def relu_sq(acc, extra_refs, out_refs):
        p = jnp.maximum(acc + extra_refs[0][...], 0.0)
        out_refs[0][...] = p
        out_refs[1][...] = jnp.square(p).astype(BF16)

    (relu_z, act_b), ((wff2_s,),) = _mm_nn(
        "ff1", h1b, wff1_g, (b_ff1,), lambda tm, tn: [_row_spec(tn)],
        [jax.ShapeDtypeStruct((t, ff), F32), jax.ShapeDtypeStruct((t, ff), BF16)],
        lambda tm, tn: [_tile_spec(tm, tn)] * 2, relu_sq, b_chips=True, phases=[_gather_ici_phase(slot["w_ff2"])])
    ((wff2_s,),) = _comm_call("gather_d2d_last", [_gather_d2d_phase(wff2_s)])
    wff2_full = unsplit(wff2_s).reshape(ff, d)

    def residual_bias(acc, extra_refs, out_refs):
        out_refs[0][...] = ALPHA * extra_refs[0][...] + (acc + extra_refs[1][...])

    r2 = _mm_nn("ff2", act_b, wff2_full, (h1, b_ff2), lambda tm, tn: [_tile_spec(tm, tn), _row_spec(tn)],
                [jax.ShapeDtypeStruct((t, d), F32)], lambda tm, tn: [_tile_spec(tm, tn)], residual_bias)[0]

    dr2, dr2b, loss_tile, g_ln2_g, g_ln2_b, g_b_ff2 = _ln2_loss_bwd(r2, target, ln2_g, ln2_b)
    loss = lax.psum(loss_tile[0, 0], ("x", "y", "c"))

    halves = lambda g: g.reshape(N_CHIPS, 2, g.shape[1] // 2, g.shape[2])
    g_ff2 = halves(_mm_tn("grad_w_ff2", act_b, dr2b).reshape(N_CHIPS, ff // N_CHIPS, d))

    def relu_sq_bwd(acc, extra_refs, out_refs):
        dz = acc * (2.0 * extra_refs[0][...])
        out_refs[0][...] = dz.astype(BF16)
        rows = lax.broadcasted_iota(jnp.int32, out_refs[1].shape, 0)
        out_refs[1][...] = jnp.where(rows == 0, jnp.sum(dz, axis=0, keepdims=True), 0.0)

    tm_ff = _tile(t, 1024)
    (dz1b, g_b_ff1_parts), ((s_ff2,),) = _mm_nt(
        "ff2_bwd", dr2b, wff2_full, (relu_z,), lambda tm, tn: [_tile_spec(tm, tn)],
        [jax.ShapeDtypeStruct((t, ff), BF16), jax.ShapeDtypeStruct((8 * (t // tm_ff), ff), F32)],
        lambda tm, tn: [_tile_spec(tm, tn), pl.BlockSpec((8, tn), lambda i, j, kk: (i, j))], relu_sq_bwd,
        phases=[_swap_phase(g_ff2)])
    p_ff2 = _pair_sum("pair_sum_w_ff2", g_ff2, s_ff2, c_arr)
    g_ff1, ((r_ff2,),) = _mm_tn("grad_w_ff1", h1b, dz1b, out_chips=True, phases=[_scatter_phase(p_ff2)])
    g_ff1 = halves(g_ff1)

    def plus_alpha(acc, extra_refs, out_refs):
        out_refs[0][...] = ALPHA * extra_refs[0][...] + acc

    (dh1,), ((s_ff1,),) = _mm_nt(
        "ff1_bwd", dz1b, wff1_g, (dr2,), lambda tm, tn: [_tile_spec(tm, tn)],
        [jax.ShapeDtypeStruct((t, d), F32)], lambda tm, tn: [_tile_spec(tm, tn)], plus_alpha, b_chips=True,
        phases=[_swap_phase(g_ff1)])
    q_ff2 = _chip_sum("chip_sum_w_ff2", g_ff2, s_ff2, r_ff2, place)
    p_ff1 = _pair_sum("pair_sum_w_ff1", g_ff1, s_ff1, c_arr)
    (dr1, dr1b, g_ln1_g, g_ln1_b), ((q_ff2,),) = _ln_bwd("ln1_bwd", dh1, xhat1, rstd1, ln1_g,
                                                         phases=[_assemble_phase(q_ff2)])

    g_out = halves(_mm_tn("grad_w_out", mix_in, dr1b).reshape(N_CHIPS, d // N_CHIPS, d))
    (dmix,), ((s_out,),) = _mm_nt(
        "out_proj_bwd", dr1b, wout_full, (), lambda tm, tn: [], [jax.ShapeDtypeStruct((t, d), F32)],
        lambda tm, tn: [_tile_spec(tm, tn)], store_f32, phases=[_swap_phase(g_out)])
    p_out = _pair_sum("pair_sum_w_out", g_out, s_out, c_arr)
    du_pool, g_w_pool_full, g_scale_full = _pool_bwd(u, dmix, wpool_full, scale_full, t, pw)
    (dq, dk, dv), ((r_ff1,), (r_out,)) = _attn_bwd(ub, dmix, t, nh, 1,
                                                   phases=[_scatter_phase(p_ff1), _scatter_phase(p_out)])
    q_ff1 = _chip_sum("chip_sum_w_ff1", g_ff1, s_ff1, r_ff1, place)
    q_out = _chip_sum("chip_sum_w_out", g_out, s_out, r_out, place)
    du = jnp.concatenate([du_pool, dq, dk.astype(BF16), dv.astype(BF16)], axis=1)
    g_in, ((q_ff1,), (q_out,)) = _mm_tn("grad_w_in", h0b, du, out_chips=True,
                                        phases=[_assemble_phase(q_ff1), _assemble_phase(q_out)])
    g_in = halves(g_in)
    g_pool = halves(g_w_pool_full.reshape(n_grp, N_CHIPS, gwc, gw).transpose(1, 0, 2, 3).reshape(N_CHIPS, gw, gw))
    (s_in,), (s_pool,) = _comm_call("rs_swap_last", [_swap_phase(g_in), _swap_phase(g_pool)])
    p_in = _pair_sum("pair_sum_w_in", g_in, s_in, c_arr)
    p_pool = _pair_sum("pair_sum_w_pool", g_pool, s_pool, c_arr)
    (dh0,), ((r_in,), (r_pool,)) = _mm_nt(
        "in_proj_bwd", du, win_g, (dr1,), lambda tm, tn: [_tile_spec(tm, tn)],
        [jax.ShapeDtypeStruct((t, d), F32)], lambda tm, tn: [_tile_spec(tm, tn)], plus_alpha, b_chips=True,
        phases=[_scatter_phase(p_in), _scatter_phase(p_pool)])
    q_in = _chip_sum("chip_sum_w_in", g_in, s_in, r_in, place)
    q_pool = _chip_sum("chip_sum_w_pool", g_pool, s_pool, r_pool, place)
    (dx, _, g_ln_in_g, g_ln_in_b), ((q_in,), (q_pool,)) = _ln_bwd(
        "ln_in_bwd", dh0, xhat0, rstd0, row(ln_in_g), phases=[_assemble_phase(q_in), _assemble_phase(q_pool)])

    big = {}
    for nm, q, w, m, v in [("w_in", q_in, w_in, m_w_in, v_w_in), ("w_out", q_out, w_out, m_w_out, v_w_out),
                           ("w_ff1", q_ff1, w_ff1, m_w_ff1, v_w_ff1), ("w_ff2", q_ff2, w_ff2, m_w_ff2, v_w_ff2),
                           ("w_pool", q_pool, w_pool, m_w_pool, v_w_pool)]:
        g = q.reshape(2 * q.shape[1], q.shape[2])
        flat = lambda arr: arr.reshape(g.shape)
        big[nm] = tuple(arr.reshape(w.shape) for arr in _adamw("adamw_" + nm, flat(w), g, flat(m), flat(v)))

    lane = 2048 if d % 2048 == 0 else d
    small_names = ["ln_in_g", "ln_in_b", "ln1_g", "ln1_b", "b_ff1", "b_ff2", "ln2_g", "ln2_b"]
    small_w = dict(ln_in_g=ln_in_g, ln_in_b=ln_in_b, ln1_g=ln1_g, ln1_b=ln1_b, b_ff1=b_ff1, b_ff2=b_ff2, ln2_g=ln2_g,
                   ln2_b=ln2_b)
    small_m = dict(ln_in_g=m_ln_in_g, ln_in_b=m_ln_in_b, ln1_g=m_ln1_g, ln1_b=m_ln1_b, b_ff1=m_b_ff1, b_ff2=m_b_ff2,
                   ln2_g=m_ln2_g, ln2_b=m_ln2_b)
    small_v = dict(ln_in_g=v_ln_in_g, ln_in_b=v_ln_in_b, ln1_g=v_ln1_g, ln1_b=v_ln1_b, b_ff1=v_b_ff1, b_ff2=v_b_ff2,
                   ln2_g=v_ln2_g, ln2_b=v_ln2_b)
    small_g = dict(ln_in_g=g_ln_in_g, ln_in_b=g_ln_in_b, ln1_g=g_ln1_g, ln1_b=g_ln1_b, b_ff2=g_b_ff2, ln2_g=g_ln2_g,
                   ln2_b=g_ln2_b)

    def pack(parts):
        flat = jnp.concatenate([p.reshape(-1) for p in parts])
        n_rows = -(-flat.shape[0] // lane)
        n_rows = -(-n_rows // 8) * 8
        return jnp.pad(flat, (0, n_rows * lane - flat.shape[0])).reshape(n_rows, lane)

    small_g["b_ff1"] = _colsum("b_ff1_colsum", g_b_ff1_parts)
    summed = _all_reduce_small(pack([small_g[nm] for nm in small_names] + [g_scale_full])).reshape(-1)

    g_small, off = {}, 0
    for nm in small_names:
        g_small[nm] = summed[off:off + small_w[nm].size]
        off += small_w[nm].size
    g_scale_all = summed[off:off + n_grp * gw].reshape(n_grp, N_CHIPS, gwc)
    g_scale = lax.dynamic_index_in_dim(g_scale_all, chip_arr[0], axis=1, keepdims=False)

    order = small_names + ["pool_scale"]
    small_w["pool_scale"], small_m["pool_scale"], small_v["pool_scale"] = pool_scale, m_pool_scale, v_pool_scale
    g_small["pool_scale"] = g_scale
    _, delta_s, new_m_s, new_v_s = _adamw("adamw_small", pack([small_w[nm] for nm in order]),
                                          pack([g_small[nm] for nm in order]), pack([small_m[nm] for nm in order]),
                                          pack([small_v[nm] for nm in order]))
    small = {}
    off = 0
    for nm in order:
        size, shape = small_w[nm].size, small_w[nm].shape
        cut = lambda arr: arr.reshape(-1)[off:off + size].reshape(shape)
        small[nm] = (g_small[nm].reshape(shape), cut(delta_s), cut(new_m_s), cut(new_v_s))
        off += size

    every = {**big, **small}
    weight_order = ["ln_in_g", "ln_in_b", "w_in", "w_pool", "pool_scale", "w_out", "ln1_g", "ln1_b", "w_ff1", "b_ff1",
                    "w_ff2", "b_ff2", "ln2_g", "ln2_b"]
    grads = [every[nm][0] for nm in weight_order]
    deltas = [every[nm][1] for nm in weight_order]
    new_ms = [every[nm][2] for nm in weight_order]
    new_vs = [every[nm][3] for nm in weight_order]
    return (loss, dx.reshape(x.shape), *grads, *deltas, *new_ms, *new_vs)
```

```python
import functools

import jax
import jax.numpy as jnp
from jax import lax
from jax.experimental import pallas as pl
from jax.experimental.pallas import tpu as pltpu

F32 = jnp.float32
BF16 = jnp.bfloat16
MESH = pl.DeviceIdType.MESH

HEAD_DIM = 128
POOL_WINDOWS = (2, 4, 8, 16)
POOL_HALO = 16
LN_EPS = 1e-5
ALPHA = 2.0 ** 0.25
ADAM_LR, ADAM_B1, ADAM_B2, ADAM_EPS, ADAM_WD, ADAM_STEP = 0.001, 0.9, 0.999, 1e-08, 0.01, 10

QB = 256
KB = 256
VMEM_LIMIT = 56 * 1024 * 1024
N_CHIPS = 4
N_DEV = 8


def _params(sem=None):
    return pltpu.CompilerParams(dimension_semantics=sem, vmem_limit_bytes=VMEM_LIMIT)


def _tile(dim, pref):
    return pref if dim % pref == 0 else dim


def _pos():
    return lax.axis_index("x"), lax.axis_index("y"), lax.axis_index("c")


def _other_chips(x, y):
    return [(1 - x, y), (x, 1 - y), (1 - x, 1 - y)]


ANY = pl.BlockSpec(memory_space=pl.ANY)


class _Phase:
    def __init__(self, ins, out_shapes, aliases, n_sems, build):
        self.ins, self.out_shapes, self.aliases, self.n_sems, self.build = ins, out_shapes, aliases, n_sems, build


def _remote(src, dst, send_sems, recv_sems, k, to):
    return pltpu.make_async_remote_copy(src_ref=src, dst_ref=dst, send_sem=send_sems.at[k], recv_sem=recv_sems.at[k],
                                        device_id=to, device_id_type=MESH)


def _swap_phase(g):
    def build(ins, outs, ss, rs):
        x, y, c = _pos()
        cp = _remote(ins[0].at[:, 1 - c], outs[0], ss, rs, 0, (x, y, 1 - c))
        return [cp], [cp]

    return _Phase([g], [jax.ShapeDtypeStruct((N_CHIPS, g.shape[2], g.shape[3]), g.dtype)], {}, 1, build)


ALL_OTHERS = (0, 1, 2)


def _scatter_phase(p, others=ALL_OTHERS):
    def build(ins, outs, ss, rs):
        x, y, c = _pos()
        chips = _other_chips(x, y)
        cps = [_remote(ins[0].at[2 * chips[j][0] + chips[j][1]], outs[0].at[k], ss, rs, k, (*chips[j], c))
               for k, j in enumerate(others)]
        return cps, cps

    return _Phase([p], [jax.ShapeDtypeStruct((len(others), p.shape[1], p.shape[2]), p.dtype)], {}, len(others), build)


def _assemble_phase(q):
    def build(ins, outs, ss, rs):
        x, y, c = _pos()
        mine, other = outs[0].at[c], outs[0].at[1 - c]
        return [_remote(mine, mine, ss, rs, 0, (x, y, 1 - c))], [_remote(other, other, ss, rs, 0, (x, y, c))]

    return _Phase([q], [jax.ShapeDtypeStruct(q.shape, q.dtype)], {0: 0}, 1, build)


def _gather_ici_phase(slot, others=ALL_OTHERS):
    def build(ins, outs, ss, rs):
        x, y, c = _pos()
        chips = _other_chips(x, y)
        mine = outs[0].at[2 * x + y, c]
        sends, recvs = [], []
        for k, j in enumerate(others):
            theirs = outs[0].at[2 * chips[j][0] + chips[j][1], c]
            sends.append(_remote(mine, mine, ss, rs, k, (*chips[j], c)))
            recvs.append(_remote(theirs, theirs, ss, rs, k, (x, y, c)))
        return sends, recvs

    return _Phase([slot], [jax.ShapeDtypeStruct(slot.shape, slot.dtype)], {0: 0}, len(others), build)


def _gather_d2d_phase(slot):
    def build(ins, outs, ss, rs):
        x, y, c = _pos()
        sends, recvs = [], []
        for j, chip in enumerate(_other_chips(x, y)):
            landed = outs[0].at[2 * chip[0] + chip[1], c]
            coming = outs[0].at[2 * chip[0] + chip[1], 1 - c]
            sends.append(_remote(landed, landed, ss, rs, j, (x, y, 1 - c)))
            recvs.append(_remote(coming, coming, ss, rs, j, (x, y, c)))
        return sends, recvs

    return _Phase([slot], [jax.ShapeDtypeStruct(slot.shape, slot.dtype)], {0: 0}, 3, build)


def _gather_whole_phase(slots):
    def build(ins, outs, ss, rs):
        x, y, c = _pos()
        mine = outs[0].at[2 * x + y]
        sends, recvs = [], []
        for j, chip in enumerate(_other_chips(x, y)):
            theirs = outs[0].at[2 * chip[0] + chip[1]]
            sends.append(_remote(mine, mine, ss, rs, j, (*chip, c)))
            recvs.append(_remote(theirs, theirs, ss, rs, j, (x, y, c)))
        return sends, recvs

    return _Phase([slots], [jax.ShapeDtypeStruct(slots.shape, slots.dtype)], {0: 0}, 3, build)


def _split_refs(refs, n_in, n_out, n_scratch, phases):
    n_pin = sum(len(ph.ins) for ph in phases)
    n_pout = sum(len(ph.out_shapes) for ph in phases)
    cuts = [n_in, n_pin, n_out, n_pout, n_scratch]
    parts, at = [], 0
    for n in cuts:
        parts.append(refs[at:at + n])
        at += n
    parts.append(refs[at:])
    return parts


def _build_phases(phases, pin, pout, sems):
    built, i, o = [], 0, 0
    for k, ph in enumerate(phases):
        built.append(ph.build(pin[i:i + len(ph.ins)], pout[o:o + len(ph.out_shapes)], sems[2 * k], sems[2 * k + 1]))
        i += len(ph.ins)
        o += len(ph.out_shapes)
    return built


def _finish_phases(built):
    for _, recvs in built:
        for cp in recvs:
            cp.wait_recv()
    for sends, _ in built:
        for cp in sends:
            cp.wait_send()


def _call(body, name, grid, in_specs, out_specs, out_shape, scratch_shapes, semantics, args, phases=()):
    n_in, n_out, n_scratch = len(args), len(out_shape), len(scratch_shapes)
    aliases, in_at, out_at = {}, n_in, n_out
    for ph in phases:
        aliases.update({in_at + i: out_at + o for i, o in ph.aliases.items()})
        in_at += len(ph.ins)
        out_at += len(ph.out_shapes)

    def hosted(*refs):
        ins, pin, outs, pout, scratch, sems = _split_refs(refs, n_in, n_out, n_scratch, phases)
        ids = [pl.program_id(a) for a in range(len(grid))]
        first = functools.reduce(jnp.logical_and, [i == 0 for i in ids])
        last = functools.reduce(jnp.logical_and, [i == g - 1 for i, g in zip(ids, grid)])

        @pl.when(first)
        def _():
            for sends, _ in _build_phases(phases, pin, pout, sems):
                for cp in sends:
                    cp.start()

        body(*ins, *outs, *scratch)

        @pl.when(last)
        def _():
            _finish_phases(_build_phases(phases, pin, pout, sems))

    p_args = [a for ph in phases for a in ph.ins]
    p_shapes = [s for ph in phases for s in ph.out_shapes]
    sem_shapes = [pltpu.SemaphoreType.DMA((ph.n_sems,)) for ph in phases for _ in range(2)]
    outs = pl.pallas_call(
        hosted if phases else body, name=name, grid=grid, in_specs=[*in_specs, *[ANY] * len(p_args)],
        out_specs=[*out_specs, *[ANY] * len(p_shapes)], out_shape=[*out_shape, *p_shapes],
        input_output_aliases=aliases, scratch_shapes=[*scratch_shapes, *sem_shapes],
        compiler_params=_params(("arbitrary",) * len(grid) if phases else semantics),
    )(*args, *p_args)
    phase_outs, at = [], n_out
    for ph in phases:
        phase_outs.append(list(outs[at:at + len(ph.out_shapes)]))
        at += len(ph.out_shapes)
    return list(outs[:n_out]), phase_outs


def _comm_call(name, phases):
    def body(*refs):
        _, pin, _, pout, _, sems = _split_refs(refs, 0, 0, 0, phases)
        built = _build_phases(phases, pin, pout, sems)
        for sends, _ in built:
            for cp in sends:
                cp.start()
        _finish_phases(built)

    aliases, in_at, out_at = {}, 0, 0
    for ph in phases:
        aliases.update({in_at + i: out_at + o for i, o in ph.aliases.items()})
        in_at += len(ph.ins)
        out_at += len(ph.out_shapes)
    p_args = [a for ph in phases for a in ph.ins]
    p_shapes = [s for ph in phases for s in ph.out_shapes]
    outs = pl.pallas_call(
        body, name=name, in_specs=[ANY] * len(p_args), out_specs=[ANY] * len(p_shapes), out_shape=p_shapes,
        input_output_aliases=aliases,
        scratch_shapes=[pltpu.SemaphoreType.DMA((ph.n_sems,)) for ph in phases for _ in range(2)],
    )(*p_args)
    phase_outs, at = [], 0
    for ph in phases:
        phase_outs.append(list(outs[at:at + len(ph.out_shapes)]))
        at += len(ph.out_shapes)
    return phase_outs


def _matmul(name, a, b, grid, a_spec, b_spec, contract, acc_shape, extras, extra_specs, out_shape, out_specs,
            epilogue, phases=()):
    n_extra, n_out, gk = len(extras), len(out_shape), grid[2]

    def product(a_ref, b_ref):
        return lax.dot_general(a_ref[...], b_ref[...], (contract, ((), ())), preferred_element_type=F32)

    def body_one_step(*refs):
        epilogue(product(refs[0], refs[1]), refs[2:2 + n_extra], refs[2 + n_extra:])

    def body(*refs):
        a_ref, b_ref = refs[0], refs[1]
        extra_refs = refs[2:2 + n_extra]
        out_refs = refs[2 + n_extra:2 + n_extra + n_out]
        acc_ref = refs[-1]
        kk = pl.program_id(2)

        @pl.when(kk == 0)
        def _():
            acc_ref[...] = product(a_ref, b_ref)

        @pl.when(kk > 0)
        def _():
            acc_ref[...] += product(a_ref, b_ref)

        @pl.when(kk == gk - 1)
        def _():
            epilogue(acc_ref[...], extra_refs, out_refs)

    outs, phase_outs = _call(
        body_one_step if gk == 1 else body, name, grid, [a_spec, b_spec, *extra_specs], out_specs, out_shape,
        [] if gk == 1 else [pltpu.VMEM(acc_shape, F32)], ("parallel", "arbitrary", "arbitrary"), (a, b, *extras),
        phases)
    return (outs, phase_outs) if phases else outs


def _mm_nn(name, a, b, extras, extra_specs, out_shape, out_specs, epilogue, b_chips=False, tm=1024, tn=1024,
           tk=2048, phases=()):
    m, k = a.shape
    n = b.shape[1] if not b_chips else b.shape[2] * N_CHIPS
    tm, tk = _tile(m, tm), _tile(k, tk)
    if b_chips:
        tn = _tile(b.shape[2], tn)
        nb = b.shape[2] // tn
        b_spec = pl.BlockSpec((None, tk, tn), lambda i, j, kk: (j // nb, kk, j % nb))
    else:
        tn = _tile(n, tn)
        b_spec = pl.BlockSpec((tk, tn), lambda i, j, kk: (kk, j))
    a_spec = pl.BlockSpec((tm, tk), lambda i, j, kk: (i, kk))
    return _matmul(name, a, b, (m // tm, n // tn, k // tk), a_spec, b_spec, ((1,), (0,)), (tm, tn), extras,
                   extra_specs(tm, tn), out_shape, out_specs(tm, tn), epilogue, phases)


def _mm_nt(name, a, b, extras, extra_specs, out_shape, out_specs, epilogue, b_chips=False, tm=1024, tn=1024,
           tk=2048, phases=()):
    m, k = a.shape
    n = b.shape[0] if not b_chips else b.shape[1]
    tm, tn = _tile(m, tm), _tile(n, tn)
    if b_chips:
        tk = _tile(b.shape[2], tk)
        nb = b.shape[2] // tk
        b_spec = pl.BlockSpec((None, tn, tk), lambda i, j, kk: (kk // nb, j, kk % nb))
    else:
        tk = _tile(k, tk)
        b_spec = pl.BlockSpec((tn, tk), lambda i, j, kk: (j, kk))
    a_spec = pl.BlockSpec((tm, tk), lambda i, j, kk: (i, kk))
    return _matmul(name, a, b, (m // tm, n // tn, k // tk), a_spec, b_spec, ((1,), (1,)), (tm, tn), extras,
                   extra_specs(tm, tn), out_shape, out_specs(tm, tn), epilogue, phases)


def _mm_tn(name, a, b, out_chips=False, tm=1024, tn=1024, tk=2048, phases=()):
    k, m = a.shape
    n = b.shape[1]
    tm, tk = _tile(m, tm), _tile(k, tk)
    a_spec = pl.BlockSpec((tk, tm), lambda i, j, kk: (kk, i))
    if out_chips:
        nc = n // N_CHIPS
        tn = _tile(nc, tn)
        nb = nc // tn
        out_shape = [jax.ShapeDtypeStruct((N_CHIPS, m, nc), F32)]
        out_specs = [pl.BlockSpec((None, tm, tn), lambda i, j, kk: (j // nb, i, j % nb))]
    else:
        tn = _tile(n, tn)
        out_shape = [jax.ShapeDtypeStruct((m, n), F32)]
        out_specs = [pl.BlockSpec((tm, tn), lambda i, j, kk: (i, j))]
    b_spec = pl.BlockSpec((tk, tn), lambda i, j, kk: (kk, j))

    def epilogue(acc, extra_refs, out_refs):
        out_refs[0][...] = acc

    res = _matmul(name, a, b, (m // tm, n // tn, k // tk), a_spec, b_spec, ((0,), (0,)), (tm, tn), (), [],
                  out_shape, out_specs, epilogue, phases)
    return (res[0][0], res[1]) if phases else res[0]


def _tile_spec(tm, tn):
    return pl.BlockSpec((tm, tn), lambda i, j, kk: (i, j))


def _row_spec(tn):
    return pl.BlockSpec((1, tn), lambda i, j, kk: (0, j))


def _ln_stats(r):
    mu = jnp.mean(r, axis=-1, keepdims=True)
    var = jnp.mean(jnp.square(r - mu), axis=-1, keepdims=True)
    rstd = lax.rsqrt(var + LN_EPS)
    return (r - mu) * rstd, rstd


def _ln_fwd(name, r, g, b, tr=256, phases=()):
    t, d = r.shape
    tr = _tile(t, tr)

    def body(r_ref, g_ref, b_ref, y_ref, yb_ref, xhat_ref, rstd_ref):
        xhat, rstd = _ln_stats(r_ref[...])
        y = xhat * g_ref[...] + b_ref[...]
        y_ref[...] = y
        yb_ref[...] = y.astype(BF16)
        xhat_ref[...] = xhat
        rstd_ref[...] = rstd

    row = pl.BlockSpec((tr, d), lambda i: (i, 0))
    vec = pl.BlockSpec((1, d), lambda i: (0, 0))
    outs, phase_outs = _call(
        body, name, (t // tr,), [row, vec, vec], [row, row, row, pl.BlockSpec((tr, 1), lambda i: (i, 0))],
        [jax.ShapeDtypeStruct((t, d), F32), jax.ShapeDtypeStruct((t, d), BF16),
         jax.ShapeDtypeStruct((t, d), F32), jax.ShapeDtypeStruct((t, 1), F32)], [], ("parallel",), (r, g, b), phases)
    return (outs, phase_outs) if phases else outs


def _ln_bwd_rows(dy, xhat, rstd, g):
    dxhat = dy * g
    m1 = jnp.mean(dxhat, axis=-1, keepdims=True)
    m2 = jnp.mean(dxhat * xhat, axis=-1, keepdims=True)
    return rstd * (dxhat - m1 - xhat * m2)


def _ln_bwd(name, dy, xhat, rstd, g, tr=256, phases=()):
    t, d = dy.shape
    tr = _tile(t, tr)

    def body(dy_ref, xhat_ref, rstd_ref, g_ref, dr_ref, drb_ref, dg_ref, db_ref):
        @pl.when(pl.program_id(0) == 0)
        def _():
            dg_ref[...] = jnp.zeros_like(dg_ref)
            db_ref[...] = jnp.zeros_like(db_ref)

        dy_t, xhat_t = dy_ref[...], xhat_ref[...]
        dr = _ln_bwd_rows(dy_t, xhat_t, rstd_ref[...], g_ref[...])
        dr_ref[...] = dr
        drb_ref[...] = dr.astype(BF16)
        dg_ref[...] += jnp.sum(dy_t * xhat_t, axis=0, keepdims=True)
        db_ref[...] += jnp.sum(dy_t, axis=0, keepdims=True)

    row = pl.BlockSpec((tr, d), lambda i: (i, 0))
    vec = pl.BlockSpec((1, d), lambda i: (0, 0))
    outs, phase_outs = _call(
        body, name, (t // tr,), [row, row, pl.BlockSpec((tr, 1), lambda i: (i, 0)), vec], [row, row, vec, vec],
        [jax.ShapeDtypeStruct((t, d), F32), jax.ShapeDtypeStruct((t, d), BF16),
         jax.ShapeDtypeStruct((1, d), F32), jax.ShapeDtypeStruct((1, d), F32)], [], ("arbitrary",),
        (dy, xhat, rstd, g), phases)
    return (outs, phase_outs) if phases else outs


def _ln2_loss_bwd(r2, target, g, b, tr=256):
    t, d = r2.shape
    tr = _tile(t, tr)

    def body(r_ref, t_ref, g_ref, b_ref, dr_ref, drb_ref, loss_ref, dg_ref, db_ref, dsum_ref):
        @pl.when(pl.program_id(0) == 0)
        def _():
            loss_ref[...] = jnp.zeros_like(loss_ref)
            dg_ref[...] = jnp.zeros_like(dg_ref)
            db_ref[...] = jnp.zeros_like(db_ref)
            dsum_ref[...] = jnp.zeros_like(dsum_ref)

        xhat, rstd = _ln_stats(r_ref[...])
        g_t = g_ref[...]
        err = xhat * g_t + b_ref[...] - t_ref[...]
        loss_ref[...] += 0.5 * jnp.sum(jnp.mean(jnp.square(err), axis=-1, keepdims=True), axis=0, keepdims=True)
        dy = err * (1.0 / d)
        dr = _ln_bwd_rows(dy, xhat, rstd, g_t)
        dr_ref[...] = dr
        drb_ref[...] = dr.astype(BF16)
        dg_ref[...] += jnp.sum(dy * xhat, axis=0, keepdims=True)
        db_ref[...] += jnp.sum(dy, axis=0, keepdims=True)
        dsum_ref[...] += jnp.sum(dr, axis=0, keepdims=True)

    row = pl.BlockSpec((tr, d), lambda i: (i, 0))
    vec = pl.BlockSpec((1, d), lambda i: (0, 0))
    return pl.pallas_call(
        body, name="ln2_loss_bwd", grid=(t // tr,), in_specs=[row, row, vec, vec],
        out_specs=[row, row, pl.BlockSpec((8, 128), lambda i: (0, 0)), vec, vec, vec],
        out_shape=[jax.ShapeDtypeStruct((t, d), F32), jax.ShapeDtypeStruct((t, d), BF16),
                   jax.ShapeDtypeStruct((8, 128), F32), jax.ShapeDtypeStruct((1, d), F32),
                   jax.ShapeDtypeStruct((1, d), F32), jax.ShapeDtypeStruct((1, d), F32)],
        compiler_params=_params(("arbitrary",)),
    )(r2, target, g, b)


POOL_ROWS = 512


def _pool_mean_minus_token(u_ref, r0, rows, grp, first):
    width = u_ref.shape[1]
    body = u_ref[pl.ds(r0, rows), :]
    halo = u_ref[pl.ds(pl.multiple_of(jnp.maximum(r0 - POOL_HALO, 0), POOL_HALO), POOL_HALO), :]
    halo = jnp.where(first, 0.0, halo)
    full = jnp.concatenate([halo, body], axis=0)
    s = full
    for step in range(len(POOL_WINDOWS)):
        shifted = pltpu.roll(s, 1 << step, axis=0)
        s = s + jnp.where(grp >= step, shifted, 0.0)
    s = s[POOL_HALO:, :]
    tpos = r0 + lax.broadcasted_iota(jnp.int32, (rows, width), 0)
    count = jnp.minimum(tpos + 1, 2 << grp).astype(F32)
    return s / count - body, count


def _pool_fwd(u, w_pool, pool_scale, t, pw):
    gw = pw // len(POOL_WINDOWS)
    rows = _tile(t, POOL_ROWS)

    def body(u_ref, w_ref, s_ref, o_ref):
        grp = pl.program_id(0)

        def chunk(ci, carry):
            r0 = pl.multiple_of(ci * rows, rows)
            y, _ = _pool_mean_minus_token(u_ref, r0, rows, grp, ci == 0)
            yw = jnp.dot(y.astype(BF16), w_ref[...], preferred_element_type=F32)
            o_ref[pl.ds(r0, rows), :] = (yw * s_ref[...]).astype(BF16)
            return carry

        lax.fori_loop(0, t // rows, chunk, 0)

    return pl.pallas_call(
        body, name="pool_fwd", grid=(len(POOL_WINDOWS),),
        in_specs=[pl.BlockSpec((t, gw), lambda g: (0, g)), pl.BlockSpec((None, gw, gw), lambda g: (g, 0, 0)),
                  pl.BlockSpec((None, 1, gw), lambda g: (g, 0, 0))],
        out_specs=pl.BlockSpec((t, gw), lambda g: (0, g)),
        out_shape=jax.ShapeDtypeStruct((t, pw), BF16),
        compiler_params=_params(("parallel",)),
    )(u, w_pool, pool_scale)


def _pool_bwd(u, dmix, w_pool, pool_scale, t, pw):
    n_grp = len(POOL_WINDOWS)
    gw = pw // n_grp
    rows = _tile(t, POOL_ROWS)

    def body(u_ref, dm_ref, w_ref, s_ref, du_ref, dw_ref, ds_ref, e_ref):
        grp = pl.program_id(0)
        dw_ref[...] = jnp.zeros_like(dw_ref)
        ds_ref[...] = jnp.zeros_like(ds_ref)
        e_ref[pl.ds(t, POOL_HALO), :] = jnp.zeros((POOL_HALO, gw), F32)

        def chunk(ci, carry):
            r0 = pl.multiple_of(ci * rows, rows)
            y, count = _pool_mean_minus_token(u_ref, r0, rows, grp, ci == 0)
            yb = y.astype(BF16)
            yw = jnp.dot(yb, w_ref[...], preferred_element_type=F32)
            dy2 = dm_ref[pl.ds(r0, rows), :]
            ds_ref[...] += jnp.sum(dy2 * yw, axis=0, keepdims=True)
            dyw = (dy2 * s_ref[...]).astype(BF16)
            dw_ref[...] += lax.dot_general(yb, dyw, (((0,), (0,)), ((), ())), preferred_element_type=F32)
            dy = lax.dot_general(dyw, w_ref[...], (((1,), (1,)), ((), ())), preferred_element_type=F32)
            e_ref[pl.ds(r0, rows), :] = dy / count
            return carry

        lax.fori_loop(0, t // rows, chunk, 0)

        def chunk2(ci, carry):
            r0 = pl.multiple_of(ci * rows, rows)
            full = e_ref[pl.ds(r0, rows + POOL_HALO), :]
            s = full
            for step in range(n_grp):
                shifted = pltpu.roll(s, rows + POOL_HALO - (1 << step), axis=0)
                s = s + jnp.where(grp >= step, shifted, 0.0)
            e = full[:rows, :]
            tpos = r0 + lax.broadcasted_iota(jnp.int32, (rows, gw), 0)
            count = jnp.minimum(tpos + 1, 2 << grp).astype(F32)
            du_ref[pl.ds(r0, rows), :] = (s[:rows, :] - e * count).astype(BF16)
            return carry

        lax.fori_loop(0, t // rows, chunk2, 0)

    return pl.pallas_call(
        body, name="pool_bwd", grid=(n_grp,),
        in_specs=[pl.BlockSpec((t, gw), lambda g: (0, g)), pl.BlockSpec((t, gw), lambda g: (0, g)),
                  pl.BlockSpec((None, gw, gw), lambda g: (g, 0, 0)),
                  pl.BlockSpec((None, 1, gw), lambda g: (g, 0, 0))],
        out_specs=[pl.BlockSpec((t, gw), lambda g: (0, g)), pl.BlockSpec((None, gw, gw), lambda g: (g, 0, 0)),
                   pl.BlockSpec((None, 1, gw), lambda g: (g, 0, 0))],
        out_shape=[jax.ShapeDtypeStruct((t, pw), BF16), jax.ShapeDtypeStruct((n_grp, gw, gw), F32),
                   jax.ShapeDtypeStruct((n_grp, 1, gw), F32)],
        scratch_shapes=[pltpu.VMEM((t + POOL_HALO, gw), F32)],
        compiler_params=_params(("parallel",)),
    )(u, dmix, w_pool, pool_scale)


def _sb_scores(q, k_blk, scale, mask):
    z = lax.dot_general(q, k_blk, (((1,), (1,)), ((), ())), preferred_element_type=F32) * scale
    log_not = jnp.minimum(-z, 0.0) - jnp.log(1.0 + jnp.exp(-jnp.abs(z)))
    return z, (log_not if mask is None else jnp.where(mask, log_not, 0.0))


def _sb_weights(e, mask):
    a = jnp.exp(e)
    return a if mask is None else jnp.where(mask, a, 0.0)


EXP_IS_ZERO_BELOW = -104.0


def _weights_alive(after):
    return (jnp.max(after) >= EXP_IS_ZERO_BELOW).astype(jnp.int32)


def _split_dot(vs, tri):
    parts = []
    for v in vs:
        hi = v.astype(BF16)
        parts += [hi, (v - hi.astype(F32)).astype(BF16)]
    prod = jnp.dot(jnp.concatenate(parts, axis=0), tri, preferred_element_type=F32)
    m = vs[0].shape[0]
    return [prod[2 * k * m:(2 * k + 1) * m] + prod[(2 * k + 1) * m:(2 * k + 2) * m] for k in range(len(vs))]


def _head(ref, h, rows=None):
    cols = slice(h * HEAD_DIM, (h + 1) * HEAD_DIM)
    return ref[:, cols] if rows is None else ref[rows, cols]


def _attn_fwd(ub, t, nh, hg, phases=()):
    scale = float(1.0 / (HEAD_DIM ** 0.5))
    ng = nh // hg

    def body(q_ref, k_ref, v_ref, o_ref):
        i = pl.program_id(1)
        row = lax.broadcasted_iota(jnp.int32, (QB, KB), 0)
        col = lax.broadcasted_iota(jnp.int32, (QB, KB), 1)
        suffix = (row >= col).astype(BF16)

        def more(carry):
            return jnp.logical_and(carry[0] <= i, carry[3] > 0)

        def block(n, accs, afters, mask):
            rows = pl.ds(pl.multiple_of((i - n) * KB, KB), KB)
            new_accs, new_afters = [], []
            scores = [_sb_scores(_head(q_ref, h), _head(k_ref, h, rows), scale, mask) for h in range(hg)]
            withins = _split_dot([log_not for _, log_not in scores], suffix)
            for h in range(hg):
                z, log_not = scores[h]
                a = _sb_weights(z + withins[h] + afters[h], mask)
                new_accs.append(accs[h] + jnp.dot(a.astype(BF16), _head(v_ref, h, rows),
                                                  preferred_element_type=F32))
                new_afters.append(afters[h] + jnp.sum(log_not, axis=1, keepdims=True))
            return n + 1, tuple(new_accs), tuple(new_afters), _weights_alive(functools.reduce(jnp.maximum, new_afters))

        first = block(jnp.int32(0), tuple(jnp.zeros((QB, HEAD_DIM), F32) for _ in range(hg)),
                      tuple(jnp.zeros((QB, 1), F32) for _ in range(hg)), col < row)
        _, accs, _, _ = lax.while_loop(more, lambda carry: block(carry[0], carry[1], carry[2], None), first)
        for h in range(hg):
            o_ref[:, h * HEAD_DIM:(h + 1) * HEAD_DIM] = accs[h].astype(BF16)

    wide = hg * HEAD_DIM
    outs, phase_outs = _call(
        body, "attn_fwd", (ng, t // QB),
        [pl.BlockSpec((QB, wide), lambda g, i: (i, ng + g)), pl.BlockSpec((t, wide), lambda g, i: (0, 2 * ng + g)),
         pl.BlockSpec((t, wide), lambda g, i: (0, 3 * ng + g))],
        [pl.BlockSpec((QB, wide), lambda g, i: (i, g))], [jax.ShapeDtypeStruct((t, nh * HEAD_DIM), BF16)], [],
        ("parallel", "arbitrary"), (ub, ub, ub), phases)
    return outs[0], phase_outs


def _attn_bwd(ub, dmix, t, nh, hg, phases=()):
    scale = float(1.0 / (HEAD_DIM ** 0.5))
    width = nh * HEAD_DIM
    ng = nh // hg

    def body(q_ref, k_ref, v_ref, do_ref, dq_ref, dk_ref, dv_ref, g_ref, z_ref):
        i = pl.program_id(1)

        @pl.when(i == 0)
        def _():
            dk_ref[...] = jnp.zeros_like(dk_ref)
            dv_ref[...] = jnp.zeros_like(dv_ref)

        row = lax.broadcasted_iota(jnp.int32, (QB, KB), 0)
        col = lax.broadcasted_iota(jnp.int32, (QB, KB), 1)
        suffix = (row >= col).astype(BF16)
        prefix = (row <= col).astype(BF16)

        def more(carry):
            return jnp.logical_and(carry[0] <= i, carry[2] > 0)

        def down(n, afters, mask):
            ks = pl.multiple_of((i - n) * KB, KB)
            rows = pl.ds(ks, KB)
            new_afters = []
            scores = [_sb_scores(_head(q_ref, h), _head(k_ref, h, rows), scale, mask) for h in range(hg)]
            withins = _split_dot([log_not for _, log_not in scores], suffix)
            for h in range(hg):
                do = _head(do_ref, h).astype(BF16)
                z, log_not = scores[h]
                a = _sb_weights(z + withins[h] + afters[h], mask)
                da = lax.dot_general(do, _head(v_ref, h, rows), (((1,), (1,)), ((), ())),
                                     preferred_element_type=F32)
                g_ref[h, :, pl.ds(ks, KB)] = a * da
                z_ref[h, :, pl.ds(ks, KB)] = z
                dv_ref[rows, h * HEAD_DIM:(h + 1) * HEAD_DIM] += lax.dot_general(
                    a.astype(BF16), do, (((0,), (0,)), ((), ())), preferred_element_type=F32)
                new_afters.append(afters[h] + jnp.sum(log_not, axis=1, keepdims=True))
            return n + 1, tuple(new_afters), _weights_alive(functools.reduce(jnp.maximum, new_afters))

        diagonal = col < row
        first = down(jnp.int32(0), tuple(jnp.zeros((QB, 1), F32) for _ in range(hg)), diagonal)
        visited, _, _ = lax.while_loop(more, lambda carry: down(carry[0], carry[1], None), first)

        def up(kb, carry, mask):
            dqs, befores = carry
            ks = pl.multiple_of(kb * KB, KB)
            rows = pl.ds(ks, KB)
            new_dqs, new_befores = [], []
            gs = [g_ref[h, :, pl.ds(ks, KB)] for h in range(hg)]
            g_withins = _split_dot(gs, prefix)
            for h in range(hg):
                g = gs[h]
                z = z_ref[h, :, pl.ds(ks, KB)]
                g_upto = g_withins[h] + befores[h]
                dz = g - jax.nn.sigmoid(z) * g_upto
                dz = dz if mask is None else jnp.where(mask, dz, 0.0)
                dzs = (dz * scale).astype(BF16)
                new_dqs.append(dqs[h] + jnp.dot(dzs, _head(k_ref, h, rows), preferred_element_type=F32))
                dk_ref[rows, h * HEAD_DIM:(h + 1) * HEAD_DIM] += lax.dot_general(
                    dzs, _head(q_ref, h), (((0,), (0,)), ((), ())), preferred_element_type=F32)
                new_befores.append(befores[h] + jnp.sum(g, axis=1, keepdims=True))
            return tuple(new_dqs), tuple(new_befores)

        below = lax.fori_loop(i + 1 - visited, i, lambda kb, carry: up(kb, carry, None),
                              (tuple(jnp.zeros((QB, HEAD_DIM), F32) for _ in range(hg)),
                               tuple(jnp.zeros((QB, 1), F32) for _ in range(hg))))
        dqs, _ = up(i, below, diagonal)
        for h in range(hg):
            dq_ref[:, h * HEAD_DIM:(h + 1) * HEAD_DIM] = dqs[h].astype(BF16)

    wide = hg * HEAD_DIM
    tile = lambda off: pl.BlockSpec((QB, wide), lambda g, i: (i, off + g))
    strip = lambda off: pl.BlockSpec((t, wide), lambda g, i: (0, off + g))
    return _call(
        body, "attn_bwd", (ng, t // QB), [tile(ng), strip(2 * ng), strip(3 * ng), tile(ng)],
        [tile(0), strip(0), strip(0)],
        [jax.ShapeDtypeStruct((t, width), BF16), jax.ShapeDtypeStruct((t, width), F32),
         jax.ShapeDtypeStruct((t, width), F32)],
        [pltpu.VMEM((hg, QB, t), F32), pltpu.VMEM((hg, QB, t), F32)], ("parallel", "arbitrary"),
        (ub, ub, ub, dmix), phases)


def _row_tile(rows, cols, pref_bytes=2 * 1024 * 1024):
    tr = max(8, pref_bytes // (4 * cols))
    while rows % tr:
        tr //= 2
    return max(tr, 1)


def _pair_sum(name, g, s, c_idx):
    _, _, r2, cols = g.shape
    tr = _row_tile(r2, cols)

    def body(c_ref, g_ref, s_ref, o_ref):
        o_ref[...] = (g_ref[...] + s_ref[...]).astype(BF16)

    return pl.pallas_call(
        body, name=name,
        grid_spec=pltpu.PrefetchScalarGridSpec(
            num_scalar_prefetch=1, grid=(N_CHIPS, r2 // tr),
            in_specs=[pl.BlockSpec((None, None, tr, cols), lambda p, i, c: (p, c[0], i, 0)),
                      pl.BlockSpec((None, tr, cols), lambda p, i, c: (p, i, 0))],
            out_specs=pl.BlockSpec((None, tr, cols), lambda p, i, c: (p, i, 0))),
        out_shape=jax.ShapeDtypeStruct((N_CHIPS, r2, cols), BF16),
        compiler_params=_params(("parallel", "parallel")),
    )(c_idx, g, s)


def _chip_sum(name, g, s, received, place):
    _, _, r2, cols = g.shape
    tr = _row_tile(r2, cols)
    counts = [r.shape[0] for r in received]

    def body(place_ref, g_ref, s_ref, *refs):
        total = g_ref[...] + s_ref[...]
        for r_ref, n in zip(refs[:-1], counts):
            for k in range(n):
                total = total + r_ref[k].astype(F32)
        refs[-1][...] = total

    return pl.pallas_call(
        body, name=name,
        grid_spec=pltpu.PrefetchScalarGridSpec(
            num_scalar_prefetch=1, grid=(r2 // tr,),
            in_specs=[pl.BlockSpec((None, None, tr, cols), lambda i, p: (p[0], p[1], i, 0)),
                      pl.BlockSpec((None, tr, cols), lambda i, p: (p[0], i, 0)),
                      *[pl.BlockSpec((n, tr, cols), lambda i, p: (0, i, 0)) for n in counts]],
            out_specs=pl.BlockSpec((None, tr, cols), lambda i, p: (p[1], i, 0))),
        out_shape=jax.ShapeDtypeStruct((2, r2, cols), F32),
        compiler_params=_params(("parallel",)),
    )(place, g, s, *received)


def _cast_into_slot(name, w, place):
    rows, cols = w.shape
    r2 = rows // 2
    tr = _row_tile(r2, cols)
    nb = r2 // tr

    def body(place_ref, w_ref, o_ref):
        o_ref[...] = w_ref[...].astype(BF16)

    return pl.pallas_call(
        body, name=name,
        grid_spec=pltpu.PrefetchScalarGridSpec(
            num_scalar_prefetch=1, grid=(2, nb),
            in_specs=[pl.BlockSpec((tr, cols), lambda h, i, s: (h * nb + i, 0))],
            out_specs=pl.BlockSpec((None, None, tr, cols), lambda h, i, s: (s[0], h, i, 0))),
        out_shape=jax.ShapeDtypeStruct((N_CHIPS, 2, r2, cols), BF16),
        compiler_params=_params(("parallel", "parallel")),
    )(place, w)


def _colsum(name, a):
    def body(a_ref, o_ref):
        o_ref[...] = jnp.sum(a_ref[...], axis=0, keepdims=True)

    whole = lambda shape: pl.BlockSpec(shape, lambda i: (0, 0))
    return pl.pallas_call(
        body, name=name, grid=(1,), in_specs=[whole(a.shape)], out_specs=whole((1, a.shape[1])),
        out_shape=jax.ShapeDtypeStruct((1, a.shape[1]), F32), compiler_params=_params(("arbitrary",)),
    )(a)


def _adamw(name, w, g, m, v):
    rows, cols = w.shape
    tr = _row_tile(rows, cols, 1024 * 1024)

    def body(w_ref, g_ref, m_ref, v_ref, g_out_ref, d_ref, nm_ref, nv_ref):
        g_t = g_ref[...]
        m_t = ADAM_B1 * m_ref[...] + (1.0 - ADAM_B1) * g_t
        v_t = ADAM_B2 * v_ref[...] + (1.0 - ADAM_B2) * jnp.square(g_t)
        m_hat = m_t / (1.0 - ADAM_B1 ** ADAM_STEP)
        v_hat = v_t / (1.0 - ADAM_B2 ** ADAM_STEP)
        g_out_ref[...] = g_t
        d_ref[...] = -ADAM_LR * (m_hat / (jnp.sqrt(v_hat) + ADAM_EPS) + ADAM_WD * w_ref[...])
        nm_ref[...] = m_t
        nv_ref[...] = v_t

    spec = pl.BlockSpec((tr, cols), lambda i: (i, 0))
    shape = jax.ShapeDtypeStruct((rows, cols), F32)
    return pl.pallas_call(
        body, name=name, grid=(rows // tr,), in_specs=[spec] * 4, out_specs=[spec] * 4, out_shape=[shape] * 4,
        compiler_params=_params(("parallel",)),
    )(w, g, m, v)


def _all_reduce_small(packed, phases=()):
    rows, cols = packed.shape

    def body(in_ref, out_ref, all_ref, send_sems, recv_sems):
        x, y, c = _pos()
        me = 4 * x + 2 * y + c
        all_ref[me] = in_ref[...]
        cps = []
        for r in range(1, N_DEV):
            bx, by, bc = (r >> 2) & 1, (r >> 1) & 1, r & 1
            peer = (1 - x if bx else x, 1 - y if by else y, 1 - c if bc else c)
            cp = pltpu.make_async_remote_copy(
                src_ref=in_ref, dst_ref=all_ref.at[me], send_sem=send_sems.at[r - 1], recv_sem=recv_sems.at[r - 1],
                device_id=peer, device_id_type=MESH)
            cp.start()
            cps.append(cp)
        for cp in cps:
            cp.wait()
        total = all_ref[0]
        for d in range(1, N_DEV):
            total = total + all_ref[d]
        out_ref[...] = total

    vmem = pl.BlockSpec(memory_space=pltpu.VMEM)
    outs, phase_outs = _call(
        body, "all_reduce_small", (1,), [vmem], [vmem], [jax.ShapeDtypeStruct((rows, cols), F32)],
        [pltpu.VMEM((N_DEV, rows, cols), F32), pltpu.SemaphoreType.DMA((N_DEV - 1,)),
         pltpu.SemaphoreType.DMA((N_DEV - 1,))], ("arbitrary",), (packed,), phases)
    return outs[0], phase_outs


def kernel(x, ln_in_g, ln_in_b, w_in, w_pool, pool_scale, w_out, ln1_g, ln1_b, w_ff1, b_ff1, w_ff2, b_ff2, ln2_g, ln2_b, loss_target, m_ln_in_g, m_ln_in_b, m_w_in, m_w_pool, m_pool_scale, m_w_out, m_ln1_g, m_ln1_b, m_w_ff1, m_b_ff1, m_w_ff2, m_b_ff2, m_ln2_g, m_ln2_b, v_ln_in_g, v_ln_in_b, v_w_in, v_w_pool, v_pool_scale, v_w_out, v_ln1_g, v_ln1_b, v_w_ff1, v_b_ff1, v_w_ff2, v_b_ff2, v_ln2_g, v_ln2_b):
    t, d = x.shape[1], x.shape[2]
    pw = d // 2
    n_grp = len(POOL_WINDOWS)
    gw = pw // n_grp
    gwc = gw // N_CHIPS
    nh = pw // HEAD_DIM
    ff = w_ff1.shape[2] * N_CHIPS
    assert w_in.shape[0] == 1 and w_in.shape[2] * N_CHIPS == 2 * d and gwc <= 128

    x_idx, y_idx, c_idx = _pos()
    chip_arr = jnp.reshape(2 * x_idx + y_idx, (1,)).astype(jnp.int32)
    c_arr = jnp.reshape(c_idx, (1,)).astype(jnp.int32)
    place = jnp.concatenate([chip_arr, c_arr])

    xs = x.reshape(t, d)
    target = loss_target.reshape(t, d)
    row = lambda vec: vec.reshape(1, -1)

    scale_tile = jnp.zeros((1, 8, 128), F32).at[0, :n_grp, :gwc].set(pool_scale[0])
    scale_slots = lax.dynamic_update_slice(jnp.zeros((N_CHIPS, 8, 128), F32), scale_tile, (chip_arr[0], 0, 0))
    shards = dict(w_in=w_in[0], w_out=w_out[0], w_ff1=w_ff1[0], w_ff2=w_ff2[0], w_pool=w_pool[0].reshape(gw, gw))
    slot = {nm: _cast_into_slot("cast_" + nm, w, place) for nm, w in shards.items()}
    unsplit = lambda s: s.reshape(N_CHIPS, 2 * s.shape[2], s.shape[3])

    (h0, h0b, xhat0, rstd0), ((win_s,), (wpool_s,), (scale_g,)) = _ln_fwd(
        "ln_in_fwd", xs, row(ln_in_g), row(ln_in_b),
        phases=[_gather_ici_phase(slot["w_in"]), _gather_ici_phase(slot["w_pool"]), _gather_whole_phase(scale_slots)])
    (win_s,), (wpool_s,) = _comm_call("gather_d2d_first", [_gather_d2d_phase(win_s), _gather_d2d_phase(wpool_s)])
    win_g = unsplit(win_s)
    wpool_full = unsplit(wpool_s).reshape(N_CHIPS, n_grp, gwc, gw).transpose(1, 0, 2, 3).reshape(n_grp, gw, gw)
    scale_full = scale_g[:, :n_grp, :gwc].transpose(1, 0, 2).reshape(n_grp, 1, gw)

    def store_f32(acc, extra_refs, out_refs):
        out_refs[0][...] = acc

    def pool_f32_all_bf16(acc, extra_refs, out_refs):
        @pl.when(pl.program_id(1) == 0)
        def _():
            out_refs[0][...] = acc

        out_refs[1][...] = acc.astype(BF16)

    assert w_in.shape[2] == pw
    (u, ub), ((wout_s,),) = _mm_nn(
        "in_proj", h0b, win_g, (), lambda tm, tn: [],
        [jax.ShapeDtypeStruct((t, pw), F32), jax.ShapeDtypeStruct((t, 2 * d), BF16)],
        lambda tm, tn: [pl.BlockSpec((tm, tn), lambda i, j, kk: (i, 0)), _tile_spec(tm, tn)],
        pool_f32_all_bf16, b_chips=True, tn=pw, phases=[_gather_ici_phase(slot["w_out"])])
    y_pool = _pool_fwd(u, wpool_full, scale_full, t, pw)
    y_sb, ((wff1_s,), (wout_s,)) = _attn_fwd(
        ub, t, nh, min(nh, 4), phases=[_gather_ici_phase(slot["w_ff1"]), _gather_d2d_phase(wout_s)])
    wout_full = unsplit(wout_s).reshape(d, d)
    mix_in = jnp.concatenate([y_pool, y_sb], axis=1)

    def residual(acc, extra_refs, out_refs):
        out_refs[0][...] = ALPHA * extra_refs[0][...] + acc

    (r1,), ((wff1_s,), (wff2_s,)) = _mm_nn(
        "out_proj", mix_in, wout_full, (h0,), lambda tm, tn: [_tile_spec(tm, tn)],
        [jax.ShapeDtypeStruct((t, d), F32)], lambda tm, tn: [_tile_spec(tm, tn)], residual,
        phases=[_gather_d2d_phase(wff1_s), _gather_ici_phase(slot["w_ff2"], others=(0,))])
    wff1_g = unsplit(wff1_s)
    h1, h1b, xhat1, rstd1 = _ln_fwd("ln1_fwd", r1, ln1_g, ln1_b)

    def relu_sq(acc, extra_refs, out_refs):
        p = jnp.maximum(acc + extra_refs[0][...], 0.0)
        out_refs[0][...] = p
        out_refs[1][...] = jnp.square(p).astype(BF16)

    (relu_z, act_b), ((wff2_s,),) = _mm_nn(
        "ff1", h1b, wff1_g, (b_ff1,), lambda tm, tn: [_row_spec(tn)],
        [jax.ShapeDtypeStruct((t, ff), F32), jax.ShapeDtypeStruct((t, ff), BF16)],
        lambda tm, tn: [_tile_spec(tm, tn)] * 2, relu_sq, b_chips=True,
        phases=[_gather_ici_phase(wff2_s, others=(1, 2))])
    ((wff2_s,),) = _comm_call("gather_d2d_last", [_gather_d2d_phase(wff2_s)])
    wff2_full = unsplit(wff2_s).reshape(ff, d)

    def residual_bias(acc, extra_refs, out_refs):
        out_refs[0][...] = ALPHA * extra_refs[0][...] + (acc + extra_refs[1][...])

    r2 = _mm_nn("ff2", act_b, wff2_full, (h1, b_ff2), lambda tm, tn: [_tile_spec(tm, tn), _row_spec(tn)],
                [jax.ShapeDtypeStruct((t, d), F32)], lambda tm, tn: [_tile_spec(tm, tn)], residual_bias)[0]

    dr2, dr2b, loss_tile, g_ln2_g, g_ln2_b, g_b_ff2 = _ln2_loss_bwd(r2, target, ln2_g, ln2_b)
    loss = lax.psum(loss_tile[0, 0], ("x", "y", "c"))

    halves = lambda g: g.reshape(N_CHIPS, 2, g.shape[1] // 2, g.shape[2])
    g_ff2 = halves(_mm_tn("grad_w_ff2", act_b, dr2b).reshape(N_CHIPS, ff // N_CHIPS, d))

    def relu_sq_bwd(acc, extra_refs, out_refs):
        dz = acc * (2.0 * extra_refs[0][...])
        out_refs[0][...] = dz.astype(BF16)
        rows = lax.broadcasted_iota(jnp.int32, out_refs[1].shape, 0)
        out_refs[1][...] = jnp.where(rows == 0, jnp.sum(dz, axis=0, keepdims=True), 0.0)

    tm_ff = _tile(t, 1024)
    (dz1b, g_b_ff1_parts), ((s_ff2,),) = _mm_nt(
        "ff2_bwd", dr2b, wff2_full, (relu_z,), lambda tm, tn: [_tile_spec(tm, tn)],
        [jax.ShapeDtypeStruct((t, ff), BF16), jax.ShapeDtypeStruct((8 * (t // tm_ff), ff), F32)],
        lambda tm, tn: [_tile_spec(tm, tn), pl.BlockSpec((8, tn), lambda i, j, kk: (i, j))], relu_sq_bwd,
        phases=[_swap_phase(g_ff2)])
    p_ff2 = _pair_sum("pair_sum_w_ff2", g_ff2, s_ff2, c_arr)
    g_ff1, ((r_ff2_a,),) = _mm_tn("grad_w_ff1", h1b, dz1b, out_chips=True,
                                  phases=[_scatter_phase(p_ff2, others=(0, 1))])
    g_ff1 = halves(g_ff1)

    def plus_alpha(acc, extra_refs, out_refs):
        out_refs[0][...] = ALPHA * extra_refs[0][...] + acc

    (dh1,), ((s_ff1,), (r_ff2_b,)) = _mm_nt(
        "ff1_bwd", dz1b, wff1_g, (dr2,), lambda tm, tn: [_tile_spec(tm, tn)],
        [jax.ShapeDtypeStruct((t, d), F32)], lambda tm, tn: [_tile_spec(tm, tn)], plus_alpha, b_chips=True,
        phases=[_swap_phase(g_ff1), _scatter_phase(p_ff2, others=(2,))])
    q_ff2 = _chip_sum("chip_sum_w_ff2", g_ff2, s_ff2, [r_ff2_a, r_ff2_b], place)
    p_ff1 = _pair_sum("pair_sum_w_ff1", g_ff1, s_ff1, c_arr)
    (dr1, dr1b, g_ln1_g, g_ln1_b), ((q_ff2,),) = _ln_bwd("ln1_bwd", dh1, xhat1, rstd1, ln1_g,
                                                         phases=[_assemble_phase(q_ff2)])

    g_out = halves(_mm_tn("grad_w_out", mix_in, dr1b).reshape(N_CHIPS, d // N_CHIPS, d))
    (dmix,), ((s_out,),) = _mm_nt(
        "out_proj_bwd", dr1b, wout_full, (), lambda tm, tn: [], [jax.ShapeDtypeStruct((t, d), F32)],
        lambda tm, tn: [_tile_spec(tm, tn)], store_f32, phases=[_swap_phase(g_out)])
    p_out = _pair_sum("pair_sum_w_out", g_out, s_out, c_arr)
    du_pool, g_w_pool_full, g_scale_full = _pool_bwd(u, dmix, wpool_full, scale_full, t, pw)
    (dq, dk, dv), ((r_ff1,), (r_out,)) = _attn_bwd(ub, dmix, t, nh, 1,
                                                   phases=[_scatter_phase(p_ff1), _scatter_phase(p_out)])
    q_ff1 = _chip_sum("chip_sum_w_ff1", g_ff1, s_ff1, [r_ff1], place)
    q_out = _chip_sum("chip_sum_w_out", g_out, s_out, [r_out], place)
    du = jnp.concatenate([du_pool, dq, dk.astype(BF16), dv.astype(BF16)], axis=1)
    g_in, ((q_ff1,), (q_out,)) = _mm_tn("grad_w_in", h0b, du, out_chips=True,
                                        phases=[_assemble_phase(q_ff1), _assemble_phase(q_out)])
    g_in = halves(g_in)
    g_pool = halves(g_w_pool_full.reshape(n_grp, N_CHIPS, gwc, gw).transpose(1, 0, 2, 3).reshape(N_CHIPS, gw, gw))
    (s_in,), (s_pool,) = _comm_call("rs_swap_last", [_swap_phase(g_in), _swap_phase(g_pool)])
    p_in = _pair_sum("pair_sum_w_in", g_in, s_in, c_arr)
    p_pool = _pair_sum("pair_sum_w_pool", g_pool, s_pool, c_arr)
    (dh0,), ((r_in,), (r_pool,)) = _mm_nt(
        "in_proj_bwd", du, win_g, (dr1,), lambda tm, tn: [_tile_spec(tm, tn)],
        [jax.ShapeDtypeStruct((t, d), F32)], lambda tm, tn: [_tile_spec(tm, tn)], plus_alpha, b_chips=True,
        phases=[_scatter_phase(p_in), _scatter_phase(p_pool)])
    q_in = _chip_sum("chip_sum_w_in", g_in, s_in, [r_in], place)
    q_pool = _chip_sum("chip_sum_w_pool", g_pool, s_pool, [r_pool], place)
    dx, _, g_ln_in_g, g_ln_in_b = _ln_bwd("ln_in_bwd", dh0, xhat0, rstd0, row(ln_in_g))

    lane = 2048 if d % 2048 == 0 else d
    small_names = ["ln_in_g", "ln_in_b", "ln1_g", "ln1_b", "b_ff1", "b_ff2", "ln2_g", "ln2_b"]
    small_w = dict(ln_in_g=ln_in_g, ln_in_b=ln_in_b, ln1_g=ln1_g, ln1_b=ln1_b, b_ff1=b_ff1, b_ff2=b_ff2, ln2_g=ln2_g,
                   ln2_b=ln2_b)
    small_m = dict(ln_in_g=m_ln_in_g, ln_in_b=m_ln_in_b, ln1_g=m_ln1_g, ln1_b=m_ln1_b, b_ff1=m_b_ff1, b_ff2=m_b_ff2,
                   ln2_g=m_ln2_g, ln2_b=m_ln2_b)
    small_v = dict(ln_in_g=v_ln_in_g, ln_in_b=v_ln_in_b, ln1_g=v_ln1_g, ln1_b=v_ln1_b, b_ff1=v_b_ff1, b_ff2=v_b_ff2,
                   ln2_g=v_ln2_g, ln2_b=v_ln2_b)
    small_g = dict(ln_in_g=g_ln_in_g, ln_in_b=g_ln_in_b, ln1_g=g_ln1_g, ln1_b=g_ln1_b, b_ff2=g_b_ff2, ln2_g=g_ln2_g,
                   ln2_b=g_ln2_b)

    def pack(parts):
        flat = jnp.concatenate([p.reshape(-1) for p in parts])
        n_rows = -(-flat.shape[0] // lane)
        n_rows = -(-n_rows // 8) * 8
        return jnp.pad(flat, (0, n_rows * lane - flat.shape[0])).reshape(n_rows, lane)

    small_g["b_ff1"] = _colsum("b_ff1_colsum", g_b_ff1_parts)
    summed, ((q_in,), (q_pool,)) = _all_reduce_small(
        pack([small_g[nm] for nm in small_names] + [g_scale_full]),
        phases=[_assemble_phase(q_in), _assemble_phase(q_pool)])
    summed = summed.reshape(-1)

    big = {}
    for nm, q, w, m, v in [("w_in", q_in, w_in, m_w_in, v_w_in), ("w_out", q_out, w_out, m_w_out, v_w_out),
                           ("w_ff1", q_ff1, w_ff1, m_w_ff1, v_w_ff1), ("w_ff2", q_ff2, w_ff2, m_w_ff2, v_w_ff2),
                           ("w_pool", q_pool, w_pool, m_w_pool, v_w_pool)]:
        g = q.reshape(2 * q.shape[1], q.shape[2])
        flat = lambda arr: arr.reshape(g.shape)
        big[nm] = tuple(arr.reshape(w.shape) for arr in _adamw("adamw_" + nm, flat(w), g, flat(m), flat(v)))

    g_small, off = {}, 0
    for nm in small_names:
        g_small[nm] = summed[off:off + small_w[nm].size]
        off += small_w[nm].size
    g_scale_all = summed[off:off + n_grp * gw].reshape(n_grp, N_CHIPS, gwc)
    g_scale = lax.dynamic_index_in_dim(g_scale_all, chip_arr[0], axis=1, keepdims=False)

    order = small_names + ["pool_scale"]
    small_w["pool_scale"], small_m["pool_scale"], small_v["pool_scale"] = pool_scale, m_pool_scale, v_pool_scale
    g_small["pool_scale"] = g_scale
    _, delta_s, new_m_s, new_v_s = _adamw("adamw_small", pack([small_w[nm] for nm in order]),
                                          pack([g_small[nm] for nm in order]), pack([small_m[nm] for nm in order]),
                                          pack([small_v[nm] for nm in order]))
    small = {}
    off = 0
    for nm in order:
        size, shape = small_w[nm].size, small_w[nm].shape
        cut = lambda arr: arr.reshape(-1)[off:off + size].reshape(shape)
        small[nm] = (g_small[nm].reshape(shape), cut(delta_s), cut(new_m_s), cut(new_v_s))
        off += size

    every = {**big, **small}
    weight_order = ["ln_in_g", "ln_in_b", "w_in", "w_pool", "pool_scale", "w_out", "ln1_g", "ln1_b", "w_ff1", "b_ff1",
                    "w_ff2", "b_ff2", "ln2_g", "ln2_b"]
    grads = [every[nm][0] for nm in weight_order]
    deltas = [every[nm][1] for nm in weight_order]
    new_ms = [every[nm][2] for nm in weight_order]
    new_vs = [every[nm][3] for nm in weight_order]
    return (loss, dx.reshape(x.shape), *grads, *deltas, *new_ms, *new_vs)
```

```python
import functools

import jax
import jax.numpy as jnp
from jax import lax
from jax.experimental import pallas as pl
from jax.experimental.pallas import tpu as pltpu

F32 = jnp.float32
BF16 = jnp.bfloat16
MESH = pl.DeviceIdType.MESH

HEAD_DIM = 128
POOL_WINDOWS = (2, 4, 8, 16)
POOL_HALO = 16
LN_EPS = 1e-5
ALPHA = 2.0 ** 0.25
ADAM_LR, ADAM_B1, ADAM_B2, ADAM_EPS, ADAM_WD, ADAM_STEP = 0.001, 0.9, 0.999, 1e-08, 0.01, 10

QB = 256
KB = 256
VMEM_LIMIT = 56 * 1024 * 1024
N_CHIPS = 4
N_DEV = 8


def _params(sem=None):
    return pltpu.CompilerParams(dimension_semantics=sem, vmem_limit_bytes=VMEM_LIMIT)


def _tile(dim, pref):
    return pref if dim % pref == 0 else dim


def _pos():
    return lax.axis_index("x"), lax.axis_index("y"), lax.axis_index("c")


def _other_chips(x, y):
    return [(1 - x, y), (x, 1 - y), (1 - x, 1 - y)]


ANY = pl.BlockSpec(memory_space=pl.ANY)


class _Phase:
    def __init__(self, ins, out_shapes, aliases, n_sems, build):
        self.ins, self.out_shapes, self.aliases, self.n_sems, self.build = ins, out_shapes, aliases, n_sems, build


def _remote(src, dst, send_sems, recv_sems, k, to):
    return pltpu.make_async_remote_copy(src_ref=src, dst_ref=dst, send_sem=send_sems.at[k], recv_sem=recv_sems.at[k],
                                        device_id=to, device_id_type=MESH)


def _swap_phase(g):
    def build(ins, outs, ss, rs):
        x, y, c = _pos()
        cp = _remote(ins[0].at[:, 1 - c], outs[0], ss, rs, 0, (x, y, 1 - c))
        return [cp], [cp]

    return _Phase([g], [jax.ShapeDtypeStruct((N_CHIPS, g.shape[2], g.shape[3]), g.dtype)], {}, 1, build)


ALL_OTHERS = (0, 1, 2)


def _scatter_phase(p, others=ALL_OTHERS):
    def build(ins, outs, ss, rs):
        x, y, c = _pos()
        chips = _other_chips(x, y)
        cps = [_remote(ins[0].at[2 * chips[j][0] + chips[j][1]], outs[0].at[k], ss, rs, k, (*chips[j], c))
               for k, j in enumerate(others)]
        return cps, cps

    return _Phase([p], [jax.ShapeDtypeStruct((len(others), p.shape[1], p.shape[2]), p.dtype)], {}, len(others), build)


def _assemble_phase(q):
    def build(ins, outs, ss, rs):
        x, y, c = _pos()
        mine, other = outs[0].at[c], outs[0].at[1 - c]
        return [_remote(mine, mine, ss, rs, 0, (x, y, 1 - c))], [_remote(other, other, ss, rs, 0, (x, y, c))]

    return _Phase([q], [jax.ShapeDtypeStruct(q.shape, q.dtype)], {0: 0}, 1, build)


def _gather_ici_phase(slot, others=ALL_OTHERS):
    def build(ins, outs, ss, rs):
        x, y, c = _pos()
        chips = _other_chips(x, y)
        mine = outs[0].at[2 * x + y, c]
        sends, recvs = [], []
        for k, j in enumerate(others):
            theirs = outs[0].at[2 * chips[j][0] + chips[j][1], c]
            sends.append(_remote(mine, mine, ss, rs, k, (*chips[j], c)))
            recvs.append(_remote(theirs, theirs, ss, rs, k, (x, y, c)))
        return sends, recvs

    return _Phase([slot], [jax.ShapeDtypeStruct(slot.shape, slot.dtype)], {0: 0}, len(others), build)


def _gather_d2d_phase(slot):
    def build(ins, outs, ss, rs):
        x, y, c = _pos()
        sends, recvs = [], []
        for j, chip in enumerate(_other_chips(x, y)):
            landed = outs[0].at[2 * chip[0] + chip[1], c]
            coming = outs[0].at[2 * chip[0] + chip[1], 1 - c]
            sends.append(_remote(landed, landed, ss, rs, j, (x, y, 1 - c)))
            recvs.append(_remote(coming, coming, ss, rs, j, (x, y, c)))
        return sends, recvs

    return _Phase([slot], [jax.ShapeDtypeStruct(slot.shape, slot.dtype)], {0: 0}, 3, build)


def _gather_whole_phase(slots):
    def build(ins, outs, ss, rs):
        x, y, c = _pos()
        mine = outs[0].at[2 * x + y]
        sends, recvs = [], []
        for j, chip in enumerate(_other_chips(x, y)):
            theirs = outs[0].at[2 * chip[0] + chip[1]]
            sends.append(_remote(mine, mine, ss, rs, j, (*chip, c)))
            recvs.append(_remote(theirs, theirs, ss, rs, j, (x, y, c)))
        return sends, recvs

    return _Phase([slots], [jax.ShapeDtypeStruct(slots.shape, slots.dtype)], {0: 0}, 3, build)


def _split_refs(refs, n_in, n_out, n_scratch, phases):
    n_pin = sum(len(ph.ins) for ph in phases)
    n_pout = sum(len(ph.out_shapes) for ph in phases)
    cuts = [n_in, n_pin, n_out, n_pout, n_scratch]
    parts, at = [], 0
    for n in cuts:
        parts.append(refs[at:at + n])
        at += n
    parts.append(refs[at:])
    return parts


def _build_phases(phases, pin, pout, sems):
    built, i, o = [], 0, 0
    for k, ph in enumerate(phases):
        built.append(ph.build(pin[i:i + len(ph.ins)], pout[o:o + len(ph.out_shapes)], sems[2 * k], sems[2 * k + 1]))
        i += len(ph.ins)
        o += len(ph.out_shapes)
    return built


def _finish_phases(built):
    for _, recvs in built:
        for cp in recvs:
            cp.wait_recv()
    for sends, _ in built:
        for cp in sends:
            cp.wait_send()


def _call(body, name, grid, in_specs, out_specs, out_shape, scratch_shapes, semantics, args, phases=(),
          in_place=None):
    n_in, n_out, n_scratch = len(args), len(out_shape), len(scratch_shapes)
    aliases, in_at, out_at = dict(in_place or {}), n_in, n_out
    for ph in phases:
        aliases.update({in_at + i: out_at + o for i, o in ph.aliases.items()})
        in_at += len(ph.ins)
        out_at += len(ph.out_shapes)

    def hosted(*refs):
        ins, pin, outs, pout, scratch, sems = _split_refs(refs, n_in, n_out, n_scratch, phases)
        ids = [pl.program_id(a) for a in range(len(grid))]
        first = functools.reduce(jnp.logical_and, [i == 0 for i in ids])
        last = functools.reduce(jnp.logical_and, [i == g - 1 for i, g in zip(ids, grid)])

        @pl.when(first)
        def _():
            for sends, _ in _build_phases(phases, pin, pout, sems):
                for cp in sends:
                    cp.start()

        body(*ins, *outs, *scratch)

        @pl.when(last)
        def _():
            _finish_phases(_build_phases(phases, pin, pout, sems))

    p_args = [a for ph in phases for a in ph.ins]
    p_shapes = [s for ph in phases for s in ph.out_shapes]
    sem_shapes = [pltpu.SemaphoreType.DMA((ph.n_sems,)) for ph in phases for _ in range(2)]
    outs = pl.pallas_call(
        hosted if phases else body, name=name, grid=grid, in_specs=[*in_specs, *[ANY] * len(p_args)],
        out_specs=[*out_specs, *[ANY] * len(p_shapes)], out_shape=[*out_shape, *p_shapes],
        input_output_aliases=aliases, scratch_shapes=[*scratch_shapes, *sem_shapes],
        compiler_params=_params(("arbitrary",) * len(grid) if phases else semantics),
    )(*args, *p_args)
    phase_outs, at = [], n_out
    for ph in phases:
        phase_outs.append(list(outs[at:at + len(ph.out_shapes)]))
        at += len(ph.out_shapes)
    return list(outs[:n_out]), phase_outs


def _comm_call(name, phases):
    def body(*refs):
        _, pin, _, pout, _, sems = _split_refs(refs, 0, 0, 0, phases)
        built = _build_phases(phases, pin, pout, sems)
        for sends, _ in built:
            for cp in sends:
                cp.start()
        _finish_phases(built)

    aliases, in_at, out_at = {}, 0, 0
    for ph in phases:
        aliases.update({in_at + i: out_at + o for i, o in ph.aliases.items()})
        in_at += len(ph.ins)
        out_at += len(ph.out_shapes)
    p_args = [a for ph in phases for a in ph.ins]
    p_shapes = [s for ph in phases for s in ph.out_shapes]
    outs = pl.pallas_call(
        body, name=name, in_specs=[ANY] * len(p_args), out_specs=[ANY] * len(p_shapes), out_shape=p_shapes,
        input_output_aliases=aliases,
        scratch_shapes=[pltpu.SemaphoreType.DMA((ph.n_sems,)) for ph in phases for _ in range(2)],
    )(*p_args)
    phase_outs, at = [], 0
    for ph in phases:
        phase_outs.append(list(outs[at:at + len(ph.out_shapes)]))
        at += len(ph.out_shapes)
    return phase_outs


def _matmul(name, a, b, grid, a_spec, b_spec, contract, acc_shape, extras, extra_specs, out_shape, out_specs,
            epilogue, phases=()):
    n_extra, n_out, gk = len(extras), len(out_shape), grid[2]

    def product(a_ref, b_ref):
        return lax.dot_general(a_ref[...], b_ref[...], (contract, ((), ())), preferred_element_type=F32)

    def body_one_step(*refs):
        epilogue(product(refs[0], refs[1]), refs[2:2 + n_extra], refs[2 + n_extra:])

    def body(*refs):
        a_ref, b_ref = refs[0], refs[1]
        extra_refs = refs[2:2 + n_extra]
        out_refs = refs[2 + n_extra:2 + n_extra + n_out]
        acc_ref = refs[-1]
        kk = pl.program_id(2)

        @pl.when(kk == 0)
        def _():
            acc_ref[...] = product(a_ref, b_ref)

        @pl.when(kk > 0)
        def _():
            acc_ref[...] += product(a_ref, b_ref)

        @pl.when(kk == gk - 1)
        def _():
            epilogue(acc_ref[...], extra_refs, out_refs)

    outs, phase_outs = _call(
        body_one_step if gk == 1 else body, name, grid, [a_spec, b_spec, *extra_specs], out_specs, out_shape,
        [] if gk == 1 else [pltpu.VMEM(acc_shape, F32)], ("parallel", "arbitrary", "arbitrary"), (a, b, *extras),
        phases)
    return (outs, phase_outs) if phases else outs


def _mm_nn(name, a, b, extras, extra_specs, out_shape, out_specs, epilogue, b_chips=False, tm=1024, tn=1024,
           tk=2048, phases=(), a_parts=None):
    m, k, tm, tk, a_spec = _lhs_rows_by_k(a, tm, tk, a_parts)
    n = b.shape[1] if not b_chips else b.shape[2] * N_CHIPS
    if b_chips:
        tn = _tile(b.shape[2], tn)
        nb = b.shape[2] // tn
        b_spec = pl.BlockSpec((None, tk, tn), lambda i, j, kk: (j // nb, kk, j % nb))
    else:
        tn = _tile(n, tn)
        b_spec = pl.BlockSpec((tk, tn), lambda i, j, kk: (kk, j))
    return _matmul(name, a, b, (m // tm, n // tn, k // tk), a_spec, b_spec, ((1,), (0,)), (tm, tn), extras,
                   extra_specs(tm, tn), out_shape, out_specs(tm, tn), epilogue, phases)


def _lhs_rows_by_k(a, tm, tk, a_parts, tk_max=None):
    if a_parts is None:
        m, k = a.shape
        tm, tk = _tile(m, tm), _tile(k if tk_max is None else tk_max, tk)
        return m, k, tm, tk, pl.BlockSpec((tm, tk), lambda i, j, kk: (i, kk))
    n_parts, m, kp = a.shape
    tm, tk = _tile(m, tm), _tile(kp if tk_max is None else min(kp, tk_max), tk)
    nb = kp // tk
    return m, n_parts * kp, tm, tk, pl.BlockSpec((None, tm, tk), lambda i, j, kk: (a_parts(kk // nb), i, kk % nb))


def _mm_nt(name, a, b, extras, extra_specs, out_shape, out_specs, epilogue, b_chips=False, tm=1024, tn=1024,
           tk=2048, phases=(), a_parts=None):
    m, k, tm, tk, a_spec = _lhs_rows_by_k(a, tm, tk, a_parts, tk_max=b.shape[2] if b_chips else None)
    n = b.shape[0] if not b_chips else b.shape[1]
    tn = _tile(n, tn)
    if b_chips:
        nb = b.shape[2] // tk
        b_spec = pl.BlockSpec((None, tn, tk), lambda i, j, kk: (kk // nb, j, kk % nb))
    else:
        b_spec = pl.BlockSpec((tn, tk), lambda i, j, kk: (j, kk))
    return _matmul(name, a, b, (m // tm, n // tn, k // tk), a_spec, b_spec, ((1,), (1,)), (tm, tn), extras,
                   extra_specs(tm, tn), out_shape, out_specs(tm, tn), epilogue, phases)


def _mm_tn(name, a, b, out_chips=False, tm=1024, tn=1024, tk=2048, phases=(), a_parts=None, b_parts=None):
    if a_parts is None:
        k, m = a.shape
        tm = _tile(m, tm)
        a_spec = pl.BlockSpec((_tile(k, tk), tm), lambda i, j, kk: (kk, i))
    else:
        n_parts, k, mp = a.shape
        m, tm = n_parts * mp, _tile(mp, tm)
        nbm = mp // tm
        a_spec = pl.BlockSpec((None, _tile(k, tk), tm), lambda i, j, kk: (a_parts(i // nbm), kk, i % nbm))
    tk = _tile(k, tk)
    n = b.shape[1] if b_parts is None else b.shape[0] * b.shape[2]
    if out_chips:
        nc = n // N_CHIPS
        tn = _tile(nc, tn)
        nb = nc // tn
        out_shape = [jax.ShapeDtypeStruct((N_CHIPS, m, nc), F32)]
        out_specs = [pl.BlockSpec((None, tm, tn), lambda i, j, kk: (j // nb, i, j % nb))]
    else:
        tn = _tile(n, tn)
        out_shape = [jax.ShapeDtypeStruct((m, n), F32)]
        out_specs = [pl.BlockSpec((tm, tn), lambda i, j, kk: (i, j))]
    if b_parts is None:
        b_spec = pl.BlockSpec((tk, tn), lambda i, j, kk: (kk, j))
    else:
        nbn = b.shape[2] // tn
        b_spec = pl.BlockSpec((None, tk, tn), lambda i, j, kk: (b_parts(j // nbn), kk, j % nbn))

    def epilogue(acc, extra_refs, out_refs):
        out_refs[0][...] = acc

    res = _matmul(name, a, b, (m // tm, n // tn, k // tk), a_spec, b_spec, ((0,), (0,)), (tm, tn), (), [],
                  out_shape, out_specs, epilogue, phases)
    return (res[0][0], res[1]) if phases else res[0]


def _tile_spec(tm, tn):
    return pl.BlockSpec((tm, tn), lambda i, j, kk: (i, j))


def _row_spec(tn):
    return pl.BlockSpec((1, tn), lambda i, j, kk: (0, j))


def _ln_stats(r):
    mu = jnp.mean(r, axis=-1, keepdims=True)
    var = jnp.mean(jnp.square(r - mu), axis=-1, keepdims=True)
    rstd = lax.rsqrt(var + LN_EPS)
    return (r - mu) * rstd, rstd


def _ln_fwd(name, r, g, b, tr=256, phases=()):
    t, d = r.shape
    tr = _tile(t, tr)

    def body(r_ref, g_ref, b_ref, y_ref, yb_ref, xhat_ref, rstd_ref):
        xhat, rstd = _ln_stats(r_ref[...])
        y = xhat * g_ref[...] + b_ref[...]
        y_ref[...] = y
        yb_ref[...] = y.astype(BF16)
        xhat_ref[...] = xhat
        rstd_ref[...] = rstd

    row = pl.BlockSpec((tr, d), lambda i: (i, 0))
    vec = pl.BlockSpec((1, d), lambda i: (0, 0))
    outs, phase_outs = _call(
        body, name, (t // tr,), [row, vec, vec], [row, row, row, pl.BlockSpec((tr, 1), lambda i: (i, 0))],
        [jax.ShapeDtypeStruct((t, d), F32), jax.ShapeDtypeStruct((t, d), BF16),
         jax.ShapeDtypeStruct((t, d), F32), jax.ShapeDtypeStruct((t, 1), F32)], [], ("parallel",), (r, g, b), phases)
    return (outs, phase_outs) if phases else outs


def _ln_bwd_rows(dy, xhat, rstd, g):
    dxhat = dy * g
    m1 = jnp.mean(dxhat, axis=-1, keepdims=True)
    m2 = jnp.mean(dxhat * xhat, axis=-1, keepdims=True)
    return rstd * (dxhat - m1 - xhat * m2)


def _ln_bwd(name, dy, xhat, rstd, g, tr=256, phases=()):
    t, d = dy.shape
    tr = _tile(t, tr)

    def body(dy_ref, xhat_ref, rstd_ref, g_ref, dr_ref, drb_ref, dg_ref, db_ref):
        @pl.when(pl.program_id(0) == 0)
        def _():
            dg_ref[...] = jnp.zeros_like(dg_ref)
            db_ref[...] = jnp.zeros_like(db_ref)

        dy_t, xhat_t = dy_ref[...], xhat_ref[...]
        dr = _ln_bwd_rows(dy_t, xhat_t, rstd_ref[...], g_ref[...])
        dr_ref[...] = dr
        drb_ref[...] = dr.astype(BF16)
        dg_ref[...] += jnp.sum(dy_t * xhat_t, axis=0, keepdims=True)
        db_ref[...] += jnp.sum(dy_t, axis=0, keepdims=True)

    row = pl.BlockSpec((tr, d), lambda i: (i, 0))
    vec = pl.BlockSpec((1, d), lambda i: (0, 0))
    outs, phase_outs = _call(
        body, name, (t // tr,), [row, row, pl.BlockSpec((tr, 1), lambda i: (i, 0)), vec], [row, row, vec, vec],
        [jax.ShapeDtypeStruct((t, d), F32), jax.ShapeDtypeStruct((t, d), BF16),
         jax.ShapeDtypeStruct((1, d), F32), jax.ShapeDtypeStruct((1, d), F32)], [], ("arbitrary",),
        (dy, xhat, rstd, g), phases)
    return (outs, phase_outs) if phases else outs


def _ln2_loss_bwd(r2, target, g, b, tr=256):
    t, d = r2.shape
    tr = _tile(t, tr)

    def body(r_ref, t_ref, g_ref, b_ref, dr_ref, drb_ref, loss_ref, dg_ref, db_ref, dsum_ref):
        @pl.when(pl.program_id(0) == 0)
        def _():
            loss_ref[...] = jnp.zeros_like(loss_ref)
            dg_ref[...] = jnp.zeros_like(dg_ref)
            db_ref[...] = jnp.zeros_like(db_ref)
            dsum_ref[...] = jnp.zeros_like(dsum_ref)

        xhat, rstd = _ln_stats(r_ref[...])
        g_t = g_ref[...]
        err = xhat * g_t + b_ref[...] - t_ref[...]
        loss_ref[...] += 0.5 * jnp.sum(jnp.mean(jnp.square(err), axis=-1, keepdims=True), axis=0, keepdims=True)
        dy = err * (1.0 / d)
        dr = _ln_bwd_rows(dy, xhat, rstd, g_t)
        dr_ref[...] = dr
        drb_ref[...] = dr.astype(BF16)
        dg_ref[...] += jnp.sum(dy * xhat, axis=0, keepdims=True)
        db_ref[...] += jnp.sum(dy, axis=0, keepdims=True)
        dsum_ref[...] += jnp.sum(dr, axis=0, keepdims=True)

    row = pl.BlockSpec((tr, d), lambda i: (i, 0))
    vec = pl.BlockSpec((1, d), lambda i: (0, 0))
    return pl.pallas_call(
        body, name="ln2_loss_bwd", grid=(t // tr,), in_specs=[row, row, vec, vec],
        out_specs=[row, row, pl.BlockSpec((8, 128), lambda i: (0, 0)), vec, vec, vec],
        out_shape=[jax.ShapeDtypeStruct((t, d), F32), jax.ShapeDtypeStruct((t, d), BF16),
                   jax.ShapeDtypeStruct((8, 128), F32), jax.ShapeDtypeStruct((1, d), F32),
                   jax.ShapeDtypeStruct((1, d), F32), jax.ShapeDtypeStruct((1, d), F32)],
        compiler_params=_params(("arbitrary",)),
    )(r2, target, g, b)


POOL_ROWS = 512

DU_POOL = 3


def _du_part(block):
    return (block + DU_POOL) % 4


def _pool_mean_minus_token(u_ref, r0, rows, grp, first):
    width = u_ref.shape[1]
    body = u_ref[pl.ds(r0, rows), :]
    halo = u_ref[pl.ds(pl.multiple_of(jnp.maximum(r0 - POOL_HALO, 0), POOL_HALO), POOL_HALO), :]
    halo = jnp.where(first, 0.0, halo)
    full = jnp.concatenate([halo, body], axis=0)
    s = full
    for step in range(len(POOL_WINDOWS)):
        shifted = pltpu.roll(s, 1 << step, axis=0)
        s = s + jnp.where(grp >= step, shifted, 0.0)
    s = s[POOL_HALO:, :]
    tpos = r0 + lax.broadcasted_iota(jnp.int32, (rows, width), 0)
    count = jnp.minimum(tpos + 1, 2 << grp).astype(F32)
    return s / count - body, count


def _pool_fwd(u, w_pool, pool_scale, t, pw):
    gw = pw // len(POOL_WINDOWS)
    rows = _tile(t, POOL_ROWS)

    def body(u_ref, w_ref, s_ref, o_ref):
        grp = pl.program_id(0)

        def chunk(ci, carry):
            r0 = pl.multiple_of(ci * rows, rows)
            y, _ = _pool_mean_minus_token(u_ref, r0, rows, grp, ci == 0)
            yw = jnp.dot(y.astype(BF16), w_ref[...], preferred_element_type=F32)
            o_ref[pl.ds(r0, rows), :] = (yw * s_ref[...]).astype(BF16)
            return carry

        lax.fori_loop(0, t // rows, chunk, 0)

    return pl.pallas_call(
        body, name="pool_fwd", grid=(len(POOL_WINDOWS),),
        in_specs=[pl.BlockSpec((t, gw), lambda g: (0, g)), pl.BlockSpec((None, gw, gw), lambda g: (g, 0, 0)),
                  pl.BlockSpec((None, 1, gw), lambda g: (g, 0, 0))],
        out_specs=pl.BlockSpec((None, t, gw), lambda g: (0, 0, g)),
        out_shape=jax.ShapeDtypeStruct((2, t, pw), BF16),
        compiler_params=_params(("parallel",)),
    )(u, w_pool, pool_scale)


def _pool_bwd(u, dmix, w_pool, pool_scale, t, pw):
    n_grp = len(POOL_WINDOWS)
    gw = pw // n_grp
    rows = _tile(t, POOL_ROWS)

    def body(u_ref, dm_ref, w_ref, s_ref, du_ref, dw_ref, ds_ref, e_ref):
        grp = pl.program_id(0)
        dw_ref[...] = jnp.zeros_like(dw_ref)
        ds_ref[...] = jnp.zeros_like(ds_ref)
        e_ref[pl.ds(t, POOL_HALO), :] = jnp.zeros((POOL_HALO, gw), F32)

        def chunk(ci, carry):
            r0 = pl.multiple_of(ci * rows, rows)
            y, count = _pool_mean_minus_token(u_ref, r0, rows, grp, ci == 0)
            yb = y.astype(BF16)
            yw = jnp.dot(yb, w_ref[...], preferred_element_type=F32)
            dy2 = dm_ref[pl.ds(r0, rows), :]
            ds_ref[...] += jnp.sum(dy2 * yw, axis=0, keepdims=True)
            dyw = (dy2 * s_ref[...]).astype(BF16)
            dw_ref[...] += lax.dot_general(yb, dyw, (((0,), (0,)), ((), ())), preferred_element_type=F32)
            dy = lax.dot_general(dyw, w_ref[...], (((1,), (1,)), ((), ())), preferred_element_type=F32)
            e_ref[pl.ds(r0, rows), :] = dy / count
            return carry

        lax.fori_loop(0, t // rows, chunk, 0)

        def chunk2(ci, carry):
            r0 = pl.multiple_of(ci * rows, rows)
            full = e_ref[pl.ds(r0, rows + POOL_HALO), :]
            s = full
            for step in range(n_grp):
                shifted = pltpu.roll(s, rows + POOL_HALO - (1 << step), axis=0)
                s = s + jnp.where(grp >= step, shifted, 0.0)
            e = full[:rows, :]
            tpos = r0 + lax.broadcasted_iota(jnp.int32, (rows, gw), 0)
            count = jnp.minimum(tpos + 1, 2 << grp).astype(F32)
            du_ref[pl.ds(r0, rows), :] = (s[:rows, :] - e * count).astype(BF16)
            return carry

        lax.fori_loop(0, t // rows, chunk2, 0)

    return pl.pallas_call(
        body, name="pool_bwd", grid=(n_grp,),
        in_specs=[pl.BlockSpec((t, gw), lambda g: (0, g)), pl.BlockSpec((t, gw), lambda g: (0, g)),
                  pl.BlockSpec((None, gw, gw), lambda g: (g, 0, 0)),
                  pl.BlockSpec((None, 1, gw), lambda g: (g, 0, 0))],
        out_specs=[pl.BlockSpec((None, t, gw), lambda g: (DU_POOL, 0, g)),
                   pl.BlockSpec((None, gw, gw), lambda g: (g, 0, 0)), pl.BlockSpec((None, 1, gw), lambda g: (g, 0, 0))],
        out_shape=[jax.ShapeDtypeStruct((4, t, pw), BF16), jax.ShapeDtypeStruct((n_grp, gw, gw), F32),
                   jax.ShapeDtypeStruct((n_grp, 1, gw), F32)],
        scratch_shapes=[pltpu.VMEM((t + POOL_HALO, gw), F32)],
        compiler_params=_params(("parallel",)),
    )(u, dmix, w_pool, pool_scale)


def _sb_scores(q, k_blk, scale, mask):
    z = lax.dot_general(q, k_blk, (((1,), (1,)), ((), ())), preferred_element_type=F32) * scale
    log_not = jnp.minimum(-z, 0.0) - jnp.log(1.0 + jnp.exp(-jnp.abs(z)))
    return z, (log_not if mask is None else jnp.where(mask, log_not, 0.0))


def _sb_weights(e, mask):
    a = jnp.exp(e)
    return a if mask is None else jnp.where(mask, a, 0.0)


EXP_IS_ZERO_BELOW = -104.0


def _weights_alive(after):
    return (jnp.max(after) >= EXP_IS_ZERO_BELOW).astype(jnp.int32)


def _split_dot(vs, tri):
    parts = []
    for v in vs:
        hi = v.astype(BF16)
        parts += [hi, (v - hi.astype(F32)).astype(BF16)]
    prod = jnp.dot(jnp.concatenate(parts, axis=0), tri, preferred_element_type=F32)
    m = vs[0].shape[0]
    return [prod[2 * k * m:(2 * k + 1) * m] + prod[(2 * k + 1) * m:(2 * k + 2) * m] for k in range(len(vs))]


def _head(ref, h, rows=None):
    cols = slice(h * HEAD_DIM, (h + 1) * HEAD_DIM)
    return ref[:, cols] if rows is None else ref[rows, cols]


def _attn_fwd(ub, mix, t, nh, hg, phases=()):
    scale = float(1.0 / (HEAD_DIM ** 0.5))
    ng = nh // hg

    def body(q_ref, k_ref, v_ref, o_ref):
        i = pl.program_id(1)
        row = lax.broadcasted_iota(jnp.int32, (QB, KB), 0)
        col = lax.broadcasted_iota(jnp.int32, (QB, KB), 1)
        suffix = (row >= col).astype(BF16)

        def more(carry):
            return jnp.logical_and(carry[0] <= i, carry[3] > 0)

        def block(n, accs, afters, mask):
            rows = pl.ds(pl.multiple_of((i - n) * KB, KB), KB)
            new_accs, new_afters = [], []
            scores = [_sb_scores(_head(q_ref, h), _head(k_ref, h, rows), scale, mask) for h in range(hg)]
            withins = _split_dot([log_not for _, log_not in scores], suffix)
            for h in range(hg):
                z, log_not = scores[h]
                a = _sb_weights(z + withins[h] + afters[h], mask)
                new_accs.append(accs[h] + jnp.dot(a.astype(BF16), _head(v_ref, h, rows),
                                                  preferred_element_type=F32))
                new_afters.append(afters[h] + jnp.sum(log_not, axis=1, keepdims=True))
            return n + 1, tuple(new_accs), tuple(new_afters), _weights_alive(functools.reduce(jnp.maximum, new_afters))

        first = block(jnp.int32(0), tuple(jnp.zeros((QB, HEAD_DIM), F32) for _ in range(hg)),
                      tuple(jnp.zeros((QB, 1), F32) for _ in range(hg)), col < row)
        _, accs, _, _ = lax.while_loop(more, lambda carry: block(carry[0], carry[1], carry[2], None), first)
        for h in range(hg):
            o_ref[:, h * HEAD_DIM:(h + 1) * HEAD_DIM] = accs[h].astype(BF16)

    wide = hg * HEAD_DIM
    outs, phase_outs = _call(
        lambda q_ref, k_ref, v_ref, mix_ref, o_ref: body(q_ref, k_ref, v_ref, o_ref), "attn_fwd", (ng, t // QB),
        [pl.BlockSpec((QB, wide), lambda g, i: (i, ng + g)), pl.BlockSpec((t, wide), lambda g, i: (0, 2 * ng + g)),
         pl.BlockSpec((t, wide), lambda g, i: (0, 3 * ng + g)), ANY],
        [pl.BlockSpec((None, QB, wide), lambda g, i: (1, i, g))], [jax.ShapeDtypeStruct(mix.shape, mix.dtype)], [],
        ("parallel", "arbitrary"), (ub, ub, ub, mix), phases, in_place={3: 0})
    return outs[0], phase_outs


def _attn_bwd(ub, dmix, du, t, nh, hg, phases=()):
    scale = float(1.0 / (HEAD_DIM ** 0.5))
    ng = nh // hg

    def body(q_ref, k_ref, v_ref, do_ref, du_in_ref, du_ref, g_ref, z_ref, dk_ref, dv_ref):
        i = pl.program_id(1)

        @pl.when(i == 0)
        def _():
            dk_ref[...] = jnp.zeros_like(dk_ref)
            dv_ref[...] = jnp.zeros_like(dv_ref)

        row = lax.broadcasted_iota(jnp.int32, (QB, KB), 0)
        col = lax.broadcasted_iota(jnp.int32, (QB, KB), 1)
        suffix = (row >= col).astype(BF16)
        prefix = (row <= col).astype(BF16)

        def more(carry):
            return jnp.logical_and(carry[0] <= i, carry[2] > 0)

        def down(n, afters, mask):
            ks = pl.multiple_of((i - n) * KB, KB)
            rows = pl.ds(ks, KB)
            new_afters = []
            scores = [_sb_scores(_head(q_ref, h), _head(k_ref, h, rows), scale, mask) for h in range(hg)]
            withins = _split_dot([log_not for _, log_not in scores], suffix)
            for h in range(hg):
                do = _head(do_ref, h).astype(BF16)
                z, log_not = scores[h]
                a = _sb_weights(z + withins[h] + afters[h], mask)
                da = lax.dot_general(do, _head(v_ref, h, rows), (((1,), (1,)), ((), ())),
                                     preferred_element_type=F32)
                g_ref[h, :, pl.ds(ks, KB)] = a * da
                z_ref[h, :, pl.ds(ks, KB)] = z
                dv_ref[rows, h * HEAD_DIM:(h + 1) * HEAD_DIM] += lax.dot_general(
                    a.astype(BF16), do, (((0,), (0,)), ((), ())), preferred_element_type=F32)
                new_afters.append(afters[h] + jnp.sum(log_not, axis=1, keepdims=True))
            return n + 1, tuple(new_afters), _weights_alive(functools.reduce(jnp.maximum, new_afters))

        diagonal = col < row
        first = down(jnp.int32(0), tuple(jnp.zeros((QB, 1), F32) for _ in range(hg)), diagonal)
        visited, _, _ = lax.while_loop(more, lambda carry: down(carry[0], carry[1], None), first)

        def up(kb, carry, mask):
            dqs, befores = carry
            ks = pl.multiple_of(kb * KB, KB)
            rows = pl.ds(ks, KB)
            new_dqs, new_befores = [], []
            gs = [g_ref[h, :, pl.ds(ks, KB)] for h in range(hg)]
            g_withins = _split_dot(gs, prefix)
            for h in range(hg):
                g = gs[h]
                z = z_ref[h, :, pl.ds(ks, KB)]
                g_upto = g_withins[h] + befores[h]
                dz = g - jax.nn.sigmoid(z) * g_upto
                dz = dz if mask is None else jnp.where(mask, dz, 0.0)
                dzs = (dz * scale).astype(BF16)
                new_dqs.append(dqs[h] + jnp.dot(dzs, _head(k_ref, h, rows), preferred_element_type=F32))
                dk_ref[rows, h * HEAD_DIM:(h + 1) * HEAD_DIM] += lax.dot_general(
                    dzs, _head(q_ref, h), (((0,), (0,)), ((), ())), preferred_element_type=F32)
                new_befores.append(befores[h] + jnp.sum(g, axis=1, keepdims=True))
            return tuple(new_dqs), tuple(new_befores)

        below = lax.fori_loop(i + 1 - visited, i, lambda kb, carry: up(kb, carry, None),
                              (tuple(jnp.zeros((QB, HEAD_DIM), F32) for _ in range(hg)),
                               tuple(jnp.zeros((QB, 1), F32) for _ in range(hg))))
        dqs, _ = up(i, below, diagonal)
        q_rows = pl.ds(pl.multiple_of(i * QB, QB), QB)
        for h in range(hg):
            du_ref[0, q_rows, h * HEAD_DIM:(h + 1) * HEAD_DIM] = dqs[h].astype(BF16)

        @pl.when(i == t // QB - 1)
        def _():
            du_ref[1] = dk_ref[...].astype(BF16)
            du_ref[2] = dv_ref[...].astype(BF16)

    wide = hg * HEAD_DIM
    tile = lambda off: pl.BlockSpec((QB, wide), lambda g, i: (i, off + g))
    strip = lambda off: pl.BlockSpec((t, wide), lambda g, i: (0, off + g))
    outs, phase_outs = _call(
        body, "attn_bwd", (ng, t // QB), [tile(ng), strip(2 * ng), strip(3 * ng), tile(ng), ANY],
        [pl.BlockSpec((3, t, wide), lambda g, i: (0, 0, g))], [jax.ShapeDtypeStruct(du.shape, du.dtype)],
        [pltpu.VMEM((hg, QB, t), F32), pltpu.VMEM((hg, QB, t), F32), pltpu.VMEM((t, wide), F32),
         pltpu.VMEM((t, wide), F32)], ("parallel", "arbitrary"), (ub, ub, ub, dmix, du), phases, in_place={4: 0})
    return outs[0], phase_outs


def _row_tile(rows, cols, pref_bytes=2 * 1024 * 1024):
    tr = max(8, pref_bytes // (4 * cols))
    while rows % tr:
        tr //= 2
    return max(tr, 1)


def _pair_sum(name, g, s, c_idx):
    _, _, r2, cols = g.shape
    tr = _row_tile(r2, cols)

    def body(c_ref, g_ref, s_ref, o_ref):
        o_ref[...] = (g_ref[...] + s_ref[...]).astype(BF16)

    return pl.pallas_call(
        body, name=name,
        grid_spec=pltpu.PrefetchScalarGridSpec(
            num_scalar_prefetch=1, grid=(N_CHIPS, r2 // tr),
            in_specs=[pl.BlockSpec((None, None, tr, cols), lambda p, i, c: (p, c[0], i, 0)),
                      pl.BlockSpec((None, tr, cols), lambda p, i, c: (p, i, 0))],
            out_specs=pl.BlockSpec((None, tr, cols), lambda p, i, c: (p, i, 0))),
        out_shape=jax.ShapeDtypeStruct((N_CHIPS, r2, cols), BF16),
        compiler_params=_params(("parallel", "parallel")),
    )(c_idx, g, s)


def _chip_sum(name, g, s, received, place):
    _, _, r2, cols = g.shape
    tr = _row_tile(r2, cols)
    counts = [r.shape[0] for r in received]

    def body(place_ref, g_ref, s_ref, *refs):
        total = g_ref[...] + s_ref[...]
        for r_ref, n in zip(refs[:-1], counts):
            for k in range(n):
                total = total + r_ref[k].astype(F32)
        refs[-1][...] = total

    return pl.pallas_call(
        body, name=name,
        grid_spec=pltpu.PrefetchScalarGridSpec(
            num_scalar_prefetch=1, grid=(r2 // tr,),
            in_specs=[pl.BlockSpec((None, None, tr, cols), lambda i, p: (p[0], p[1], i, 0)),
                      pl.BlockSpec((None, tr, cols), lambda i, p: (p[0], i, 0)),
                      *[pl.BlockSpec((n, tr, cols), lambda i, p: (0, i, 0)) for n in counts]],
            out_specs=pl.BlockSpec((None, tr, cols), lambda i, p: (p[1], i, 0))),
        out_shape=jax.ShapeDtypeStruct((2, r2, cols), F32),
        compiler_params=_params(("parallel",)),
    )(place, g, s, *received)


def _cast_into_slot(name, w, place):
    rows, cols = w.shape
    r2 = rows // 2
    tr = _row_tile(r2, cols)
    nb = r2 // tr

    def body(place_ref, w_ref, o_ref):
        o_ref[...] = w_ref[...].astype(BF16)

    return pl.pallas_call(
        body, name=name,
        grid_spec=pltpu.PrefetchScalarGridSpec(
            num_scalar_prefetch=1, grid=(2, nb),
            in_specs=[pl.BlockSpec((tr, cols), lambda h, i, s: (h * nb + i, 0))],
            out_specs=pl.BlockSpec((None, None, tr, cols), lambda h, i, s: (s[0], h, i, 0))),
        out_shape=jax.ShapeDtypeStruct((N_CHIPS, 2, r2, cols), BF16),
        compiler_params=_params(("parallel", "parallel")),
    )(place, w)


def _colsum(name, a):
    def body(a_ref, o_ref):
        o_ref[...] = jnp.sum(a_ref[...], axis=0, keepdims=True)

    whole = lambda shape: pl.BlockSpec(shape, lambda i: (0, 0))
    return pl.pallas_call(
        body, name=name, grid=(1,), in_specs=[whole(a.shape)], out_specs=whole((1, a.shape[1])),
        out_shape=jax.ShapeDtypeStruct((1, a.shape[1]), F32), compiler_params=_params(("arbitrary",)),
    )(a)


def _adamw(name, w, g, m, v):
    rows, cols = w.shape
    tr = _row_tile(rows, cols, 1024 * 1024)

    def body(w_ref, g_ref, m_ref, v_ref, g_out_ref, d_ref, nm_ref, nv_ref):
        g_t = g_ref[...]
        m_t = ADAM_B1 * m_ref[...] + (1.0 - ADAM_B1) * g_t
        v_t = ADAM_B2 * v_ref[...] + (1.0 - ADAM_B2) * jnp.square(g_t)
        m_hat = m_t / (1.0 - ADAM_B1 ** ADAM_STEP)
        v_hat = v_t / (1.0 - ADAM_B2 ** ADAM_STEP)
        g_out_ref[...] = g_t
        d_ref[...] = -ADAM_LR * (m_hat / (jnp.sqrt(v_hat) + ADAM_EPS) + ADAM_WD * w_ref[...])
        nm_ref[...] = m_t
        nv_ref[...] = v_t

    spec = pl.BlockSpec((tr, cols), lambda i: (i, 0))
    shape = jax.ShapeDtypeStruct((rows, cols), F32)
    return pl.pallas_call(
        body, name=name, grid=(rows // tr,), in_specs=[spec] * 4, out_specs=[spec] * 4, out_shape=[shape] * 4,
        compiler_params=_params(("parallel",)),
    )(w, g, m, v)


def _all_reduce_small(packed, phases=()):
    rows, cols = packed.shape

    def body(in_ref, out_ref, all_ref, send_sems, recv_sems):
        x, y, c = _pos()
        me = 4 * x + 2 * y + c
        all_ref[me] = in_ref[...]
        cps = []
        for r in range(1, N_DEV):
            bx, by, bc = (r >> 2) & 1, (r >> 1) & 1, r & 1
            peer = (1 - x if bx else x, 1 - y if by else y, 1 - c if bc else c)
            cp = pltpu.make_async_remote_copy(
                src_ref=in_ref, dst_ref=all_ref.at[me], send_sem=send_sems.at[r - 1], recv_sem=recv_sems.at[r - 1],
                device_id=peer, device_id_type=MESH)
            cp.start()
            cps.append(cp)
        for cp in cps:
            cp.wait()
        total = all_ref[0]
        for d in range(1, N_DEV):
            total = total + all_ref[d]
        out_ref[...] = total

    vmem = pl.BlockSpec(memory_space=pltpu.VMEM)
    outs, phase_outs = _call(
        body, "all_reduce_small", (1,), [vmem], [vmem], [jax.ShapeDtypeStruct((rows, cols), F32)],
        [pltpu.VMEM((N_DEV, rows, cols), F32), pltpu.SemaphoreType.DMA((N_DEV - 1,)),
         pltpu.SemaphoreType.DMA((N_DEV - 1,))], ("arbitrary",), (packed,), phases)
    return outs[0], phase_outs


def kernel(x, ln_in_g, ln_in_b, w_in, w_pool, pool_scale, w_out, ln1_g, ln1_b, w_ff1, b_ff1, w_ff2, b_ff2, ln2_g, ln2_b, loss_target, m_ln_in_g, m_ln_in_b, m_w_in, m_w_pool, m_pool_scale, m_w_out, m_ln1_g, m_ln1_b, m_w_ff1, m_b_ff1, m_w_ff2, m_b_ff2, m_ln2_g, m_ln2_b, v_ln_in_g, v_ln_in_b, v_w_in, v_w_pool, v_pool_scale, v_w_out, v_ln1_g, v_ln1_b, v_w_ff1, v_b_ff1, v_w_ff2, v_b_ff2, v_ln2_g, v_ln2_b):
    t, d = x.shape[1], x.shape[2]
    pw = d // 2
    n_grp = len(POOL_WINDOWS)
    gw = pw // n_grp
    gwc = gw // N_CHIPS
    nh = pw // HEAD_DIM
    ff = w_ff1.shape[2] * N_CHIPS
    assert w_in.shape[0] == 1 and w_in.shape[2] * N_CHIPS == 2 * d and gwc <= 128

    x_idx, y_idx, c_idx = _pos()
    chip_arr = jnp.reshape(2 * x_idx + y_idx, (1,)).astype(jnp.int32)
    c_arr = jnp.reshape(c_idx, (1,)).astype(jnp.int32)
    place = jnp.concatenate([chip_arr, c_arr])

    xs = x.reshape(t, d)
    target = loss_target.reshape(t, d)
    row = lambda vec: vec.reshape(1, -1)

    scale_tile = jnp.zeros((1, 8, 128), F32).at[0, :n_grp, :gwc].set(pool_scale[0])
    scale_slots = lax.dynamic_update_slice(jnp.zeros((N_CHIPS, 8, 128), F32), scale_tile, (chip_arr[0], 0, 0))
    shards = dict(w_in=w_in[0], w_out=w_out[0], w_ff1=w_ff1[0], w_ff2=w_ff2[0], w_pool=w_pool[0].reshape(gw, gw))
    slot = {nm: _cast_into_slot("cast_" + nm, w, place) for nm, w in shards.items()}
    unsplit = lambda s: s.reshape(N_CHIPS, 2 * s.shape[2], s.shape[3])

    (h0, h0b, xhat0, rstd0), ((win_s,), (wpool_s,), (scale_g,)) = _ln_fwd(
        "ln_in_fwd", xs, row(ln_in_g), row(ln_in_b),
        phases=[_gather_ici_phase(slot["w_in"]), _gather_ici_phase(slot["w_pool"]), _gather_whole_phase(scale_slots)])
    (win_s,), (wpool_s,) = _comm_call("gather_d2d_first", [_gather_d2d_phase(win_s), _gather_d2d_phase(wpool_s)])
    win_g = unsplit(win_s)
    wpool_full = unsplit(wpool_s).reshape(N_CHIPS, n_grp, gwc, gw).transpose(1, 0, 2, 3).reshape(n_grp, gw, gw)
    scale_full = scale_g[:, :n_grp, :gwc].transpose(1, 0, 2).reshape(n_grp, 1, gw)

    def store_f32(acc, extra_refs, out_refs):
        out_refs[0][...] = acc

    def pool_f32_all_bf16(acc, extra_refs, out_refs):
        @pl.when(pl.program_id(1) == 0)
        def _():
            out_refs[0][...] = acc

        out_refs[1][...] = acc.astype(BF16)

    assert w_in.shape[2] == pw
    (u, ub), ((wout_s,),) = _mm_nn(
        "in_proj", h0b, win_g, (), lambda tm, tn: [],
        [jax.ShapeDtypeStruct((t, pw), F32), jax.ShapeDtypeStruct((t, 2 * d), BF16)],
        lambda tm, tn: [pl.BlockSpec((tm, tn), lambda i, j, kk: (i, 0)), _tile_spec(tm, tn)],
        pool_f32_all_bf16, b_chips=True, tn=pw, phases=[_gather_ici_phase(slot["w_out"])])
    mix_in = _pool_fwd(u, wpool_full, scale_full, t, pw)
    mix_in, ((wff1_s,), (wout_s,)) = _attn_fwd(
        ub, mix_in, t, nh, min(nh, 4), phases=[_gather_ici_phase(slot["w_ff1"]), _gather_d2d_phase(wout_s)])
    same_part = lambda block: block
    wout_full = unsplit(wout_s).reshape(d, d)

    def residual(acc, extra_refs, out_refs):
        out_refs[0][...] = ALPHA * extra_refs[0][...] + acc

    (r1,), ((wff1_s,), (wff2_s,)) = _mm_nn(
        "out_proj", mix_in, wout_full, (h0,), lambda tm, tn: [_tile_spec(tm, tn)],
        [jax.ShapeDtypeStruct((t, d), F32)], lambda tm, tn: [_tile_spec(tm, tn)], residual, a_parts=same_part,
        phases=[_gather_d2d_phase(wff1_s), _gather_ici_phase(slot["w_ff2"], others=(1,))])
    wff1_g = unsplit(wff1_s)
    h1, h1b, xhat1, rstd1 = _ln_fwd("ln1_fwd", r1, ln1_g, ln1_b)

    def relu_sq(acc, extra_refs, out_refs):
        p = jnp.maximum(acc + extra_refs[0][...], 0.0)
        out_refs[0][...] = p
        out_refs[1][...] = jnp.square(p).astype(BF16)

    (relu_z, act_b), ((wff2_s,),) = _mm_nn(
        "ff1", h1b, wff1_g, (b_ff1,), lambda tm, tn: [_row_spec(tn)],
        [jax.ShapeDtypeStruct((t, ff), F32), jax.ShapeDtypeStruct((t, ff), BF16)],
        lambda tm, tn: [_tile_spec(tm, tn)] * 2, relu_sq, b_chips=True,
        phases=[_gather_ici_phase(wff2_s, others=(0, 2))])
    ((wff2_s,),) = _comm_call("gather_d2d_last", [_gather_d2d_phase(wff2_s)])
    wff2_full = unsplit(wff2_s).reshape(ff, d)

    def residual_bias(acc, extra_refs, out_refs):
        out_refs[0][...] = ALPHA * extra_refs[0][...] + (acc + extra_refs[1][...])

    r2 = _mm_nn("ff2", act_b, wff2_full, (h1, b_ff2), lambda tm, tn: [_tile_spec(tm, tn), _row_spec(tn)],
                [jax.ShapeDtypeStruct((t, d), F32)], lambda tm, tn: [_tile_spec(tm, tn)], residual_bias)[0]

    dr2, dr2b, loss_tile, g_ln2_g, g_ln2_b, g_b_ff2 = _ln2_loss_bwd(r2, target, ln2_g, ln2_b)
    loss = lax.psum(loss_tile[0, 0], ("x", "y", "c"))

    halves = lambda g: g.reshape(N_CHIPS, 2, g.shape[1] // 2, g.shape[2])
    g_ff2 = halves(_mm_tn("grad_w_ff2", act_b, dr2b).reshape(N_CHIPS, ff // N_CHIPS, d))

    def relu_sq_bwd(acc, extra_refs, out_refs):
        dz = acc * (2.0 * extra_refs[0][...])
        out_refs[0][...] = dz.astype(BF16)
        rows = lax.broadcasted_iota(jnp.int32, out_refs[1].shape, 0)
        out_refs[1][...] = jnp.where(rows == 0, jnp.sum(dz, axis=0, keepdims=True), 0.0)

    tm_ff = _tile(t, 1024)
    (dz1b, g_b_ff1_parts), ((s_ff2,),) = _mm_nt(
        "ff2_bwd", dr2b, wff2_full, (relu_z,), lambda tm, tn: [_tile_spec(tm, tn)],
        [jax.ShapeDtypeStruct((t, ff), BF16), jax.ShapeDtypeStruct((8 * (t // tm_ff), ff), F32)],
        lambda tm, tn: [_tile_spec(tm, tn), pl.BlockSpec((8, tn), lambda i, j, kk: (i, j))], relu_sq_bwd,
        phases=[_swap_phase(g_ff2)])
    p_ff2 = _pair_sum("pair_sum_w_ff2", g_ff2, s_ff2, c_arr)
    g_ff1, ((r_ff2_a,),) = _mm_tn("grad_w_ff1", h1b, dz1b, out_chips=True,
                                  phases=[_scatter_phase(p_ff2, others=(0, 1))])
    g_ff1 = halves(g_ff1)

    def plus_alpha(acc, extra_refs, out_refs):
        out_refs[0][...] = ALPHA * extra_refs[0][...] + acc

    (dh1,), ((s_ff1,), (r_ff2_b,)) = _mm_nt(
        "ff1_bwd", dz1b, wff1_g, (dr2,), lambda tm, tn: [_tile_spec(tm, tn)],
        [jax.ShapeDtypeStruct((t, d), F32)], lambda tm, tn: [_tile_spec(tm, tn)], plus_alpha, b_chips=True,
        phases=[_swap_phase(g_ff1), _scatter_phase(p_ff2, others=(2,))])
    q_ff2 = _chip_sum("chip_sum_w_ff2", g_ff2, s_ff2, [r_ff2_a, r_ff2_b], place)
    p_ff1 = _pair_sum("pair_sum_w_ff1", g_ff1, s_ff1, c_arr)
    (dr1, dr1b, g_ln1_g, g_ln1_b), ((q_ff2,),) = _ln_bwd("ln1_bwd", dh1, xhat1, rstd1, ln1_g,
                                                         phases=[_assemble_phase(q_ff2)])

    g_out = halves(_mm_tn("grad_w_out", mix_in, dr1b, a_parts=same_part).reshape(N_CHIPS, d // N_CHIPS, d))
    (dmix,), ((s_out,),) = _mm_nt(
        "out_proj_bwd", dr1b, wout_full, (), lambda tm, tn: [], [jax.ShapeDtypeStruct((t, d), F32)],
        lambda tm, tn: [_tile_spec(tm, tn)], store_f32, phases=[_swap_phase(g_out)])
    p_out = _pair_sum("pair_sum_w_out", g_out, s_out, c_arr)
    du, g_w_pool_full, g_scale_full = _pool_bwd(u, dmix, wpool_full, scale_full, t, pw)
    du, ((r_ff1,), (r_out,)) = _attn_bwd(ub, dmix, du, t, nh, 1,
                                         phases=[_scatter_phase(p_ff1), _scatter_phase(p_out)])
    q_ff1 = _chip_sum("chip_sum_w_ff1", g_ff1, s_ff1, [r_ff1], place)
    q_out = _chip_sum("chip_sum_w_out", g_out, s_out, [r_out], place)
    g_in, ((q_ff1,), (q_out,)) = _mm_tn("grad_w_in", h0b, du, out_chips=True, b_parts=_du_part,
                                        phases=[_assemble_phase(q_ff1), _assemble_phase(q_out)])
    g_in = halves(g_in)
    g_pool = halves(g_w_pool_full.reshape(n_grp, N_CHIPS, gwc, gw).transpose(1, 0, 2, 3).reshape(N_CHIPS, gw, gw))
    (s_in,), (s_pool,) = _comm_call("rs_swap_last", [_swap_phase(g_in), _swap_phase(g_pool)])
    p_in = _pair_sum("pair_sum_w_in", g_in, s_in, c_arr)
    p_pool = _pair_sum("pair_sum_w_pool", g_pool, s_pool, c_arr)
    (dh0,), ((r_in,), (r_pool,)) = _mm_nt(
        "in_proj_bwd", du, win_g, (dr1,), lambda tm, tn: [_tile_spec(tm, tn)],
        [jax.ShapeDtypeStruct((t, d), F32)], lambda tm, tn: [_tile_spec(tm, tn)], plus_alpha, b_chips=True,
        a_parts=_du_part, phases=[_scatter_phase(p_in), _scatter_phase(p_pool)])
    q_in = _chip_sum("chip_sum_w_in", g_in, s_in, [r_in], place)
    q_pool = _chip_sum("chip_sum_w_pool", g_pool, s_pool, [r_pool], place)
    dx, _, g_ln_in_g, g_ln_in_b = _ln_bwd("ln_in_bwd", dh0, xhat0, rstd0, row(ln_in_g))

    lane = 2048 if d % 2048 == 0 else d
    small_names = ["ln_in_g", "ln_in_b", "ln1_g", "ln1_b", "b_ff1", "b_ff2", "ln2_g", "ln2_b"]
    small_w = dict(ln_in_g=ln_in_g, ln_in_b=ln_in_b, ln1_g=ln1_g, ln1_b=ln1_b, b_ff1=b_ff1, b_ff2=b_ff2, ln2_g=ln2_g,
                   ln2_b=ln2_b)
    small_m = dict(ln_in_g=m_ln_in_g, ln_in_b=m_ln_in_b, ln1_g=m_ln1_g, ln1_b=m_ln1_b, b_ff1=m_b_ff1, b_ff2=m_b_ff2,
                   ln2_g=m_ln2_g, ln2_b=m_ln2_b)
    small_v = dict(ln_in_g=v_ln_in_g, ln_in_b=v_ln_in_b, ln1_g=v_ln1_g, ln1_b=v_ln1_b, b_ff1=v_b_ff1, b_ff2=v_b_ff2,
                   ln2_g=v_ln2_g, ln2_b=v_ln2_b)
    small_g = dict(ln_in_g=g_ln_in_g, ln_in_b=g_ln_in_b, ln1_g=g_ln1_g, ln1_b=g_ln1_b, b_ff2=g_b_ff2, ln2_g=g_ln2_g,
                   ln2_b=g_ln2_b)

    def pack(parts):
        flat = jnp.concatenate([p.reshape(-1) for p in parts])
        n_rows = -(-flat.shape[0] // lane)
        n_rows = -(-n_rows // 8) * 8
        return jnp.pad(flat, (0, n_rows * lane - flat.shape[0])).reshape(n_rows, lane)

    small_g["b_ff1"] = _colsum("b_ff1_colsum", g_b_ff1_parts)
    summed, ((q_in,), (q_pool,)) = _all_reduce_small(
        pack([small_g[nm] for nm in small_names] + [g_scale_full]),
        phases=[_assemble_phase(q_in), _assemble_phase(q_pool)])
    summed = summed.reshape(-1)

    big = {}
    for nm, q, w, m, v in [("w_in", q_in, w_in, m_w_in, v_w_in), ("w_out", q_out, w_out, m_w_out, v_w_out),
                           ("w_ff1", q_ff1, w_ff1, m_w_ff1, v_w_ff1), ("w_ff2", q_ff2, w_ff2, m_w_ff2, v_w_ff2),
                           ("w_pool", q_pool, w_pool, m_w_pool, v_w_pool)]:
        g = q.reshape(2 * q.shape[1], q.shape[2])
        flat = lambda arr: arr.reshape(g.shape)
        big[nm] = tuple(arr.reshape(w.shape) for arr in _adamw("adamw_" + nm, flat(w), g, flat(m), flat(v)))

    g_small, off = {}, 0
    for nm in small_names:
        g_small[nm] = summed[off:off + small_w[nm].size]
        off += small_w[nm].size
    g_scale_all = summed[off:off + n_grp * gw].reshape(n_grp, N_CHIPS, gwc)
    g_scale = lax.dynamic_index_in_dim(g_scale_all, chip_arr[0], axis=1, keepdims=False)

    order = small_names + ["pool_scale"]
    small_w["pool_scale"], small_m["pool_scale"], small_v["pool_scale"] = pool_scale, m_pool_scale, v_pool_scale
    g_small["pool_scale"] = g_scale
    _, delta_s, new_m_s, new_v_s = _adamw("adamw_small", pack([small_w[nm] for nm in order]),
                                          pack([g_small[nm] for nm in order]), pack([small_m[nm] for nm in order]),
                                          pack([small_v[nm] for nm in order]))
    small = {}
    off = 0
    for nm in order:
        size, shape = small_w[nm].size, small_w[nm].shape
        cut = lambda arr: arr.reshape(-1)[off:off + size].reshape(shape)
        small[nm] = (g_small[nm].reshape(shape), cut(delta_s), cut(new_m_s), cut(new_v_s))
        off += size

    every = {**big, **small}
    weight_order = ["ln_in_g", "ln_in_b", "w_in", "w_pool", "pool_scale", "w_out", "ln1_g", "ln1_b", "w_ff1", "b_ff1",
                    "w_ff2", "b_ff2", "ln2_g", "ln2_b"]
    grads = [every[nm][0] for nm in weight_order]
    deltas = [every[nm][1] for nm in weight_order]
    new_ms = [every[nm][2] for nm in weight_order]
    new_vs = [every[nm][3] for nm in weight_order]
    return (loss, dx.reshape(x.shape), *grads, *deltas, *new_ms, *new_vs)
```

```python
import functools

import jax
import jax.numpy as jnp
from jax import lax
from jax.experimental import pallas as pl
from jax.experimental.pallas import tpu as pltpu

F32 = jnp.float32
BF16 = jnp.bfloat16
MESH = pl.DeviceIdType.MESH

HEAD_DIM = 128
POOL_WINDOWS = (2, 4, 8, 16)
POOL_HALO = 16
LN_EPS = 1e-5
ALPHA = 2.0 ** 0.25
ADAM_LR, ADAM_B1, ADAM_B2, ADAM_EPS, ADAM_WD, ADAM_STEP = 0.001, 0.9, 0.999, 1e-08, 0.01, 10

QB = 256
KB = 256
VMEM_LIMIT = 56 * 1024 * 1024
N_CHIPS = 4
N_DEV = 8


def _params(sem=None):
    return pltpu.CompilerParams(dimension_semantics=sem, vmem_limit_bytes=VMEM_LIMIT)


def _tile(dim, pref):
    return pref if dim % pref == 0 else dim


def _pos():
    return lax.axis_index("x"), lax.axis_index("y"), lax.axis_index("c")


def _other_chips(x, y):
    return [(1 - x, y), (x, 1 - y), (1 - x, 1 - y)]


ANY = pl.BlockSpec(memory_space=pl.ANY)


class _Phase:
    def __init__(self, ins, out_shapes, aliases, n_sems, build):
        self.ins, self.out_shapes, self.aliases, self.n_sems, self.build = ins, out_shapes, aliases, n_sems, build


def _remote(src, dst, send_sems, recv_sems, k, to):
    return pltpu.make_async_remote_copy(src_ref=src, dst_ref=dst, send_sem=send_sems.at[k], recv_sem=recv_sems.at[k],
                                        device_id=to, device_id_type=MESH)


def _swap_phase(g):
    def build(ins, outs, ss, rs):
        x, y, c = _pos()
        cp = _remote(ins[0].at[:, 1 - c], outs[0], ss, rs, 0, (x, y, 1 - c))
        return [cp], [cp]

    return _Phase([g], [jax.ShapeDtypeStruct((N_CHIPS, g.shape[2], g.shape[3]), g.dtype)], {}, 1, build)


ALL_OTHERS = (0, 1, 2)


def _scatter_phase(p, others=ALL_OTHERS):
    def build(ins, outs, ss, rs):
        x, y, c = _pos()
        chips = _other_chips(x, y)
        cps = [_remote(ins[0].at[2 * chips[j][0] + chips[j][1]], outs[0].at[k], ss, rs, k, (*chips[j], c))
               for k, j in enumerate(others)]
        return cps, cps

    return _Phase([p], [jax.ShapeDtypeStruct((len(others), p.shape[1], p.shape[2]), p.dtype)], {}, len(others), build)


def _assemble_phase(q):
    def build(ins, outs, ss, rs):
        x, y, c = _pos()
        mine, other = outs[0].at[c], outs[0].at[1 - c]
        return [_remote(mine, mine, ss, rs, 0, (x, y, 1 - c))], [_remote(other, other, ss, rs, 0, (x, y, c))]

    return _Phase([q], [jax.ShapeDtypeStruct(q.shape, q.dtype)], {0: 0}, 1, build)


def _gather_ici_phase(slot, others=ALL_OTHERS):
    def build(ins, outs, ss, rs):
        x, y, c = _pos()
        chips = _other_chips(x, y)
        mine = outs[0].at[2 * x + y, c]
        sends, recvs = [], []
        for k, j in enumerate(others):
            theirs = outs[0].at[2 * chips[j][0] + chips[j][1], c]
            sends.append(_remote(mine, mine, ss, rs, k, (*chips[j], c)))
            recvs.append(_remote(theirs, theirs, ss, rs, k, (x, y, c)))
        return sends, recvs

    return _Phase([slot], [jax.ShapeDtypeStruct(slot.shape, slot.dtype)], {0: 0}, len(others), build)


def _gather_d2d_phase(slot):
    def build(ins, outs, ss, rs):
        x, y, c = _pos()
        sends, recvs = [], []
        for j, chip in enumerate(_other_chips(x, y)):
            landed = outs[0].at[2 * chip[0] + chip[1], c]
            coming = outs[0].at[2 * chip[0] + chip[1], 1 - c]
            sends.append(_remote(landed, landed, ss, rs, j, (x, y, 1 - c)))
            recvs.append(_remote(coming, coming, ss, rs, j, (x, y, c)))
        return sends, recvs

    return _Phase([slot], [jax.ShapeDtypeStruct(slot.shape, slot.dtype)], {0: 0}, 3, build)


def _gather_whole_phase(slots):
    def build(ins, outs, ss, rs):
        x, y, c = _pos()
        mine = outs[0].at[2 * x + y]
        sends, recvs = [], []
        for j, chip in enumerate(_other_chips(x, y)):
            theirs = outs[0].at[2 * chip[0] + chip[1]]
            sends.append(_remote(mine, mine, ss, rs, j, (*chip, c)))
            recvs.append(_remote(theirs, theirs, ss, rs, j, (x, y, c)))
        return sends, recvs

    return _Phase([slots], [jax.ShapeDtypeStruct(slots.shape, slots.dtype)], {0: 0}, 3, build)


def _split_refs(refs, n_in, n_out, n_scratch, phases):
    n_pin = sum(len(ph.ins) for ph in phases)
    n_pout = sum(len(ph.out_shapes) for ph in phases)
    cuts = [n_in, n_pin, n_out, n_pout, n_scratch]
    parts, at = [], 0
    for n in cuts:
        parts.append(refs[at:at + n])
        at += n
    parts.append(refs[at:])
    return parts


def _build_phases(phases, pin, pout, sems):
    built, i, o = [], 0, 0
    for k, ph in enumerate(phases):
        built.append(ph.build(pin[i:i + len(ph.ins)], pout[o:o + len(ph.out_shapes)], sems[2 * k], sems[2 * k + 1]))
        i += len(ph.ins)
        o += len(ph.out_shapes)
    return built


def _finish_phases(built):
    for _, recvs in built:
        for cp in recvs:
            cp.wait_recv()
    for sends, _ in built:
        for cp in sends:
            cp.wait_send()


def _call(body, name, grid, in_specs, out_specs, out_shape, scratch_shapes, semantics, args, phases=(),
          in_place=None):
    n_in, n_out, n_scratch = len(args), len(out_shape), len(scratch_shapes)
    aliases, in_at, out_at = dict(in_place or {}), n_in, n_out
    for ph in phases:
        aliases.update({in_at + i: out_at + o for i, o in ph.aliases.items()})
        in_at += len(ph.ins)
        out_at += len(ph.out_shapes)

    def hosted(*refs):
        ins, pin, outs, pout, scratch, sems = _split_refs(refs, n_in, n_out, n_scratch, phases)
        ids = [pl.program_id(a) for a in range(len(grid))]
        first = functools.reduce(jnp.logical_and, [i == 0 for i in ids])
        last = functools.reduce(jnp.logical_and, [i == g - 1 for i, g in zip(ids, grid)])

        @pl.when(first)
        def _():
            for sends, _ in _build_phases(phases, pin, pout, sems):
                for cp in sends:
                    cp.start()

        body(*ins, *outs, *scratch)

        @pl.when(last)
        def _():
            _finish_phases(_build_phases(phases, pin, pout, sems))

    p_args = [a for ph in phases for a in ph.ins]
    p_shapes = [s for ph in phases for s in ph.out_shapes]
    sem_shapes = [pltpu.SemaphoreType.DMA((ph.n_sems,)) for ph in phases for _ in range(2)]
    outs = pl.pallas_call(
        hosted if phases else body, name=name, grid=grid, in_specs=[*in_specs, *[ANY] * len(p_args)],
        out_specs=[*out_specs, *[ANY] * len(p_shapes)], out_shape=[*out_shape, *p_shapes],
        input_output_aliases=aliases, scratch_shapes=[*scratch_shapes, *sem_shapes],
        compiler_params=_params(("arbitrary",) * len(grid) if phases else semantics),
    )(*args, *p_args)
    phase_outs, at = [], n_out
    for ph in phases:
        phase_outs.append(list(outs[at:at + len(ph.out_shapes)]))
        at += len(ph.out_shapes)
    return list(outs[:n_out]), phase_outs


def _comm_call(name, phases):
    def body(*refs):
        _, pin, _, pout, _, sems = _split_refs(refs, 0, 0, 0, phases)
        built = _build_phases(phases, pin, pout, sems)
        for sends, _ in built:
            for cp in sends:
                cp.start()
        _finish_phases(built)

    aliases, in_at, out_at = {}, 0, 0
    for ph in phases:
        aliases.update({in_at + i: out_at + o for i, o in ph.aliases.items()})
        in_at += len(ph.ins)
        out_at += len(ph.out_shapes)
    p_args = [a for ph in phases for a in ph.ins]
    p_shapes = [s for ph in phases for s in ph.out_shapes]
    outs = pl.pallas_call(
        body, name=name, in_specs=[ANY] * len(p_args), out_specs=[ANY] * len(p_shapes), out_shape=p_shapes,
        input_output_aliases=aliases,
        scratch_shapes=[pltpu.SemaphoreType.DMA((ph.n_sems,)) for ph in phases for _ in range(2)],
    )(*p_args)
    phase_outs, at = [], 0
    for ph in phases:
        phase_outs.append(list(outs[at:at + len(ph.out_shapes)]))
        at += len(ph.out_shapes)
    return phase_outs


def _matmul(name, a, b, grid, a_spec, b_spec, contract, acc_shape, extras, extra_specs, out_shape, out_specs,
            epilogue, phases=()):
    n_extra, n_out, gk = len(extras), len(out_shape), grid[2]

    def product(a_ref, b_ref):
        return lax.dot_general(a_ref[...], b_ref[...], (contract, ((), ())), preferred_element_type=F32)

    def body_one_step(*refs):
        epilogue(product(refs[0], refs[1]), refs[2:2 + n_extra], refs[2 + n_extra:])

    def body(*refs):
        a_ref, b_ref = refs[0], refs[1]
        extra_refs = refs[2:2 + n_extra]
        out_refs = refs[2 + n_extra:2 + n_extra + n_out]
        acc_ref = refs[-1]
        kk = pl.program_id(2)

        @pl.when(kk == 0)
        def _():
            acc_ref[...] = product(a_ref, b_ref)

        @pl.when(kk > 0)
        def _():
            acc_ref[...] += product(a_ref, b_ref)

        @pl.when(kk == gk - 1)
        def _():
            epilogue(acc_ref[...], extra_refs, out_refs)

    outs, phase_outs = _call(
        body_one_step if gk == 1 else body, name, grid, [a_spec, b_spec, *extra_specs], out_specs, out_shape,
        [] if gk == 1 else [pltpu.VMEM(acc_shape, F32)], ("parallel", "arbitrary", "arbitrary"), (a, b, *extras),
        phases)
    return (outs, phase_outs) if phases else outs


def _mm_nn(name, a, b, extras, extra_specs, out_shape, out_specs, epilogue, b_chips=False, tm=1024, tn=1024,
           tk=2048, phases=(), a_parts=None):
    m, k, tm, tk, a_spec = _lhs_rows_by_k(a, tm, tk, a_parts)
    n = b.shape[1] if not b_chips else b.shape[2] * N_CHIPS
    if b_chips:
        tn = _tile(b.shape[2], tn)
        nb = b.shape[2] // tn
        b_spec = pl.BlockSpec((None, tk, tn), lambda i, j, kk: (j // nb, kk, j % nb))
    else:
        tn = _tile(n, tn)
        b_spec = pl.BlockSpec((tk, tn), lambda i, j, kk: (kk, j))
    return _matmul(name, a, b, (m // tm, n // tn, k // tk), a_spec, b_spec, ((1,), (0,)), (tm, tn), extras,
                   extra_specs(tm, tn), out_shape, out_specs(tm, tn), epilogue, phases)


def _lhs_rows_by_k(a, tm, tk, a_parts, tk_max=None):
    if a_parts is None:
        m, k = a.shape
        tm, tk = _tile(m, tm), _tile(k if tk_max is None else tk_max, tk)
        return m, k, tm, tk, pl.BlockSpec((tm, tk), lambda i, j, kk: (i, kk))
    n_parts, m, kp = a.shape
    tm, tk = _tile(m, tm), _tile(kp if tk_max is None else min(kp, tk_max), tk)
    nb = kp // tk
    return m, n_parts * kp, tm, tk, pl.BlockSpec((None, tm, tk), lambda i, j, kk: (a_parts(kk // nb), i, kk % nb))


def _mm_nt(name, a, b, extras, extra_specs, out_shape, out_specs, epilogue, b_chips=False, tm=1024, tn=1024,
           tk=2048, phases=(), a_parts=None):
    m, k, tm, tk, a_spec = _lhs_rows_by_k(a, tm, tk, a_parts, tk_max=b.shape[2] if b_chips else None)
    n = b.shape[0] if not b_chips else b.shape[1]
    tn = _tile(n, tn)
    if b_chips:
        nb = b.shape[2] // tk
        b_spec = pl.BlockSpec((None, tn, tk), lambda i, j, kk: (kk // nb, j, kk % nb))
    else:
        b_spec = pl.BlockSpec((tn, tk), lambda i, j, kk: (j, kk))
    return _matmul(name, a, b, (m // tm, n // tn, k // tk), a_spec, b_spec, ((1,), (1,)), (tm, tn), extras,
                   extra_specs(tm, tn), out_shape, out_specs(tm, tn), epilogue, phases)


def _mm_tn(name, a, b, out_chips=False, tm=1024, tn=1024, tk=2048, phases=(), a_parts=None, b_parts=None):
    if a_parts is None:
        k, m = a.shape
        tm = _tile(m, tm)
        a_spec = pl.BlockSpec((_tile(k, tk), tm), lambda i, j, kk: (kk, i))
    else:
        n_parts, k, mp = a.shape
        m, tm = n_parts * mp, _tile(mp, tm)
        nbm = mp // tm
        a_spec = pl.BlockSpec((None, _tile(k, tk), tm), lambda i, j, kk: (a_parts(i // nbm), kk, i % nbm))
    tk = _tile(k, tk)
    n = b.shape[1] if b_parts is None else b.shape[0] * b.shape[2]
    if out_chips:
        nc = n // N_CHIPS
        tn = _tile(nc, tn)
        nb = nc // tn
        out_shape = [jax.ShapeDtypeStruct((N_CHIPS, m, nc), F32)]
        out_specs = [pl.BlockSpec((None, tm, tn), lambda i, j, kk: (j // nb, i, j % nb))]
    else:
        tn = _tile(n, tn)
        out_shape = [jax.ShapeDtypeStruct((m, n), F32)]
        out_specs = [pl.BlockSpec((tm, tn), lambda i, j, kk: (i, j))]
    if b_parts is None:
        b_spec = pl.BlockSpec((tk, tn), lambda i, j, kk: (kk, j))
    else:
        nbn = b.shape[2] // tn
        b_spec = pl.BlockSpec((None, tk, tn), lambda i, j, kk: (b_parts(j // nbn), kk, j % nbn))

    def epilogue(acc, extra_refs, out_refs):
        out_refs[0][...] = acc

    res = _matmul(name, a, b, (m // tm, n // tn, k // tk), a_spec, b_spec, ((0,), (0,)), (tm, tn), (), [],
                  out_shape, out_specs, epilogue, phases)
    return (res[0][0], res[1]) if phases else res[0]


def _tile_spec(tm, tn):
    return pl.BlockSpec((tm, tn), lambda i, j, kk: (i, j))


def _row_spec(tn):
    return pl.BlockSpec((1, tn), lambda i, j, kk: (0, j))


def _ln_stats(r):
    mu = jnp.mean(r, axis=-1, keepdims=True)
    var = jnp.mean(jnp.square(r - mu), axis=-1, keepdims=True)
    rstd = lax.rsqrt(var + LN_EPS)
    return (r - mu) * rstd, rstd


def _ln_fwd(name, r, g, b, tr=256, phases=()):
    t, d = r.shape
    tr = _tile(t, tr)

    def body(r_ref, g_ref, b_ref, y_ref, yb_ref, xhat_ref, rstd_ref):
        xhat, rstd = _ln_stats(r_ref[...])
        y = xhat * g_ref[...] + b_ref[...]
        y_ref[...] = y
        yb_ref[...] = y.astype(BF16)
        xhat_ref[...] = xhat
        rstd_ref[...] = rstd

    row = pl.BlockSpec((tr, d), lambda i: (i, 0))
    vec = pl.BlockSpec((1, d), lambda i: (0, 0))
    outs, phase_outs = _call(
        body, name, (t // tr,), [row, vec, vec], [row, row, row, pl.BlockSpec((tr, 1), lambda i: (i, 0))],
        [jax.ShapeDtypeStruct((t, d), F32), jax.ShapeDtypeStruct((t, d), BF16),
         jax.ShapeDtypeStruct((t, d), F32), jax.ShapeDtypeStruct((t, 1), F32)], [], ("parallel",), (r, g, b), phases)
    return (outs, phase_outs) if phases else outs


def _ln_bwd_rows(dy, xhat, rstd, g):
    dxhat = dy * g
    m1 = jnp.mean(dxhat, axis=-1, keepdims=True)
    m2 = jnp.mean(dxhat * xhat, axis=-1, keepdims=True)
    return rstd * (dxhat - m1 - xhat * m2)


def _ln_bwd(name, dy, xhat, rstd, g, tr=256, phases=()):
    t, d = dy.shape
    tr = _tile(t, tr)

    def body(dy_ref, xhat_ref, rstd_ref, g_ref, dr_ref, drb_ref, dg_ref, db_ref):
        @pl.when(pl.program_id(0) == 0)
        def _():
            dg_ref[...] = jnp.zeros_like(dg_ref)
            db_ref[...] = jnp.zeros_like(db_ref)

        dy_t, xhat_t = dy_ref[...], xhat_ref[...]
        dr = _ln_bwd_rows(dy_t, xhat_t, rstd_ref[...], g_ref[...])
        dr_ref[...] = dr
        drb_ref[...] = dr.astype(BF16)
        dg_ref[...] += jnp.sum(dy_t * xhat_t, axis=0, keepdims=True)
        db_ref[...] += jnp.sum(dy_t, axis=0, keepdims=True)

    row = pl.BlockSpec((tr, d), lambda i: (i, 0))
    vec = pl.BlockSpec((1, d), lambda i: (0, 0))
    outs, phase_outs = _call(
        body, name, (t // tr,), [row, row, pl.BlockSpec((tr, 1), lambda i: (i, 0)), vec], [row, row, vec, vec],
        [jax.ShapeDtypeStruct((t, d), F32), jax.ShapeDtypeStruct((t, d), BF16),
         jax.ShapeDtypeStruct((1, d), F32), jax.ShapeDtypeStruct((1, d), F32)], [], ("arbitrary",),
        (dy, xhat, rstd, g), phases)
    return (outs, phase_outs) if phases else outs


def _ln2_loss_bwd(r2, target, g, b, tr=256):
    t, d = r2.shape
    tr = _tile(t, tr)

    def body(r_ref, t_ref, g_ref, b_ref, dr_ref, drb_ref, loss_ref, dg_ref, db_ref, dsum_ref):
        @pl.when(pl.program_id(0) == 0)
        def _():
            loss_ref[...] = jnp.zeros_like(loss_ref)
            dg_ref[...] = jnp.zeros_like(dg_ref)
            db_ref[...] = jnp.zeros_like(db_ref)
            dsum_ref[...] = jnp.zeros_like(dsum_ref)

        xhat, rstd = _ln_stats(r_ref[...])
        g_t = g_ref[...]
        err = xhat * g_t + b_ref[...] - t_ref[...]
        loss_ref[...] += 0.5 * jnp.sum(jnp.mean(jnp.square(err), axis=-1, keepdims=True), axis=0, keepdims=True)
        dy = err * (1.0 / d)
        dr = _ln_bwd_rows(dy, xhat, rstd, g_t)
        dr_ref[...] = dr
        drb_ref[...] = dr.astype(BF16)
        dg_ref[...] += jnp.sum(dy * xhat, axis=0, keepdims=True)
        db_ref[...] += jnp.sum(dy, axis=0, keepdims=True)
        dsum_ref[...] += jnp.sum(dr, axis=0, keepdims=True)

    row = pl.BlockSpec((tr, d), lambda i: (i, 0))
    vec = pl.BlockSpec((1, d), lambda i: (0, 0))
    return pl.pallas_call(
        body, name="ln2_loss_bwd", grid=(t // tr,), in_specs=[row, row, vec, vec],
        out_specs=[row, row, pl.BlockSpec((8, 128), lambda i: (0, 0)), vec, vec, vec],
        out_shape=[jax.ShapeDtypeStruct((t, d), F32), jax.ShapeDtypeStruct((t, d), BF16),
                   jax.ShapeDtypeStruct((8, 128), F32), jax.ShapeDtypeStruct((1, d), F32),
                   jax.ShapeDtypeStruct((1, d), F32), jax.ShapeDtypeStruct((1, d), F32)],
        compiler_params=_params(("arbitrary",)),
    )(r2, target, g, b)


POOL_ROWS = 512

DU_POOL = 3


def _du_part(block):
    return (block + DU_POOL) % 4


def _pool_mean_minus_token(u_ref, r0, rows, grp, first):
    width = u_ref.shape[1]
    body = u_ref[pl.ds(r0, rows), :]
    halo = u_ref[pl.ds(pl.multiple_of(jnp.maximum(r0 - POOL_HALO, 0), POOL_HALO), POOL_HALO), :]
    halo = jnp.where(first, 0.0, halo)
    full = jnp.concatenate([halo, body], axis=0)
    s = full
    for step in range(len(POOL_WINDOWS)):
        shifted = pltpu.roll(s, 1 << step, axis=0)
        s = s + jnp.where(grp >= step, shifted, 0.0)
    s = s[POOL_HALO:, :]
    tpos = r0 + lax.broadcasted_iota(jnp.int32, (rows, width), 0)
    count = jnp.minimum(tpos + 1, 2 << grp).astype(F32)
    return s / count - body, count


def _pool_fwd(u, w_pool, pool_scale, t, pw):
    gw = pw // len(POOL_WINDOWS)
    rows = _tile(t, POOL_ROWS)

    def body(u_ref, w_ref, s_ref, o_ref):
        grp = pl.program_id(0)

        def chunk(ci, carry):
            r0 = pl.multiple_of(ci * rows, rows)
            y, _ = _pool_mean_minus_token(u_ref, r0, rows, grp, ci == 0)
            yw = jnp.dot(y.astype(BF16), w_ref[...], preferred_element_type=F32)
            o_ref[pl.ds(r0, rows), :] = (yw * s_ref[...]).astype(BF16)
            return carry

        lax.fori_loop(0, t // rows, chunk, 0)

    return pl.pallas_call(
        body, name="pool_fwd", grid=(len(POOL_WINDOWS),),
        in_specs=[pl.BlockSpec((t, gw), lambda g: (0, g)), pl.BlockSpec((None, gw, gw), lambda g: (g, 0, 0)),
                  pl.BlockSpec((None, 1, gw), lambda g: (g, 0, 0))],
        out_specs=pl.BlockSpec((None, t, gw), lambda g: (0, 0, g)),
        out_shape=jax.ShapeDtypeStruct((2, t, pw), BF16),
        compiler_params=_params(("parallel",)),
    )(u, w_pool, pool_scale)


def _pool_bwd(u, dmix, w_pool, pool_scale, t, pw):
    n_grp = len(POOL_WINDOWS)
    gw = pw // n_grp
    rows = _tile(t, POOL_ROWS)

    def body(u_ref, dm_ref, w_ref, s_ref, du_ref, dw_ref, ds_ref, e_ref):
        grp = pl.program_id(0)
        dw_ref[...] = jnp.zeros_like(dw_ref)
        ds_ref[...] = jnp.zeros_like(ds_ref)
        e_ref[pl.ds(t, POOL_HALO), :] = jnp.zeros((POOL_HALO, gw), F32)

        def chunk(ci, carry):
            r0 = pl.multiple_of(ci * rows, rows)
            y, count = _pool_mean_minus_token(u_ref, r0, rows, grp, ci == 0)
            yb = y.astype(BF16)
            yw = jnp.dot(yb, w_ref[...], preferred_element_type=F32)
            dy2 = dm_ref[pl.ds(r0, rows), :]
            ds_ref[...] += jnp.sum(dy2 * yw, axis=0, keepdims=True)
            dyw = (dy2 * s_ref[...]).astype(BF16)
            dw_ref[...] += lax.dot_general(yb, dyw, (((0,), (0,)), ((), ())), preferred_element_type=F32)
            dy = lax.dot_general(dyw, w_ref[...], (((1,), (1,)), ((), ())), preferred_element_type=F32)
            e_ref[pl.ds(r0, rows), :] = dy / count
            return carry

        lax.fori_loop(0, t // rows, chunk, 0)

        def chunk2(ci, carry):
            r0 = pl.multiple_of(ci * rows, rows)
            full = e_ref[pl.ds(r0, rows + POOL_HALO), :]
            s = full
            for step in range(n_grp):
                shifted = pltpu.roll(s, rows + POOL_HALO - (1 << step), axis=0)
                s = s + jnp.where(grp >= step, shifted, 0.0)
            e = full[:rows, :]
            tpos = r0 + lax.broadcasted_iota(jnp.int32, (rows, gw), 0)
            count = jnp.minimum(tpos + 1, 2 << grp).astype(F32)
            du_ref[pl.ds(r0, rows), :] = (s[:rows, :] - e * count).astype(BF16)
            return carry

        lax.fori_loop(0, t // rows, chunk2, 0)

    return pl.pallas_call(
        body, name="pool_bwd", grid=(n_grp,),
        in_specs=[pl.BlockSpec((t, gw), lambda g: (0, g)), pl.BlockSpec((t, gw), lambda g: (0, g)),
                  pl.BlockSpec((None, gw, gw), lambda g: (g, 0, 0)),
                  pl.BlockSpec((None, 1, gw), lambda g: (g, 0, 0))],
        out_specs=[pl.BlockSpec((None, t, gw), lambda g: (DU_POOL, 0, g)),
                   pl.BlockSpec((None, gw, gw), lambda g: (g, 0, 0)), pl.BlockSpec((None, 1, gw), lambda g: (g, 0, 0))],
        out_shape=[jax.ShapeDtypeStruct((4, t, pw), BF16), jax.ShapeDtypeStruct((n_grp, gw, gw), F32),
                   jax.ShapeDtypeStruct((n_grp, 1, gw), F32)],
        scratch_shapes=[pltpu.VMEM((t + POOL_HALO, gw), F32)],
        compiler_params=_params(("parallel",)),
    )(u, dmix, w_pool, pool_scale)


def _sb_scores(q, k_blk, scale, mask):
    z = lax.dot_general(q, k_blk, (((1,), (1,)), ((), ())), preferred_element_type=F32) * scale
    log_not = jnp.minimum(-z, 0.0) - jnp.log(1.0 + jnp.exp(-jnp.abs(z)))
    return z, (log_not if mask is None else jnp.where(mask, log_not, 0.0))


def _sb_weights(e, mask):
    a = jnp.exp(e)
    return a if mask is None else jnp.where(mask, a, 0.0)


EXP_IS_ZERO_BELOW = -104.0


def _weights_alive(after):
    return (jnp.max(after) >= EXP_IS_ZERO_BELOW).astype(jnp.int32)


def _lower_half_done(afters):
    worst = functools.reduce(jnp.maximum, [jnp.max(after[QB // 2:]) for after in afters])
    return worst < EXP_IS_ZERO_BELOW


def _keep_below(top, whole, live):
    return top if live == whole.shape[0] else jnp.concatenate([top, whole[live:]], axis=0)


def _split_dot(vs, tri):
    parts = []
    for v in vs:
        hi = v.astype(BF16)
        parts += [hi, (v - hi.astype(F32)).astype(BF16)]
    prod = jnp.dot(jnp.concatenate(parts, axis=0), tri, preferred_element_type=F32)
    m = vs[0].shape[0]
    return [prod[2 * k * m:(2 * k + 1) * m] + prod[(2 * k + 1) * m:(2 * k + 2) * m] for k in range(len(vs))]


def _head(ref, h, rows=None):
    cols = slice(h * HEAD_DIM, (h + 1) * HEAD_DIM)
    return ref[:, cols] if rows is None else ref[rows, cols]


def _attn_fwd(ub, mix, t, nh, hg, phases=()):
    scale = float(1.0 / (HEAD_DIM ** 0.5))
    ng = nh // hg

    def body(q_ref, k_ref, v_ref, o_ref):
        i = pl.program_id(1)
        row = lax.broadcasted_iota(jnp.int32, (QB, KB), 0)
        col = lax.broadcasted_iota(jnp.int32, (QB, KB), 1)
        suffix = (row >= col).astype(BF16)

        def more(carry):
            return jnp.logical_and(carry[0] <= i, carry[3] > 0)

        def block(n, accs, afters, mask, live):
            rows = pl.ds(pl.multiple_of((i - n) * KB, KB), KB)
            new_accs, new_afters = [], []
            scores = [_sb_scores(_head(q_ref, h, slice(0, live)), _head(k_ref, h, rows), scale, mask)
                      for h in range(hg)]
            withins = _split_dot([log_not for _, log_not in scores], suffix)
            for h in range(hg):
                z, log_not = scores[h]
                a = _sb_weights(z + withins[h] + afters[h][:live], mask)
                acc = accs[h][:live] + jnp.dot(a.astype(BF16), _head(v_ref, h, rows), preferred_element_type=F32)
                after = afters[h][:live] + jnp.sum(log_not, axis=1, keepdims=True)
                new_accs.append(_keep_below(acc, accs[h], live))
                new_afters.append(_keep_below(after, afters[h], live))
            return n + 1, tuple(new_accs), tuple(new_afters), _weights_alive(functools.reduce(jnp.maximum, new_afters))

        def below_diagonal(carry):
            n, accs, afters, _ = carry
            return lax.cond(_lower_half_done(afters), lambda: block(n, accs, afters, None, QB // 2),
                            lambda: block(n, accs, afters, None, QB))

        first = block(jnp.int32(0), tuple(jnp.zeros((QB, HEAD_DIM), F32) for _ in range(hg)),
                      tuple(jnp.zeros((QB, 1), F32) for _ in range(hg)), col < row, QB)
        _, accs, _, _ = lax.while_loop(more, below_diagonal, first)
        for h in range(hg):
            o_ref[:, h * HEAD_DIM:(h + 1) * HEAD_DIM] = accs[h].astype(BF16)

    wide = hg * HEAD_DIM
    outs, phase_outs = _call(
        lambda q_ref, k_ref, v_ref, mix_ref, o_ref: body(q_ref, k_ref, v_ref, o_ref), "attn_fwd", (ng, t // QB),
        [pl.BlockSpec((QB, wide), lambda g, i: (i, ng + g)), pl.BlockSpec((t, wide), lambda g, i: (0, 2 * ng + g)),
         pl.BlockSpec((t, wide), lambda g, i: (0, 3 * ng + g)), ANY],
        [pl.BlockSpec((None, QB, wide), lambda g, i: (1, i, g))], [jax.ShapeDtypeStruct(mix.shape, mix.dtype)], [],
        ("parallel", "arbitrary"), (ub, ub, ub, mix), phases, in_place={3: 0})
    return outs[0], phase_outs


def _attn_bwd(ub, dmix, du, t, nh, hg, phases=()):
    scale = float(1.0 / (HEAD_DIM ** 0.5))
    ng = nh // hg

    def body(q_ref, k_ref, v_ref, do_ref, du_in_ref, du_ref, g_ref, z_ref, dk_ref, dv_ref, half_ref):
        i = pl.program_id(1)

        @pl.when(i == 0)
        def _():
            dk_ref[...] = jnp.zeros_like(dk_ref)
            dv_ref[...] = jnp.zeros_like(dv_ref)

        row = lax.broadcasted_iota(jnp.int32, (QB, KB), 0)
        col = lax.broadcasted_iota(jnp.int32, (QB, KB), 1)
        suffix = (row >= col).astype(BF16)
        prefix = (row <= col).astype(BF16)

        def more(carry):
            return jnp.logical_and(carry[0] <= i, carry[2] > 0)

        def down(n, afters, mask, live):
            ks = pl.multiple_of((i - n) * KB, KB)
            rows, top = pl.ds(ks, KB), slice(0, live)
            new_afters = []
            scores = [_sb_scores(_head(q_ref, h, top), _head(k_ref, h, rows), scale, mask) for h in range(hg)]
            withins = _split_dot([log_not for _, log_not in scores], suffix)
            for h in range(hg):
                do = _head(do_ref, h, top).astype(BF16)
                z, log_not = scores[h]
                a = _sb_weights(z + withins[h] + afters[h][:live], mask)
                da = lax.dot_general(do, _head(v_ref, h, rows), (((1,), (1,)), ((), ())),
                                     preferred_element_type=F32)
                g_ref[h, top, pl.ds(ks, KB)] = a * da
                z_ref[h, top, pl.ds(ks, KB)] = z
                dv_ref[rows, h * HEAD_DIM:(h + 1) * HEAD_DIM] += lax.dot_general(
                    a.astype(BF16), do, (((0,), (0,)), ((), ())), preferred_element_type=F32)
                after = afters[h][:live] + jnp.sum(log_not, axis=1, keepdims=True)
                new_afters.append(_keep_below(after, afters[h], live))
            half_ref[n] = jnp.int32(live < QB)
            return n + 1, tuple(new_afters), _weights_alive(functools.reduce(jnp.maximum, new_afters))

        def down_below_diagonal(carry):
            n, afters, _ = carry
            return lax.cond(_lower_half_done(afters), lambda: down(n, afters, None, QB // 2),
                            lambda: down(n, afters, None, QB))

        diagonal = col < row
        first = down(jnp.int32(0), tuple(jnp.zeros((QB, 1), F32) for _ in range(hg)), diagonal, QB)
        visited, _, _ = lax.while_loop(more, down_below_diagonal, first)

        def up(kb, carry, mask, live):
            dqs, befores = carry
            ks = pl.multiple_of(kb * KB, KB)
            rows, top = pl.ds(ks, KB), slice(0, live)
            new_dqs, new_befores = [], []
            gs = [g_ref[h, top, pl.ds(ks, KB)] for h in range(hg)]
            g_withins = _split_dot(gs, prefix)
            for h in range(hg):
                g = gs[h]
                z = z_ref[h, top, pl.ds(ks, KB)]
                g_upto = g_withins[h] + befores[h][:live]
                dz = g - jax.nn.sigmoid(z) * g_upto
                dz = dz if mask is None else jnp.where(mask, dz, 0.0)
                dzs = (dz * scale).astype(BF16)
                dq = dqs[h][:live] + jnp.dot(dzs, _head(k_ref, h, rows), preferred_element_type=F32)
                dk_ref[rows, h * HEAD_DIM:(h + 1) * HEAD_DIM] += lax.dot_general(
                    dzs, _head(q_ref, h, top), (((0,), (0,)), ((), ())), preferred_element_type=F32)
                before = befores[h][:live] + jnp.sum(g, axis=1, keepdims=True)
                new_dqs.append(_keep_below(dq, dqs[h], live))
                new_befores.append(_keep_below(before, befores[h], live))
            return tuple(new_dqs), tuple(new_befores)

        def up_below_diagonal(kb, carry):
            return lax.cond(half_ref[i - kb] > 0, lambda: up(kb, carry, None, QB // 2),
                            lambda: up(kb, carry, None, QB))

        below = lax.fori_loop(i + 1 - visited, i, up_below_diagonal,
                              (tuple(jnp.zeros((QB, HEAD_DIM), F32) for _ in range(hg)),
                               tuple(jnp.zeros((QB, 1), F32) for _ in range(hg))))
        dqs, _ = up(i, below, diagonal, QB)
        q_rows = pl.ds(pl.multiple_of(i * QB, QB), QB)
        for h in range(hg):
            du_ref[0, q_rows, h * HEAD_DIM:(h + 1) * HEAD_DIM] = dqs[h].astype(BF16)

        @pl.when(i == t // QB - 1)
        def _():
            du_ref[1] = dk_ref[...].astype(BF16)
            du_ref[2] = dv_ref[...].astype(BF16)

    wide = hg * HEAD_DIM
    tile = lambda off: pl.BlockSpec((QB, wide), lambda g, i: (i, off + g))
    strip = lambda off: pl.BlockSpec((t, wide), lambda g, i: (0, off + g))
    outs, phase_outs = _call(
        body, "attn_bwd", (ng, t // QB), [tile(ng), strip(2 * ng), strip(3 * ng), tile(ng), ANY],
        [pl.BlockSpec((3, t, wide), lambda g, i: (0, 0, g))], [jax.ShapeDtypeStruct(du.shape, du.dtype)],
        [pltpu.VMEM((hg, QB, t), F32), pltpu.VMEM((hg, QB, t), F32), pltpu.VMEM((t, wide), F32),
         pltpu.VMEM((t, wide), F32), pltpu.SMEM((t // KB,), jnp.int32)], ("parallel", "arbitrary"),
        (ub, ub, ub, dmix, du), phases, in_place={4: 0})
    return outs[0], phase_outs


def _row_tile(rows, cols, pref_bytes=2 * 1024 * 1024):
    tr = max(8, pref_bytes // (4 * cols))
    while rows % tr:
        tr //= 2
    return max(tr, 1)


def _pair_sum(name, g, s, c_idx):
    _, _, r2, cols = g.shape
    tr = _row_tile(r2, cols)

    def body(c_ref, g_ref, s_ref, o_ref):
        o_ref[...] = (g_ref[...] + s_ref[...]).astype(BF16)

    return pl.pallas_call(
        body, name=name,
        grid_spec=pltpu.PrefetchScalarGridSpec(
            num_scalar_prefetch=1, grid=(N_CHIPS, r2 // tr),
            in_specs=[pl.BlockSpec((None, None, tr, cols), lambda p, i, c: (p, c[0], i, 0)),
                      pl.BlockSpec((None, tr, cols), lambda p, i, c: (p, i, 0))],
            out_specs=pl.BlockSpec((None, tr, cols), lambda p, i, c: (p, i, 0))),
        out_shape=jax.ShapeDtypeStruct((N_CHIPS, r2, cols), BF16),
        compiler_params=_params(("parallel", "parallel")),
    )(c_idx, g, s)


def _chip_sum(name, g, s, received, place):
    _, _, r2, cols = g.shape
    tr = _row_tile(r2, cols)
    counts = [r.shape[0] for r in received]

    def body(place_ref, g_ref, s_ref, *refs):
        total = g_ref[...] + s_ref[...]
        for r_ref, n in zip(refs[:-1], counts):
            for k in range(n):
                total = total + r_ref[k].astype(F32)
        refs[-1][...] = total

    return pl.pallas_call(
        body, name=name,
        grid_spec=pltpu.PrefetchScalarGridSpec(
            num_scalar_prefetch=1, grid=(r2 // tr,),
            in_specs=[pl.BlockSpec((None, None, tr, cols), lambda i, p: (p[0], p[1], i, 0)),
                      pl.BlockSpec((None, tr, cols), lambda i, p: (p[0], i, 0)),
                      *[pl.BlockSpec((n, tr, cols), lambda i, p: (0, i, 0)) for n in counts]],
            out_specs=pl.BlockSpec((None, tr, cols), lambda i, p: (p[1], i, 0))),
        out_shape=jax.ShapeDtypeStruct((2, r2, cols), F32),
        compiler_params=_params(("parallel",)),
    )(place, g, s, *received)


def _cast_into_slot(name, w, place):
    rows, cols = w.shape
    r2 = rows // 2
    tr = _row_tile(r2, cols)
    nb = r2 // tr

    def body(place_ref, w_ref, o_ref):
        o_ref[...] = w_ref[...].astype(BF16)

    return pl.pallas_call(
        body, name=name,
        grid_spec=pltpu.PrefetchScalarGridSpec(
            num_scalar_prefetch=1, grid=(2, nb),
            in_specs=[pl.BlockSpec((tr, cols), lambda h, i, s: (h * nb + i, 0))],
            out_specs=pl.BlockSpec((None, None, tr, cols), lambda h, i, s: (s[0], h, i, 0))),
        out_shape=jax.ShapeDtypeStruct((N_CHIPS, 2, r2, cols), BF16),
        compiler_params=_params(("parallel", "parallel")),
    )(place, w)


def _colsum(name, a):
    def body(a_ref, o_ref):
        o_ref[...] = jnp.sum(a_ref[...], axis=0, keepdims=True)

    whole = lambda shape: pl.BlockSpec(shape, lambda i: (0, 0))
    return pl.pallas_call(
        body, name=name, grid=(1,), in_specs=[whole(a.shape)], out_specs=whole((1, a.shape[1])),
        out_shape=jax.ShapeDtypeStruct((1, a.shape[1]), F32), compiler_params=_params(("arbitrary",)),
    )(a)


def _adamw(name, w, g, m, v):
    rows, cols = w.shape
    tr = _row_tile(rows, cols, 1024 * 1024)

    def body(w_ref, g_ref, m_ref, v_ref, g_out_ref, d_ref, nm_ref, nv_ref):
        g_t = g_ref[...]
        m_t = ADAM_B1 * m_ref[...] + (1.0 - ADAM_B1) * g_t
        v_t = ADAM_B2 * v_ref[...] + (1.0 - ADAM_B2) * jnp.square(g_t)
        m_hat = m_t / (1.0 - ADAM_B1 ** ADAM_STEP)
        v_hat = v_t / (1.0 - ADAM_B2 ** ADAM_STEP)
        g_out_ref[...] = g_t
        d_ref[...] = -ADAM_LR * (m_hat / (jnp.sqrt(v_hat) + ADAM_EPS) + ADAM_WD * w_ref[...])
        nm_ref[...] = m_t
        nv_ref[...] = v_t

    spec = pl.BlockSpec((tr, cols), lambda i: (i, 0))
    shape = jax.ShapeDtypeStruct((rows, cols), F32)
    return pl.pallas_call(
        body, name=name, grid=(rows // tr,), in_specs=[spec] * 4, out_specs=[spec] * 4, out_shape=[shape] * 4,
        compiler_params=_params(("parallel",)),
    )(w, g, m, v)


def _all_reduce_small(packed, phases=()):
    rows, cols = packed.shape

    def body(in_ref, out_ref, all_ref, send_sems, recv_sems):
        x, y, c = _pos()
        me = 4 * x + 2 * y + c
        all_ref[me] = in_ref[...]
        cps = []
        for r in range(1, N_DEV):
            bx, by, bc = (r >> 2) & 1, (r >> 1) & 1, r & 1
            peer = (1 - x if bx else x, 1 - y if by else y, 1 - c if bc else c)
            cp = pltpu.make_async_remote_copy(
                src_ref=in_ref, dst_ref=all_ref.at[me], send_sem=send_sems.at[r - 1], recv_sem=recv_sems.at[r - 1],
                device_id=peer, device_id_type=MESH)
            cp.start()
            cps.append(cp)
        for cp in cps:
            cp.wait()
        total = all_ref[0]
        for d in range(1, N_DEV):
            total = total + all_ref[d]
        out_ref[...] = total

    vmem = pl.BlockSpec(memory_space=pltpu.VMEM)
    outs, phase_outs = _call(
        body, "all_reduce_small", (1,), [vmem], [vmem], [jax.ShapeDtypeStruct((rows, cols), F32)],
        [pltpu.VMEM((N_DEV, rows, cols), F32), pltpu.SemaphoreType.DMA((N_DEV - 1,)),
         pltpu.SemaphoreType.DMA((N_DEV - 1,))], ("arbitrary",), (packed,), phases)
    return outs[0], phase_outs


def kernel(x, ln_in_g, ln_in_b, w_in, w_pool, pool_scale, w_out, ln1_g, ln1_b, w_ff1, b_ff1, w_ff2, b_ff2, ln2_g, ln2_b, loss_target, m_ln_in_g, m_ln_in_b, m_w_in, m_w_pool, m_pool_scale, m_w_out, m_ln1_g, m_ln1_b, m_w_ff1, m_b_ff1, m_w_ff2, m_b_ff2, m_ln2_g, m_ln2_b, v_ln_in_g, v_ln_in_b, v_w_in, v_w_pool, v_pool_scale, v_w_out, v_ln1_g, v_ln1_b, v_w_ff1, v_b_ff1, v_w_ff2, v_b_ff2, v_ln2_g, v_ln2_b):
    t, d = x.shape[1], x.shape[2]
    pw = d // 2
    n_grp = len(POOL_WINDOWS)
    gw = pw // n_grp
    gwc = gw // N_CHIPS
    nh = pw // HEAD_DIM
    ff = w_ff1.shape[2] * N_CHIPS
    assert w_in.shape[0] == 1 and w_in.shape[2] * N_CHIPS == 2 * d and gwc <= 128

    x_idx, y_idx, c_idx = _pos()
    chip_arr = jnp.reshape(2 * x_idx + y_idx, (1,)).astype(jnp.int32)
    c_arr = jnp.reshape(c_idx, (1,)).astype(jnp.int32)
    place = jnp.concatenate([chip_arr, c_arr])

    xs = x.reshape(t, d)
    target = loss_target.reshape(t, d)
    row = lambda vec: vec.reshape(1, -1)

    scale_tile = jnp.zeros((1, 8, 128), F32).at[0, :n_grp, :gwc].set(pool_scale[0])
    scale_slots = lax.dynamic_update_slice(jnp.zeros((N_CHIPS, 8, 128), F32), scale_tile, (chip_arr[0], 0, 0))
    shards = dict(w_in=w_in[0], w_out=w_out[0], w_ff1=w_ff1[0], w_ff2=w_ff2[0], w_pool=w_pool[0].reshape(gw, gw))
    slot = {nm: _cast_into_slot("cast_" + nm, w, place) for nm, w in shards.items()}
    unsplit = lambda s: s.reshape(N_CHIPS, 2 * s.shape[2], s.shape[3])

    (h0, h0b, xhat0, rstd0), ((win_s,), (wpool_s,), (scale_g,)) = _ln_fwd(
        "ln_in_fwd", xs, row(ln_in_g), row(ln_in_b),
        phases=[_gather_ici_phase(slot["w_in"]), _gather_ici_phase(slot["w_pool"]), _gather_whole_phase(scale_slots)])
    (win_s,), (wpool_s,) = _comm_call("gather_d2d_first", [_gather_d2d_phase(win_s), _gather_d2d_phase(wpool_s)])
    win_g = unsplit(win_s)
    wpool_full = unsplit(wpool_s).reshape(N_CHIPS, n_grp, gwc, gw).transpose(1, 0, 2, 3).reshape(n_grp, gw, gw)
    scale_full = scale_g[:, :n_grp, :gwc].transpose(1, 0, 2).reshape(n_grp, 1, gw)

    def store_f32(acc, extra_refs, out_refs):
        out_refs[0][...] = acc

    def pool_f32_all_bf16(acc, extra_refs, out_refs):
        @pl.when(pl.program_id(1) == 0)
        def _():
            out_refs[0][...] = acc

        out_refs[1][...] = acc.astype(BF16)

    assert w_in.shape[2] == pw
    (u, ub), ((wout_s,),) = _mm_nn(
        "in_proj", h0b, win_g, (), lambda tm, tn: [],
        [jax.ShapeDtypeStruct((t, pw), F32), jax.ShapeDtypeStruct((t, 2 * d), BF16)],
        lambda tm, tn: [pl.BlockSpec((tm, tn), lambda i, j, kk: (i, 0)), _tile_spec(tm, tn)],
        pool_f32_all_bf16, b_chips=True, tn=pw, phases=[_gather_ici_phase(slot["w_out"])])
    mix_in = _pool_fwd(u, wpool_full, scale_full, t, pw)
    mix_in, ((wff1_s,), (wout_s,)) = _attn_fwd(
        ub, mix_in, t, nh, min(nh, 4), phases=[_gather_ici_phase(slot["w_ff1"]), _gather_d2d_phase(wout_s)])
    same_part = lambda block: block
    wout_full = unsplit(wout_s).reshape(d, d)

    def residual(acc, extra_refs, out_refs):
        out_refs[0][...] = ALPHA * extra_refs[0][...] + acc

    (r1,), ((wff1_s,), (wff2_s,)) = _mm_nn(
        "out_proj", mix_in, wout_full, (h0,), lambda tm, tn: [_tile_spec(tm, tn)],
        [jax.ShapeDtypeStruct((t, d), F32)], lambda tm, tn: [_tile_spec(tm, tn)], residual, a_parts=same_part,
        phases=[_gather_d2d_phase(wff1_s), _gather_ici_phase(slot["w_ff2"], others=(2,))])
    wff1_g = unsplit(wff1_s)
    h1, h1b, xhat1, rstd1 = _ln_fwd("ln1_fwd", r1, ln1_g, ln1_b)

    def relu_sq(acc, extra_refs, out_refs):
        p = jnp.maximum(acc + extra_refs[0][...], 0.0)
        out_refs[0][...] = p
        out_refs[1][...] = jnp.square(p).astype(BF16)

    (relu_z, act_b), ((wff2_s,),) = _mm_nn(
        "ff1", h1b, wff1_g, (b_ff1,), lambda tm, tn: [_row_spec(tn)],
        [jax.ShapeDtypeStruct((t, ff), F32), jax.ShapeDtypeStruct((t, ff), BF16)],
        lambda tm, tn: [_tile_spec(tm, tn)] * 2, relu_sq, b_chips=True,
        phases=[_gather_ici_phase(wff2_s, others=(0, 1))])
    ((wff2_s,),) = _comm_call("gather_d2d_last", [_gather_d2d_phase(wff2_s)])
    wff2_full = unsplit(wff2_s).reshape(ff, d)

    def residual_bias(acc, extra_refs, out_refs):
        out_refs[0][...] = ALPHA * extra_refs[0][...] + (acc + extra_refs[1][...])

    r2 = _mm_nn("ff2", act_b, wff2_full, (h1, b_ff2), lambda tm, tn: [_tile_spec(tm, tn), _row_spec(tn)],
                [jax.ShapeDtypeStruct((t, d), F32)], lambda tm, tn: [_tile_spec(tm, tn)], residual_bias)[0]

    dr2, dr2b, loss_tile, g_ln2_g, g_ln2_b, g_b_ff2 = _ln2_loss_bwd(r2, target, ln2_g, ln2_b)
    loss = lax.psum(loss_tile[0, 0], ("x", "y", "c"))

    halves = lambda g: g.reshape(N_CHIPS, 2, g.shape[1] // 2, g.shape[2])
    g_ff2 = halves(_mm_tn("grad_w_ff2", act_b, dr2b).reshape(N_CHIPS, ff // N_CHIPS, d))

    def relu_sq_bwd(acc, extra_refs, out_refs):
        dz = acc * (2.0 * extra_refs[0][...])
        out_refs[0][...] = dz.astype(BF16)
        rows = lax.broadcasted_iota(jnp.int32, out_refs[1].shape, 0)
        out_refs[1][...] = jnp.where(rows == 0, jnp.sum(dz, axis=0, keepdims=True), 0.0)

    tm_ff = _tile(t, 1024)
    (dz1b, g_b_ff1_parts), ((s_ff2,),) = _mm_nt(
        "ff2_bwd", dr2b, wff2_full, (relu_z,), lambda tm, tn: [_tile_spec(tm, tn)],
        [jax.ShapeDtypeStruct((t, ff), BF16), jax.ShapeDtypeStruct((8 * (t // tm_ff), ff), F32)],
        lambda tm, tn: [_tile_spec(tm, tn), pl.BlockSpec((8, tn), lambda i, j, kk: (i, j))], relu_sq_bwd,
        phases=[_swap_phase(g_ff2)])
    p_ff2 = _pair_sum("pair_sum_w_ff2", g_ff2, s_ff2, c_arr)
    g_ff1, ((r_ff2_a,),) = _mm_tn("grad_w_ff1", h1b, dz1b, out_chips=True,
                                  phases=[_scatter_phase(p_ff2, others=(0, 1))])
    g_ff1 = halves(g_ff1)

    def plus_alpha(acc, extra_refs, out_refs):
        out_refs[0][...] = ALPHA * extra_refs[0][...] + acc

    (dh1,), ((s_ff1,), (r_ff2_b,)) = _mm_nt(
        "ff1_bwd", dz1b, wff1_g, (dr2,), lambda tm, tn: [_tile_spec(tm, tn)],
        [jax.ShapeDtypeStruct((t, d), F32)], lambda tm, tn: [_tile_spec(tm, tn)], plus_alpha, b_chips=True,
        phases=[_swap_phase(g_ff1), _scatter_phase(p_ff2, others=(2,))])
    q_ff2 = _chip_sum("chip_sum_w_ff2", g_ff2, s_ff2, [r_ff2_a, r_ff2_b], place)
    p_ff1 = _pair_sum("pair_sum_w_ff1", g_ff1, s_ff1, c_arr)
    (dr1, dr1b, g_ln1_g, g_ln1_b), ((q_ff2,),) = _ln_bwd("ln1_bwd", dh1, xhat1, rstd1, ln1_g,
                                                         phases=[_assemble_phase(q_ff2)])

    g_out = halves(_mm_tn("grad_w_out", mix_in, dr1b, a_parts=same_part).reshape(N_CHIPS, d // N_CHIPS, d))
    (dmix,), ((s_out,),) = _mm_nt(
        "out_proj_bwd", dr1b, wout_full, (), lambda tm, tn: [], [jax.ShapeDtypeStruct((t, d), F32)],
        lambda tm, tn: [_tile_spec(tm, tn)], store_f32, phases=[_swap_phase(g_out)])
    p_out = _pair_sum("pair_sum_w_out", g_out, s_out, c_arr)
    du, g_w_pool_full, g_scale_full = _pool_bwd(u, dmix, wpool_full, scale_full, t, pw)
    du, ((r_ff1,), (r_out,)) = _attn_bwd(ub, dmix, du, t, nh, 1,
                                         phases=[_scatter_phase(p_ff1), _scatter_phase(p_out)])
    q_ff1 = _chip_sum("chip_sum_w_ff1", g_ff1, s_ff1, [r_ff1], place)
    q_out = _chip_sum("chip_sum_w_out", g_out, s_out, [r_out], place)
    g_in, ((q_ff1,), (q_out,)) = _mm_tn("grad_w_in", h0b, du, out_chips=True, b_parts=_du_part,
                                        phases=[_assemble_phase(q_ff1), _assemble_phase(q_out)])
    g_in = halves(g_in)
    g_pool = halves(g_w_pool_full.reshape(n_grp, N_CHIPS, gwc, gw).transpose(1, 0, 2, 3).reshape(N_CHIPS, gw, gw))
    (s_in,), (s_pool,) = _comm_call("rs_swap_last", [_swap_phase(g_in), _swap_phase(g_pool)])
    p_in = _pair_sum("pair_sum_w_in", g_in, s_in, c_arr)
    p_pool = _pair_sum("pair_sum_w_pool", g_pool, s_pool, c_arr)
    (dh0,), ((r_in,), (r_pool,)) = _mm_nt(
        "in_proj_bwd", du, win_g, (dr1,), lambda tm, tn: [_tile_spec(tm, tn)],
        [jax.ShapeDtypeStruct((t, d), F32)], lambda tm, tn: [_tile_spec(tm, tn)], plus_alpha, b_chips=True,
        a_parts=_du_part, phases=[_scatter_phase(p_in), _scatter_phase(p_pool)])
    q_in = _chip_sum("chip_sum_w_in", g_in, s_in, [r_in], place)
    q_pool = _chip_sum("chip_sum_w_pool", g_pool, s_pool, [r_pool], place)
    dx, _, g_ln_in_g, g_ln_in_b = _ln_bwd("ln_in_bwd", dh0, xhat0, rstd0, row(ln_in_g))

    lane = 2048 if d % 2048 == 0 else d
    small_names = ["ln_in_g", "ln_in_b", "ln1_g", "ln1_b", "b_ff1", "b_ff2", "ln2_g", "ln2_b"]
    small_w = dict(ln_in_g=ln_in_g, ln_in_b=ln_in_b, ln1_g=ln1_g, ln1_b=ln1_b, b_ff1=b_ff1, b_ff2=b_ff2, ln2_g=ln2_g,
                   ln2_b=ln2_b)
    small_m = dict(ln_in_g=m_ln_in_g, ln_in_b=m_ln_in_b, ln1_g=m_ln1_g, ln1_b=m_ln1_b, b_ff1=m_b_ff1, b_ff2=m_b_ff2,
                   ln2_g=m_ln2_g, ln2_b=m_ln2_b)
    small_v = dict(ln_in_g=v_ln_in_g, ln_in_b=v_ln_in_b, ln1_g=v_ln1_g, ln1_b=v_ln1_b, b_ff1=v_b_ff1, b_ff2=v_b_ff2,
                   ln2_g=v_ln2_g, ln2_b=v_ln2_b)
    small_g = dict(ln_in_g=g_ln_in_g, ln_in_b=g_ln_in_b, ln1_g=g_ln1_g, ln1_b=g_ln1_b, b_ff2=g_b_ff2, ln2_g=g_ln2_g,
                   ln2_b=g_ln2_b)

    def pack(parts):
        flat = jnp.concatenate([p.reshape(-1) for p in parts])
        n_rows = -(-flat.shape[0] // lane)
        n_rows = -(-n_rows // 8) * 8
        return jnp.pad(flat, (0, n_rows * lane - flat.shape[0])).reshape(n_rows, lane)

    small_g["b_ff1"] = _colsum("b_ff1_colsum", g_b_ff1_parts)
    summed, ((q_in,), (q_pool,)) = _all_reduce_small(
        pack([small_g[nm] for nm in small_names] + [g_scale_full]),
        phases=[_assemble_phase(q_in), _assemble_phase(q_pool)])
    summed = summed.reshape(-1)

    big = {}
    for nm, q, w, m, v in [("w_in", q_in, w_in, m_w_in, v_w_in), ("w_out", q_out, w_out, m_w_out, v_w_out),
                           ("w_ff1", q_ff1, w_ff1, m_w_ff1, v_w_ff1), ("w_ff2", q_ff2, w_ff2, m_w_ff2, v_w_ff2),
                           ("w_pool", q_pool, w_pool, m_w_pool, v_w_pool)]:
        g = q.reshape(2 * q.shape[1], q.shape[2])
        flat = lambda arr: arr.reshape(g.shape)
        big[nm] = tuple(arr.reshape(w.shape) for arr in _adamw("adamw_" + nm, flat(w), g, flat(m), flat(v)))

    g_small, off = {}, 0
    for nm in small_names:
        g_small[nm] = summed[off:off + small_w[nm].size]
        off += small_w[nm].size
    g_scale_all = summed[off:off + n_grp * gw].reshape(n_grp, N_CHIPS, gwc)
    g_scale = lax.dynamic_index_in_dim(g_scale_all, chip_arr[0], axis=1, keepdims=False)

    order = small_names + ["pool_scale"]
    small_w["pool_scale"], small_m["pool_scale"], small_v["pool_scale"] = pool_scale, m_pool_scale, v_pool_scale
    g_small["pool_scale"] = g_scale
    _, delta_s, new_m_s, new_v_s = _adamw("adamw_small", pack([small_w[nm] for nm in order]),
                                          pack([g_small[nm] for nm in order]), pack([small_m[nm] for nm in order]),
                                          pack([small_v[nm] for nm in order]))
    small = {}
    off = 0
    for nm in order:
        size, shape = small_w[nm].size, small_w[nm].shape
        cut = lambda arr: arr.reshape(-1)[off:off + size].reshape(shape)
        small[nm] = (g_small[nm].reshape(shape), cut(delta_s), cut(new_m_s), cut(new_v_s))
        off += size

    every = {**big, **small}
    weight_order = ["ln_in_g", "ln_in_b", "w_in", "w_pool", "pool_scale", "w_out", "ln1_g", "ln1_b", "w_ff1", "b_ff1",
                    "w_ff2", "b_ff2", "ln2_g", "ln2_b"]
    grads = [every[nm][0] for nm in weight_order]
    deltas = [every[nm][1] for nm in weight_order]
    new_ms = [every[nm][2] for nm in weight_order]
    new_vs = [every[nm][3] for nm in weight_order]
    return (loss, dx.reshape(x.shape), *grads, *deltas, *new_ms, *new_vs)
```

```python
import functools

import jax
import jax.numpy as jnp
from jax import lax
from jax.experimental import pallas as pl
from jax.experimental.pallas import tpu as pltpu

F32 = jnp.float32
BF16 = jnp.bfloat16
MESH = pl.DeviceIdType.MESH

HEAD_DIM = 128
POOL_WINDOWS = (2, 4, 8, 16)
POOL_HALO = 16
LN_EPS = 1e-5
ALPHA = 2.0 ** 0.25
ADAM_LR, ADAM_B1, ADAM_B2, ADAM_EPS, ADAM_WD, ADAM_STEP = 0.001, 0.9, 0.999, 1e-08, 0.01, 10

QB = 256
KB = 256
VMEM_LIMIT = 56 * 1024 * 1024
N_CHIPS = 4
N_DEV = 8


def _params(sem=None):
    return pltpu.CompilerParams(dimension_semantics=sem, vmem_limit_bytes=VMEM_LIMIT)


def _tile(dim, pref):
    return pref if dim % pref == 0 else dim


def _pos():
    return lax.axis_index("x"), lax.axis_index("y"), lax.axis_index("c")


def _other_chips(x, y):
    return [(1 - x, y), (x, 1 - y), (1 - x, 1 - y)]


ANY = pl.BlockSpec(memory_space=pl.ANY)


class _Phase:
    def __init__(self, ins, out_shapes, aliases, n_sems, build):
        self.ins, self.out_shapes, self.aliases, self.n_sems, self.build = ins, out_shapes, aliases, n_sems, build


def _remote(src, dst, send_sems, recv_sems, k, to):
    return pltpu.make_async_remote_copy(src_ref=src, dst_ref=dst, send_sem=send_sems.at[k], recv_sem=recv_sems.at[k],
                                        device_id=to, device_id_type=MESH)


def _swap_phase(g):
    def build(ins, outs, ss, rs):
        x, y, c = _pos()
        cp = _remote(ins[0].at[:, 1 - c], outs[0], ss, rs, 0, (x, y, 1 - c))
        return [cp], [cp]

    return _Phase([g], [jax.ShapeDtypeStruct((N_CHIPS, g.shape[2], g.shape[3]), g.dtype)], {}, 1, build)


ALL_OTHERS = (0, 1, 2)


def _scatter_phase(p, others=ALL_OTHERS):
    def build(ins, outs, ss, rs):
        x, y, c = _pos()
        chips = _other_chips(x, y)
        cps = [_remote(ins[0].at[2 * chips[j][0] + chips[j][1]], outs[0].at[k], ss, rs, k, (*chips[j], c))
               for k, j in enumerate(others)]
        return cps, cps

    return _Phase([p], [jax.ShapeDtypeStruct((len(others), p.shape[1], p.shape[2]), p.dtype)], {}, len(others), build)


def _assemble_phase(q):
    def build(ins, outs, ss, rs):
        x, y, c = _pos()
        mine, other = outs[0].at[c], outs[0].at[1 - c]
        return [_remote(mine, mine, ss, rs, 0, (x, y, 1 - c))], [_remote(other, other, ss, rs, 0, (x, y, c))]

    return _Phase([q], [jax.ShapeDtypeStruct(q.shape, q.dtype)], {0: 0}, 1, build)


def _gather_ici_phase(slot, others=ALL_OTHERS):
    def build(ins, outs, ss, rs):
        x, y, c = _pos()
        chips = _other_chips(x, y)
        mine = outs[0].at[2 * x + y, c]
        sends, recvs = [], []
        for k, j in enumerate(others):
            theirs = outs[0].at[2 * chips[j][0] + chips[j][1], c]
            sends.append(_remote(mine, mine, ss, rs, k, (*chips[j], c)))
            recvs.append(_remote(theirs, theirs, ss, rs, k, (x, y, c)))
        return sends, recvs

    return _Phase([slot], [jax.ShapeDtypeStruct(slot.shape, slot.dtype)], {0: 0}, len(others), build)


def _gather_d2d_phase(slot):
    def build(ins, outs, ss, rs):
        x, y, c = _pos()
        sends, recvs = [], []
        for j, chip in enumerate(_other_chips(x, y)):
            landed = outs[0].at[2 * chip[0] + chip[1], c]
            coming = outs[0].at[2 * chip[0] + chip[1], 1 - c]
            sends.append(_remote(landed, landed, ss, rs, j, (x, y, 1 - c)))
            recvs.append(_remote(coming, coming, ss, rs, j, (x, y, c)))
        return sends, recvs

    return _Phase([slot], [jax.ShapeDtypeStruct(slot.shape, slot.dtype)], {0: 0}, 3, build)


def _gather_whole_phase(slots):
    def build(ins, outs, ss, rs):
        x, y, c = _pos()
        mine = outs[0].at[2 * x + y]
        sends, recvs = [], []
        for j, chip in enumerate(_other_chips(x, y)):
            theirs = outs[0].at[2 * chip[0] + chip[1]]
            sends.append(_remote(mine, mine, ss, rs, j, (*chip, c)))
            recvs.append(_remote(theirs, theirs, ss, rs, j, (x, y, c)))
        return sends, recvs

    return _Phase([slots], [jax.ShapeDtypeStruct(slots.shape, slots.dtype)], {0: 0}, 3, build)


def _split_refs(refs, n_in, n_out, n_scratch, phases):
    n_pin = sum(len(ph.ins) for ph in phases)
    n_pout = sum(len(ph.out_shapes) for ph in phases)
    cuts = [n_in, n_pin, n_out, n_pout, n_scratch]
    parts, at = [], 0
    for n in cuts:
        parts.append(refs[at:at + n])
        at += n
    parts.append(refs[at:])
    return parts


def _build_phases(phases, pin, pout, sems):
    built, i, o = [], 0, 0
    for k, ph in enumerate(phases):
        built.append(ph.build(pin[i:i + len(ph.ins)], pout[o:o + len(ph.out_shapes)], sems[2 * k], sems[2 * k + 1]))
        i += len(ph.ins)
        o += len(ph.out_shapes)
    return built


def _finish_phases(built):
    for _, recvs in built:
        for cp in recvs:
            cp.wait_recv()
    for sends, _ in built:
        for cp in sends:
            cp.wait_send()


def _call(body, name, grid, in_specs, out_specs, out_shape, scratch_shapes, semantics, args, phases=(),
          in_place=None):
    n_in, n_out, n_scratch = len(args), len(out_shape), len(scratch_shapes)
    aliases, in_at, out_at = dict(in_place or {}), n_in, n_out
    for ph in phases:
        aliases.update({in_at + i: out_at + o for i, o in ph.aliases.items()})
        in_at += len(ph.ins)
        out_at += len(ph.out_shapes)

    def hosted(*refs):
        ins, pin, outs, pout, scratch, sems = _split_refs(refs, n_in, n_out, n_scratch, phases)
        ids = [pl.program_id(a) for a in range(len(grid))]
        first = functools.reduce(jnp.logical_and, [i == 0 for i in ids])
        last = functools.reduce(jnp.logical_and, [i == g - 1 for i, g in zip(ids, grid)])

        @pl.when(first)
        def _():
            for sends, _ in _build_phases(phases, pin, pout, sems):
                for cp in sends:
                    cp.start()

        body(*ins, *outs, *scratch)

        @pl.when(last)
        def _():
            _finish_phases(_build_phases(phases, pin, pout, sems))

    p_args = [a for ph in phases for a in ph.ins]
    p_shapes = [s for ph in phases for s in ph.out_shapes]
    sem_shapes = [pltpu.SemaphoreType.DMA((ph.n_sems,)) for ph in phases for _ in range(2)]
    outs = pl.pallas_call(
        hosted if phases else body, name=name, grid=grid, in_specs=[*in_specs, *[ANY] * len(p_args)],
        out_specs=[*out_specs, *[ANY] * len(p_shapes)], out_shape=[*out_shape, *p_shapes],
        input_output_aliases=aliases, scratch_shapes=[*scratch_shapes, *sem_shapes],
        compiler_params=_params(("arbitrary",) * len(grid) if phases else semantics),
    )(*args, *p_args)
    phase_outs, at = [], n_out
    for ph in phases:
        phase_outs.append(list(outs[at:at + len(ph.out_shapes)]))
        at += len(ph.out_shapes)
    return list(outs[:n_out]), phase_outs


def _comm_call(name, phases):
    def body(*refs):
        _, pin, _, pout, _, sems = _split_refs(refs, 0, 0, 0, phases)
        built = _build_phases(phases, pin, pout, sems)
        for sends, _ in built:
            for cp in sends:
                cp.start()
        _finish_phases(built)

    aliases, in_at, out_at = {}, 0, 0
    for ph in phases:
        aliases.update({in_at + i: out_at + o for i, o in ph.aliases.items()})
        in_at += len(ph.ins)
        out_at += len(ph.out_shapes)
    p_args = [a for ph in phases for a in ph.ins]
    p_shapes = [s for ph in phases for s in ph.out_shapes]
    outs = pl.pallas_call(
        body, name=name, in_specs=[ANY] * len(p_args), out_specs=[ANY] * len(p_shapes), out_shape=p_shapes,
        input_output_aliases=aliases,
        scratch_shapes=[pltpu.SemaphoreType.DMA((ph.n_sems,)) for ph in phases for _ in range(2)],
    )(*p_args)
    phase_outs, at = [], 0
    for ph in phases:
        phase_outs.append(list(outs[at:at + len(ph.out_shapes)]))
        at += len(ph.out_shapes)
    return phase_outs


def _matmul(name, a, b, grid, a_spec, b_spec, contract, acc_shape, extras, extra_specs, out_shape, out_specs,
            epilogue, phases=()):
    n_extra, n_out, gk = len(extras), len(out_shape), grid[2]

    def product(a_ref, b_ref):
        return lax.dot_general(a_ref[...], b_ref[...], (contract, ((), ())), preferred_element_type=F32)

    def body_one_step(*refs):
        epilogue(product(refs[0], refs[1]), refs[2:2 + n_extra], refs[2 + n_extra:])

    def body(*refs):
        a_ref, b_ref = refs[0], refs[1]
        extra_refs = refs[2:2 + n_extra]
        out_refs = refs[2 + n_extra:2 + n_extra + n_out]
        acc_ref = refs[-1]
        kk = pl.program_id(2)

        @pl.when(kk == 0)
        def _():
            acc_ref[...] = product(a_ref, b_ref)

        @pl.when(kk > 0)
        def _():
            acc_ref[...] += product(a_ref, b_ref)

        @pl.when(kk == gk - 1)
        def _():
            epilogue(acc_ref[...], extra_refs, out_refs)

    outs, phase_outs = _call(
        body_one_step if gk == 1 else body, name, grid, [a_spec, b_spec, *extra_specs], out_specs, out_shape,
        [] if gk == 1 else [pltpu.VMEM(acc_shape, F32)], ("parallel", "arbitrary", "arbitrary"), (a, b, *extras),
        phases)
    return (outs, phase_outs) if phases else outs


def _mm_nn(name, a, b, extras, extra_specs, out_shape, out_specs, epilogue, b_chips=False, tm=1024, tn=1024,
           tk=2048, phases=(), a_parts=None):
    m, k, tm, tk, a_spec = _lhs_rows_by_k(a, tm, tk, a_parts)
    n = b.shape[1] if not b_chips else b.shape[2] * N_CHIPS
    if b_chips:
        tn = _tile(b.shape[2], tn)
        nb = b.shape[2] // tn
        b_spec = pl.BlockSpec((None, tk, tn), lambda i, j, kk: (j // nb, kk, j % nb))
    else:
        tn = _tile(n, tn)
        b_spec = pl.BlockSpec((tk, tn), lambda i, j, kk: (kk, j))
    return _matmul(name, a, b, (m // tm, n // tn, k // tk), a_spec, b_spec, ((1,), (0,)), (tm, tn), extras,
                   extra_specs(tm, tn), out_shape, out_specs(tm, tn), epilogue, phases)


def _lhs_rows_by_k(a, tm, tk, a_parts, tk_max=None):
    if a_parts is None:
        m, k = a.shape
        tm, tk = _tile(m, tm), _tile(k if tk_max is None else tk_max, tk)
        return m, k, tm, tk, pl.BlockSpec((tm, tk), lambda i, j, kk: (i, kk))
    n_parts, m, kp = a.shape
    tm, tk = _tile(m, tm), _tile(kp if tk_max is None else min(kp, tk_max), tk)
    nb = kp // tk
    return m, n_parts * kp, tm, tk, pl.BlockSpec((None, tm, tk), lambda i, j, kk: (a_parts(kk // nb), i, kk % nb))


def _mm_nt(name, a, b, extras, extra_specs, out_shape, out_specs, epilogue, b_chips=False, tm=1024, tn=1024,
           tk=2048, phases=(), a_parts=None):
    m, k, tm, tk, a_spec = _lhs_rows_by_k(a, tm, tk, a_parts, tk_max=b.shape[2] if b_chips else None)
    n = b.shape[0] if not b_chips else b.shape[1]
    tn = _tile(n, tn)
    if b_chips:
        nb = b.shape[2] // tk
        b_spec = pl.BlockSpec((None, tn, tk), lambda i, j, kk: (kk // nb, j, kk % nb))
    else:
        b_spec = pl.BlockSpec((tn, tk), lambda i, j, kk: (j, kk))
    return _matmul(name, a, b, (m // tm, n // tn, k // tk), a_spec, b_spec, ((1,), (1,)), (tm, tn), extras,
                   extra_specs(tm, tn), out_shape, out_specs(tm, tn), epilogue, phases)


def _mm_tn(name, a, b, out_chips=False, tm=1024, tn=1024, tk=2048, phases=(), a_parts=None, b_parts=None):
    if a_parts is None:
        k, m = a.shape
        tm = _tile(m, tm)
        a_spec = pl.BlockSpec((_tile(k, tk), tm), lambda i, j, kk: (kk, i))
    else:
        n_parts, k, mp = a.shape
        m, tm = n_parts * mp, _tile(mp, tm)
        nbm = mp // tm
        a_spec = pl.BlockSpec((None, _tile(k, tk), tm), lambda i, j, kk: (a_parts(i // nbm), kk, i % nbm))
    tk = _tile(k, tk)
    n = b.shape[1] if b_parts is None else b.shape[0] * b.shape[2]
    if out_chips:
        nc = n // N_CHIPS
        tn = _tile(nc, tn)
        nb = nc // tn
        out_shape = [jax.ShapeDtypeStruct((N_CHIPS, m, nc), F32)]
        out_specs = [pl.BlockSpec((None, tm, tn), lambda i, j, kk: (j // nb, i, j % nb))]
    else:
        tn = _tile(n, tn)
        out_shape = [jax.ShapeDtypeStruct((m, n), F32)]
        out_specs = [pl.BlockSpec((tm, tn), lambda i, j, kk: (i, j))]
    if b_parts is None:
        b_spec = pl.BlockSpec((tk, tn), lambda i, j, kk: (kk, j))
    else:
        nbn = b.shape[2] // tn
        b_spec = pl.BlockSpec((None, tk, tn), lambda i, j, kk: (b_parts(j // nbn), kk, j % nbn))

    def epilogue(acc, extra_refs, out_refs):
        out_refs[0][...] = acc

    res = _matmul(name, a, b, (m // tm, n // tn, k // tk), a_spec, b_spec, ((0,), (0,)), (tm, tn), (), [],
                  out_shape, out_specs, epilogue, phases)
    return (res[0][0], res[1]) if phases else res[0]


def _tile_spec(tm, tn):
    return pl.BlockSpec((tm, tn), lambda i, j, kk: (i, j))


def _row_spec(tn):
    return pl.BlockSpec((1, tn), lambda i, j, kk: (0, j))


def _ln_stats(r):
    mu = jnp.mean(r, axis=-1, keepdims=True)
    var = jnp.mean(jnp.square(r - mu), axis=-1, keepdims=True)
    rstd = lax.rsqrt(var + LN_EPS)
    return (r - mu) * rstd, rstd


def _ln_fwd(name, r, g, b, tr=256, phases=()):
    t, d = r.shape
    tr = _tile(t, tr)

    def body(r_ref, g_ref, b_ref, y_ref, yb_ref, xhat_ref, rstd_ref):
        xhat, rstd = _ln_stats(r_ref[...])
        y = xhat * g_ref[...] + b_ref[...]
        y_ref[...] = y
        yb_ref[...] = y.astype(BF16)
        xhat_ref[...] = xhat
        rstd_ref[...] = rstd

    row = pl.BlockSpec((tr, d), lambda i: (i, 0))
    vec = pl.BlockSpec((1, d), lambda i: (0, 0))
    outs, phase_outs = _call(
        body, name, (t // tr,), [row, vec, vec], [row, row, row, pl.BlockSpec((tr, 1), lambda i: (i, 0))],
        [jax.ShapeDtypeStruct((t, d), F32), jax.ShapeDtypeStruct((t, d), BF16),
         jax.ShapeDtypeStruct((t, d), F32), jax.ShapeDtypeStruct((t, 1), F32)], [], ("parallel",), (r, g, b), phases)
    return (outs, phase_outs) if phases else outs


def _ln_bwd_rows(dy, xhat, rstd, g):
    dxhat = dy * g
    m1 = jnp.mean(dxhat, axis=-1, keepdims=True)
    m2 = jnp.mean(dxhat * xhat, axis=-1, keepdims=True)
    return rstd * (dxhat - m1 - xhat * m2)


def _ln_bwd(name, dy, xhat, rstd, g, tr=256, phases=()):
    t, d = dy.shape
    tr = _tile(t, tr)

    def body(dy_ref, xhat_ref, rstd_ref, g_ref, dr_ref, drb_ref, dg_ref, db_ref):
        @pl.when(pl.program_id(0) == 0)
        def _():
            dg_ref[...] = jnp.zeros_like(dg_ref)
            db_ref[...] = jnp.zeros_like(db_ref)

        dy_t, xhat_t = dy_ref[...], xhat_ref[...]
        dr = _ln_bwd_rows(dy_t, xhat_t, rstd_ref[...], g_ref[...])
        dr_ref[...] = dr
        drb_ref[...] = dr.astype(BF16)
        dg_ref[...] += jnp.sum(dy_t * xhat_t, axis=0, keepdims=True)
        db_ref[...] += jnp.sum(dy_t, axis=0, keepdims=True)

    row = pl.BlockSpec((tr, d), lambda i: (i, 0))
    vec = pl.BlockSpec((1, d), lambda i: (0, 0))
    outs, phase_outs = _call(
        body, name, (t // tr,), [row, row, pl.BlockSpec((tr, 1), lambda i: (i, 0)), vec], [row, row, vec, vec],
        [jax.ShapeDtypeStruct((t, d), F32), jax.ShapeDtypeStruct((t, d), BF16),
         jax.ShapeDtypeStruct((1, d), F32), jax.ShapeDtypeStruct((1, d), F32)], [], ("arbitrary",),
        (dy, xhat, rstd, g), phases)
    return (outs, phase_outs) if phases else outs


def _ln2_loss_bwd(r2, target, g, b, tr=256):
    t, d = r2.shape
    tr = _tile(t, tr)

    def body(r_ref, t_ref, g_ref, b_ref, dr_ref, drb_ref, loss_ref, dg_ref, db_ref, dsum_ref):
        @pl.when(pl.program_id(0) == 0)
        def _():
            loss_ref[...] = jnp.zeros_like(loss_ref)
            dg_ref[...] = jnp.zeros_like(dg_ref)
            db_ref[...] = jnp.zeros_like(db_ref)
            dsum_ref[...] = jnp.zeros_like(dsum_ref)

        xhat, rstd = _ln_stats(r_ref[...])
        g_t = g_ref[...]
        err = xhat * g_t + b_ref[...] - t_ref[...]
        loss_ref[...] += 0.5 * jnp.sum(jnp.mean(jnp.square(err), axis=-1, keepdims=True), axis=0, keepdims=True)
        dy = err * (1.0 / d)
        dr = _ln_bwd_rows(dy, xhat, rstd, g_t)
        dr_ref[...] = dr
        drb_ref[...] = dr.astype(BF16)
        dg_ref[...] += jnp.sum(dy * xhat, axis=0, keepdims=True)
        db_ref[...] += jnp.sum(dy, axis=0, keepdims=True)
        dsum_ref[...] += jnp.sum(dr, axis=0, keepdims=True)

    row = pl.BlockSpec((tr, d), lambda i: (i, 0))
    vec = pl.BlockSpec((1, d), lambda i: (0, 0))
    return pl.pallas_call(
        body, name="ln2_loss_bwd", grid=(t // tr,), in_specs=[row, row, vec, vec],
        out_specs=[row, row, pl.BlockSpec((8, 128), lambda i: (0, 0)), vec, vec, vec],
        out_shape=[jax.ShapeDtypeStruct((t, d), F32), jax.ShapeDtypeStruct((t, d), BF16),
                   jax.ShapeDtypeStruct((8, 128), F32), jax.ShapeDtypeStruct((1, d), F32),
                   jax.ShapeDtypeStruct((1, d), F32), jax.ShapeDtypeStruct((1, d), F32)],
        compiler_params=_params(("arbitrary",)),
    )(r2, target, g, b)


POOL_ROWS = 512

DU_POOL = 3


def _du_part(block):
    return (block + DU_POOL) % 4


def _pool_mean_minus_token(u_ref, r0, rows, grp, first):
    width = u_ref.shape[1]
    body = u_ref[pl.ds(r0, rows), :]
    halo = u_ref[pl.ds(pl.multiple_of(jnp.maximum(r0 - POOL_HALO, 0), POOL_HALO), POOL_HALO), :]
    halo = jnp.where(first, 0.0, halo)
    full = jnp.concatenate([halo, body], axis=0)
    s = full
    for step in range(len(POOL_WINDOWS)):
        shifted = pltpu.roll(s, 1 << step, axis=0)
        s = s + jnp.where(grp >= step, shifted, 0.0)
    s = s[POOL_HALO:, :]
    tpos = r0 + lax.broadcasted_iota(jnp.int32, (rows, width), 0)
    count = jnp.minimum(tpos + 1, 2 << grp).astype(F32)
    return s / count - body, count


def _pool_fwd(u, w_pool, pool_scale, t, pw):
    gw = pw // len(POOL_WINDOWS)
    rows = _tile(t, POOL_ROWS)

    def body(u_ref, w_ref, s_ref, o_ref):
        grp = pl.program_id(0)

        def chunk(ci, carry):
            r0 = pl.multiple_of(ci * rows, rows)
            y, _ = _pool_mean_minus_token(u_ref, r0, rows, grp, ci == 0)
            yw = jnp.dot(y.astype(BF16), w_ref[...], preferred_element_type=F32)
            o_ref[pl.ds(r0, rows), :] = (yw * s_ref[...]).astype(BF16)
            return carry

        lax.fori_loop(0, t // rows, chunk, 0)

    return pl.pallas_call(
        body, name="pool_fwd", grid=(len(POOL_WINDOWS),),
        in_specs=[pl.BlockSpec((t, gw), lambda g: (0, g)), pl.BlockSpec((None, gw, gw), lambda g: (g, 0, 0)),
                  pl.BlockSpec((None, 1, gw), lambda g: (g, 0, 0))],
        out_specs=pl.BlockSpec((None, t, gw), lambda g: (0, 0, g)),
        out_shape=jax.ShapeDtypeStruct((2, t, pw), BF16),
        compiler_params=_params(("parallel",)),
    )(u, w_pool, pool_scale)


def _pool_bwd(u, dmix, w_pool, pool_scale, t, pw):
    n_grp = len(POOL_WINDOWS)
    gw = pw // n_grp
    rows = _tile(t, POOL_ROWS)

    def body(u_ref, dm_ref, w_ref, s_ref, du_ref, dw_ref, ds_ref, e_ref):
        grp = pl.program_id(0)
        dw_ref[...] = jnp.zeros_like(dw_ref)
        ds_ref[...] = jnp.zeros_like(ds_ref)
        e_ref[pl.ds(t, POOL_HALO), :] = jnp.zeros((POOL_HALO, gw), F32)

        def chunk(ci, carry):
            r0 = pl.multiple_of(ci * rows, rows)
            y, count = _pool_mean_minus_token(u_ref, r0, rows, grp, ci == 0)
            yb = y.astype(BF16)
            yw = jnp.dot(yb, w_ref[...], preferred_element_type=F32)
            dy2 = dm_ref[pl.ds(r0, rows), :]
            ds_ref[...] += jnp.sum(dy2 * yw, axis=0, keepdims=True)
            dyw = (dy2 * s_ref[...]).astype(BF16)
            dw_ref[...] += lax.dot_general(yb, dyw, (((0,), (0,)), ((), ())), preferred_element_type=F32)
            dy = lax.dot_general(dyw, w_ref[...], (((1,), (1,)), ((), ())), preferred_element_type=F32)
            e_ref[pl.ds(r0, rows), :] = dy / count
            return carry

        lax.fori_loop(0, t // rows, chunk, 0)

        def chunk2(ci, carry):
            r0 = pl.multiple_of(ci * rows, rows)
            full = e_ref[pl.ds(r0, rows + POOL_HALO), :]
            s = full
            for step in range(n_grp):
                shifted = pltpu.roll(s, rows + POOL_HALO - (1 << step), axis=0)
                s = s + jnp.where(grp >= step, shifted, 0.0)
            e = full[:rows, :]
            tpos = r0 + lax.broadcasted_iota(jnp.int32, (rows, gw), 0)
            count = jnp.minimum(tpos + 1, 2 << grp).astype(F32)
            du_ref[pl.ds(r0, rows), :] = (s[:rows, :] - e * count).astype(BF16)
            return carry

        lax.fori_loop(0, t // rows, chunk2, 0)

    return pl.pallas_call(
        body, name="pool_bwd", grid=(n_grp,),
        in_specs=[pl.BlockSpec((t, gw), lambda g: (0, g)), pl.BlockSpec((t, gw), lambda g: (0, g)),
                  pl.BlockSpec((None, gw, gw), lambda g: (g, 0, 0)),
                  pl.BlockSpec((None, 1, gw), lambda g: (g, 0, 0))],
        out_specs=[pl.BlockSpec((None, t, gw), lambda g: (DU_POOL, 0, g)),
                   pl.BlockSpec((None, gw, gw), lambda g: (g, 0, 0)), pl.BlockSpec((None, 1, gw), lambda g: (g, 0, 0))],
        out_shape=[jax.ShapeDtypeStruct((4, t, pw), BF16), jax.ShapeDtypeStruct((n_grp, gw, gw), F32),
                   jax.ShapeDtypeStruct((n_grp, 1, gw), F32)],
        scratch_shapes=[pltpu.VMEM((t + POOL_HALO, gw), F32)],
        compiler_params=_params(("parallel",)),
    )(u, dmix, w_pool, pool_scale)


def _sb_scores(q, k_blk, scale, mask):
    z = lax.dot_general(q, k_blk, (((1,), (1,)), ((), ())), preferred_element_type=F32) * scale
    log_not = jnp.minimum(-z, 0.0) - jnp.log(1.0 + jnp.exp(-jnp.abs(z)))
    return z, (log_not if mask is None else jnp.where(mask, log_not, 0.0))


def _sb_weights(e, mask):
    a = jnp.exp(e)
    return a if mask is None else jnp.where(mask, a, 0.0)


EXP_IS_ZERO_BELOW = -104.0


def _weights_alive(after):
    return (jnp.max(after) >= EXP_IS_ZERO_BELOW).astype(jnp.int32)


def _split_dot(vs, tri):
    parts = []
    for v in vs:
        hi = v.astype(BF16)
        parts += [hi, (v - hi.astype(F32)).astype(BF16)]
    prod = jnp.dot(jnp.concatenate(parts, axis=0), tri, preferred_element_type=F32)
    m = vs[0].shape[0]
    return [prod[2 * k * m:(2 * k + 1) * m] + prod[(2 * k + 1) * m:(2 * k + 2) * m] for k in range(len(vs))]


def _head(ref, h, rows=None):
    cols = slice(h * HEAD_DIM, (h + 1) * HEAD_DIM)
    return ref[:, cols] if rows is None else ref[rows, cols]


def _attn_fwd(ub, mix, t, nh, hg, phases=()):
    scale = float(1.0 / (HEAD_DIM ** 0.5))
    ng = nh // hg

    def body(q_ref, k_ref, v_ref, o_ref):
        i = pl.program_id(1)
        row = lax.broadcasted_iota(jnp.int32, (QB, KB), 0)
        col = lax.broadcasted_iota(jnp.int32, (QB, KB), 1)
        suffix = (row >= col).astype(BF16)

        def more(carry):
            return jnp.logical_and(carry[0] <= i, carry[3] > 0)

        def block(n, accs, afters, mask):
            rows = pl.ds(pl.multiple_of((i - n) * KB, KB), KB)
            new_accs, new_afters = [], []
            scores = [_sb_scores(_head(q_ref, h), _head(k_ref, h, rows), scale, mask) for h in range(hg)]
            withins = _split_dot([log_not for _, log_not in scores], suffix)
            for h in range(hg):
                z, log_not = scores[h]
                a = _sb_weights(z + withins[h] + afters[h], mask)
                new_accs.append(accs[h] + jnp.dot(a.astype(BF16), _head(v_ref, h, rows),
                                                  preferred_element_type=F32))
                new_afters.append(afters[h] + jnp.sum(log_not, axis=1, keepdims=True))
            return n + 1, tuple(new_accs), tuple(new_afters), _weights_alive(functools.reduce(jnp.maximum, new_afters))

        first = block(jnp.int32(0), tuple(jnp.zeros((QB, HEAD_DIM), F32) for _ in range(hg)),
                      tuple(jnp.zeros((QB, 1), F32) for _ in range(hg)), col < row)
        _, accs, _, _ = lax.while_loop(more, lambda carry: block(carry[0], carry[1], carry[2], None), first)
        for h in range(hg):
            o_ref[:, h * HEAD_DIM:(h + 1) * HEAD_DIM] = accs[h].astype(BF16)

    wide = hg * HEAD_DIM
    outs, phase_outs = _call(
        lambda q_ref, k_ref, v_ref, mix_ref, o_ref: body(q_ref, k_ref, v_ref, o_ref), "attn_fwd", (ng, t // QB),
        [pl.BlockSpec((QB, wide), lambda g, i: (i, ng + g)), pl.BlockSpec((t, wide), lambda g, i: (0, 2 * ng + g)),
         pl.BlockSpec((t, wide), lambda g, i: (0, 3 * ng + g)), ANY],
        [pl.BlockSpec((None, QB, wide), lambda g, i: (1, i, g))], [jax.ShapeDtypeStruct(mix.shape, mix.dtype)], [],
        ("parallel", "arbitrary"), (ub, ub, ub, mix), phases, in_place={3: 0})
    return outs[0], phase_outs


def _attn_bwd(ub, dmix, du, t, nh, hg, phases=()):
    scale = float(1.0 / (HEAD_DIM ** 0.5))
    ng = nh // hg

    def body(q_ref, k_ref, v_ref, do_ref, du_in_ref, du_ref, g_ref, z_ref, dk_ref, dv_ref):
        i = pl.program_id(1)

        @pl.when(i == 0)
        def _():
            dk_ref[...] = jnp.zeros_like(dk_ref)
            dv_ref[...] = jnp.zeros_like(dv_ref)

        row = lax.broadcasted_iota(jnp.int32, (QB, KB), 0)
        col = lax.broadcasted_iota(jnp.int32, (QB, KB), 1)
        suffix = (row >= col).astype(BF16)
        prefix = (row <= col).astype(BF16)

        def more(carry):
            return jnp.logical_and(carry[0] <= i, carry[2] > 0)

        def down(n, afters, mask):
            ks = pl.multiple_of((i - n) * KB, KB)
            rows = pl.ds(ks, KB)
            new_afters = []
            scores = [_sb_scores(_head(q_ref, h), _head(k_ref, h, rows), scale, mask) for h in range(hg)]
            withins = _split_dot([log_not for _, log_not in scores], suffix)
            for h in range(hg):
                do = _head(do_ref, h).astype(BF16)
                z, log_not = scores[h]
                a = _sb_weights(z + withins[h] + afters[h], mask)
                da = lax.dot_general(do, _head(v_ref, h, rows), (((1,), (1,)), ((), ())),
                                     preferred_element_type=F32)
                g_ref[h, :, pl.ds(ks, KB)] = a * da
                z_ref[h, :, pl.ds(ks, KB)] = z
                dv_ref[rows, h * HEAD_DIM:(h + 1) * HEAD_DIM] += lax.dot_general(
                    a.astype(BF16), do, (((0,), (0,)), ((), ())), preferred_element_type=F32)
                new_afters.append(afters[h] + jnp.sum(log_not, axis=1, keepdims=True))
            return n + 1, tuple(new_afters), _weights_alive(functools.reduce(jnp.maximum, new_afters))

        diagonal = col < row
        first = down(jnp.int32(0), tuple(jnp.zeros((QB, 1), F32) for _ in range(hg)), diagonal)
        visited, _, _ = lax.while_loop(more, lambda carry: down(carry[0], carry[1], None), first)

        def up(kb, carry, mask):
            dqs, befores = carry
            ks = pl.multiple_of(kb * KB, KB)
            rows = pl.ds(ks, KB)
            new_dqs, new_befores = [], []
            gs = [g_ref[h, :, pl.ds(ks, KB)] for h in range(hg)]
            g_withins = _split_dot(gs, prefix)
            for h in range(hg):
                g = gs[h]
                z = z_ref[h, :, pl.ds(ks, KB)]
                g_upto = g_withins[h] + befores[h]
                dz = g - jax.nn.sigmoid(z) * g_upto
                dz = dz if mask is None else jnp.where(mask, dz, 0.0)
                dzs = (dz * scale).astype(BF16)
                new_dqs.append(dqs[h] + jnp.dot(dzs, _head(k_ref, h, rows), preferred_element_type=F32))
                dk_ref[rows, h * HEAD_DIM:(h + 1) * HEAD_DIM] += lax.dot_general(
                    dzs, _head(q_ref, h), (((0,), (0,)), ((), ())), preferred_element_type=F32)
                new_befores.append(befores[h] + jnp.sum(g, axis=1, keepdims=True))
            return tuple(new_dqs), tuple(new_befores)

        below = lax.fori_loop(i + 1 - visited, i, lambda kb, carry: up(kb, carry, None),
                              (tuple(jnp.zeros((QB, HEAD_DIM), F32) for _ in range(hg)),
                               tuple(jnp.zeros((QB, 1), F32) for _ in range(hg))))
        dqs, _ = up(i, below, diagonal)
        q_rows = pl.ds(pl.multiple_of(i * QB, QB), QB)
        for h in range(hg):
            du_ref[0, q_rows, h * HEAD_DIM:(h + 1) * HEAD_DIM] = dqs[h].astype(BF16)

        @pl.when(i == t // QB - 1)
        def _():
            du_ref[1] = dk_ref[...].astype(BF16)
            du_ref[2] = dv_ref[...].astype(BF16)

    wide = hg * HEAD_DIM
    tile = lambda off: pl.BlockSpec((QB, wide), lambda g, i: (i, off + g))
    strip = lambda off: pl.BlockSpec((t, wide), lambda g, i: (0, off + g))
    outs, phase_outs = _call(
        body, "attn_bwd", (ng, t // QB), [tile(ng), strip(2 * ng), strip(3 * ng), tile(ng), ANY],
        [pl.BlockSpec((3, t, wide), lambda g, i: (0, 0, g))], [jax.ShapeDtypeStruct(du.shape, du.dtype)],
        [pltpu.VMEM((hg, QB, t), F32), pltpu.VMEM((hg, QB, t), F32), pltpu.VMEM((t, wide), F32),
         pltpu.VMEM((t, wide), F32)], ("parallel", "arbitrary"), (ub, ub, ub, dmix, du), phases, in_place={4: 0})
    return outs[0], phase_outs


def _row_tile(rows, cols, pref_bytes=2 * 1024 * 1024):
    tr = max(8, pref_bytes // (4 * cols))
    while rows % tr:
        tr //= 2
    return max(tr, 1)


def _pair_sum(name, g, s, c_idx):
    _, _, r2, cols = g.shape
    tr = _row_tile(r2, cols)

    def body(c_ref, g_ref, s_ref, o_ref):
        o_ref[...] = (g_ref[...] + s_ref[...]).astype(BF16)

    return pl.pallas_call(
        body, name=name,
        grid_spec=pltpu.PrefetchScalarGridSpec(
            num_scalar_prefetch=1, grid=(N_CHIPS, r2 // tr),
            in_specs=[pl.BlockSpec((None, None, tr, cols), lambda p, i, c: (p, c[0], i, 0)),
                      pl.BlockSpec((None, tr, cols), lambda p, i, c: (p, i, 0))],
            out_specs=pl.BlockSpec((None, tr, cols), lambda p, i, c: (p, i, 0))),
        out_shape=jax.ShapeDtypeStruct((N_CHIPS, r2, cols), BF16),
        compiler_params=_params(("parallel", "parallel")),
    )(c_idx, g, s)


def _chip_sum(name, g, s, received, place):
    _, _, r2, cols = g.shape
    tr = _row_tile(r2, cols)
    counts = [r.shape[0] for r in received]

    def body(place_ref, g_ref, s_ref, *refs):
        total = g_ref[...] + s_ref[...]
        for r_ref, n in zip(refs[:-1], counts):
            for k in range(n):
                total = total + r_ref[k].astype(F32)
        refs[-1][...] = total

    return pl.pallas_call(
        body, name=name,
        grid_spec=pltpu.PrefetchScalarGridSpec(
            num_scalar_prefetch=1, grid=(r2 // tr,),
            in_specs=[pl.BlockSpec((None, None, tr, cols), lambda i, p: (p[0], p[1], i, 0)),
                      pl.BlockSpec((None, tr, cols), lambda i, p: (p[0], i, 0)),
                      *[pl.BlockSpec((n, tr, cols), lambda i, p: (0, i, 0)) for n in counts]],
            out_specs=pl.BlockSpec((None, tr, cols), lambda i, p: (p[1], i, 0))),
        out_shape=jax.ShapeDtypeStruct((2, r2, cols), F32),
        compiler_params=_params(("parallel",)),
    )(place, g, s, *received)


def _cast_into_slot(name, w, place):
    rows, cols = w.shape
    r2 = rows // 2
    tr = _row_tile(r2, cols)
    nb = r2 // tr

    def body(place_ref, w_ref, o_ref):
        o_ref[...] = w_ref[...].astype(BF16)

    return pl.pallas_call(
        body, name=name,
        grid_spec=pltpu.PrefetchScalarGridSpec(
            num_scalar_prefetch=1, grid=(2, nb),
            in_specs=[pl.BlockSpec((tr, cols), lambda h, i, s: (h * nb + i, 0))],
            out_specs=pl.BlockSpec((None, None, tr, cols), lambda h, i, s: (s[0], h, i, 0))),
        out_shape=jax.ShapeDtypeStruct((N_CHIPS, 2, r2, cols), BF16),
        compiler_params=_params(("parallel", "parallel")),
    )(place, w)


def _colsum(name, a):
    def body(a_ref, o_ref):
        o_ref[...] = jnp.sum(a_ref[...], axis=0, keepdims=True)

    whole = lambda shape: pl.BlockSpec(shape, lambda i: (0, 0))
    return pl.pallas_call(
        body, name=name, grid=(1,), in_specs=[whole(a.shape)], out_specs=whole((1, a.shape[1])),
        out_shape=jax.ShapeDtypeStruct((1, a.shape[1]), F32), compiler_params=_params(("arbitrary",)),
    )(a)


def _adamw(name, w, g, m, v):
    rows, cols = w.shape
    tr = _row_tile(rows, cols, 1024 * 1024)

    def body(w_ref, g_ref, m_ref, v_ref, g_out_ref, d_ref, nm_ref, nv_ref):
        g_t = g_ref[...]
        m_t = ADAM_B1 * m_ref[...] + (1.0 - ADAM_B1) * g_t
        v_t = ADAM_B2 * v_ref[...] + (1.0 - ADAM_B2) * jnp.square(g_t)
        m_hat = m_t / (1.0 - ADAM_B1 ** ADAM_STEP)
        v_hat = v_t / (1.0 - ADAM_B2 ** ADAM_STEP)
        g_out_ref[...] = g_t
        d_ref[...] = -ADAM_LR * (m_hat / (jnp.sqrt(v_hat) + ADAM_EPS) + ADAM_WD * w_ref[...])
        nm_ref[...] = m_t
        nv_ref[...] = v_t

    spec = pl.BlockSpec((tr, cols), lambda i: (i, 0))
    shape = jax.ShapeDtypeStruct((rows, cols), F32)
    return pl.pallas_call(
        body, name=name, grid=(rows // tr,), in_specs=[spec] * 4, out_specs=[spec] * 4, out_shape=[shape] * 4,
        compiler_params=_params(("parallel",)),
    )(w, g, m, v)


def _all_reduce_small(packed, phases=()):
    rows, cols = packed.shape

    def body(in_ref, out_ref, all_ref, send_sems, recv_sems):
        x, y, c = _pos()
        me = 4 * x + 2 * y + c
        all_ref[me] = in_ref[...]
        cps = []
        for r in range(1, N_DEV):
            bx, by, bc = (r >> 2) & 1, (r >> 1) & 1, r & 1
            peer = (1 - x if bx else x, 1 - y if by else y, 1 - c if bc else c)
            cp = pltpu.make_async_remote_copy(
                src_ref=in_ref, dst_ref=all_ref.at[me], send_sem=send_sems.at[r - 1], recv_sem=recv_sems.at[r - 1],
                device_id=peer, device_id_type=MESH)
            cp.start()
            cps.append(cp)
        for cp in cps:
            cp.wait()
        total = all_ref[0]
        for d in range(1, N_DEV):
            total = total + all_ref[d]
        out_ref[...] = total

    vmem = pl.BlockSpec(memory_space=pltpu.VMEM)
    outs, phase_outs = _call(
        body, "all_reduce_small", (1,), [vmem], [vmem], [jax.ShapeDtypeStruct((rows, cols), F32)],
        [pltpu.VMEM((N_DEV, rows, cols), F32), pltpu.SemaphoreType.DMA((N_DEV - 1,)),
         pltpu.SemaphoreType.DMA((N_DEV - 1,))], ("arbitrary",), (packed,), phases)
    return outs[0], phase_outs


def kernel(x, ln_in_g, ln_in_b, w_in, w_pool, pool_scale, w_out, ln1_g, ln1_b, w_ff1, b_ff1, w_ff2, b_ff2, ln2_g, ln2_b, loss_target, m_ln_in_g, m_ln_in_b, m_w_in, m_w_pool, m_pool_scale, m_w_out, m_ln1_g, m_ln1_b, m_w_ff1, m_b_ff1, m_w_ff2, m_b_ff2, m_ln2_g, m_ln2_b, v_ln_in_g, v_ln_in_b, v_w_in, v_w_pool, v_pool_scale, v_w_out, v_ln1_g, v_ln1_b, v_w_ff1, v_b_ff1, v_w_ff2, v_b_ff2, v_ln2_g, v_ln2_b):
    t, d = x.shape[1], x.shape[2]
    pw = d // 2
    n_grp = len(POOL_WINDOWS)
    gw = pw // n_grp
    gwc = gw // N_CHIPS
    nh = pw // HEAD_DIM
    ff = w_ff1.shape[2] * N_CHIPS
    assert w_in.shape[0] == 1 and w_in.shape[2] * N_CHIPS == 2 * d and gwc <= 128

    x_idx, y_idx, c_idx = _pos()
    chip_arr = jnp.reshape(2 * x_idx + y_idx, (1,)).astype(jnp.int32)
    c_arr = jnp.reshape(c_idx, (1,)).astype(jnp.int32)
    place = jnp.concatenate([chip_arr, c_arr])

    xs = x.reshape(t, d)
    target = loss_target.reshape(t, d)
    row = lambda vec: vec.reshape(1, -1)

    scale_tile = jnp.zeros((1, 8, 128), F32).at[0, :n_grp, :gwc].set(pool_scale[0])
    scale_slots = lax.dynamic_update_slice(jnp.zeros((N_CHIPS, 8, 128), F32), scale_tile, (chip_arr[0], 0, 0))
    shards = dict(w_in=w_in[0], w_out=w_out[0], w_ff1=w_ff1[0], w_ff2=w_ff2[0], w_pool=w_pool[0].reshape(gw, gw))
    slot = {nm: _cast_into_slot("cast_" + nm, w, place) for nm, w in shards.items()}
    unsplit = lambda s: s.reshape(N_CHIPS, 2 * s.shape[2], s.shape[3])

    (h0, h0b, xhat0, rstd0), ((win_s,), (wpool_s,), (scale_g,)) = _ln_fwd(
        "ln_in_fwd", xs, row(ln_in_g), row(ln_in_b),
        phases=[_gather_ici_phase(slot["w_in"]), _gather_ici_phase(slot["w_pool"]), _gather_whole_phase(scale_slots)])
    (win_s,), (wpool_s,) = _comm_call("gather_d2d_first", [_gather_d2d_phase(win_s), _gather_d2d_phase(wpool_s)])
    win_g = unsplit(win_s)
    wpool_full = unsplit(wpool_s).reshape(N_CHIPS, n_grp, gwc, gw).transpose(1, 0, 2, 3).reshape(n_grp, gw, gw)
    scale_full = scale_g[:, :n_grp, :gwc].transpose(1, 0, 2).reshape(n_grp, 1, gw)

    def store_f32(acc, extra_refs, out_refs):
        out_refs[0][...] = acc

    def pool_f32_all_bf16(acc, extra_refs, out_refs):
        @pl.when(pl.program_id(1) == 0)
        def _():
            out_refs[0][...] = acc

        out_refs[1][...] = acc.astype(BF16)

    assert w_in.shape[2] == pw
    (u, ub), ((wout_s,),) = _mm_nn(
        "in_proj", h0b, win_g, (), lambda tm, tn: [],
        [jax.ShapeDtypeStruct((t, pw), F32), jax.ShapeDtypeStruct((t, 2 * d), BF16)],
        lambda tm, tn: [pl.BlockSpec((tm, tn), lambda i, j, kk: (i, 0)), _tile_spec(tm, tn)],
        pool_f32_all_bf16, b_chips=True, tn=pw, phases=[_gather_ici_phase(slot["w_out"])])
    mix_in = _pool_fwd(u, wpool_full, scale_full, t, pw)
    mix_in, ((wff1_s,), (wout_s,)) = _attn_fwd(
        ub, mix_in, t, nh, min(nh, 4), phases=[_gather_ici_phase(slot["w_ff1"]), _gather_d2d_phase(wout_s)])
    same_part = lambda block: block
    wout_full = unsplit(wout_s).reshape(d, d)

    def residual(acc, extra_refs, out_refs):
        out_refs[0][...] = ALPHA * extra_refs[0][...] + acc

    (r1,), ((wff1_s,), (wff2_s,)) = _mm_nn(
        "out_proj", mix_in, wout_full, (h0,), lambda tm, tn: [_tile_spec(tm, tn)],
        [jax.ShapeDtypeStruct((t, d), F32)], lambda tm, tn: [_tile_spec(tm, tn)], residual, a_parts=same_part,
        phases=[_gather_d2d_phase(wff1_s), _gather_ici_phase(slot["w_ff2"], others=(2,))])
    wff1_g = unsplit(wff1_s)
    h1, h1b, xhat1, rstd1 = _ln_fwd("ln1_fwd", r1, ln1_g, ln1_b)

    def relu_sq(acc, extra_refs, out_refs):
        p = jnp.maximum(acc + extra_refs[0][...], 0.0)
        out_refs[0][...] = p
        out_refs[1][...] = jnp.square(p).astype(BF16)

    (relu_z, act_b), ((wff2_s,),) = _mm_nn(
        "ff1", h1b, wff1_g, (b_ff1,), lambda tm, tn: [_row_spec(tn)],
        [jax.ShapeDtypeStruct((t, ff), F32), jax.ShapeDtypeStruct((t, ff), BF16)],
        lambda tm, tn: [_tile_spec(tm, tn)] * 2, relu_sq, b_chips=True,
        phases=[_gather_ici_phase(wff2_s, others=(0, 1))])
    ((wff2_s,),) = _comm_call("gather_d2d_last", [_gather_d2d_phase(wff2_s)])
    wff2_full = unsplit(wff2_s).reshape(ff, d)

    def residual_bias(acc, extra_refs, out_refs):
        out_refs[0][...] = ALPHA * extra_refs[0][...] + (acc + extra_refs[1][...])

    r2 = _mm_nn("ff2", act_b, wff2_full, (h1, b_ff2), lambda tm, tn: [_tile_spec(tm, tn), _row_spec(tn)],
                [jax.ShapeDtypeStruct((t, d), F32)], lambda tm, tn: [_tile_spec(tm, tn)], residual_bias)[0]

    dr2, dr2b, loss_tile, g_ln2_g, g_ln2_b, g_b_ff2 = _ln2_loss_bwd(r2, target, ln2_g, ln2_b)
    loss = lax.psum(loss_tile[0, 0], ("x", "y", "c"))

    halves = lambda g: g.reshape(N_CHIPS, 2, g.shape[1] // 2, g.shape[2])
    g_ff2 = halves(_mm_tn("grad_w_ff2", act_b, dr2b).reshape(N_CHIPS, ff // N_CHIPS, d))

    def relu_sq_bwd(acc, extra_refs, out_refs):
        dz = acc * (2.0 * extra_refs[0][...])
        out_refs[0][...] = dz.astype(BF16)
        rows = lax.broadcasted_iota(jnp.int32, out_refs[1].shape, 0)
        out_refs[1][...] = jnp.where(rows == 0, jnp.sum(dz, axis=0, keepdims=True), 0.0)

    tm_ff = _tile(t, 1024)
    (dz1b, g_b_ff1_parts), ((s_ff2,),) = _mm_nt(
        "ff2_bwd", dr2b, wff2_full, (relu_z,), lambda tm, tn: [_tile_spec(tm, tn)],
        [jax.ShapeDtypeStruct((t, ff), BF16), jax.ShapeDtypeStruct((8 * (t // tm_ff), ff), F32)],
        lambda tm, tn: [_tile_spec(tm, tn), pl.BlockSpec((8, tn), lambda i, j, kk: (i, j))], relu_sq_bwd,
        phases=[_swap_phase(g_ff2)])
    p_ff2 = _pair_sum("pair_sum_w_ff2", g_ff2, s_ff2, c_arr)
    g_ff1, ((r_ff2_a,),) = _mm_tn("grad_w_ff1", h1b, dz1b, out_chips=True,
                                  phases=[_scatter_phase(p_ff2, others=(0, 1))])
    g_ff1 = halves(g_ff1)

    def plus_alpha(acc, extra_refs, out_refs):
        out_refs[0][...] = ALPHA * extra_refs[0][...] + acc

    (dh1,), ((s_ff1,), (r_ff2_b,)) = _mm_nt(
        "ff1_bwd", dz1b, wff1_g, (dr2,), lambda tm, tn: [_tile_spec(tm, tn)],
        [jax.ShapeDtypeStruct((t, d), F32)], lambda tm, tn: [_tile_spec(tm, tn)], plus_alpha, b_chips=True,
        phases=[_swap_phase(g_ff1), _scatter_phase(p_ff2, others=(2,))])
    q_ff2 = _chip_sum("chip_sum_w_ff2", g_ff2, s_ff2, [r_ff2_a, r_ff2_b], place)
    p_ff1 = _pair_sum("pair_sum_w_ff1", g_ff1, s_ff1, c_arr)
    (dr1, dr1b, g_ln1_g, g_ln1_b), ((q_ff2,),) = _ln_bwd("ln1_bwd", dh1, xhat1, rstd1, ln1_g,
                                                         phases=[_assemble_phase(q_ff2)])

    g_out = halves(_mm_tn("grad_w_out", mix_in, dr1b, a_parts=same_part).reshape(N_CHIPS, d // N_CHIPS, d))
    (dmix,), ((s_out,),) = _mm_nt(
        "out_proj_bwd", dr1b, wout_full, (), lambda tm, tn: [], [jax.ShapeDtypeStruct((t, d), F32)],
        lambda tm, tn: [_tile_spec(tm, tn)], store_f32, phases=[_swap_phase(g_out)])
    p_out = _pair_sum("pair_sum_w_out", g_out, s_out, c_arr)
    du, g_w_pool_full, g_scale_full = _pool_bwd(u, dmix, wpool_full, scale_full, t, pw)
    du, ((r_ff1,), (r_out,)) = _attn_bwd(ub, dmix, du, t, nh, min(nh, 2),
                                         phases=[_scatter_phase(p_ff1), _scatter_phase(p_out)])
    q_ff1 = _chip_sum("chip_sum_w_ff1", g_ff1, s_ff1, [r_ff1], place)
    q_out = _chip_sum("chip_sum_w_out", g_out, s_out, [r_out], place)
    g_in, ((q_ff1,), (q_out,)) = _mm_tn("grad_w_in", h0b, du, out_chips=True, b_parts=_du_part,
                                        phases=[_assemble_phase(q_ff1), _assemble_phase(q_out)])
    g_in = halves(g_in)
    g_pool = halves(g_w_pool_full.reshape(n_grp, N_CHIPS, gwc, gw).transpose(1, 0, 2, 3).reshape(N_CHIPS, gw, gw))
    (s_in,), (s_pool,) = _comm_call("rs_swap_last", [_swap_phase(g_in), _swap_phase(g_pool)])
    p_in = _pair_sum("pair_sum_w_in", g_in, s_in, c_arr)
    p_pool = _pair_sum("pair_sum_w_pool", g_pool, s_pool, c_arr)
    (dh0,), ((r_in,), (r_pool,)) = _mm_nt(
        "in_proj_bwd", du, win_g, (dr1,), lambda tm, tn: [_tile_spec(tm, tn)],
        [jax.ShapeDtypeStruct((t, d), F32)], lambda tm, tn: [_tile_spec(tm, tn)], plus_alpha, b_chips=True,
        a_parts=_du_part, phases=[_scatter_phase(p_in), _scatter_phase(p_pool)])
    q_in = _chip_sum("chip_sum_w_in", g_in, s_in, [r_in], place)
    q_pool = _chip_sum("chip_sum_w_pool", g_pool, s_pool, [r_pool], place)
    dx, _, g_ln_in_g, g_ln_in_b = _ln_bwd("ln_in_bwd", dh0, xhat0, rstd0, row(ln_in_g))

    lane = 2048 if d % 2048 == 0 else d
    small_names = ["ln_in_g", "ln_in_b", "ln1_g", "ln1_b", "b_ff1", "b_ff2", "ln2_g", "ln2_b"]
    small_w = dict(ln_in_g=ln_in_g, ln_in_b=ln_in_b, ln1_g=ln1_g, ln1_b=ln1_b, b_ff1=b_ff1, b_ff2=b_ff2, ln2_g=ln2_g,
                   ln2_b=ln2_b)
    small_m = dict(ln_in_g=m_ln_in_g, ln_in_b=m_ln_in_b, ln1_g=m_ln1_g, ln1_b=m_ln1_b, b_ff1=m_b_ff1, b_ff2=m_b_ff2,
                   ln2_g=m_ln2_g, ln2_b=m_ln2_b)
    small_v = dict(ln_in_g=v_ln_in_g, ln_in_b=v_ln_in_b, ln1_g=v_ln1_g, ln1_b=v_ln1_b, b_ff1=v_b_ff1, b_ff2=v_b_ff2,
                   ln2_g=v_ln2_g, ln2_b=v_ln2_b)
    small_g = dict(ln_in_g=g_ln_in_g, ln_in_b=g_ln_in_b, ln1_g=g_ln1_g, ln1_b=g_ln1_b, b_ff2=g_b_ff2, ln2_g=g_ln2_g,
                   ln2_b=g_ln2_b)

    def pack(parts):
        flat = jnp.concatenate([p.reshape(-1) for p in parts])
        n_rows = -(-flat.shape[0] // lane)
        n_rows = -(-n_rows // 8) * 8
        return jnp.pad(flat, (0, n_rows * lane - flat.shape[0])).reshape(n_rows, lane)

    small_g["b_ff1"] = _colsum("b_ff1_colsum", g_b_ff1_parts)
    summed, ((q_in,), (q_pool,)) = _all_reduce_small(
        pack([small_g[nm] for nm in small_names] + [g_scale_full]),
        phases=[_assemble_phase(q_in), _assemble_phase(q_pool)])
    summed = summed.reshape(-1)

    big = {}
    for nm, q, w, m, v in [("w_in", q_in, w_in, m_w_in, v_w_in), ("w_out", q_out, w_out, m_w_out, v_w_out),
                           ("w_ff1", q_ff1, w_ff1, m_w_ff1, v_w_ff1), ("w_ff2", q_ff2, w_ff2, m_w_ff2, v_w_ff2),
                           ("w_pool", q_pool, w_pool, m_w_pool, v_w_pool)]:
        g = q.reshape(2 * q.shape[1], q.shape[2])
        flat = lambda arr: arr.reshape(g.shape)
        big[nm] = tuple(arr.reshape(w.shape) for arr in _adamw("adamw_" + nm, flat(w), g, flat(m), flat(v)))

    g_small, off = {}, 0
    for nm in small_names:
        g_small[nm] = summed[off:off + small_w[nm].size]
        off += small_w[nm].size
    g_scale_all = summed[off:off + n_grp * gw].reshape(n_grp, N_CHIPS, gwc)
    g_scale = lax.dynamic_index_in_dim(g_scale_all, chip_arr[0], axis=1, keepdims=False)

    order = small_names + ["pool_scale"]
    small_w["pool_scale"], small_m["pool_scale"], small_v["pool_scale"] = pool_scale, m_pool_scale, v_pool_scale
    g_small["pool_scale"] = g_scale
    _, delta_s, new_m_s, new_v_s = _adamw("adamw_small", pack([small_w[nm] for nm in order]),
                                          pack([g_small[nm] for nm in order]), pack([small_m[nm] for nm in order]),
                                          pack([small_v[nm] for nm in order]))
    small = {}
    off = 0
    for nm in order:
        size, shape = small_w[nm].size, small_w[nm].shape
        cut = lambda arr: arr.reshape(-1)[off:off + size].reshape(shape)
        small[nm] = (g_small[nm].reshape(shape), cut(delta_s), cut(new_m_s), cut(new_v_s))
        off += size

    every = {**big, **small}
    weight_order = ["ln_in_g", "ln_in_b", "w_in", "w_pool", "pool_scale", "w_out", "ln1_g", "ln1_b", "w_ff1", "b_ff1",
                    "w_ff2", "b_ff2", "ln2_g", "ln2_b"]
    grads = [every[nm][0] for nm in weight_order]
    deltas = [every[nm][1] for nm in weight_order]
    new_ms = [every[nm][2] for nm in weight_order]
    new_vs = [every[nm][3] for nm in weight_order]
    return (loss, dx.reshape(x.shape), *grads, *deltas, *new_ms, *new_vs)
```

```python
import functools

import jax
import jax.numpy as jnp
from jax import lax
from jax.experimental import pallas as pl
from jax.experimental.pallas import tpu as pltpu

F32 = jnp.float32
BF16 = jnp.bfloat16
MESH = pl.DeviceIdType.MESH

HEAD_DIM = 128
POOL_WINDOWS = (2, 4, 8, 16)
POOL_HALO = 16
LN_EPS = 1e-5
ALPHA = 2.0 ** 0.25
ADAM_LR, ADAM_B1, ADAM_B2, ADAM_EPS, ADAM_WD, ADAM_STEP = 0.001, 0.9, 0.999, 1e-08, 0.01, 10

QB = 256
KB = 256
VMEM_LIMIT = 56 * 1024 * 1024
N_CHIPS = 4
N_DEV = 8


def _params(sem=None):
    return pltpu.CompilerParams(dimension_semantics=sem, vmem_limit_bytes=VMEM_LIMIT)


def _tile(dim, pref):
    return pref if dim % pref == 0 else dim


def _pos():
    return lax.axis_index("x"), lax.axis_index("y"), lax.axis_index("c")


def _other_chips(x, y):
    return [(1 - x, y), (x, 1 - y), (1 - x, 1 - y)]


ANY = pl.BlockSpec(memory_space=pl.ANY)


class _Phase:
    def __init__(self, ins, out_shapes, aliases, n_sems, build):
        self.ins, self.out_shapes, self.aliases, self.n_sems, self.build = ins, out_shapes, aliases, n_sems, build


def _remote(src, dst, send_sems, recv_sems, k, to):
    return pltpu.make_async_remote_copy(src_ref=src, dst_ref=dst, send_sem=send_sems.at[k], recv_sem=recv_sems.at[k],
                                        device_id=to, device_id_type=MESH)


def _swap_phase(g):
    def build(ins, outs, ss, rs):
        x, y, c = _pos()
        cp = _remote(ins[0].at[:, 1 - c], outs[0], ss, rs, 0, (x, y, 1 - c))
        return [cp], [cp]

    return _Phase([g], [jax.ShapeDtypeStruct((N_CHIPS, g.shape[2], g.shape[3]), g.dtype)], {}, 1, build)


ALL_OTHERS = (0, 1, 2)


def _scatter_phase(p, others=ALL_OTHERS):
    def build(ins, outs, ss, rs):
        x, y, c = _pos()
        chips = _other_chips(x, y)
        cps = [_remote(ins[0].at[2 * chips[j][0] + chips[j][1]], outs[0].at[k], ss, rs, k, (*chips[j], c))
               for k, j in enumerate(others)]
        return cps, cps

    return _Phase([p], [jax.ShapeDtypeStruct((len(others), p.shape[1], p.shape[2]), p.dtype)], {}, len(others), build)


def _assemble_phase(q):
    def build(ins, outs, ss, rs):
        x, y, c = _pos()
        mine, other = outs[0].at[c], outs[0].at[1 - c]
        return [_remote(mine, mine, ss, rs, 0, (x, y, 1 - c))], [_remote(other, other, ss, rs, 0, (x, y, c))]

    return _Phase([q], [jax.ShapeDtypeStruct(q.shape, q.dtype)], {0: 0}, 1, build)


def _gather_ici_phase(slot, others=ALL_OTHERS, rows=(0, 1, 1)):
    chunk = slot.shape[2] // rows[2]
    span = pl.ds(rows[0] * chunk, rows[1] * chunk)

    def build(ins, outs, ss, rs):
        x, y, c = _pos()
        chips = _other_chips(x, y)
        mine = outs[0].at[2 * x + y, c, span]
        sends, recvs = [], []
        for k, j in enumerate(others):
            theirs = outs[0].at[2 * chips[j][0] + chips[j][1], c, span]
            sends.append(_remote(mine, mine, ss, rs, k, (*chips[j], c)))
            recvs.append(_remote(theirs, theirs, ss, rs, k, (x, y, c)))
        return sends, recvs

    return _Phase([slot], [jax.ShapeDtypeStruct(slot.shape, slot.dtype)], {0: 0}, len(others), build)


def _gather_d2d_phase(slot):
    def build(ins, outs, ss, rs):
        x, y, c = _pos()
        sends, recvs = [], []
        for j, chip in enumerate(_other_chips(x, y)):
            landed = outs[0].at[2 * chip[0] + chip[1], c]
            coming = outs[0].at[2 * chip[0] + chip[1], 1 - c]
            sends.append(_remote(landed, landed, ss, rs, j, (x, y, 1 - c)))
            recvs.append(_remote(coming, coming, ss, rs, j, (x, y, c)))
        return sends, recvs

    return _Phase([slot], [jax.ShapeDtypeStruct(slot.shape, slot.dtype)], {0: 0}, 3, build)


def _gather_whole_phase(slots):
    def build(ins, outs, ss, rs):
        x, y, c = _pos()
        mine = outs[0].at[2 * x + y]
        sends, recvs = [], []
        for j, chip in enumerate(_other_chips(x, y)):
            theirs = outs[0].at[2 * chip[0] + chip[1]]
            sends.append(_remote(mine, mine, ss, rs, j, (*chip, c)))
            recvs.append(_remote(theirs, theirs, ss, rs, j, (x, y, c)))
        return sends, recvs

    return _Phase([slots], [jax.ShapeDtypeStruct(slots.shape, slots.dtype)], {0: 0}, 3, build)


def _split_refs(refs, n_in, n_out, n_scratch, phases):
    n_pin = sum(len(ph.ins) for ph in phases)
    n_pout = sum(len(ph.out_shapes) for ph in phases)
    cuts = [n_in, n_pin, n_out, n_pout, n_scratch]
    parts, at = [], 0
    for n in cuts:
        parts.append(refs[at:at + n])
        at += n
    parts.append(refs[at:])
    return parts


def _build_phases(phases, pin, pout, sems):
    built, i, o = [], 0, 0
    for k, ph in enumerate(phases):
        built.append(ph.build(pin[i:i + len(ph.ins)], pout[o:o + len(ph.out_shapes)], sems[2 * k], sems[2 * k + 1]))
        i += len(ph.ins)
        o += len(ph.out_shapes)
    return built


def _finish_phases(built):
    for _, recvs in built:
        for cp in recvs:
            cp.wait_recv()
    for sends, _ in built:
        for cp in sends:
            cp.wait_send()


def _call(body, name, grid, in_specs, out_specs, out_shape, scratch_shapes, semantics, args, phases=(),
          in_place=None):
    n_in, n_out, n_scratch = len(args), len(out_shape), len(scratch_shapes)
    aliases, in_at, out_at = dict(in_place or {}), n_in, n_out
    for ph in phases:
        aliases.update({in_at + i: out_at + o for i, o in ph.aliases.items()})
        in_at += len(ph.ins)
        out_at += len(ph.out_shapes)

    def hosted(*refs):
        ins, pin, outs, pout, scratch, sems = _split_refs(refs, n_in, n_out, n_scratch, phases)
        ids = [pl.program_id(a) for a in range(len(grid))]
        first = functools.reduce(jnp.logical_and, [i == 0 for i in ids])
        last = functools.reduce(jnp.logical_and, [i == g - 1 for i, g in zip(ids, grid)])

        @pl.when(first)
        def _():
            for sends, _ in _build_phases(phases, pin, pout, sems):
                for cp in sends:
                    cp.start()

        body(*ins, *outs, *scratch)

        @pl.when(last)
        def _():
            _finish_phases(_build_phases(phases, pin, pout, sems))

    p_args = [a for ph in phases for a in ph.ins]
    p_shapes = [s for ph in phases for s in ph.out_shapes]
    sem_shapes = [pltpu.SemaphoreType.DMA((ph.n_sems,)) for ph in phases for _ in range(2)]
    outs = pl.pallas_call(
        hosted if phases else body, name=name, grid=grid, in_specs=[*in_specs, *[ANY] * len(p_args)],
        out_specs=[*out_specs, *[ANY] * len(p_shapes)], out_shape=[*out_shape, *p_shapes],
        input_output_aliases=aliases, scratch_shapes=[*scratch_shapes, *sem_shapes],
        compiler_params=_params(("arbitrary",) * len(grid) if phases else semantics),
    )(*args, *p_args)
    phase_outs, at = [], n_out
    for ph in phases:
        phase_outs.append(list(outs[at:at + len(ph.out_shapes)]))
        at += len(ph.out_shapes)
    return list(outs[:n_out]), phase_outs


def _comm_call(name, phases):
    def body(*refs):
        _, pin, _, pout, _, sems = _split_refs(refs, 0, 0, 0, phases)
        built = _build_phases(phases, pin, pout, sems)
        for sends, _ in built:
            for cp in sends:
                cp.start()
        _finish_phases(built)

    aliases, in_at, out_at = {}, 0, 0
    for ph in phases:
        aliases.update({in_at + i: out_at + o for i, o in ph.aliases.items()})
        in_at += len(ph.ins)
        out_at += len(ph.out_shapes)
    p_args = [a for ph in phases for a in ph.ins]
    p_shapes = [s for ph in phases for s in ph.out_shapes]
    outs = pl.pallas_call(
        body, name=name, in_specs=[ANY] * len(p_args), out_specs=[ANY] * len(p_shapes), out_shape=p_shapes,
        input_output_aliases=aliases,
        scratch_shapes=[pltpu.SemaphoreType.DMA((ph.n_sems,)) for ph in phases for _ in range(2)],
    )(*p_args)
    phase_outs, at = [], 0
    for ph in phases:
        phase_outs.append(list(outs[at:at + len(ph.out_shapes)]))
        at += len(ph.out_shapes)
    return phase_outs


def _matmul(name, a, b, grid, a_spec, b_spec, contract, acc_shape, extras, extra_specs, out_shape, out_specs,
            epilogue, phases=()):
    n_extra, n_out, gk = len(extras), len(out_shape), grid[2]

    def product(a_ref, b_ref):
        return lax.dot_general(a_ref[...], b_ref[...], (contract, ((), ())), preferred_element_type=F32)

    def body_one_step(*refs):
        epilogue(product(refs[0], refs[1]), refs[2:2 + n_extra], refs[2 + n_extra:])

    def body(*refs):
        a_ref, b_ref = refs[0], refs[1]
        extra_refs = refs[2:2 + n_extra]
        out_refs = refs[2 + n_extra:2 + n_extra + n_out]
        acc_ref = refs[-1]
        kk = pl.program_id(2)

        @pl.when(kk == 0)
        def _():
            acc_ref[...] = product(a_ref, b_ref)

        @pl.when(kk > 0)
        def _():
            acc_ref[...] += product(a_ref, b_ref)

        @pl.when(kk == gk - 1)
        def _():
            epilogue(acc_ref[...], extra_refs, out_refs)

    outs, phase_outs = _call(
        body_one_step if gk == 1 else body, name, grid, [a_spec, b_spec, *extra_specs], out_specs, out_shape,
        [] if gk == 1 else [pltpu.VMEM(acc_shape, F32)], ("parallel", "arbitrary", "arbitrary"), (a, b, *extras),
        phases)
    return (outs, phase_outs) if phases else outs


def _mm_nn(name, a, b, extras, extra_specs, out_shape, out_specs, epilogue, b_chips=False, tm=1024, tn=1024,
           tk=2048, phases=(), a_parts=None):
    m, k, tm, tk, a_spec = _lhs_rows_by_k(a, tm, tk, a_parts)
    n = b.shape[1] if not b_chips else b.shape[2] * N_CHIPS
    if b_chips:
        tn = _tile(b.shape[2], tn)
        nb = b.shape[2] // tn
        b_spec = pl.BlockSpec((None, tk, tn), lambda i, j, kk: (j // nb, kk, j % nb))
    else:
        tn = _tile(n, tn)
        b_spec = pl.BlockSpec((tk, tn), lambda i, j, kk: (kk, j))
    return _matmul(name, a, b, (m // tm, n // tn, k // tk), a_spec, b_spec, ((1,), (0,)), (tm, tn), extras,
                   extra_specs(tm, tn), out_shape, out_specs(tm, tn), epilogue, phases)


def _lhs_rows_by_k(a, tm, tk, a_parts, tk_max=None):
    if a_parts is None:
        m, k = a.shape
        tm, tk = _tile(m, tm), _tile(k if tk_max is None else tk_max, tk)
        return m, k, tm, tk, pl.BlockSpec((tm, tk), lambda i, j, kk: (i, kk))
    n_parts, m, kp = a.shape
    tm, tk = _tile(m, tm), _tile(kp if tk_max is None else min(kp, tk_max), tk)
    nb = kp // tk
    return m, n_parts * kp, tm, tk, pl.BlockSpec((None, tm, tk), lambda i, j, kk: (a_parts(kk // nb), i, kk % nb))


def _mm_nt(name, a, b, extras, extra_specs, out_shape, out_specs, epilogue, b_chips=False, tm=1024, tn=1024,
           tk=2048, phases=(), a_parts=None):
    m, k, tm, tk, a_spec = _lhs_rows_by_k(a, tm, tk, a_parts, tk_max=b.shape[2] if b_chips else None)
    n = b.shape[0] if not b_chips else b.shape[1]
    tn = _tile(n, tn)
    if b_chips:
        nb = b.shape[2] // tk
        b_spec = pl.BlockSpec((None, tn, tk), lambda i, j, kk: (kk // nb, j, kk % nb))
    else:
        b_spec = pl.BlockSpec((tn, tk), lambda i, j, kk: (j, kk))
    return _matmul(name, a, b, (m // tm, n // tn, k // tk), a_spec, b_spec, ((1,), (1,)), (tm, tn), extras,
                   extra_specs(tm, tn), out_shape, out_specs(tm, tn), epilogue, phases)


def _mm_tn(name, a, b, out_chips=False, tm=1024, tn=1024, tk=2048, phases=(), a_parts=None, b_parts=None):
    if a_parts is None:
        k, m = a.shape
        tm = _tile(m, tm)
        a_spec = pl.BlockSpec((_tile(k, tk), tm), lambda i, j, kk: (kk, i))
    else:
        n_parts, k, mp = a.shape
        m, tm = n_parts * mp, _tile(mp, tm)
        nbm = mp // tm
        a_spec = pl.BlockSpec((None, _tile(k, tk), tm), lambda i, j, kk: (a_parts(i // nbm), kk, i % nbm))
    tk = _tile(k, tk)
    n = b.shape[1] if b_parts is None else b.shape[0] * b.shape[2]
    if out_chips:
        nc = n // N_CHIPS
        tn = _tile(nc, tn)
        nb = nc // tn
        out_shape = [jax.ShapeDtypeStruct((N_CHIPS, m, nc), F32)]
        out_specs = [pl.BlockSpec((None, tm, tn), lambda i, j, kk: (j // nb, i, j % nb))]
    else:
        tn = _tile(n, tn)
        out_shape = [jax.ShapeDtypeStruct((m, n), F32)]
        out_specs = [pl.BlockSpec((tm, tn), lambda i, j, kk: (i, j))]
    if b_parts is None:
        b_spec = pl.BlockSpec((tk, tn), lambda i, j, kk: (kk, j))
    else:
        nbn = b.shape[2] // tn
        b_spec = pl.BlockSpec((None, tk, tn), lambda i, j, kk: (b_parts(j // nbn), kk, j % nbn))

    def epilogue(acc, extra_refs, out_refs):
        out_refs[0][...] = acc

    res = _matmul(name, a, b, (m // tm, n // tn, k // tk), a_spec, b_spec, ((0,), (0,)), (tm, tn), (), [],
                  out_shape, out_specs, epilogue, phases)
    return (res[0][0], res[1]) if phases else res[0]


def _tile_spec(tm, tn):
    return pl.BlockSpec((tm, tn), lambda i, j, kk: (i, j))


def _row_spec(tn):
    return pl.BlockSpec((1, tn), lambda i, j, kk: (0, j))


def _ln_stats(r):
    mu = jnp.mean(r, axis=-1, keepdims=True)
    var = jnp.mean(jnp.square(r - mu), axis=-1, keepdims=True)
    rstd = lax.rsqrt(var + LN_EPS)
    return (r - mu) * rstd, rstd


def _ln_fwd(name, r, g, b, tr=256, phases=()):
    t, d = r.shape
    tr = _tile(t, tr)

    def body(r_ref, g_ref, b_ref, y_ref, yb_ref, xhat_ref, rstd_ref):
        xhat, rstd = _ln_stats(r_ref[...])
        y = xhat * g_ref[...] + b_ref[...]
        y_ref[...] = y
        yb_ref[...] = y.astype(BF16)
        xhat_ref[...] = xhat
        rstd_ref[...] = rstd

    row = pl.BlockSpec((tr, d), lambda i: (i, 0))
    vec = pl.BlockSpec((1, d), lambda i: (0, 0))
    outs, phase_outs = _call(
        body, name, (t // tr,), [row, vec, vec], [row, row, row, pl.BlockSpec((tr, 1), lambda i: (i, 0))],
        [jax.ShapeDtypeStruct((t, d), F32), jax.ShapeDtypeStruct((t, d), BF16),
         jax.ShapeDtypeStruct((t, d), F32), jax.ShapeDtypeStruct((t, 1), F32)], [], ("parallel",), (r, g, b), phases)
    return (outs, phase_outs) if phases else outs


def _ln_bwd_rows(dy, xhat, rstd, g):
    dxhat = dy * g
    m1 = jnp.mean(dxhat, axis=-1, keepdims=True)
    m2 = jnp.mean(dxhat * xhat, axis=-1, keepdims=True)
    return rstd * (dxhat - m1 - xhat * m2)


def _ln_bwd(name, dy, xhat, rstd, g, tr=256, phases=()):
    t, d = dy.shape
    tr = _tile(t, tr)

    def body(dy_ref, xhat_ref, rstd_ref, g_ref, dr_ref, drb_ref, dg_ref, db_ref):
        @pl.when(pl.program_id(0) == 0)
        def _():
            dg_ref[...] = jnp.zeros_like(dg_ref)
            db_ref[...] = jnp.zeros_like(db_ref)

        dy_t, xhat_t = dy_ref[...], xhat_ref[...]
        dr = _ln_bwd_rows(dy_t, xhat_t, rstd_ref[...], g_ref[...])
        dr_ref[...] = dr
        drb_ref[...] = dr.astype(BF16)
        dg_ref[...] += jnp.sum(dy_t * xhat_t, axis=0, keepdims=True)
        db_ref[...] += jnp.sum(dy_t, axis=0, keepdims=True)

    row = pl.BlockSpec((tr, d), lambda i: (i, 0))
    vec = pl.BlockSpec((1, d), lambda i: (0, 0))
    outs, phase_outs = _call(
        body, name, (t // tr,), [row, row, pl.BlockSpec((tr, 1), lambda i: (i, 0)), vec], [row, row, vec, vec],
        [jax.ShapeDtypeStruct((t, d), F32), jax.ShapeDtypeStruct((t, d), BF16),
         jax.ShapeDtypeStruct((1, d), F32), jax.ShapeDtypeStruct((1, d), F32)], [], ("arbitrary",),
        (dy, xhat, rstd, g), phases)
    return (outs, phase_outs) if phases else outs


def _ln2_loss_bwd(r2, target, g, b, tr=256):
    t, d = r2.shape
    tr = _tile(t, tr)

    def body(r_ref, t_ref, g_ref, b_ref, dr_ref, drb_ref, loss_ref, dg_ref, db_ref, dsum_ref):
        @pl.when(pl.program_id(0) == 0)
        def _():
            loss_ref[...] = jnp.zeros_like(loss_ref)
            dg_ref[...] = jnp.zeros_like(dg_ref)
            db_ref[...] = jnp.zeros_like(db_ref)
            dsum_ref[...] = jnp.zeros_like(dsum_ref)

        xhat, rstd = _ln_stats(r_ref[...])
        g_t = g_ref[...]
        err = xhat * g_t + b_ref[...] - t_ref[...]
        loss_ref[...] += 0.5 * jnp.sum(jnp.mean(jnp.square(err), axis=-1, keepdims=True), axis=0, keepdims=True)
        dy = err * (1.0 / d)
        dr = _ln_bwd_rows(dy, xhat, rstd, g_t)
        dr_ref[...] = dr
        drb_ref[...] = dr.astype(BF16)
        dg_ref[...] += jnp.sum(dy * xhat, axis=0, keepdims=True)
        db_ref[...] += jnp.sum(dy, axis=0, keepdims=True)
        dsum_ref[...] += jnp.sum(dr, axis=0, keepdims=True)

    row = pl.BlockSpec((tr, d), lambda i: (i, 0))
    vec = pl.BlockSpec((1, d), lambda i: (0, 0))
    return pl.pallas_call(
        body, name="ln2_loss_bwd", grid=(t // tr,), in_specs=[row, row, vec, vec],
        out_specs=[row, row, pl.BlockSpec((8, 128), lambda i: (0, 0)), vec, vec, vec],
        out_shape=[jax.ShapeDtypeStruct((t, d), F32), jax.ShapeDtypeStruct((t, d), BF16),
                   jax.ShapeDtypeStruct((8, 128), F32), jax.ShapeDtypeStruct((1, d), F32),
                   jax.ShapeDtypeStruct((1, d), F32), jax.ShapeDtypeStruct((1, d), F32)],
        compiler_params=_params(("arbitrary",)),
    )(r2, target, g, b)


POOL_ROWS = 512

DU_POOL = 3


def _du_part(block):
    return (block + DU_POOL) % 4


def _pool_mean_minus_token(u_ref, r0, rows, grp, first):
    width = u_ref.shape[1]
    body = u_ref[pl.ds(r0, rows), :]
    halo = u_ref[pl.ds(pl.multiple_of(jnp.maximum(r0 - POOL_HALO, 0), POOL_HALO), POOL_HALO), :]
    halo = jnp.where(first, 0.0, halo)
    full = jnp.concatenate([halo, body], axis=0)
    s = full
    for step in range(len(POOL_WINDOWS)):
        shifted = pltpu.roll(s, 1 << step, axis=0)
        s = s + jnp.where(grp >= step, shifted, 0.0)
    s = s[POOL_HALO:, :]
    tpos = r0 + lax.broadcasted_iota(jnp.int32, (rows, width), 0)
    count = jnp.minimum(tpos + 1, 2 << grp).astype(F32)
    return s / count - body, count


def _pool_fwd(u, w_pool, pool_scale, t, pw):
    gw = pw // len(POOL_WINDOWS)
    rows = _tile(t, POOL_ROWS)

    def body(u_ref, w_ref, s_ref, o_ref):
        grp = pl.program_id(0)

        def chunk(ci, carry):
            r0 = pl.multiple_of(ci * rows, rows)
            y, _ = _pool_mean_minus_token(u_ref, r0, rows, grp, ci == 0)
            yw = jnp.dot(y.astype(BF16), w_ref[...], preferred_element_type=F32)
            o_ref[pl.ds(r0, rows), :] = (yw * s_ref[...]).astype(BF16)
            return carry

        lax.fori_loop(0, t // rows, chunk, 0)

    return pl.pallas_call(
        body, name="pool_fwd", grid=(len(POOL_WINDOWS),),
        in_specs=[pl.BlockSpec((t, gw), lambda g: (0, g)), pl.BlockSpec((None, gw, gw), lambda g: (g, 0, 0)),
                  pl.BlockSpec((None, 1, gw), lambda g: (g, 0, 0))],
        out_specs=pl.BlockSpec((None, t, gw), lambda g: (0, 0, g)),
        out_shape=jax.ShapeDtypeStruct((2, t, pw), BF16),
        compiler_params=_params(("parallel",)),
    )(u, w_pool, pool_scale)


def _pool_bwd(u, dmix, w_pool, pool_scale, t, pw):
    n_grp = len(POOL_WINDOWS)
    gw = pw // n_grp
    rows = _tile(t, POOL_ROWS)

    def body(u_ref, dm_ref, w_ref, s_ref, du_ref, dw_ref, ds_ref, e_ref):
        grp = pl.program_id(0)
        dw_ref[...] = jnp.zeros_like(dw_ref)
        ds_ref[...] = jnp.zeros_like(ds_ref)
        e_ref[pl.ds(t, POOL_HALO), :] = jnp.zeros((POOL_HALO, gw), F32)

        def chunk(ci, carry):
            r0 = pl.multiple_of(ci * rows, rows)
            y, count = _pool_mean_minus_token(u_ref, r0, rows, grp, ci == 0)
            yb = y.astype(BF16)
            yw = jnp.dot(yb, w_ref[...], preferred_element_type=F32)
            dy2 = dm_ref[pl.ds(r0, rows), :]
            ds_ref[...] += jnp.sum(dy2 * yw, axis=0, keepdims=True)
            dyw = (dy2 * s_ref[...]).astype(BF16)
            dw_ref[...] += lax.dot_general(yb, dyw, (((0,), (0,)), ((), ())), preferred_element_type=F32)
            dy = lax.dot_general(dyw, w_ref[...], (((1,), (1,)), ((), ())), preferred_element_type=F32)
            e_ref[pl.ds(r0, rows), :] = dy / count
            return carry

        lax.fori_loop(0, t // rows, chunk, 0)

        def chunk2(ci, carry):
            r0 = pl.multiple_of(ci * rows, rows)
            full = e_ref[pl.ds(r0, rows + POOL_HALO), :]
            s = full
            for step in range(n_grp):
                shifted = pltpu.roll(s, rows + POOL_HALO - (1 << step), axis=0)
                s = s + jnp.where(grp >= step, shifted, 0.0)
            e = full[:rows, :]
            tpos = r0 + lax.broadcasted_iota(jnp.int32, (rows, gw), 0)
            count = jnp.minimum(tpos + 1, 2 << grp).astype(F32)
            du_ref[pl.ds(r0, rows), :] = (s[:rows, :] - e * count).astype(BF16)
            return carry

        lax.fori_loop(0, t // rows, chunk2, 0)

    return pl.pallas_call(
        body, name="pool_bwd", grid=(n_grp,),
        in_specs=[pl.BlockSpec((t, gw), lambda g: (0, g)), pl.BlockSpec((t, gw), lambda g: (0, g)),
                  pl.BlockSpec((None, gw, gw), lambda g: (g, 0, 0)),
                  pl.BlockSpec((None, 1, gw), lambda g: (g, 0, 0))],
        out_specs=[pl.BlockSpec((None, t, gw), lambda g: (DU_POOL, 0, g)),
                   pl.BlockSpec((None, gw, gw), lambda g: (g, 0, 0)), pl.BlockSpec((None, 1, gw), lambda g: (g, 0, 0))],
        out_shape=[jax.ShapeDtypeStruct((4, t, pw), BF16), jax.ShapeDtypeStruct((n_grp, gw, gw), F32),
                   jax.ShapeDtypeStruct((n_grp, 1, gw), F32)],
        scratch_shapes=[pltpu.VMEM((t + POOL_HALO, gw), F32)],
        compiler_params=_params(("parallel",)),
    )(u, dmix, w_pool, pool_scale)


def _sb_scores(q, k_blk, scale, mask):
    z = lax.dot_general(q, k_blk, (((1,), (1,)), ((), ())), preferred_element_type=F32) * scale
    log_not = jnp.minimum(-z, 0.0) - jnp.log(1.0 + jnp.exp(-jnp.abs(z)))
    return z, (log_not if mask is None else jnp.where(mask, log_not, 0.0))


def _sb_weights(e, mask):
    a = jnp.exp(e)
    return a if mask is None else jnp.where(mask, a, 0.0)


EXP_IS_ZERO_BELOW = -104.0


def _weights_alive(after):
    return (jnp.max(after) >= EXP_IS_ZERO_BELOW).astype(jnp.int32)


def _split_dot(vs, tri):
    parts = []
    for v in vs:
        hi = v.astype(BF16)
        parts += [hi, (v - hi.astype(F32)).astype(BF16)]
    prod = jnp.dot(jnp.concatenate(parts, axis=0), tri, preferred_element_type=F32)
    m = vs[0].shape[0]
    return [prod[2 * k * m:(2 * k + 1) * m] + prod[(2 * k + 1) * m:(2 * k + 2) * m] for k in range(len(vs))]


def _head(ref, h, rows=None):
    cols = slice(h * HEAD_DIM, (h + 1) * HEAD_DIM)
    return ref[:, cols] if rows is None else ref[rows, cols]


def _attn_fwd(ub, mix, t, nh, hg, phases=()):
    scale = float(1.0 / (HEAD_DIM ** 0.5))
    ng = nh // hg

    def body(q_ref, k_ref, v_ref, o_ref):
        i = pl.program_id(1)
        row = lax.broadcasted_iota(jnp.int32, (QB, KB), 0)
        col = lax.broadcasted_iota(jnp.int32, (QB, KB), 1)
        suffix = (row >= col).astype(BF16)

        def more(carry):
            return jnp.logical_and(carry[0] <= i, carry[3] > 0)

        def block(n, accs, afters, mask):
            rows = pl.ds(pl.multiple_of((i - n) * KB, KB), KB)
            new_accs, new_afters = [], []
            scores = [_sb_scores(_head(q_ref, h), _head(k_ref, h, rows), scale, mask) for h in range(hg)]
            withins = _split_dot([log_not for _, log_not in scores], suffix)
            for h in range(hg):
                z, log_not = scores[h]
                a = _sb_weights(z + withins[h] + afters[h], mask)
                new_accs.append(accs[h] + jnp.dot(a.astype(BF16), _head(v_ref, h, rows),
                                                  preferred_element_type=F32))
                new_afters.append(afters[h] + jnp.sum(log_not, axis=1, keepdims=True))
            return n + 1, tuple(new_accs), tuple(new_afters), _weights_alive(functools.reduce(jnp.maximum, new_afters))

        first = block(jnp.int32(0), tuple(jnp.zeros((QB, HEAD_DIM), F32) for _ in range(hg)),
                      tuple(jnp.zeros((QB, 1), F32) for _ in range(hg)), col < row)
        _, accs, _, _ = lax.while_loop(more, lambda carry: block(carry[0], carry[1], carry[2], None), first)
        for h in range(hg):
            o_ref[:, h * HEAD_DIM:(h + 1) * HEAD_DIM] = accs[h].astype(BF16)

    wide = hg * HEAD_DIM
    outs, phase_outs = _call(
        lambda q_ref, k_ref, v_ref, mix_ref, o_ref: body(q_ref, k_ref, v_ref, o_ref), "attn_fwd", (ng, t // QB),
        [pl.BlockSpec((QB, wide), lambda g, i: (i, ng + g)), pl.BlockSpec((t, wide), lambda g, i: (0, 2 * ng + g)),
         pl.BlockSpec((t, wide), lambda g, i: (0, 3 * ng + g)), ANY],
        [pl.BlockSpec((None, QB, wide), lambda g, i: (1, i, g))], [jax.ShapeDtypeStruct(mix.shape, mix.dtype)], [],
        ("parallel", "arbitrary"), (ub, ub, ub, mix), phases, in_place={3: 0})
    return outs[0], phase_outs


def _attn_bwd(ub, dmix, du, t, nh, hg, phases=()):
    scale = float(1.0 / (HEAD_DIM ** 0.5))
    ng = nh // hg

    def body(q_ref, k_ref, v_ref, do_ref, du_in_ref, du_ref, g_ref, z_ref, dk_ref, dv_ref):
        i = pl.program_id(1)

        @pl.when(i == 0)
        def _():
            dk_ref[...] = jnp.zeros_like(dk_ref)
            dv_ref[...] = jnp.zeros_like(dv_ref)

        row = lax.broadcasted_iota(jnp.int32, (QB, KB), 0)
        col = lax.broadcasted_iota(jnp.int32, (QB, KB), 1)
        suffix = (row >= col).astype(BF16)
        prefix = (row <= col).astype(BF16)

        def more(carry):
            return jnp.logical_and(carry[0] <= i, carry[2] > 0)

        def down(n, afters, mask):
            ks = pl.multiple_of((i - n) * KB, KB)
            rows = pl.ds(ks, KB)
            new_afters = []
            scores = [_sb_scores(_head(q_ref, h), _head(k_ref, h, rows), scale, mask) for h in range(hg)]
            withins = _split_dot([log_not for _, log_not in scores], suffix)
            for h in range(hg):
                do = _head(do_ref, h).astype(BF16)
                z, log_not = scores[h]
                a = _sb_weights(z + withins[h] + afters[h], mask)
                da = lax.dot_general(do, _head(v_ref, h, rows), (((1,), (1,)), ((), ())),
                                     preferred_element_type=F32)
                g_ref[h, :, pl.ds(ks, KB)] = a * da
                z_ref[h, :, pl.ds(ks, KB)] = z
                dv_ref[rows, h * HEAD_DIM:(h + 1) * HEAD_DIM] += lax.dot_general(
                    a.astype(BF16), do, (((0,), (0,)), ((), ())), preferred_element_type=F32)
                new_afters.append(afters[h] + jnp.sum(log_not, axis=1, keepdims=True))
            return n + 1, tuple(new_afters), _weights_alive(functools.reduce(jnp.maximum, new_afters))

        diagonal = col < row
        first = down(jnp.int32(0), tuple(jnp.zeros((QB, 1), F32) for _ in range(hg)), diagonal)
        visited, _, _ = lax.while_loop(more, lambda carry: down(carry[0], carry[1], None), first)

        def up(kb, carry, mask):
            dqs, befores = carry
            ks = pl.multiple_of(kb * KB, KB)
            rows = pl.ds(ks, KB)
            new_dqs, new_befores = [], []
            gs = [g_ref[h, :, pl.ds(ks, KB)] for h in range(hg)]
            g_withins = _split_dot(gs, prefix)
            for h in range(hg):
                g = gs[h]
                z = z_ref[h, :, pl.ds(ks, KB)]
                g_upto = g_withins[h] + befores[h]
                dz = g - jax.nn.sigmoid(z) * g_upto
                dz = dz if mask is None else jnp.where(mask, dz, 0.0)
                dzs = (dz * scale).astype(BF16)
                new_dqs.append(dqs[h] + jnp.dot(dzs, _head(k_ref, h, rows), preferred_element_type=F32))
                dk_ref[rows, h * HEAD_DIM:(h + 1) * HEAD_DIM] += lax.dot_general(
                    dzs, _head(q_ref, h), (((0,), (0,)), ((), ())), preferred_element_type=F32)
                new_befores.append(befores[h] + jnp.sum(g, axis=1, keepdims=True))
            return tuple(new_dqs), tuple(new_befores)

        below = lax.fori_loop(i + 1 - visited, i, lambda kb, carry: up(kb, carry, None),
                              (tuple(jnp.zeros((QB, HEAD_DIM), F32) for _ in range(hg)),
                               tuple(jnp.zeros((QB, 1), F32) for _ in range(hg))))
        dqs, _ = up(i, below, diagonal)
        q_rows = pl.ds(pl.multiple_of(i * QB, QB), QB)
        for h in range(hg):
            du_ref[0, q_rows, h * HEAD_DIM:(h + 1) * HEAD_DIM] = dqs[h].astype(BF16)

        @pl.when(i == t // QB - 1)
        def _():
            du_ref[1] = dk_ref[...].astype(BF16)
            du_ref[2] = dv_ref[...].astype(BF16)

    wide = hg * HEAD_DIM
    tile = lambda off: pl.BlockSpec((QB, wide), lambda g, i: (i, off + g))
    strip = lambda off: pl.BlockSpec((t, wide), lambda g, i: (0, off + g))
    outs, phase_outs = _call(
        body, "attn_bwd", (ng, t // QB), [tile(ng), strip(2 * ng), strip(3 * ng), tile(ng), ANY],
        [pl.BlockSpec((3, t, wide), lambda g, i: (0, 0, g))], [jax.ShapeDtypeStruct(du.shape, du.dtype)],
        [pltpu.VMEM((hg, QB, t), F32), pltpu.VMEM((hg, QB, t), F32), pltpu.VMEM((t, wide), F32),
         pltpu.VMEM((t, wide), F32)], ("parallel", "arbitrary"), (ub, ub, ub, dmix, du), phases, in_place={4: 0})
    return outs[0], phase_outs


def _row_tile(rows, cols, pref_bytes=2 * 1024 * 1024):
    tr = max(8, pref_bytes // (4 * cols))
    while rows % tr:
        tr //= 2
    return max(tr, 1)


def _pair_sum(name, g, s, c_idx):
    _, _, r2, cols = g.shape
    tr = _row_tile(r2, cols)

    def body(c_ref, g_ref, s_ref, o_ref):
        o_ref[...] = (g_ref[...] + s_ref[...]).astype(BF16)

    return pl.pallas_call(
        body, name=name,
        grid_spec=pltpu.PrefetchScalarGridSpec(
            num_scalar_prefetch=1, grid=(N_CHIPS, r2 // tr),
            in_specs=[pl.BlockSpec((None, None, tr, cols), lambda p, i, c: (p, c[0], i, 0)),
                      pl.BlockSpec((None, tr, cols), lambda p, i, c: (p, i, 0))],
            out_specs=pl.BlockSpec((None, tr, cols), lambda p, i, c: (p, i, 0))),
        out_shape=jax.ShapeDtypeStruct((N_CHIPS, r2, cols), BF16),
        compiler_params=_params(("parallel", "parallel")),
    )(c_idx, g, s)


def _chip_sum(name, g, s, received, place):
    _, _, r2, cols = g.shape
    tr = _row_tile(r2, cols)
    counts = [r.shape[0] for r in received]

    def body(place_ref, g_ref, s_ref, *refs):
        total = g_ref[...] + s_ref[...]
        for r_ref, n in zip(refs[:-1], counts):
            for k in range(n):
                total = total + r_ref[k].astype(F32)
        refs[-1][...] = total

    return pl.pallas_call(
        body, name=name,
        grid_spec=pltpu.PrefetchScalarGridSpec(
            num_scalar_prefetch=1, grid=(r2 // tr,),
            in_specs=[pl.BlockSpec((None, None, tr, cols), lambda i, p: (p[0], p[1], i, 0)),
                      pl.BlockSpec((None, tr, cols), lambda i, p: (p[0], i, 0)),
                      *[pl.BlockSpec((n, tr, cols), lambda i, p: (0, i, 0)) for n in counts]],
            out_specs=pl.BlockSpec((None, tr, cols), lambda i, p: (p[1], i, 0))),
        out_shape=jax.ShapeDtypeStruct((2, r2, cols), F32),
        compiler_params=_params(("parallel",)),
    )(place, g, s, *received)


def _cast_into_slot(name, w, place):
    rows, cols = w.shape
    r2 = rows // 2
    tr = _row_tile(r2, cols)
    nb = r2 // tr

    def body(place_ref, w_ref, o_ref):
        o_ref[...] = w_ref[...].astype(BF16)

    return pl.pallas_call(
        body, name=name,
        grid_spec=pltpu.PrefetchScalarGridSpec(
            num_scalar_prefetch=1, grid=(2, nb),
            in_specs=[pl.BlockSpec((tr, cols), lambda h, i, s: (h * nb + i, 0))],
            out_specs=pl.BlockSpec((None, None, tr, cols), lambda h, i, s: (s[0], h, i, 0))),
        out_shape=jax.ShapeDtypeStruct((N_CHIPS, 2, r2, cols), BF16),
        compiler_params=_params(("parallel", "parallel")),
    )(place, w)


def _colsum(name, a):
    def body(a_ref, o_ref):
        o_ref[...] = jnp.sum(a_ref[...], axis=0, keepdims=True)

    whole = lambda shape: pl.BlockSpec(shape, lambda i: (0, 0))
    return pl.pallas_call(
        body, name=name, grid=(1,), in_specs=[whole(a.shape)], out_specs=whole((1, a.shape[1])),
        out_shape=jax.ShapeDtypeStruct((1, a.shape[1]), F32), compiler_params=_params(("arbitrary",)),
    )(a)


def _adamw(name, w, g, m, v):
    rows, cols = w.shape
    tr = _row_tile(rows, cols, 1024 * 1024)

    def body(w_ref, g_ref, m_ref, v_ref, g_out_ref, d_ref, nm_ref, nv_ref):
        g_t = g_ref[...]
        m_t = ADAM_B1 * m_ref[...] + (1.0 - ADAM_B1) * g_t
        v_t = ADAM_B2 * v_ref[...] + (1.0 - ADAM_B2) * jnp.square(g_t)
        m_hat = m_t / (1.0 - ADAM_B1 ** ADAM_STEP)
        v_hat = v_t / (1.0 - ADAM_B2 ** ADAM_STEP)
        g_out_ref[...] = g_t
        d_ref[...] = -ADAM_LR * (m_hat / (jnp.sqrt(v_hat) + ADAM_EPS) + ADAM_WD * w_ref[...])
        nm_ref[...] = m_t
        nv_ref[...] = v_t

    spec = pl.BlockSpec((tr, cols), lambda i: (i, 0))
    shape = jax.ShapeDtypeStruct((rows, cols), F32)
    return pl.pallas_call(
        body, name=name, grid=(rows // tr,), in_specs=[spec] * 4, out_specs=[spec] * 4, out_shape=[shape] * 4,
        compiler_params=_params(("parallel",)),
    )(w, g, m, v)


def _all_reduce_small(packed, phases=()):
    rows, cols = packed.shape

    def body(in_ref, out_ref, all_ref, send_sems, recv_sems):
        x, y, c = _pos()
        me = 4 * x + 2 * y + c
        all_ref[me] = in_ref[...]
        cps = []
        for r in range(1, N_DEV):
            bx, by, bc = (r >> 2) & 1, (r >> 1) & 1, r & 1
            peer = (1 - x if bx else x, 1 - y if by else y, 1 - c if bc else c)
            cp = pltpu.make_async_remote_copy(
                src_ref=in_ref, dst_ref=all_ref.at[me], send_sem=send_sems.at[r - 1], recv_sem=recv_sems.at[r - 1],
                device_id=peer, device_id_type=MESH)
            cp.start()
            cps.append(cp)
        for cp in cps:
            cp.wait()
        total = all_ref[0]
        for d in range(1, N_DEV):
            total = total + all_ref[d]
        out_ref[...] = total

    vmem = pl.BlockSpec(memory_space=pltpu.VMEM)
    outs, phase_outs = _call(
        body, "all_reduce_small", (1,), [vmem], [vmem], [jax.ShapeDtypeStruct((rows, cols), F32)],
        [pltpu.VMEM((N_DEV, rows, cols), F32), pltpu.SemaphoreType.DMA((N_DEV - 1,)),
         pltpu.SemaphoreType.DMA((N_DEV - 1,))], ("arbitrary",), (packed,), phases)
    return outs[0], phase_outs


def kernel(x, ln_in_g, ln_in_b, w_in, w_pool, pool_scale, w_out, ln1_g, ln1_b, w_ff1, b_ff1, w_ff2, b_ff2, ln2_g, ln2_b, loss_target, m_ln_in_g, m_ln_in_b, m_w_in, m_w_pool, m_pool_scale, m_w_out, m_ln1_g, m_ln1_b, m_w_ff1, m_b_ff1, m_w_ff2, m_b_ff2, m_ln2_g, m_ln2_b, v_ln_in_g, v_ln_in_b, v_w_in, v_w_pool, v_pool_scale, v_w_out, v_ln1_g, v_ln1_b, v_w_ff1, v_b_ff1, v_w_ff2, v_b_ff2, v_ln2_g, v_ln2_b):
    t, d = x.shape[1], x.shape[2]
    pw = d // 2
    n_grp = len(POOL_WINDOWS)
    gw = pw // n_grp
    gwc = gw // N_CHIPS
    nh = pw // HEAD_DIM
    ff = w_ff1.shape[2] * N_CHIPS
    assert w_in.shape[0] == 1 and w_in.shape[2] * N_CHIPS == 2 * d and gwc <= 128

    x_idx, y_idx, c_idx = _pos()
    chip_arr = jnp.reshape(2 * x_idx + y_idx, (1,)).astype(jnp.int32)
    c_arr = jnp.reshape(c_idx, (1,)).astype(jnp.int32)
    place = jnp.concatenate([chip_arr, c_arr])

    xs = x.reshape(t, d)
    target = loss_target.reshape(t, d)
    row = lambda vec: vec.reshape(1, -1)

    scale_tile = jnp.zeros((1, 8, 128), F32).at[0, :n_grp, :gwc].set(pool_scale[0])
    scale_slots = lax.dynamic_update_slice(jnp.zeros((N_CHIPS, 8, 128), F32), scale_tile, (chip_arr[0], 0, 0))
    shards = dict(w_in=w_in[0], w_out=w_out[0], w_ff1=w_ff1[0], w_ff2=w_ff2[0], w_pool=w_pool[0].reshape(gw, gw))
    slot = {nm: _cast_into_slot("cast_" + nm, w, place) for nm, w in shards.items()}
    unsplit = lambda s: s.reshape(N_CHIPS, 2 * s.shape[2], s.shape[3])

    (h0, h0b, xhat0, rstd0), ((win_s,), (wpool_s,), (scale_g,)) = _ln_fwd(
        "ln_in_fwd", xs, row(ln_in_g), row(ln_in_b),
        phases=[_gather_ici_phase(slot["w_in"]), _gather_ici_phase(slot["w_pool"]), _gather_whole_phase(scale_slots)])
    (win_s,), (wpool_s,) = _comm_call("gather_d2d_first", [_gather_d2d_phase(win_s), _gather_d2d_phase(wpool_s)])
    win_g = unsplit(win_s)
    wpool_full = unsplit(wpool_s).reshape(N_CHIPS, n_grp, gwc, gw).transpose(1, 0, 2, 3).reshape(n_grp, gw, gw)
    scale_full = scale_g[:, :n_grp, :gwc].transpose(1, 0, 2).reshape(n_grp, 1, gw)

    def store_f32(acc, extra_refs, out_refs):
        out_refs[0][...] = acc

    def pool_f32_all_bf16(acc, extra_refs, out_refs):
        @pl.when(pl.program_id(1) == 0)
        def _():
            out_refs[0][...] = acc

        out_refs[1][...] = acc.astype(BF16)

    assert w_in.shape[2] == pw
    (u, ub), ((wout_s,), (wff1_s,)) = _mm_nn(
        "in_proj", h0b, win_g, (), lambda tm, tn: [],
        [jax.ShapeDtypeStruct((t, pw), F32), jax.ShapeDtypeStruct((t, 2 * d), BF16)],
        lambda tm, tn: [pl.BlockSpec((tm, tn), lambda i, j, kk: (i, 0)), _tile_spec(tm, tn)],
        pool_f32_all_bf16, b_chips=True, tn=pw,
        phases=[_gather_ici_phase(slot["w_out"]), _gather_ici_phase(slot["w_ff1"], rows=(0, 1, 8))])
    mix_in = _pool_fwd(u, wpool_full, scale_full, t, pw)
    mix_in, ((wff1_s,), (wout_s,)) = _attn_fwd(
        ub, mix_in, t, nh, min(nh, 4),
        phases=[_gather_ici_phase(wff1_s, rows=(1, 6, 8)), _gather_d2d_phase(wout_s)])
    same_part = lambda block: block
    wout_full = unsplit(wout_s).reshape(d, d)

    def residual(acc, extra_refs, out_refs):
        out_refs[0][...] = ALPHA * extra_refs[0][...] + acc

    (r1,), ((wff1_s,), (wff2_s,)) = _mm_nn(
        "out_proj", mix_in, wout_full, (h0,), lambda tm, tn: [_tile_spec(tm, tn)],
        [jax.ShapeDtypeStruct((t, d), F32)], lambda tm, tn: [_tile_spec(tm, tn)], residual, a_parts=same_part,
        phases=[_gather_ici_phase(wff1_s, rows=(7, 1, 8)), _gather_ici_phase(slot["w_ff2"], rows=(0, 1, 8))])
    (h1, h1b, xhat1, rstd1), ((wff1_s,), (wff2_s,)) = _ln_fwd(
        "ln1_fwd", r1, ln1_g, ln1_b, phases=[_gather_d2d_phase(wff1_s), _gather_ici_phase(wff2_s, rows=(1, 1, 8))])
    wff1_g = unsplit(wff1_s)

    def relu_sq(acc, extra_refs, out_refs):
        p = jnp.maximum(acc + extra_refs[0][...], 0.0)
        out_refs[0][...] = p
        out_refs[1][...] = jnp.square(p).astype(BF16)

    (relu_z, act_b), ((wff2_s,),) = _mm_nn(
        "ff1", h1b, wff1_g, (b_ff1,), lambda tm, tn: [_row_spec(tn)],
        [jax.ShapeDtypeStruct((t, ff), F32), jax.ShapeDtypeStruct((t, ff), BF16)],
        lambda tm, tn: [_tile_spec(tm, tn)] * 2, relu_sq, b_chips=True,
        phases=[_gather_ici_phase(wff2_s, rows=(2, 6, 8))])
    ((wff2_s,),) = _comm_call("gather_d2d_last", [_gather_d2d_phase(wff2_s)])
    wff2_full = unsplit(wff2_s).reshape(ff, d)

    def residual_bias(acc, extra_refs, out_refs):
        out_refs[0][...] = ALPHA * extra_refs[0][...] + (acc + extra_refs[1][...])

    r2 = _mm_nn("ff2", act_b, wff2_full, (h1, b_ff2), lambda tm, tn: [_tile_spec(tm, tn), _row_spec(tn)],
                [jax.ShapeDtypeStruct((t, d), F32)], lambda tm, tn: [_tile_spec(tm, tn)], residual_bias)[0]

    dr2, dr2b, loss_tile, g_ln2_g, g_ln2_b, g_b_ff2 = _ln2_loss_bwd(r2, target, ln2_g, ln2_b)
    loss = lax.psum(loss_tile[0, 0], ("x", "y", "c"))

    halves = lambda g: g.reshape(N_CHIPS, 2, g.shape[1] // 2, g.shape[2])
    g_ff2 = halves(_mm_tn("grad_w_ff2", act_b, dr2b).reshape(N_CHIPS, ff // N_CHIPS, d))

    def relu_sq_bwd(acc, extra_refs, out_refs):
        dz = acc * (2.0 * extra_refs[0][...])
        out_refs[0][...] = dz.astype(BF16)
        rows = lax.broadcasted_iota(jnp.int32, out_refs[1].shape, 0)
        out_refs[1][...] = jnp.where(rows == 0, jnp.sum(dz, axis=0, keepdims=True), 0.0)

    tm_ff = _tile(t, 1024)
    (dz1b, g_b_ff1_parts), ((s_ff2,),) = _mm_nt(
        "ff2_bwd", dr2b, wff2_full, (relu_z,), lambda tm, tn: [_tile_spec(tm, tn)],
        [jax.ShapeDtypeStruct((t, ff), BF16), jax.ShapeDtypeStruct((8 * (t // tm_ff), ff), F32)],
        lambda tm, tn: [_tile_spec(tm, tn), pl.BlockSpec((8, tn), lambda i, j, kk: (i, j))], relu_sq_bwd,
        phases=[_swap_phase(g_ff2)])
    p_ff2 = _pair_sum("pair_sum_w_ff2", g_ff2, s_ff2, c_arr)
    g_ff1, ((r_ff2_a,),) = _mm_tn("grad_w_ff1", h1b, dz1b, out_chips=True,
                                  phases=[_scatter_phase(p_ff2, others=(0, 1))])
    g_ff1 = halves(g_ff1)

    def plus_alpha(acc, extra_refs, out_refs):
        out_refs[0][...] = ALPHA * extra_refs[0][...] + acc

    (dh1,), ((s_ff1,), (r_ff2_b,)) = _mm_nt(
        "ff1_bwd", dz1b, wff1_g, (dr2,), lambda tm, tn: [_tile_spec(tm, tn)],
        [jax.ShapeDtypeStruct((t, d), F32)], lambda tm, tn: [_tile_spec(tm, tn)], plus_alpha, b_chips=True,
        phases=[_swap_phase(g_ff1), _scatter_phase(p_ff2, others=(2,))])
    q_ff2 = _chip_sum("chip_sum_w_ff2", g_ff2, s_ff2, [r_ff2_a, r_ff2_b], place)
    p_ff1 = _pair_sum("pair_sum_w_ff1", g_ff1, s_ff1, c_arr)
    (dr1, dr1b, g_ln1_g, g_ln1_b), ((q_ff2,),) = _ln_bwd("ln1_bwd", dh1, xhat1, rstd1, ln1_g,
                                                         phases=[_assemble_phase(q_ff2)])

    g_out = halves(_mm_tn("grad_w_out", mix_in, dr1b, a_parts=same_part).reshape(N_CHIPS, d // N_CHIPS, d))
    (dmix,), ((s_out,),) = _mm_nt(
        "out_proj_bwd", dr1b, wout_full, (), lambda tm, tn: [], [jax.ShapeDtypeStruct((t, d), F32)],
        lambda tm, tn: [_tile_spec(tm, tn)], store_f32, phases=[_swap_phase(g_out)])
    p_out = _pair_sum("pair_sum_w_out", g_out, s_out, c_arr)
    du, g_w_pool_full, g_scale_full = _pool_bwd(u, dmix, wpool_full, scale_full, t, pw)
    du, ((r_ff1,), (r_out,)) = _attn_bwd(ub, dmix, du, t, nh, min(nh, 2),
                                         phases=[_scatter_phase(p_ff1), _scatter_phase(p_out)])
    q_ff1 = _chip_sum("chip_sum_w_ff1", g_ff1, s_ff1, [r_ff1], place)
    q_out = _chip_sum("chip_sum_w_out", g_out, s_out, [r_out], place)
    g_in, ((q_ff1,), (q_out,)) = _mm_tn("grad_w_in", h0b, du, out_chips=True, b_parts=_du_part,
                                        phases=[_assemble_phase(q_ff1), _assemble_phase(q_out)])
    g_in = halves(g_in)
    g_pool = halves(g_w_pool_full.reshape(n_grp, N_CHIPS, gwc, gw).transpose(1, 0, 2, 3).reshape(N_CHIPS, gw, gw))
    (s_in,), (s_pool,) = _comm_call("rs_swap_last", [_swap_phase(g_in), _swap_phase(g_pool)])
    p_in = _pair_sum("pair_sum_w_in", g_in, s_in, c_arr)
    p_pool = _pair_sum("pair_sum_w_pool", g_pool, s_pool, c_arr)
    (dh0,), ((r_in,), (r_pool,)) = _mm_nt(
        "in_proj_bwd", du, win_g, (dr1,), lambda tm, tn: [_tile_spec(tm, tn)],
        [jax.ShapeDtypeStruct((t, d), F32)], lambda tm, tn: [_tile_spec(tm, tn)], plus_alpha, b_chips=True,
        a_parts=_du_part, phases=[_scatter_phase(p_in), _scatter_phase(p_pool)])
    q_in = _chip_sum("chip_sum_w_in", g_in, s_in, [r_in], place)
    q_pool = _chip_sum("chip_sum_w_pool", g_pool, s_pool, [r_pool], place)
    dx, _, g_ln_in_g, g_ln_in_b = _ln_bwd("ln_in_bwd", dh0, xhat0, rstd0, row(ln_in_g))

    lane = 2048 if d % 2048 == 0 else d
    small_names = ["ln_in_g", "ln_in_b", "ln1_g", "ln1_b", "b_ff1", "b_ff2", "ln2_g", "ln2_b"]
    small_w = dict(ln_in_g=ln_in_g, ln_in_b=ln_in_b, ln1_g=ln1_g, ln1_b=ln1_b, b_ff1=b_ff1, b_ff2=b_ff2, ln2_g=ln2_g,
                   ln2_b=ln2_b)
    small_m = dict(ln_in_g=m_ln_in_g, ln_in_b=m_ln_in_b, ln1_g=m_ln1_g, ln1_b=m_ln1_b, b_ff1=m_b_ff1, b_ff2=m_b_ff2,
                   ln2_g=m_ln2_g, ln2_b=m_ln2_b)
    small_v = dict(ln_in_g=v_ln_in_g, ln_in_b=v_ln_in_b, ln1_g=v_ln1_g, ln1_b=v_ln1_b, b_ff1=v_b_ff1, b_ff2=v_b_ff2,
                   ln2_g=v_ln2_g, ln2_b=v_ln2_b)
    small_g = dict(ln_in_g=g_ln_in_g, ln_in_b=g_ln_in_b, ln1_g=g_ln1_g, ln1_b=g_ln1_b, b_ff2=g_b_ff2, ln2_g=g_ln2_g,
                   ln2_b=g_ln2_b)

    def pack(parts):
        flat = jnp.concatenate([p.reshape(-1) for p in parts])
        n_rows = -(-flat.shape[0] // lane)
        n_rows = -(-n_rows // 8) * 8
        return jnp.pad(flat, (0, n_rows * lane - flat.shape[0])).reshape(n_rows, lane)

    small_g["b_ff1"] = _colsum("b_ff1_colsum", g_b_ff1_parts)
    summed, ((q_in,), (q_pool,)) = _all_reduce_small(
        pack([small_g[nm] for nm in small_names] + [g_scale_full]),
        phases=[_assemble_phase(q_in), _assemble_phase(q_pool)])
    summed = summed.reshape(-1)

    big = {}
    for nm, q, w, m, v in [("w_in", q_in, w_in, m_w_in, v_w_in), ("w_out", q_out, w_out, m_w_out, v_w_out),
                           ("w_ff1", q_ff1, w_ff1, m_w_ff1, v_w_ff1), ("w_ff2", q_ff2, w_ff2, m_w_ff2, v_w_ff2),
                           ("w_pool", q_pool, w_pool, m_w_pool, v_w_pool)]:
        g = q.reshape(2 * q.shape[1], q.shape[2])
        flat = lambda arr: arr.reshape(g.shape)
        big[nm] = tuple(arr.reshape(w.shape) for arr in _adamw("adamw_" + nm, flat(w), g, flat(m), flat(v)))

    g_small, off = {}, 0
    for nm in small_names:
        g_small[nm] = summed[off:off + small_w[nm].size]
        off += small_w[nm].size
    g_scale_all = summed[off:off + n_grp * gw].reshape(n_grp, N_CHIPS, gwc)
    g_scale = lax.dynamic_index_in_dim(g_scale_all, chip_arr[0], axis=1, keepdims=False)

    order = small_names + ["pool_scale"]
    small_w["pool_scale"], small_m["pool_scale"], small_v["pool_scale"] = pool_scale, m_pool_scale, v_pool_scale
    g_small["pool_scale"] = g_scale
    _, delta_s, new_m_s, new_v_s = _adamw("adamw_small", pack([small_w[nm] for nm in order]),
                                          pack([g_small[nm] for nm in order]), pack([small_m[nm] for nm in order]),
                                          pack([small_v[nm] for nm in order]))
    small = {}
    off = 0
    for nm in order:
        size, shape = small_w[nm].size, small_w[nm].shape
        cut = lambda arr: arr.reshape(-1)[off:off + size].reshape(shape)
        small[nm] = (g_small[nm].reshape(shape), cut(delta_s), cut(new_m_s), cut(new_v_s))
        off += size

    every = {**big, **small}
    weight_order = ["ln_in_g", "ln_in_b", "w_in", "w_pool", "pool_scale", "w_out", "ln1_g", "ln1_b", "w_ff1", "b_ff1",
                    "w_ff2", "b_ff2", "ln2_g", "ln2_b"]
    grads = [every[nm][0] for nm in weight_order]
    deltas = [every[nm][1] for nm in weight_order]
    new_ms = [every[nm][2] for nm in weight_order]
    new_vs = [every[nm][3] for nm in weight_order]
    return (loss, dx.reshape(x.shape), *grads, *deltas, *new_ms, *new_vs)
```

```python
import functools

import jax
import jax.numpy as jnp
from jax import lax
from jax.experimental import pallas as pl
from jax.experimental.pallas import tpu as pltpu

F32 = jnp.float32
BF16 = jnp.bfloat16
MESH = pl.DeviceIdType.MESH

HEAD_DIM = 128
POOL_WINDOWS = (2, 4, 8, 16)
POOL_HALO = 16
LN_EPS = 1e-5
ALPHA = 2.0 ** 0.25
ADAM_LR, ADAM_B1, ADAM_B2, ADAM_EPS, ADAM_WD, ADAM_STEP = 0.001, 0.9, 0.999, 1e-08, 0.01, 10

QB = 256
KB = 256
VMEM_LIMIT = 56 * 1024 * 1024
N_CHIPS = 4
N_DEV = 8


def _params(sem=None):
    return pltpu.CompilerParams(dimension_semantics=sem, vmem_limit_bytes=VMEM_LIMIT)


def _tile(dim, pref):
    return pref if dim % pref == 0 else dim


def _pos():
    return lax.axis_index("x"), lax.axis_index("y"), lax.axis_index("c")


def _other_chips(x, y):
    return [(1 - x, y), (x, 1 - y), (1 - x, 1 - y)]


ANY = pl.BlockSpec(memory_space=pl.ANY)


class _Phase:
    def __init__(self, ins, out_shapes, aliases, n_sems, build):
        self.ins, self.out_shapes, self.aliases, self.n_sems, self.build = ins, out_shapes, aliases, n_sems, build


def _remote(src, dst, send_sems, recv_sems, k, to):
    return pltpu.make_async_remote_copy(src_ref=src, dst_ref=dst, send_sem=send_sems.at[k], recv_sem=recv_sems.at[k],
                                        device_id=to, device_id_type=MESH)


def _swap_phase(g):
    def build(ins, outs, ss, rs):
        x, y, c = _pos()
        cp = _remote(ins[0].at[:, 1 - c], outs[0], ss, rs, 0, (x, y, 1 - c))
        return [cp], [cp]

    return _Phase([g], [jax.ShapeDtypeStruct((N_CHIPS, g.shape[2], g.shape[3]), g.dtype)], {}, 1, build)


ALL_OTHERS = (0, 1, 2)


def _scatter_phase(p, others=ALL_OTHERS):
    def build(ins, outs, ss, rs):
        x, y, c = _pos()
        chips = _other_chips(x, y)
        cps = [_remote(ins[0].at[2 * chips[j][0] + chips[j][1]], outs[0].at[k], ss, rs, k, (*chips[j], c))
               for k, j in enumerate(others)]
        return cps, cps

    return _Phase([p], [jax.ShapeDtypeStruct((len(others), p.shape[1], p.shape[2]), p.dtype)], {}, len(others), build)


def _assemble_phase(q):
    def build(ins, outs, ss, rs):
        x, y, c = _pos()
        mine, other = outs[0].at[c], outs[0].at[1 - c]
        return [_remote(mine, mine, ss, rs, 0, (x, y, 1 - c))], [_remote(other, other, ss, rs, 0, (x, y, c))]

    return _Phase([q], [jax.ShapeDtypeStruct(q.shape, q.dtype)], {0: 0}, 1, build)


def _gather_ici_phase(slot, others=ALL_OTHERS, rows=(0, 1, 1)):
    chunk = slot.shape[2] // rows[2]
    span = pl.ds(rows[0] * chunk, rows[1] * chunk)

    def build(ins, outs, ss, rs):
        x, y, c = _pos()
        chips = _other_chips(x, y)
        mine = outs[0].at[2 * x + y, c, span]
        sends, recvs = [], []
        for k, j in enumerate(others):
            theirs = outs[0].at[2 * chips[j][0] + chips[j][1], c, span]
            sends.append(_remote(mine, mine, ss, rs, k, (*chips[j], c)))
            recvs.append(_remote(theirs, theirs, ss, rs, k, (x, y, c)))
        return sends, recvs

    return _Phase([slot], [jax.ShapeDtypeStruct(slot.shape, slot.dtype)], {0: 0}, len(others), build)


def _gather_d2d_phase(slot):
    def build(ins, outs, ss, rs):
        x, y, c = _pos()
        sends, recvs = [], []
        for j, chip in enumerate(_other_chips(x, y)):
            landed = outs[0].at[2 * chip[0] + chip[1], c]
            coming = outs[0].at[2 * chip[0] + chip[1], 1 - c]
            sends.append(_remote(landed, landed, ss, rs, j, (x, y, 1 - c)))
            recvs.append(_remote(coming, coming, ss, rs, j, (x, y, c)))
        return sends, recvs

    return _Phase([slot], [jax.ShapeDtypeStruct(slot.shape, slot.dtype)], {0: 0}, 3, build)


def _gather_whole_phase(slots):
    def build(ins, outs, ss, rs):
        x, y, c = _pos()
        mine = outs[0].at[2 * x + y]
        sends, recvs = [], []
        for j, chip in enumerate(_other_chips(x, y)):
            theirs = outs[0].at[2 * chip[0] + chip[1]]
            sends.append(_remote(mine, mine, ss, rs, j, (*chip, c)))
            recvs.append(_remote(theirs, theirs, ss, rs, j, (x, y, c)))
        return sends, recvs

    return _Phase([slots], [jax.ShapeDtypeStruct(slots.shape, slots.dtype)], {0: 0}, 3, build)


def _split_refs(refs, n_in, n_out, n_scratch, phases):
    n_pin = sum(len(ph.ins) for ph in phases)
    n_pout = sum(len(ph.out_shapes) for ph in phases)
    cuts = [n_in, n_pin, n_out, n_pout, n_scratch]
    parts, at = [], 0
    for n in cuts:
        parts.append(refs[at:at + n])
        at += n
    parts.append(refs[at:])
    return parts


def _build_phases(phases, pin, pout, sems):
    built, i, o = [], 0, 0
    for k, ph in enumerate(phases):
        built.append(ph.build(pin[i:i + len(ph.ins)], pout[o:o + len(ph.out_shapes)], sems[2 * k], sems[2 * k + 1]))
        i += len(ph.ins)
        o += len(ph.out_shapes)
    return built


def _finish_phases(built):
    for _, recvs in built:
        for cp in recvs:
            cp.wait_recv()
    for sends, _ in built:
        for cp in sends:
            cp.wait_send()


def _call(body, name, grid, in_specs, out_specs, out_shape, scratch_shapes, semantics, args, phases=(),
          in_place=None):
    n_in, n_out, n_scratch = len(args), len(out_shape), len(scratch_shapes)
    aliases, in_at, out_at = dict(in_place or {}), n_in, n_out
    for ph in phases:
        aliases.update({in_at + i: out_at + o for i, o in ph.aliases.items()})
        in_at += len(ph.ins)
        out_at += len(ph.out_shapes)

    def hosted(*refs):
        ins, pin, outs, pout, scratch, sems = _split_refs(refs, n_in, n_out, n_scratch, phases)
        ids = [pl.program_id(a) for a in range(len(grid))]
        first = functools.reduce(jnp.logical_and, [i == 0 for i in ids])
        last = functools.reduce(jnp.logical_and, [i == g - 1 for i, g in zip(ids, grid)])

        @pl.when(first)
        def _():
            for sends, _ in _build_phases(phases, pin, pout, sems):
                for cp in sends:
                    cp.start()

        body(*ins, *outs, *scratch)

        @pl.when(last)
        def _():
            _finish_phases(_build_phases(phases, pin, pout, sems))

    p_args = [a for ph in phases for a in ph.ins]
    p_shapes = [s for ph in phases for s in ph.out_shapes]
    sem_shapes = [pltpu.SemaphoreType.DMA((ph.n_sems,)) for ph in phases for _ in range(2)]
    outs = pl.pallas_call(
        hosted if phases else body, name=name, grid=grid, in_specs=[*in_specs, *[ANY] * len(p_args)],
        out_specs=[*out_specs, *[ANY] * len(p_shapes)], out_shape=[*out_shape, *p_shapes],
        input_output_aliases=aliases, scratch_shapes=[*scratch_shapes, *sem_shapes],
        compiler_params=_params(("arbitrary",) * len(grid) if phases else semantics),
    )(*args, *p_args)
    phase_outs, at = [], n_out
    for ph in phases:
        phase_outs.append(list(outs[at:at + len(ph.out_shapes)]))
        at += len(ph.out_shapes)
    return list(outs[:n_out]), phase_outs


def _comm_call(name, phases):
    def body(*refs):
        _, pin, _, pout, _, sems = _split_refs(refs, 0, 0, 0, phases)
        built = _build_phases(phases, pin, pout, sems)
        for sends, _ in built:
            for cp in sends:
                cp.start()
        _finish_phases(built)

    aliases, in_at, out_at = {}, 0, 0
    for ph in phases:
        aliases.update({in_at + i: out_at + o for i, o in ph.aliases.items()})
        in_at += len(ph.ins)
        out_at += len(ph.out_shapes)
    p_args = [a for ph in phases for a in ph.ins]
    p_shapes = [s for ph in phases for s in ph.out_shapes]
    outs = pl.pallas_call(
        body, name=name, in_specs=[ANY] * len(p_args), out_specs=[ANY] * len(p_shapes), out_shape=p_shapes,
        input_output_aliases=aliases,
        scratch_shapes=[pltpu.SemaphoreType.DMA((ph.n_sems,)) for ph in phases for _ in range(2)],
    )(*p_args)
    phase_outs, at = [], 0
    for ph in phases:
        phase_outs.append(list(outs[at:at + len(ph.out_shapes)]))
        at += len(ph.out_shapes)
    return phase_outs


def _matmul(name, a, b, grid, a_spec, b_spec, contract, acc_shape, extras, extra_specs, out_shape, out_specs,
            epilogue, phases=()):
    n_extra, n_out, gk = len(extras), len(out_shape), grid[2]

    def product(a_ref, b_ref):
        return lax.dot_general(a_ref[...], b_ref[...], (contract, ((), ())), preferred_element_type=F32)

    def body_one_step(*refs):
        epilogue(product(refs[0], refs[1]), refs[2:2 + n_extra], refs[2 + n_extra:])

    def body(*refs):
        a_ref, b_ref = refs[0], refs[1]
        extra_refs = refs[2:2 + n_extra]
        out_refs = refs[2 + n_extra:2 + n_extra + n_out]
        acc_ref = refs[-1]
        kk = pl.program_id(2)

        @pl.when(kk == 0)
        def _():
            acc_ref[...] = product(a_ref, b_ref)

        @pl.when(kk > 0)
        def _():
            acc_ref[...] += product(a_ref, b_ref)

        @pl.when(kk == gk - 1)
        def _():
            epilogue(acc_ref[...], extra_refs, out_refs)

    outs, phase_outs = _call(
        body_one_step if gk == 1 else body, name, grid, [a_spec, b_spec, *extra_specs], out_specs, out_shape,
        [] if gk == 1 else [pltpu.VMEM(acc_shape, F32)], ("parallel", "arbitrary", "arbitrary"), (a, b, *extras),
        phases)
    return (outs, phase_outs) if phases else outs


def _mm_nn(name, a, b, extras, extra_specs, out_shape, out_specs, epilogue, b_chips=False, tm=1024, tn=1024,
           tk=2048, phases=(), a_parts=None):
    m, k, tm, tk, a_spec = _lhs_rows_by_k(a, tm, tk, a_parts)
    n = b.shape[1] if not b_chips else b.shape[2] * N_CHIPS
    if b_chips:
        tn = _tile(b.shape[2], tn)
        nb = b.shape[2] // tn
        b_spec = pl.BlockSpec((None, tk, tn), lambda i, j, kk: (j // nb, kk, j % nb))
    else:
        tn = _tile(n, tn)
        b_spec = pl.BlockSpec((tk, tn), lambda i, j, kk: (kk, j))
    return _matmul(name, a, b, (m // tm, n // tn, k // tk), a_spec, b_spec, ((1,), (0,)), (tm, tn), extras,
                   extra_specs(tm, tn), out_shape, out_specs(tm, tn), epilogue, phases)


def _lhs_rows_by_k(a, tm, tk, a_parts, tk_max=None):
    if a_parts is None:
        m, k = a.shape
        tm, tk = _tile(m, tm), _tile(k if tk_max is None else tk_max, tk)
        return m, k, tm, tk, pl.BlockSpec((tm, tk), lambda i, j, kk: (i, kk))
    n_parts, m, kp = a.shape
    tm, tk = _tile(m, tm), _tile(kp if tk_max is None else min(kp, tk_max), tk)
    nb = kp // tk
    return m, n_parts * kp, tm, tk, pl.BlockSpec((None, tm, tk), lambda i, j, kk: (a_parts(kk // nb), i, kk % nb))


def _mm_nt(name, a, b, extras, extra_specs, out_shape, out_specs, epilogue, b_chips=False, tm=1024, tn=1024,
           tk=2048, phases=(), a_parts=None):
    m, k, tm, tk, a_spec = _lhs_rows_by_k(a, tm, tk, a_parts, tk_max=b.shape[2] if b_chips else None)
    n = b.shape[0] if not b_chips else b.shape[1]
    tn = _tile(n, tn)
    if b_chips:
        nb = b.shape[2] // tk
        b_spec = pl.BlockSpec((None, tn, tk), lambda i, j, kk: (kk // nb, j, kk % nb))
    else:
        b_spec = pl.BlockSpec((tn, tk), lambda i, j, kk: (j, kk))
    return _matmul(name, a, b, (m // tm, n // tn, k // tk), a_spec, b_spec, ((1,), (1,)), (tm, tn), extras,
                   extra_specs(tm, tn), out_shape, out_specs(tm, tn), epilogue, phases)


def _mm_tn(name, a, b, out_chips=False, tm=1024, tn=1024, tk=2048, phases=(), a_parts=None, b_parts=None):
    if a_parts is None:
        k, m = a.shape
        tm = _tile(m, tm)
        a_spec = pl.BlockSpec((_tile(k, tk), tm), lambda i, j, kk: (kk, i))
    else:
        n_parts, k, mp = a.shape
        m, tm = n_parts * mp, _tile(mp, tm)
        nbm = mp // tm
        a_spec = pl.BlockSpec((None, _tile(k, tk), tm), lambda i, j, kk: (a_parts(i // nbm), kk, i % nbm))
    tk = _tile(k, tk)
    n = b.shape[1] if b_parts is None else b.shape[0] * b.shape[2]
    if out_chips:
        nc = n // N_CHIPS
        tn = _tile(nc, tn)
        nb = nc // tn
        out_shape = [jax.ShapeDtypeStruct((N_CHIPS, m, nc), F32)]
        out_specs = [pl.BlockSpec((None, tm, tn), lambda i, j, kk: (j // nb, i, j % nb))]
    else:
        tn = _tile(n, tn)
        out_shape = [jax.ShapeDtypeStruct((m, n), F32)]
        out_specs = [pl.BlockSpec((tm, tn), lambda i, j, kk: (i, j))]
    if b_parts is None:
        b_spec = pl.BlockSpec((tk, tn), lambda i, j, kk: (kk, j))
    else:
        nbn = b.shape[2] // tn
        b_spec = pl.BlockSpec((None, tk, tn), lambda i, j, kk: (b_parts(j // nbn), kk, j % nbn))

    def epilogue(acc, extra_refs, out_refs):
        out_refs[0][...] = acc

    res = _matmul(name, a, b, (m // tm, n // tn, k // tk), a_spec, b_spec, ((0,), (0,)), (tm, tn), (), [],
                  out_shape, out_specs, epilogue, phases)
    return (res[0][0], res[1]) if phases else res[0]


def _tile_spec(tm, tn):
    return pl.BlockSpec((tm, tn), lambda i, j, kk: (i, j))


def _row_spec(tn):
    return pl.BlockSpec((1, tn), lambda i, j, kk: (0, j))


def _ln_stats(r):
    mu = jnp.mean(r, axis=-1, keepdims=True)
    var = jnp.mean(jnp.square(r - mu), axis=-1, keepdims=True)
    rstd = lax.rsqrt(var + LN_EPS)
    return (r - mu) * rstd, rstd


def _ln_fwd(name, r, g, b, tr=256, phases=()):
    t, d = r.shape
    tr = _tile(t, tr)

    def body(r_ref, g_ref, b_ref, y_ref, yb_ref, xhat_ref, rstd_ref):
        xhat, rstd = _ln_stats(r_ref[...])
        y = xhat * g_ref[...] + b_ref[...]
        y_ref[...] = y
        yb_ref[...] = y.astype(BF16)
        xhat_ref[...] = xhat
        rstd_ref[...] = rstd

    row = pl.BlockSpec((tr, d), lambda i: (i, 0))
    vec = pl.BlockSpec((1, d), lambda i: (0, 0))
    outs, phase_outs = _call(
        body, name, (t // tr,), [row, vec, vec], [row, row, row, pl.BlockSpec((tr, 1), lambda i: (i, 0))],
        [jax.ShapeDtypeStruct((t, d), F32), jax.ShapeDtypeStruct((t, d), BF16),
         jax.ShapeDtypeStruct((t, d), F32), jax.ShapeDtypeStruct((t, 1), F32)], [], ("parallel",), (r, g, b), phases)
    return (outs, phase_outs) if phases else outs


def _ln_bwd_rows(dy, xhat, rstd, g):
    dxhat = dy * g
    m1 = jnp.mean(dxhat, axis=-1, keepdims=True)
    m2 = jnp.mean(dxhat * xhat, axis=-1, keepdims=True)
    return rstd * (dxhat - m1 - xhat * m2)


def _ln_bwd(name, dy, xhat, rstd, g, tr=256, phases=()):
    t, d = dy.shape
    tr = _tile(t, tr)

    def body(dy_ref, xhat_ref, rstd_ref, g_ref, dr_ref, drb_ref, dg_ref, db_ref):
        @pl.when(pl.program_id(0) == 0)
        def _():
            dg_ref[...] = jnp.zeros_like(dg_ref)
            db_ref[...] = jnp.zeros_like(db_ref)

        dy_t, xhat_t = dy_ref[...], xhat_ref[...]
        dr = _ln_bwd_rows(dy_t, xhat_t, rstd_ref[...], g_ref[...])
        dr_ref[...] = dr
        drb_ref[...] = dr.astype(BF16)
        dg_ref[...] += jnp.sum(dy_t * xhat_t, axis=0, keepdims=True)
        db_ref[...] += jnp.sum(dy_t, axis=0, keepdims=True)

    row = pl.BlockSpec((tr, d), lambda i: (i, 0))
    vec = pl.BlockSpec((1, d), lambda i: (0, 0))
    outs, phase_outs = _call(
        body, name, (t // tr,), [row, row, pl.BlockSpec((tr, 1), lambda i: (i, 0)), vec], [row, row, vec, vec],
        [jax.ShapeDtypeStruct((t, d), F32), jax.ShapeDtypeStruct((t, d), BF16),
         jax.ShapeDtypeStruct((1, d), F32), jax.ShapeDtypeStruct((1, d), F32)], [], ("arbitrary",),
        (dy, xhat, rstd, g), phases)
    return (outs, phase_outs) if phases else outs


def _ln2_loss_bwd(r2, target, g, b, tr=256):
    t, d = r2.shape
    tr = _tile(t, tr)

    def body(r_ref, t_ref, g_ref, b_ref, dr_ref, drb_ref, loss_ref, dg_ref, db_ref, dsum_ref):
        @pl.when(pl.program_id(0) == 0)
        def _():
            loss_ref[...] = jnp.zeros_like(loss_ref)
            dg_ref[...] = jnp.zeros_like(dg_ref)
            db_ref[...] = jnp.zeros_like(db_ref)
            dsum_ref[...] = jnp.zeros_like(dsum_ref)

        xhat, rstd = _ln_stats(r_ref[...])
        g_t = g_ref[...]
        err = xhat * g_t + b_ref[...] - t_ref[...]
        loss_ref[...] += 0.5 * jnp.sum(jnp.mean(jnp.square(err), axis=-1, keepdims=True), axis=0, keepdims=True)
        dy = err * (1.0 / d)
        dr = _ln_bwd_rows(dy, xhat, rstd, g_t)
        dr_ref[...] = dr
        drb_ref[...] = dr.astype(BF16)
        dg_ref[...] += jnp.sum(dy * xhat, axis=0, keepdims=True)
        db_ref[...] += jnp.sum(dy, axis=0, keepdims=True)
        dsum_ref[...] += jnp.sum(dr, axis=0, keepdims=True)

    row = pl.BlockSpec((tr, d), lambda i: (i, 0))
    vec = pl.BlockSpec((1, d), lambda i: (0, 0))
    return pl.pallas_call(
        body, name="ln2_loss_bwd", grid=(t // tr,), in_specs=[row, row, vec, vec],
        out_specs=[row, row, pl.BlockSpec((8, 128), lambda i: (0, 0)), vec, vec, vec],
        out_shape=[jax.ShapeDtypeStruct((t, d), F32), jax.ShapeDtypeStruct((t, d), BF16),
                   jax.ShapeDtypeStruct((8, 128), F32), jax.ShapeDtypeStruct((1, d), F32),
                   jax.ShapeDtypeStruct((1, d), F32), jax.ShapeDtypeStruct((1, d), F32)],
        compiler_params=_params(("arbitrary",)),
    )(r2, target, g, b)


POOL_ROWS = 512

DU_POOL = 3


def _du_part(block):
    return (block + DU_POOL) % 4


def _pool_mean_minus_token(u_ref, r0, rows, grp, first):
    width = u_ref.shape[1]
    body = u_ref[pl.ds(r0, rows), :]
    halo = u_ref[pl.ds(pl.multiple_of(jnp.maximum(r0 - POOL_HALO, 0), POOL_HALO), POOL_HALO), :]
    halo = jnp.where(first, 0.0, halo)
    full = jnp.concatenate([halo, body], axis=0)
    s = full
    for step in range(len(POOL_WINDOWS)):
        shifted = pltpu.roll(s, 1 << step, axis=0)
        s = s + jnp.where(grp >= step, shifted, 0.0)
    s = s[POOL_HALO:, :]
    tpos = r0 + lax.broadcasted_iota(jnp.int32, (rows, width), 0)
    count = jnp.minimum(tpos + 1, 2 << grp).astype(F32)
    return s / count - body, count


def _pool_fwd(u, w_pool, pool_scale, t, pw):
    gw = pw // len(POOL_WINDOWS)
    rows = _tile(t, POOL_ROWS)

    def body(u_ref, w_ref, s_ref, o_ref):
        grp = pl.program_id(0)

        def chunk(ci, carry):
            r0 = pl.multiple_of(ci * rows, rows)
            y, _ = _pool_mean_minus_token(u_ref, r0, rows, grp, ci == 0)
            yw = jnp.dot(y.astype(BF16), w_ref[...], preferred_element_type=F32)
            o_ref[pl.ds(r0, rows), :] = (yw * s_ref[...]).astype(BF16)
            return carry

        lax.fori_loop(0, t // rows, chunk, 0)

    return pl.pallas_call(
        body, name="pool_fwd", grid=(len(POOL_WINDOWS),),
        in_specs=[pl.BlockSpec((t, gw), lambda g: (0, g)), pl.BlockSpec((None, gw, gw), lambda g: (g, 0, 0)),
                  pl.BlockSpec((None, 1, gw), lambda g: (g, 0, 0))],
        out_specs=pl.BlockSpec((None, t, gw), lambda g: (0, 0, g)),
        out_shape=jax.ShapeDtypeStruct((2, t, pw), BF16),
        compiler_params=_params(("parallel",)),
    )(u, w_pool, pool_scale)


def _pool_bwd(u, dmix, w_pool, pool_scale, t, pw):
    n_grp = len(POOL_WINDOWS)
    gw = pw // n_grp
    rows = _tile(t, POOL_ROWS)

    def body(u_ref, dm_ref, w_ref, s_ref, du_ref, dw_ref, ds_ref, e_ref):
        grp = pl.program_id(0)
        dw_ref[...] = jnp.zeros_like(dw_ref)
        ds_ref[...] = jnp.zeros_like(ds_ref)
        e_ref[pl.ds(t, POOL_HALO), :] = jnp.zeros((POOL_HALO, gw), F32)

        def chunk(ci, carry):
            r0 = pl.multiple_of(ci * rows, rows)
            y, count = _pool_mean_minus_token(u_ref, r0, rows, grp, ci == 0)
            yb = y.astype(BF16)
            yw = jnp.dot(yb, w_ref[...], preferred_element_type=F32)
            dy2 = dm_ref[pl.ds(r0, rows), :]
            ds_ref[...] += jnp.sum(dy2 * yw, axis=0, keepdims=True)
            dyw = (dy2 * s_ref[...]).astype(BF16)
            dw_ref[...] += lax.dot_general(yb, dyw, (((0,), (0,)), ((), ())), preferred_element_type=F32)
            dy = lax.dot_general(dyw, w_ref[...], (((1,), (1,)), ((), ())), preferred_element_type=F32)
            e_ref[pl.ds(r0, rows), :] = dy / count
            return carry

        lax.fori_loop(0, t // rows, chunk, 0)

        def chunk2(ci, carry):
            r0 = pl.multiple_of(ci * rows, rows)
            full = e_ref[pl.ds(r0, rows + POOL_HALO), :]
            s = full
            for step in range(n_grp):
                shifted = pltpu.roll(s, rows + POOL_HALO - (1 << step), axis=0)
                s = s + jnp.where(grp >= step, shifted, 0.0)
            e = full[:rows, :]
            tpos = r0 + lax.broadcasted_iota(jnp.int32, (rows, gw), 0)
            count = jnp.minimum(tpos + 1, 2 << grp).astype(F32)
            du_ref[pl.ds(r0, rows), :] = (s[:rows, :] - e * count).astype(BF16)
            return carry

        lax.fori_loop(0, t // rows, chunk2, 0)

    return pl.pallas_call(
        body, name="pool_bwd", grid=(n_grp,),
        in_specs=[pl.BlockSpec((t, gw), lambda g: (0, g)), pl.BlockSpec((t, gw), lambda g: (0, g)),
                  pl.BlockSpec((None, gw, gw), lambda g: (g, 0, 0)),
                  pl.BlockSpec((None, 1, gw), lambda g: (g, 0, 0))],
        out_specs=[pl.BlockSpec((None, t, gw), lambda g: (DU_POOL, 0, g)),
                   pl.BlockSpec((None, gw, gw), lambda g: (g, 0, 0)), pl.BlockSpec((None, 1, gw), lambda g: (g, 0, 0))],
        out_shape=[jax.ShapeDtypeStruct((4, t, pw), BF16), jax.ShapeDtypeStruct((n_grp, gw, gw), F32),
                   jax.ShapeDtypeStruct((n_grp, 1, gw), F32)],
        scratch_shapes=[pltpu.VMEM((t + POOL_HALO, gw), F32)],
        compiler_params=_params(("parallel",)),
    )(u, dmix, w_pool, pool_scale)


def _sb_scores(q, k_blk, scale, mask):
    z = lax.dot_general(q, k_blk, (((1,), (1,)), ((), ())), preferred_element_type=F32) * scale
    log_not = jnp.minimum(-z, 0.0) - jnp.log(1.0 + jnp.exp(-jnp.abs(z)))
    return z, (log_not if mask is None else jnp.where(mask, log_not, 0.0))


def _sb_weights(e, mask):
    a = jnp.exp(e)
    return a if mask is None else jnp.where(mask, a, 0.0)


EXP_IS_ZERO_BELOW = -104.0


def _weights_alive(after):
    return (jnp.max(after) >= EXP_IS_ZERO_BELOW).astype(jnp.int32)


def _split_dot(vs, tri):
    parts = []
    for v in vs:
        hi = v.astype(BF16)
        parts += [hi, (v - hi.astype(F32)).astype(BF16)]
    prod = jnp.dot(jnp.concatenate(parts, axis=0), tri, preferred_element_type=F32)
    m = vs[0].shape[0]
    return [prod[2 * k * m:(2 * k + 1) * m] + prod[(2 * k + 1) * m:(2 * k + 2) * m] for k in range(len(vs))]


def _head(ref, h, rows=None):
    cols = slice(h * HEAD_DIM, (h + 1) * HEAD_DIM)
    return ref[:, cols] if rows is None else ref[rows, cols]


def _attn_fwd(ub, mix, t, nh, hg, phases=()):
    scale = float(1.0 / (HEAD_DIM ** 0.5))
    ng = nh // hg

    def body(q_ref, k_ref, v_ref, o_ref):
        i = pl.program_id(1)
        row = lax.broadcasted_iota(jnp.int32, (QB, KB), 0)
        col = lax.broadcasted_iota(jnp.int32, (QB, KB), 1)
        suffix = (row >= col).astype(BF16)

        def more(carry):
            return jnp.logical_and(carry[0] <= i, carry[3] > 0)

        def block(n, accs, afters, mask):
            rows = pl.ds(pl.multiple_of((i - n) * KB, KB), KB)
            new_accs, new_afters = [], []
            scores = [_sb_scores(_head(q_ref, h), _head(k_ref, h, rows), scale, mask) for h in range(hg)]
            withins = _split_dot([log_not for _, log_not in scores], suffix)
            for h in range(hg):
                z, log_not = scores[h]
                a = _sb_weights(z + withins[h] + afters[h], mask)
                new_accs.append(accs[h] + jnp.dot(a.astype(BF16), _head(v_ref, h, rows),
                                                  preferred_element_type=F32))
                new_afters.append(afters[h] + jnp.sum(log_not, axis=1, keepdims=True))
            return n + 1, tuple(new_accs), tuple(new_afters), _weights_alive(functools.reduce(jnp.maximum, new_afters))

        first = block(jnp.int32(0), tuple(jnp.zeros((QB, HEAD_DIM), F32) for _ in range(hg)),
                      tuple(jnp.zeros((QB, 1), F32) for _ in range(hg)), col < row)
        _, accs, _, _ = lax.while_loop(more, lambda carry: block(carry[0], carry[1], carry[2], None), first)
        for h in range(hg):
            o_ref[:, h * HEAD_DIM:(h + 1) * HEAD_DIM] = accs[h].astype(BF16)

    wide = hg * HEAD_DIM
    outs, phase_outs = _call(
        lambda q_ref, k_ref, v_ref, mix_ref, o_ref: body(q_ref, k_ref, v_ref, o_ref), "attn_fwd", (ng, t // QB),
        [pl.BlockSpec((QB, wide), lambda g, i: (i, ng + g)), pl.BlockSpec((t, wide), lambda g, i: (0, 2 * ng + g)),
         pl.BlockSpec((t, wide), lambda g, i: (0, 3 * ng + g)), ANY],
        [pl.BlockSpec((None, QB, wide), lambda g, i: (1, i, g))], [jax.ShapeDtypeStruct(mix.shape, mix.dtype)], [],
        ("parallel", "arbitrary"), (ub, ub, ub, mix), phases, in_place={3: 0})
    return outs[0], phase_outs


def _attn_bwd(ub, dmix, du, t, nh, hg, phases=()):
    scale = float(1.0 / (HEAD_DIM ** 0.5))
    ng = nh // hg

    def body(q_ref, k_ref, v_ref, do_ref, du_in_ref, du_ref, g_ref, z_ref, dk_ref, dv_ref):
        i = pl.program_id(1)

        @pl.when(i == 0)
        def _():
            dk_ref[...] = jnp.zeros_like(dk_ref)
            dv_ref[...] = jnp.zeros_like(dv_ref)

        row = lax.broadcasted_iota(jnp.int32, (QB, KB), 0)
        col = lax.broadcasted_iota(jnp.int32, (QB, KB), 1)
        suffix = (row >= col).astype(BF16)
        prefix = (row <= col).astype(BF16)

        def more(carry):
            return jnp.logical_and(carry[0] <= i, carry[2] > 0)

        def down(n, afters, mask):
            ks = pl.multiple_of((i - n) * KB, KB)
            rows = pl.ds(ks, KB)
            new_afters = []
            scores = [_sb_scores(_head(q_ref, h), _head(k_ref, h, rows), scale, mask) for h in range(hg)]
            withins = _split_dot([log_not for _, log_not in scores], suffix)
            for h in range(hg):
                do = _head(do_ref, h).astype(BF16)
                z, log_not = scores[h]
                a = _sb_weights(z + withins[h] + afters[h], mask)
                da = lax.dot_general(do, _head(v_ref, h, rows), (((1,), (1,)), ((), ())),
                                     preferred_element_type=F32)
                g_ref[h, :, pl.ds(ks, KB)] = a * da
                z_ref[h, :, pl.ds(ks, KB)] = z
                dv_ref[rows, h * HEAD_DIM:(h + 1) * HEAD_DIM] += lax.dot_general(
                    a.astype(BF16), do, (((0,), (0,)), ((), ())), preferred_element_type=F32)
                new_afters.append(afters[h] + jnp.sum(log_not, axis=1, keepdims=True))
            return n + 1, tuple(new_afters), _weights_alive(functools.reduce(jnp.maximum, new_afters))

        diagonal = col < row
        first = down(jnp.int32(0), tuple(jnp.zeros((QB, 1), F32) for _ in range(hg)), diagonal)
        visited, _, _ = lax.while_loop(more, lambda carry: down(carry[0], carry[1], None), first)

        def up(kb, carry, mask):
            dqs, befores = carry
            ks = pl.multiple_of(kb * KB, KB)
            rows = pl.ds(ks, KB)
            new_dqs, new_befores = [], []
            gs = [g_ref[h, :, pl.ds(ks, KB)] for h in range(hg)]
            g_withins = _split_dot(gs, prefix)
            for h in range(hg):
                g = gs[h]
                z = z_ref[h, :, pl.ds(ks, KB)]
                g_upto = g_withins[h] + befores[h]
                dz = g - jax.nn.sigmoid(z) * g_upto
                dz = dz if mask is None else jnp.where(mask, dz, 0.0)
                dzs = (dz * scale).astype(BF16)
                new_dqs.append(dqs[h] + jnp.dot(dzs, _head(k_ref, h, rows), preferred_element_type=F32))
                dk_ref[rows, h * HEAD_DIM:(h + 1) * HEAD_DIM] += lax.dot_general(
                    dzs, _head(q_ref, h), (((0,), (0,)), ((), ())), preferred_element_type=F32)
                new_befores.append(befores[h] + jnp.sum(g, axis=1, keepdims=True))
            return tuple(new_dqs), tuple(new_befores)

        below = lax.fori_loop(i + 1 - visited, i, lambda kb, carry: up(kb, carry, None),
                              (tuple(jnp.zeros((QB, HEAD_DIM), F32) for _ in range(hg)),
                               tuple(jnp.zeros((QB, 1), F32) for _ in range(hg))))
        dqs, _ = up(i, below, diagonal)
        q_rows = pl.ds(pl.multiple_of(i * QB, QB), QB)
        for h in range(hg):
            du_ref[0, q_rows, h * HEAD_DIM:(h + 1) * HEAD_DIM] = dqs[h].astype(BF16)

        @pl.when(i == t // QB - 1)
        def _():
            du_ref[1] = dk_ref[...].astype(BF16)
            du_ref[2] = dv_ref[...].astype(BF16)

    wide = hg * HEAD_DIM
    tile = lambda off: pl.BlockSpec((QB, wide), lambda g, i: (i, off + g))
    strip = lambda off: pl.BlockSpec((t, wide), lambda g, i: (0, off + g))
    outs, phase_outs = _call(
        body, "attn_bwd", (ng, t // QB), [tile(ng), strip(2 * ng), strip(3 * ng), tile(ng), ANY],
        [pl.BlockSpec((3, t, wide), lambda g, i: (0, 0, g))], [jax.ShapeDtypeStruct(du.shape, du.dtype)],
        [pltpu.VMEM((hg, QB, t), F32), pltpu.VMEM((hg, QB, t), F32), pltpu.VMEM((t, wide), F32),
         pltpu.VMEM((t, wide), F32)], ("parallel", "arbitrary"), (ub, ub, ub, dmix, du), phases, in_place={4: 0})
    return outs[0], phase_outs


def _row_tile(rows, cols, pref_bytes=2 * 1024 * 1024):
    tr = max(8, pref_bytes // (4 * cols))
    while rows % tr:
        tr //= 2
    return max(tr, 1)


def _pair_sum(name, g, s, c_idx):
    _, _, r2, cols = g.shape
    tr = _row_tile(r2, cols)

    def body(c_ref, g_ref, s_ref, o_ref):
        o_ref[...] = (g_ref[...] + s_ref[...]).astype(BF16)

    return pl.pallas_call(
        body, name=name,
        grid_spec=pltpu.PrefetchScalarGridSpec(
            num_scalar_prefetch=1, grid=(N_CHIPS - 1, r2 // tr),
            in_specs=[pl.BlockSpec((None, None, tr, cols), lambda p, i, c: (c[1 + p], c[0], i, 0)),
                      pl.BlockSpec((None, tr, cols), lambda p, i, c: (c[1 + p], i, 0))],
            out_specs=pl.BlockSpec((None, tr, cols), lambda p, i, c: (c[1 + p], i, 0))),
        out_shape=jax.ShapeDtypeStruct((N_CHIPS, r2, cols), BF16),
        compiler_params=_params(("parallel", "parallel")),
    )(c_idx, g, s)


def _chip_sum(name, g, s, received, place):
    _, _, r2, cols = g.shape
    tr = _row_tile(r2, cols)
    counts = [r.shape[0] for r in received]

    def body(place_ref, g_ref, s_ref, *refs):
        total = g_ref[...] + s_ref[...]
        for r_ref, n in zip(refs[:-1], counts):
            for k in range(n):
                total = total + r_ref[k].astype(F32)
        refs[-1][...] = total

    return pl.pallas_call(
        body, name=name,
        grid_spec=pltpu.PrefetchScalarGridSpec(
            num_scalar_prefetch=1, grid=(r2 // tr,),
            in_specs=[pl.BlockSpec((None, None, tr, cols), lambda i, p: (p[0], p[1], i, 0)),
                      pl.BlockSpec((None, tr, cols), lambda i, p: (p[0], i, 0)),
                      *[pl.BlockSpec((n, tr, cols), lambda i, p: (0, i, 0)) for n in counts]],
            out_specs=pl.BlockSpec((None, tr, cols), lambda i, p: (p[1], i, 0))),
        out_shape=jax.ShapeDtypeStruct((2, r2, cols), F32),
        compiler_params=_params(("parallel",)),
    )(place, g, s, *received)


def _cast_into_slot(name, w, place):
    rows, cols = w.shape
    r2 = rows // 2
    tr = _row_tile(r2, cols)
    nb = r2 // tr

    def body(place_ref, w_ref, o_ref):
        o_ref[...] = w_ref[...].astype(BF16)

    return pl.pallas_call(
        body, name=name,
        grid_spec=pltpu.PrefetchScalarGridSpec(
            num_scalar_prefetch=1, grid=(2, nb),
            in_specs=[pl.BlockSpec((tr, cols), lambda h, i, s: (h * nb + i, 0))],
            out_specs=pl.BlockSpec((None, None, tr, cols), lambda h, i, s: (s[0], h, i, 0))),
        out_shape=jax.ShapeDtypeStruct((N_CHIPS, 2, r2, cols), BF16),
        compiler_params=_params(("parallel", "parallel")),
    )(place, w)


def _colsum(name, a):
    def body(a_ref, o_ref):
        o_ref[...] = jnp.sum(a_ref[...], axis=0, keepdims=True)

    whole = lambda shape: pl.BlockSpec(shape, lambda i: (0, 0))
    return pl.pallas_call(
        body, name=name, grid=(1,), in_specs=[whole(a.shape)], out_specs=whole((1, a.shape[1])),
        out_shape=jax.ShapeDtypeStruct((1, a.shape[1]), F32), compiler_params=_params(("arbitrary",)),
    )(a)


def _adamw(name, w, g, m, v):
    rows, cols = w.shape
    tr = _row_tile(rows, cols, 1024 * 1024)

    def body(w_ref, g_ref, m_ref, v_ref, g_out_ref, d_ref, nm_ref, nv_ref):
        g_t = g_ref[...]
        m_t = ADAM_B1 * m_ref[...] + (1.0 - ADAM_B1) * g_t
        v_t = ADAM_B2 * v_ref[...] + (1.0 - ADAM_B2) * jnp.square(g_t)
        m_hat = m_t / (1.0 - ADAM_B1 ** ADAM_STEP)
        v_hat = v_t / (1.0 - ADAM_B2 ** ADAM_STEP)
        g_out_ref[...] = g_t
        d_ref[...] = -ADAM_LR * (m_hat / (jnp.sqrt(v_hat) + ADAM_EPS) + ADAM_WD * w_ref[...])
        nm_ref[...] = m_t
        nv_ref[...] = v_t

    spec = pl.BlockSpec((tr, cols), lambda i: (i, 0))
    shape = jax.ShapeDtypeStruct((rows, cols), F32)
    return pl.pallas_call(
        body, name=name, grid=(rows // tr,), in_specs=[spec] * 4, out_specs=[spec] * 4, out_shape=[shape] * 4,
        compiler_params=_params(("parallel",)),
    )(w, g, m, v)


def _all_reduce_small(packed, phases=()):
    rows, cols = packed.shape

    def body(in_ref, out_ref, all_ref, send_sems, recv_sems):
        x, y, c = _pos()
        me = 4 * x + 2 * y + c
        all_ref[me] = in_ref[...]
        cps = []
        for r in range(1, N_DEV):
            bx, by, bc = (r >> 2) & 1, (r >> 1) & 1, r & 1
            peer = (1 - x if bx else x, 1 - y if by else y, 1 - c if bc else c)
            cp = pltpu.make_async_remote_copy(
                src_ref=in_ref, dst_ref=all_ref.at[me], send_sem=send_sems.at[r - 1], recv_sem=recv_sems.at[r - 1],
                device_id=peer, device_id_type=MESH)
            cp.start()
            cps.append(cp)
        for cp in cps:
            cp.wait()
        total = all_ref[0]
        for d in range(1, N_DEV):
            total = total + all_ref[d]
        out_ref[...] = total

    vmem = pl.BlockSpec(memory_space=pltpu.VMEM)
    outs, phase_outs = _call(
        body, "all_reduce_small", (1,), [vmem], [vmem], [jax.ShapeDtypeStruct((rows, cols), F32)],
        [pltpu.VMEM((N_DEV, rows, cols), F32), pltpu.SemaphoreType.DMA((N_DEV - 1,)),
         pltpu.SemaphoreType.DMA((N_DEV - 1,))], ("arbitrary",), (packed,), phases)
    return outs[0], phase_outs


def kernel(x, ln_in_g, ln_in_b, w_in, w_pool, pool_scale, w_out, ln1_g, ln1_b, w_ff1, b_ff1, w_ff2, b_ff2, ln2_g, ln2_b, loss_target, m_ln_in_g, m_ln_in_b, m_w_in, m_w_pool, m_pool_scale, m_w_out, m_ln1_g, m_ln1_b, m_w_ff1, m_b_ff1, m_w_ff2, m_b_ff2, m_ln2_g, m_ln2_b, v_ln_in_g, v_ln_in_b, v_w_in, v_w_pool, v_pool_scale, v_w_out, v_ln1_g, v_ln1_b, v_w_ff1, v_b_ff1, v_w_ff2, v_b_ff2, v_ln2_g, v_ln2_b):
    t, d = x.shape[1], x.shape[2]
    pw = d // 2
    n_grp = len(POOL_WINDOWS)
    gw = pw // n_grp
    gwc = gw // N_CHIPS
    nh = pw // HEAD_DIM
    ff = w_ff1.shape[2] * N_CHIPS
    assert w_in.shape[0] == 1 and w_in.shape[2] * N_CHIPS == 2 * d and gwc <= 128

    x_idx, y_idx, c_idx = _pos()
    chip_arr = jnp.reshape(2 * x_idx + y_idx, (1,)).astype(jnp.int32)
    c_arr = jnp.reshape(c_idx, (1,)).astype(jnp.int32)
    place = jnp.concatenate([chip_arr, c_arr])
    core_and_others = jnp.stack([c_idx, *[2 * cx + cy for cx, cy in _other_chips(x_idx, y_idx)]]).astype(jnp.int32)

    xs = x.reshape(t, d)
    target = loss_target.reshape(t, d)
    row = lambda vec: vec.reshape(1, -1)

    scale_tile = jnp.zeros((1, 8, 128), F32).at[0, :n_grp, :gwc].set(pool_scale[0])
    scale_slots = lax.dynamic_update_slice(jnp.zeros((N_CHIPS, 8, 128), F32), scale_tile, (chip_arr[0], 0, 0))
    shards = dict(w_in=w_in[0], w_out=w_out[0], w_ff1=w_ff1[0], w_ff2=w_ff2[0], w_pool=w_pool[0].reshape(gw, gw))
    slot = {nm: _cast_into_slot("cast_" + nm, w, place) for nm, w in shards.items()}
    unsplit = lambda s: s.reshape(N_CHIPS, 2 * s.shape[2], s.shape[3])

    (h0, h0b, xhat0, rstd0), ((win_s,), (wpool_s,), (scale_g,)) = _ln_fwd(
        "ln_in_fwd", xs, row(ln_in_g), row(ln_in_b),
        phases=[_gather_ici_phase(slot["w_in"]), _gather_ici_phase(slot["w_pool"]), _gather_whole_phase(scale_slots)])
    (win_s,), (wpool_s,) = _comm_call("gather_d2d_first", [_gather_d2d_phase(win_s), _gather_d2d_phase(wpool_s)])
    win_g = unsplit(win_s)
    wpool_full = unsplit(wpool_s).reshape(N_CHIPS, n_grp, gwc, gw).transpose(1, 0, 2, 3).reshape(n_grp, gw, gw)
    scale_full = scale_g[:, :n_grp, :gwc].transpose(1, 0, 2).reshape(n_grp, 1, gw)

    def store_f32(acc, extra_refs, out_refs):
        out_refs[0][...] = acc

    def pool_f32_all_bf16(acc, extra_refs, out_refs):
        @pl.when(pl.program_id(1) == 0)
        def _():
            out_refs[0][...] = acc

        out_refs[1][...] = acc.astype(BF16)

    assert w_in.shape[2] == pw
    (u, ub), ((wout_s,), (wff1_s,)) = _mm_nn(
        "in_proj", h0b, win_g, (), lambda tm, tn: [],
        [jax.ShapeDtypeStruct((t, pw), F32), jax.ShapeDtypeStruct((t, 2 * d), BF16)],
        lambda tm, tn: [pl.BlockSpec((tm, tn), lambda i, j, kk: (i, 0)), _tile_spec(tm, tn)],
        pool_f32_all_bf16, b_chips=True, tn=pw,
        phases=[_gather_ici_phase(slot["w_out"]), _gather_ici_phase(slot["w_ff1"], rows=(0, 1, 8))])
    mix_in = _pool_fwd(u, wpool_full, scale_full, t, pw)
    mix_in, ((wff1_s,), (wout_s,)) = _attn_fwd(
        ub, mix_in, t, nh, min(nh, 4),
        phases=[_gather_ici_phase(wff1_s, rows=(1, 6, 8)), _gather_d2d_phase(wout_s)])
    same_part = lambda block: block
    wout_full = unsplit(wout_s).reshape(d, d)

    def residual(acc, extra_refs, out_refs):
        out_refs[0][...] = ALPHA * extra_refs[0][...] + acc

    (r1,), ((wff1_s,), (wff2_s,)) = _mm_nn(
        "out_proj", mix_in, wout_full, (h0,), lambda tm, tn: [_tile_spec(tm, tn)],
        [jax.ShapeDtypeStruct((t, d), F32)], lambda tm, tn: [_tile_spec(tm, tn)], residual, a_parts=same_part,
        phases=[_gather_ici_phase(wff1_s, rows=(7, 1, 8)), _gather_ici_phase(slot["w_ff2"], rows=(0, 1, 8))])
    (h1, h1b, xhat1, rstd1), ((wff1_s,), (wff2_s,)) = _ln_fwd(
        "ln1_fwd", r1, ln1_g, ln1_b, phases=[_gather_d2d_phase(wff1_s), _gather_ici_phase(wff2_s, rows=(1, 1, 8))])
    wff1_g = unsplit(wff1_s)

    def relu_sq(acc, extra_refs, out_refs):
        p = jnp.maximum(acc + extra_refs[0][...], 0.0)
        out_refs[0][...] = p
        out_refs[1][...] = jnp.square(p).astype(BF16)

    (relu_z, act_b), ((wff2_s,),) = _mm_nn(
        "ff1", h1b, wff1_g, (b_ff1,), lambda tm, tn: [_row_spec(tn)],
        [jax.ShapeDtypeStruct((t, ff), F32), jax.ShapeDtypeStruct((t, ff), BF16)],
        lambda tm, tn: [_tile_spec(tm, tn)] * 2, relu_sq, b_chips=True,
        phases=[_gather_ici_phase(wff2_s, rows=(2, 6, 8))])
    ((wff2_s,),) = _comm_call("gather_d2d_last", [_gather_d2d_phase(wff2_s)])
    wff2_full = unsplit(wff2_s).reshape(ff, d)

    def residual_bias(acc, extra_refs, out_refs):
        out_refs[0][...] = ALPHA * extra_refs[0][...] + (acc + extra_refs[1][...])

    r2 = _mm_nn("ff2", act_b, wff2_full, (h1, b_ff2), lambda tm, tn: [_tile_spec(tm, tn), _row_spec(tn)],
                [jax.ShapeDtypeStruct((t, d), F32)], lambda tm, tn: [_tile_spec(tm, tn)], residual_bias)[0]

    dr2, dr2b, loss_tile, g_ln2_g, g_ln2_b, g_b_ff2 = _ln2_loss_bwd(r2, target, ln2_g, ln2_b)

    halves = lambda g: g.reshape(N_CHIPS, 2, g.shape[1] // 2, g.shape[2])
    g_ff2 = halves(_mm_tn("grad_w_ff2", act_b, dr2b).reshape(N_CHIPS, ff // N_CHIPS, d))

    def relu_sq_bwd(acc, extra_refs, out_refs):
        dz = acc * (2.0 * extra_refs[0][...])
        out_refs[0][...] = dz.astype(BF16)
        rows = lax.broadcasted_iota(jnp.int32, out_refs[1].shape, 0)
        out_refs[1][...] = jnp.where(rows == 0, jnp.sum(dz, axis=0, keepdims=True), 0.0)

    tm_ff = _tile(t, 1024)
    (dz1b, g_b_ff1_parts), ((s_ff2,),) = _mm_nt(
        "ff2_bwd", dr2b, wff2_full, (relu_z,), lambda tm, tn: [_tile_spec(tm, tn)],
        [jax.ShapeDtypeStruct((t, ff), BF16), jax.ShapeDtypeStruct((8 * (t // tm_ff), ff), F32)],
        lambda tm, tn: [_tile_spec(tm, tn), pl.BlockSpec((8, tn), lambda i, j, kk: (i, j))], relu_sq_bwd,
        phases=[_swap_phase(g_ff2)])
    p_ff2 = _pair_sum("pair_sum_w_ff2", g_ff2, s_ff2, core_and_others)
    g_ff1, ((r_ff2_a,),) = _mm_tn("grad_w_ff1", h1b, dz1b, out_chips=True,
                                  phases=[_scatter_phase(p_ff2, others=(0, 1))])
    g_ff1 = halves(g_ff1)

    def plus_alpha(acc, extra_refs, out_refs):
        out_refs[0][...] = ALPHA * extra_refs[0][...] + acc

    (dh1,), ((s_ff1,), (r_ff2_b,)) = _mm_nt(
        "ff1_bwd", dz1b, wff1_g, (dr2,), lambda tm, tn: [_tile_spec(tm, tn)],
        [jax.ShapeDtypeStruct((t, d), F32)], lambda tm, tn: [_tile_spec(tm, tn)], plus_alpha, b_chips=True,
        phases=[_swap_phase(g_ff1), _scatter_phase(p_ff2, others=(2,))])
    q_ff2 = _chip_sum("chip_sum_w_ff2", g_ff2, s_ff2, [r_ff2_a, r_ff2_b], place)
    p_ff1 = _pair_sum("pair_sum_w_ff1", g_ff1, s_ff1, core_and_others)
    (dr1, dr1b, g_ln1_g, g_ln1_b), ((q_ff2,),) = _ln_bwd("ln1_bwd", dh1, xhat1, rstd1, ln1_g,
                                                         phases=[_assemble_phase(q_ff2)])

    g_out = halves(_mm_tn("grad_w_out", mix_in, dr1b, a_parts=same_part).reshape(N_CHIPS, d // N_CHIPS, d))
    (dmix,), ((s_out,),) = _mm_nt(
        "out_proj_bwd", dr1b, wout_full, (), lambda tm, tn: [], [jax.ShapeDtypeStruct((t, d), F32)],
        lambda tm, tn: [_tile_spec(tm, tn)], store_f32, phases=[_swap_phase(g_out)])
    p_out = _pair_sum("pair_sum_w_out", g_out, s_out, core_and_others)
    du, g_w_pool_full, g_scale_full = _pool_bwd(u, dmix, wpool_full, scale_full, t, pw)
    du, ((r_ff1,), (r_out,)) = _attn_bwd(ub, dmix, du, t, nh, min(nh, 2),
                                         phases=[_scatter_phase(p_ff1), _scatter_phase(p_out)])
    q_ff1 = _chip_sum("chip_sum_w_ff1", g_ff1, s_ff1, [r_ff1], place)
    q_out = _chip_sum("chip_sum_w_out", g_out, s_out, [r_out], place)
    g_in, ((q_ff1,), (q_out,)) = _mm_tn("grad_w_in", h0b, du, out_chips=True, b_parts=_du_part,
                                        phases=[_assemble_phase(q_ff1), _assemble_phase(q_out)])
    g_in = halves(g_in)
    g_pool = halves(g_w_pool_full.reshape(n_grp, N_CHIPS, gwc, gw).transpose(1, 0, 2, 3).reshape(N_CHIPS, gw, gw))
    (s_in,), (s_pool,) = _comm_call("rs_swap_last", [_swap_phase(g_in), _swap_phase(g_pool)])
    p_in = _pair_sum("pair_sum_w_in", g_in, s_in, core_and_others)
    p_pool = _pair_sum("pair_sum_w_pool", g_pool, s_pool, core_and_others)
    (dh0,), ((r_in,), (r_pool,)) = _mm_nt(
        "in_proj_bwd", du, win_g, (dr1,), lambda tm, tn: [_tile_spec(tm, tn)],
        [jax.ShapeDtypeStruct((t, d), F32)], lambda tm, tn: [_tile_spec(tm, tn)], plus_alpha, b_chips=True,
        a_parts=_du_part, phases=[_scatter_phase(p_in), _scatter_phase(p_pool)])
    q_in = _chip_sum("chip_sum_w_in", g_in, s_in, [r_in], place)
    q_pool = _chip_sum("chip_sum_w_pool", g_pool, s_pool, [r_pool], place)
    dx, _, g_ln_in_g, g_ln_in_b = _ln_bwd("ln_in_bwd", dh0, xhat0, rstd0, row(ln_in_g))

    lane = 2048 if d % 2048 == 0 else d
    small_names = ["ln_in_g", "ln_in_b", "ln1_g", "ln1_b", "b_ff1", "b_ff2", "ln2_g", "ln2_b"]
    small_w = dict(ln_in_g=ln_in_g, ln_in_b=ln_in_b, ln1_g=ln1_g, ln1_b=ln1_b, b_ff1=b_ff1, b_ff2=b_ff2, ln2_g=ln2_g,
                   ln2_b=ln2_b)
    small_m = dict(ln_in_g=m_ln_in_g, ln_in_b=m_ln_in_b, ln1_g=m_ln1_g, ln1_b=m_ln1_b, b_ff1=m_b_ff1, b_ff2=m_b_ff2,
                   ln2_g=m_ln2_g, ln2_b=m_ln2_b)
    small_v = dict(ln_in_g=v_ln_in_g, ln_in_b=v_ln_in_b, ln1_g=v_ln1_g, ln1_b=v_ln1_b, b_ff1=v_b_ff1, b_ff2=v_b_ff2,
                   ln2_g=v_ln2_g, ln2_b=v_ln2_b)
    small_g = dict(ln_in_g=g_ln_in_g, ln_in_b=g_ln_in_b, ln1_g=g_ln1_g, ln1_b=g_ln1_b, b_ff2=g_b_ff2, ln2_g=g_ln2_g,
                   ln2_b=g_ln2_b)

    def pack(parts):
        flat = jnp.concatenate([p.reshape(-1) for p in parts])
        n_rows = -(-flat.shape[0] // lane)
        n_rows = -(-n_rows // 8) * 8
        return jnp.pad(flat, (0, n_rows * lane - flat.shape[0])).reshape(n_rows, lane)

    small_g["b_ff1"] = _colsum("b_ff1_colsum", g_b_ff1_parts)
    summed, ((q_in,), (q_pool,)) = _all_reduce_small(
        pack([small_g[nm] for nm in small_names] + [g_scale_full, loss_tile[0, :1]]),
        phases=[_assemble_phase(q_in), _assemble_phase(q_pool)])
    summed = summed.reshape(-1)

    big = {}
    for nm, q, w, m, v in [("w_in", q_in, w_in, m_w_in, v_w_in), ("w_out", q_out, w_out, m_w_out, v_w_out),
                           ("w_ff1", q_ff1, w_ff1, m_w_ff1, v_w_ff1), ("w_ff2", q_ff2, w_ff2, m_w_ff2, v_w_ff2),
                           ("w_pool", q_pool, w_pool, m_w_pool, v_w_pool)]:
        g = q.reshape(2 * q.shape[1], q.shape[2])
        flat = lambda arr: arr.reshape(g.shape)
        big[nm] = tuple(arr.reshape(w.shape) for arr in _adamw("adamw_" + nm, flat(w), g, flat(m), flat(v)))

    g_small, off = {}, 0
    for nm in small_names:
        g_small[nm] = summed[off:off + small_w[nm].size]
        off += small_w[nm].size
    g_scale_all = summed[off:off + n_grp * gw].reshape(n_grp, N_CHIPS, gwc)
    loss = summed[off + n_grp * gw]
    g_scale = lax.dynamic_index_in_dim(g_scale_all, chip_arr[0], axis=1, keepdims=False)

    order = small_names + ["pool_scale"]
    small_w["pool_scale"], small_m["pool_scale"], small_v["pool_scale"] = pool_scale, m_pool_scale, v_pool_scale
    g_small["pool_scale"] = g_scale
    _, delta_s, new_m_s, new_v_s = _adamw("adamw_small", pack([small_w[nm] for nm in order]),
                                          pack([g_small[nm] for nm in order]), pack([small_m[nm] for nm in order]),
                                          pack([small_v[nm] for nm in order]))
    small = {}
    off = 0
    for nm in order:
        size, shape = small_w[nm].size, small_w[nm].shape
        cut = lambda arr: arr.reshape(-1)[off:off + size].reshape(shape)
        small[nm] = (g_small[nm].reshape(shape), cut(delta_s), cut(new_m_s), cut(new_v_s))
        off += size

    every = {**big, **small}
    weight_order = ["ln_in_g", "ln_in_b", "w_in", "w_pool", "pool_scale", "w_out", "ln1_g", "ln1_b", "w_ff1", "b_ff1",
                    "w_ff2", "b_ff2", "ln2_g", "ln2_b"]
    grads = [every[nm][0] for nm in weight_order]
    deltas = [every[nm][1] for nm in weight_order]
    new_ms = [every[nm][2] for nm in weight_order]
    new_vs = [every[nm][3] for nm in weight_order]
    return (loss, dx.reshape(x.shape), *grads, *deltas, *new_ms, *new_vs)
```

```python
import functools

import jax
import jax.numpy as jnp
from jax import lax
from jax.experimental import pallas as pl
from jax.experimental.pallas import tpu as pltpu

F32 = jnp.float32
BF16 = jnp.bfloat16
MESH = pl.DeviceIdType.MESH

HEAD_DIM = 128
POOL_WINDOWS = (2, 4, 8, 16)
POOL_HALO = 16
LN_EPS = 1e-5
ALPHA = 2.0 ** 0.25
ADAM_LR, ADAM_B1, ADAM_B2, ADAM_EPS, ADAM_WD, ADAM_STEP = 0.001, 0.9, 0.999, 1e-08, 0.01, 10

QB = 256
KB = 256
VMEM_LIMIT = 56 * 1024 * 1024
N_CHIPS = 4
N_DEV = 8


def _params(sem=None):
    return pltpu.CompilerParams(dimension_semantics=sem, vmem_limit_bytes=VMEM_LIMIT)


def _tile(dim, pref):
    return pref if dim % pref == 0 else dim


def _pos():
    return lax.axis_index("x"), lax.axis_index("y"), lax.axis_index("c")


def _other_chips(x, y):
    return [(1 - x, y), (x, 1 - y), (1 - x, 1 - y)]


ANY = pl.BlockSpec(memory_space=pl.ANY)


class _Phase:
    def __init__(self, ins, out_shapes, aliases, n_sems, build):
        self.ins, self.out_shapes, self.aliases, self.n_sems, self.build = ins, out_shapes, aliases, n_sems, build


def _remote(src, dst, send_sems, recv_sems, k, to):
    return pltpu.make_async_remote_copy(src_ref=src, dst_ref=dst, send_sem=send_sems.at[k], recv_sem=recv_sems.at[k],
                                        device_id=to, device_id_type=MESH)


def _swap_phase(g):
    def build(ins, outs, ss, rs):
        x, y, c = _pos()
        cp = _remote(ins[0].at[:, 1 - c], outs[0], ss, rs, 0, (x, y, 1 - c))
        return [cp], [cp]

    return _Phase([g], [jax.ShapeDtypeStruct((N_CHIPS, g.shape[2], g.shape[3]), g.dtype)], {}, 1, build)


ALL_OTHERS = (0, 1, 2)


def _scatter_phase(p, others=ALL_OTHERS):
    def build(ins, outs, ss, rs):
        x, y, c = _pos()
        chips = _other_chips(x, y)
        cps = [_remote(ins[0].at[2 * chips[j][0] + chips[j][1]], outs[0].at[k], ss, rs, k, (*chips[j], c))
               for k, j in enumerate(others)]
        return cps, cps

    return _Phase([p], [jax.ShapeDtypeStruct((len(others), p.shape[1], p.shape[2]), p.dtype)], {}, len(others), build)


def _assemble_phase(q):
    def build(ins, outs, ss, rs):
        x, y, c = _pos()
        mine, other = outs[0].at[c], outs[0].at[1 - c]
        return [_remote(mine, mine, ss, rs, 0, (x, y, 1 - c))], [_remote(other, other, ss, rs, 0, (x, y, c))]

    return _Phase([q], [jax.ShapeDtypeStruct(q.shape, q.dtype)], {0: 0}, 1, build)


def _gather_ici_phase(slot, others=ALL_OTHERS, rows=(0, 1, 1)):
    chunk = slot.shape[2] // rows[2]
    span = pl.ds(rows[0] * chunk, rows[1] * chunk)

    def build(ins, outs, ss, rs):
        x, y, c = _pos()
        chips = _other_chips(x, y)
        mine = outs[0].at[2 * x + y, c, span]
        sends, recvs = [], []
        for k, j in enumerate(others):
            theirs = outs[0].at[2 * chips[j][0] + chips[j][1], c, span]
            sends.append(_remote(mine, mine, ss, rs, k, (*chips[j], c)))
            recvs.append(_remote(theirs, theirs, ss, rs, k, (x, y, c)))
        return sends, recvs

    return _Phase([slot], [jax.ShapeDtypeStruct(slot.shape, slot.dtype)], {0: 0}, len(others), build)


def _gather_d2d_phase(slot):
    def build(ins, outs, ss, rs):
        x, y, c = _pos()
        sends, recvs = [], []
        for j, chip in enumerate(_other_chips(x, y)):
            landed = outs[0].at[2 * chip[0] + chip[1], c]
            coming = outs[0].at[2 * chip[0] + chip[1], 1 - c]
            sends.append(_remote(landed, landed, ss, rs, j, (x, y, 1 - c)))
            recvs.append(_remote(coming, coming, ss, rs, j, (x, y, c)))
        return sends, recvs

    return _Phase([slot], [jax.ShapeDtypeStruct(slot.shape, slot.dtype)], {0: 0}, 3, build)


def _gather_whole_phase(slots):
    def build(ins, outs, ss, rs):
        x, y, c = _pos()
        mine = outs[0].at[2 * x + y]
        sends, recvs = [], []
        for j, chip in enumerate(_other_chips(x, y)):
            theirs = outs[0].at[2 * chip[0] + chip[1]]
            sends.append(_remote(mine, mine, ss, rs, j, (*chip, c)))
            recvs.append(_remote(theirs, theirs, ss, rs, j, (x, y, c)))
        return sends, recvs

    return _Phase([slots], [jax.ShapeDtypeStruct(slots.shape, slots.dtype)], {0: 0}, 3, build)


def _split_refs(refs, n_in, n_out, n_scratch, phases):
    n_pin = sum(len(ph.ins) for ph in phases)
    n_pout = sum(len(ph.out_shapes) for ph in phases)
    cuts = [n_in, n_pin, n_out, n_pout, n_scratch]
    parts, at = [], 0
    for n in cuts:
        parts.append(refs[at:at + n])
        at += n
    parts.append(refs[at:])
    return parts


def _build_phases(phases, pin, pout, sems):
    built, i, o = [], 0, 0
    for k, ph in enumerate(phases):
        built.append(ph.build(pin[i:i + len(ph.ins)], pout[o:o + len(ph.out_shapes)], sems[2 * k], sems[2 * k + 1]))
        i += len(ph.ins)
        o += len(ph.out_shapes)
    return built


def _finish_phases(built):
    for _, recvs in built:
        for cp in recvs:
            cp.wait_recv()
    for sends, _ in built:
        for cp in sends:
            cp.wait_send()


def _call(body, name, grid, in_specs, out_specs, out_shape, scratch_shapes, semantics, args, phases=(),
          in_place=None):
    n_in, n_out, n_scratch = len(args), len(out_shape), len(scratch_shapes)
    aliases, in_at, out_at = dict(in_place or {}), n_in, n_out
    for ph in phases:
        aliases.update({in_at + i: out_at + o for i, o in ph.aliases.items()})
        in_at += len(ph.ins)
        out_at += len(ph.out_shapes)

    def hosted(*refs):
        ins, pin, outs, pout, scratch, sems = _split_refs(refs, n_in, n_out, n_scratch, phases)
        ids = [pl.program_id(a) for a in range(len(grid))]
        first = functools.reduce(jnp.logical_and, [i == 0 for i in ids])
        last = functools.reduce(jnp.logical_and, [i == g - 1 for i, g in zip(ids, grid)])

        @pl.when(first)
        def _():
            for sends, _ in _build_phases(phases, pin, pout, sems):
                for cp in sends:
                    cp.start()

        body(*ins, *outs, *scratch)

        @pl.when(last)
        def _():
            _finish_phases(_build_phases(phases, pin, pout, sems))

    p_args = [a for ph in phases for a in ph.ins]
    p_shapes = [s for ph in phases for s in ph.out_shapes]
    sem_shapes = [pltpu.SemaphoreType.DMA((ph.n_sems,)) for ph in phases for _ in range(2)]
    outs = pl.pallas_call(
        hosted if phases else body, name=name, grid=grid, in_specs=[*in_specs, *[ANY] * len(p_args)],
        out_specs=[*out_specs, *[ANY] * len(p_shapes)], out_shape=[*out_shape, *p_shapes],
        input_output_aliases=aliases, scratch_shapes=[*scratch_shapes, *sem_shapes],
        compiler_params=_params(("arbitrary",) * len(grid) if phases else semantics),
    )(*args, *p_args)
    phase_outs, at = [], n_out
    for ph in phases:
        phase_outs.append(list(outs[at:at + len(ph.out_shapes)]))
        at += len(ph.out_shapes)
    return list(outs[:n_out]), phase_outs


def _comm_call(name, phases):
    def body(*refs):
        _, pin, _, pout, _, sems = _split_refs(refs, 0, 0, 0, phases)
        built = _build_phases(phases, pin, pout, sems)
        for sends, _ in built:
            for cp in sends:
                cp.start()
        _finish_phases(built)

    aliases, in_at, out_at = {}, 0, 0
    for ph in phases:
        aliases.update({in_at + i: out_at + o for i, o in ph.aliases.items()})
        in_at += len(ph.ins)
        out_at += len(ph.out_shapes)
    p_args = [a for ph in phases for a in ph.ins]
    p_shapes = [s for ph in phases for s in ph.out_shapes]
    outs = pl.pallas_call(
        body, name=name, in_specs=[ANY] * len(p_args), out_specs=[ANY] * len(p_shapes), out_shape=p_shapes,
        input_output_aliases=aliases,
        scratch_shapes=[pltpu.SemaphoreType.DMA((ph.n_sems,)) for ph in phases for _ in range(2)],
    )(*p_args)
    phase_outs, at = [], 0
    for ph in phases:
        phase_outs.append(list(outs[at:at + len(ph.out_shapes)]))
        at += len(ph.out_shapes)
    return phase_outs


def _matmul(name, a, b, grid, a_spec, b_spec, contract, acc_shape, extras, extra_specs, out_shape, out_specs,
            epilogue, phases=()):
    n_extra, n_out, gk = len(extras), len(out_shape), grid[2]

    def product(a_ref, b_ref):
        return lax.dot_general(a_ref[...], b_ref[...], (contract, ((), ())), preferred_element_type=F32)

    def body_one_step(*refs):
        epilogue(product(refs[0], refs[1]), refs[2:2 + n_extra], refs[2 + n_extra:])

    def body(*refs):
        a_ref, b_ref = refs[0], refs[1]
        extra_refs = refs[2:2 + n_extra]
        out_refs = refs[2 + n_extra:2 + n_extra + n_out]
        acc_ref = refs[-1]
        kk = pl.program_id(2)

        @pl.when(kk == 0)
        def _():
            acc_ref[...] = product(a_ref, b_ref)

        @pl.when(kk > 0)
        def _():
            acc_ref[...] += product(a_ref, b_ref)

        @pl.when(kk == gk - 1)
        def _():
            epilogue(acc_ref[...], extra_refs, out_refs)

    outs, phase_outs = _call(
        body_one_step if gk == 1 else body, name, grid, [a_spec, b_spec, *extra_specs], out_specs, out_shape,
        [] if gk == 1 else [pltpu.VMEM(acc_shape, F32)], ("parallel", "arbitrary", "arbitrary"), (a, b, *extras),
        phases)
    return (outs, phase_outs) if phases else outs


def _mm_nn(name, a, b, extras, extra_specs, out_shape, out_specs, epilogue, b_chips=False, tm=1024, tn=1024,
           tk=2048, phases=(), a_parts=None):
    m, k, tm, tk, a_spec = _lhs_rows_by_k(a, tm, tk, a_parts)
    n = b.shape[1] if not b_chips else b.shape[2] * N_CHIPS
    if b_chips:
        tn = _tile(b.shape[2], tn)
        nb = b.shape[2] // tn
        b_spec = pl.BlockSpec((None, tk, tn), lambda i, j, kk: (j // nb, kk, j % nb))
    else:
        tn = _tile(n, tn)
        b_spec = pl.BlockSpec((tk, tn), lambda i, j, kk: (kk, j))
    return _matmul(name, a, b, (m // tm, n // tn, k // tk), a_spec, b_spec, ((1,), (0,)), (tm, tn), extras,
                   extra_specs(tm, tn), out_shape, out_specs(tm, tn), epilogue, phases)


def _lhs_rows_by_k(a, tm, tk, a_parts, tk_max=None):
    if a_parts is None:
        m, k = a.shape
        tm, tk = _tile(m, tm), _tile(k if tk_max is None else tk_max, tk)
        return m, k, tm, tk, pl.BlockSpec((tm, tk), lambda i, j, kk: (i, kk))
    n_parts, m, kp = a.shape
    tm, tk = _tile(m, tm), _tile(kp if tk_max is None else min(kp, tk_max), tk)
    nb = kp // tk
    return m, n_parts * kp, tm, tk, pl.BlockSpec((None, tm, tk), lambda i, j, kk: (a_parts(kk // nb), i, kk % nb))


def _mm_nt(name, a, b, extras, extra_specs, out_shape, out_specs, epilogue, b_chips=False, tm=1024, tn=1024,
           tk=2048, phases=(), a_parts=None):
    m, k, tm, tk, a_spec = _lhs_rows_by_k(a, tm, tk, a_parts, tk_max=b.shape[2] if b_chips else None)
    n = b.shape[0] if not b_chips else b.shape[1]
    tn = _tile(n, tn)
    if b_chips:
        nb = b.shape[2] // tk
        b_spec = pl.BlockSpec((None, tn, tk), lambda i, j, kk: (kk // nb, j, kk % nb))
    else:
        b_spec = pl.BlockSpec((tn, tk), lambda i, j, kk: (j, kk))
    return _matmul(name, a, b, (m // tm, n // tn, k // tk), a_spec, b_spec, ((1,), (1,)), (tm, tn), extras,
                   extra_specs(tm, tn), out_shape, out_specs(tm, tn), epilogue, phases)


def _mm_tn(name, a, b, out_chips=False, tm=1024, tn=1024, tk=2048, phases=(), a_parts=None, b_parts=None):
    if a_parts is None:
        k, m = a.shape
        tm = _tile(m, tm)
        a_spec = pl.BlockSpec((_tile(k, tk), tm), lambda i, j, kk: (kk, i))
    else:
        n_parts, k, mp = a.shape
        m, tm = n_parts * mp, _tile(mp, tm)
        nbm = mp // tm
        a_spec = pl.BlockSpec((None, _tile(k, tk), tm), lambda i, j, kk: (a_parts(i // nbm), kk, i % nbm))
    tk = _tile(k, tk)
    n = b.shape[1] if b_parts is None else b.shape[0] * b.shape[2]
    if out_chips:
        nc = n // N_CHIPS
        tn = _tile(nc, tn)
        nb = nc // tn
        out_shape = [jax.ShapeDtypeStruct((N_CHIPS, m, nc), F32)]
        out_specs = [pl.BlockSpec((None, tm, tn), lambda i, j, kk: (j // nb, i, j % nb))]
    else:
        tn = _tile(n, tn)
        out_shape = [jax.ShapeDtypeStruct((m, n), F32)]
        out_specs = [pl.BlockSpec((tm, tn), lambda i, j, kk: (i, j))]
    if b_parts is None:
        b_spec = pl.BlockSpec((tk, tn), lambda i, j, kk: (kk, j))
    else:
        nbn = b.shape[2] // tn
        b_spec = pl.BlockSpec((None, tk, tn), lambda i, j, kk: (b_parts(j // nbn), kk, j % nbn))

    def epilogue(acc, extra_refs, out_refs):
        out_refs[0][...] = acc

    res = _matmul(name, a, b, (m // tm, n // tn, k // tk), a_spec, b_spec, ((0,), (0,)), (tm, tn), (), [],
                  out_shape, out_specs, epilogue, phases)
    return (res[0][0], res[1]) if phases else res[0]


def _tile_spec(tm, tn):
    return pl.BlockSpec((tm, tn), lambda i, j, kk: (i, j))


def _row_spec(tn):
    return pl.BlockSpec((1, tn), lambda i, j, kk: (0, j))


def _ln_stats(r):
    mu = jnp.mean(r, axis=-1, keepdims=True)
    var = jnp.mean(jnp.square(r - mu), axis=-1, keepdims=True)
    rstd = lax.rsqrt(var + LN_EPS)
    return (r - mu) * rstd, rstd


def _ln_fwd(name, r, g, b, tr=256, phases=()):
    t, d = r.shape
    tr = _tile(t, tr)

    def body(r_ref, g_ref, b_ref, y_ref, yb_ref):
        xhat, _ = _ln_stats(r_ref[...])
        y = xhat * g_ref[...] + b_ref[...]
        y_ref[...] = y
        yb_ref[...] = y.astype(BF16)

    row = pl.BlockSpec((tr, d), lambda i: (i, 0))
    vec = pl.BlockSpec((1, d), lambda i: (0, 0))
    outs, phase_outs = _call(
        body, name, (t // tr,), [row, vec, vec], [row, row],
        [jax.ShapeDtypeStruct((t, d), F32), jax.ShapeDtypeStruct((t, d), BF16)], [], ("parallel",), (r, g, b), phases)
    return (outs, phase_outs) if phases else outs


def _ln_bwd_rows(dy, xhat, rstd, g):
    dxhat = dy * g
    m1 = jnp.mean(dxhat, axis=-1, keepdims=True)
    m2 = jnp.mean(dxhat * xhat, axis=-1, keepdims=True)
    return rstd * (dxhat - m1 - xhat * m2)


def _ln_bwd(name, dy, r, g, tr=256, phases=()):
    t, d = dy.shape
    tr = _tile(t, tr)

    def body(dy_ref, r_ref, g_ref, dr_ref, drb_ref, dg_ref, db_ref):
        @pl.when(pl.program_id(0) == 0)
        def _():
            dg_ref[...] = jnp.zeros_like(dg_ref)
            db_ref[...] = jnp.zeros_like(db_ref)

        dy_t = dy_ref[...]
        xhat_t, rstd = _ln_stats(r_ref[...])
        dr = _ln_bwd_rows(dy_t, xhat_t, rstd, g_ref[...])
        dr_ref[...] = dr
        drb_ref[...] = dr.astype(BF16)
        dg_ref[...] += jnp.sum(dy_t * xhat_t, axis=0, keepdims=True)
        db_ref[...] += jnp.sum(dy_t, axis=0, keepdims=True)

    row = pl.BlockSpec((tr, d), lambda i: (i, 0))
    vec = pl.BlockSpec((1, d), lambda i: (0, 0))
    outs, phase_outs = _call(
        body, name, (t // tr,), [row, row, vec], [row, row, vec, vec],
        [jax.ShapeDtypeStruct((t, d), F32), jax.ShapeDtypeStruct((t, d), BF16),
         jax.ShapeDtypeStruct((1, d), F32), jax.ShapeDtypeStruct((1, d), F32)], [], ("arbitrary",),
        (dy, r, g), phases)
    return (outs, phase_outs) if phases else outs


def _ln2_loss_bwd(r2, target, g, b, tr=256):
    t, d = r2.shape
    tr = _tile(t, tr)

    def body(r_ref, t_ref, g_ref, b_ref, dr_ref, drb_ref, loss_ref, dg_ref, db_ref, dsum_ref):
        @pl.when(pl.program_id(0) == 0)
        def _():
            loss_ref[...] = jnp.zeros_like(loss_ref)
            dg_ref[...] = jnp.zeros_like(dg_ref)
            db_ref[...] = jnp.zeros_like(db_ref)
            dsum_ref[...] = jnp.zeros_like(dsum_ref)

        xhat, rstd = _ln_stats(r_ref[...])
        g_t = g_ref[...]
        err = xhat * g_t + b_ref[...] - t_ref[...]
        loss_ref[...] += 0.5 * jnp.sum(jnp.mean(jnp.square(err), axis=-1, keepdims=True), axis=0, keepdims=True)
        dy = err * (1.0 / d)
        dr = _ln_bwd_rows(dy, xhat, rstd, g_t)
        dr_ref[...] = dr
        drb_ref[...] = dr.astype(BF16)
        dg_ref[...] += jnp.sum(dy * xhat, axis=0, keepdims=True)
        db_ref[...] += jnp.sum(dy, axis=0, keepdims=True)
        dsum_ref[...] += jnp.sum(dr, axis=0, keepdims=True)

    row = pl.BlockSpec((tr, d), lambda i: (i, 0))
    vec = pl.BlockSpec((1, d), lambda i: (0, 0))
    return pl.pallas_call(
        body, name="ln2_loss_bwd", grid=(t // tr,), in_specs=[row, row, vec, vec],
        out_specs=[row, row, pl.BlockSpec((8, 128), lambda i: (0, 0)), vec, vec, vec],
        out_shape=[jax.ShapeDtypeStruct((t, d), F32), jax.ShapeDtypeStruct((t, d), BF16),
                   jax.ShapeDtypeStruct((8, 128), F32), jax.ShapeDtypeStruct((1, d), F32),
                   jax.ShapeDtypeStruct((1, d), F32), jax.ShapeDtypeStruct((1, d), F32)],
        compiler_params=_params(("arbitrary",)),
    )(r2, target, g, b)


POOL_ROWS = 512

DU_POOL = 3


def _du_part(block):
    return (block + DU_POOL) % 4


def _pool_mean_minus_token(u_ref, r0, rows, grp, first):
    width = u_ref.shape[1]
    body = u_ref[pl.ds(r0, rows), :]
    halo = u_ref[pl.ds(pl.multiple_of(jnp.maximum(r0 - POOL_HALO, 0), POOL_HALO), POOL_HALO), :]
    halo = jnp.where(first, 0.0, halo)
    full = jnp.concatenate([halo, body], axis=0)
    s = full
    for step in range(len(POOL_WINDOWS)):
        shifted = pltpu.roll(s, 1 << step, axis=0)
        s = s + jnp.where(grp >= step, shifted, 0.0)
    s = s[POOL_HALO:, :]
    tpos = r0 + lax.broadcasted_iota(jnp.int32, (rows, width), 0)
    count = jnp.minimum(tpos + 1, 2 << grp).astype(F32)
    return s / count - body, count


def _pool_fwd(u, w_pool, pool_scale, t, pw):
    gw = pw // len(POOL_WINDOWS)
    rows = _tile(t, POOL_ROWS)

    def body(u_ref, w_ref, s_ref, o_ref):
        grp = pl.program_id(0)

        def chunk(ci, carry):
            r0 = pl.multiple_of(ci * rows, rows)
            y, _ = _pool_mean_minus_token(u_ref, r0, rows, grp, ci == 0)
            yw = jnp.dot(y.astype(BF16), w_ref[...], preferred_element_type=F32)
            o_ref[pl.ds(r0, rows), :] = (yw * s_ref[...]).astype(BF16)
            return carry

        lax.fori_loop(0, t // rows, chunk, 0)

    return pl.pallas_call(
        body, name="pool_fwd", grid=(len(POOL_WINDOWS),),
        in_specs=[pl.BlockSpec((t, gw), lambda g: (0, g)), pl.BlockSpec((None, gw, gw), lambda g: (g, 0, 0)),
                  pl.BlockSpec((None, 1, gw), lambda g: (g, 0, 0))],
        out_specs=pl.BlockSpec((None, t, gw), lambda g: (0, 0, g)),
        out_shape=jax.ShapeDtypeStruct((2, t, pw), BF16),
        compiler_params=_params(("parallel",)),
    )(u, w_pool, pool_scale)


def _pool_bwd(u, dmix, w_pool, pool_scale, t, pw):
    n_grp = len(POOL_WINDOWS)
    gw = pw // n_grp
    rows = _tile(t, POOL_ROWS)

    def body(u_ref, dm_ref, w_ref, s_ref, du_ref, dw_ref, ds_ref, e_ref):
        grp = pl.program_id(0)
        dw_ref[...] = jnp.zeros_like(dw_ref)
        ds_ref[...] = jnp.zeros_like(ds_ref)
        e_ref[pl.ds(t, POOL_HALO), :] = jnp.zeros((POOL_HALO, gw), F32)

        def chunk(ci, carry):
            r0 = pl.multiple_of(ci * rows, rows)
            y, count = _pool_mean_minus_token(u_ref, r0, rows, grp, ci == 0)
            yb = y.astype(BF16)
            yw = jnp.dot(yb, w_ref[...], preferred_element_type=F32)
            dy2 = dm_ref[pl.ds(r0, rows), :]
            ds_ref[...] += jnp.sum(dy2 * yw, axis=0, keepdims=True)
            dyw = (dy2 * s_ref[...]).astype(BF16)
            dw_ref[...] += lax.dot_general(yb, dyw, (((0,), (0,)), ((), ())), preferred_element_type=F32)
            dy = lax.dot_general(dyw, w_ref[...], (((1,), (1,)), ((), ())), preferred_element_type=F32)
            e_ref[pl.ds(r0, rows), :] = dy / count
            return carry

        lax.fori_loop(0, t // rows, chunk, 0)

        def chunk2(ci, carry):
            r0 = pl.multiple_of(ci * rows, rows)
            full = e_ref[pl.ds(r0, rows + POOL_HALO), :]
            s = full
            for step in range(n_grp):
                shifted = pltpu.roll(s, rows + POOL_HALO - (1 << step), axis=0)
                s = s + jnp.where(grp >= step, shifted, 0.0)
            e = full[:rows, :]
            tpos = r0 + lax.broadcasted_iota(jnp.int32, (rows, gw), 0)
            count = jnp.minimum(tpos + 1, 2 << grp).astype(F32)
            du_ref[pl.ds(r0, rows), :] = (s[:rows, :] - e * count).astype(BF16)
            return carry

        lax.fori_loop(0, t // rows, chunk2, 0)

    return pl.pallas_call(
        body, name="pool_bwd", grid=(n_grp,),
        in_specs=[pl.BlockSpec((t, gw), lambda g: (0, g)), pl.BlockSpec((t, gw), lambda g: (0, g)),
                  pl.BlockSpec((None, gw, gw), lambda g: (g, 0, 0)),
                  pl.BlockSpec((None, 1, gw), lambda g: (g, 0, 0))],
        out_specs=[pl.BlockSpec((None, t, gw), lambda g: (DU_POOL, 0, g)),
                   pl.BlockSpec((None, gw, gw), lambda g: (g, 0, 0)), pl.BlockSpec((None, 1, gw), lambda g: (g, 0, 0))],
        out_shape=[jax.ShapeDtypeStruct((4, t, pw), BF16), jax.ShapeDtypeStruct((n_grp, gw, gw), F32),
                   jax.ShapeDtypeStruct((n_grp, 1, gw), F32)],
        scratch_shapes=[pltpu.VMEM((t + POOL_HALO, gw), F32)],
        compiler_params=_params(("parallel",)),
    )(u, dmix, w_pool, pool_scale)


def _sb_scores(q, k_blk, scale, mask):
    z = lax.dot_general(q, k_blk, (((1,), (1,)), ((), ())), preferred_element_type=F32) * scale
    log_not = jnp.minimum(-z, 0.0) - jnp.log(1.0 + jnp.exp(-jnp.abs(z)))
    return z, (log_not if mask is None else jnp.where(mask, log_not, 0.0))


def _sb_weights(e, mask):
    a = jnp.exp(e)
    return a if mask is None else jnp.where(mask, a, 0.0)


EXP_IS_ZERO_BELOW = -104.0


def _weights_alive(after):
    return (jnp.max(after) >= EXP_IS_ZERO_BELOW).astype(jnp.int32)


def _split_dot(vs, tri):
    parts = []
    for v in vs:
        hi = v.astype(BF16)
        parts += [hi, (v - hi.astype(F32)).astype(BF16)]
    prod = jnp.dot(jnp.concatenate(parts, axis=0), tri, preferred_element_type=F32)
    m = vs[0].shape[0]
    return [prod[2 * k * m:(2 * k + 1) * m] + prod[(2 * k + 1) * m:(2 * k + 2) * m] for k in range(len(vs))]


def _head(ref, h, rows=None):
    cols = slice(h * HEAD_DIM, (h + 1) * HEAD_DIM)
    return ref[:, cols] if rows is None else ref[rows, cols]


def _attn_fwd(ub, mix, t, nh, hg, phases=()):
    scale = float(1.0 / (HEAD_DIM ** 0.5))
    ng = nh // hg

    def body(q_ref, k_ref, v_ref, o_ref):
        i = pl.program_id(1)
        row = lax.broadcasted_iota(jnp.int32, (QB, KB), 0)
        col = lax.broadcasted_iota(jnp.int32, (QB, KB), 1)
        suffix = (row >= col).astype(BF16)

        def more(carry):
            return jnp.logical_and(carry[0] <= i, carry[3] > 0)

        def block(n, accs, afters, mask):
            rows = pl.ds(pl.multiple_of((i - n) * KB, KB), KB)
            new_accs, new_afters = [], []
            scores = [_sb_scores(_head(q_ref, h), _head(k_ref, h, rows), scale, mask) for h in range(hg)]
            withins = _split_dot([log_not for _, log_not in scores], suffix)
            for h in range(hg):
                z, log_not = scores[h]
                a = _sb_weights(z + withins[h] + afters[h], mask)
                new_accs.append(accs[h] + jnp.dot(a.astype(BF16), _head(v_ref, h, rows),
                                                  preferred_element_type=F32))
                new_afters.append(afters[h] + jnp.sum(log_not, axis=1, keepdims=True))
            return n + 1, tuple(new_accs), tuple(new_afters), _weights_alive(functools.reduce(jnp.maximum, new_afters))

        first = block(jnp.int32(0), tuple(jnp.zeros((QB, HEAD_DIM), F32) for _ in range(hg)),
                      tuple(jnp.zeros((QB, 1), F32) for _ in range(hg)), col < row)
        _, accs, _, _ = lax.while_loop(more, lambda carry: block(carry[0], carry[1], carry[2], None), first)
        for h in range(hg):
            o_ref[:, h * HEAD_DIM:(h + 1) * HEAD_DIM] = accs[h].astype(BF16)

    wide = hg * HEAD_DIM
    outs, phase_outs = _call(
        lambda q_ref, k_ref, v_ref, mix_ref, o_ref: body(q_ref, k_ref, v_ref, o_ref), "attn_fwd", (ng, t // QB),
        [pl.BlockSpec((QB, wide), lambda g, i: (i, ng + g)), pl.BlockSpec((t, wide), lambda g, i: (0, 2 * ng + g)),
         pl.BlockSpec((t, wide), lambda g, i: (0, 3 * ng + g)), ANY],
        [pl.BlockSpec((None, QB, wide), lambda g, i: (1, i, g))], [jax.ShapeDtypeStruct(mix.shape, mix.dtype)], [],
        ("parallel", "arbitrary"), (ub, ub, ub, mix), phases, in_place={3: 0})
    return outs[0], phase_outs


def _attn_bwd(ub, dmix, du, t, nh, hg, phases=()):
    scale = float(1.0 / (HEAD_DIM ** 0.5))
    ng = nh // hg

    def body(q_ref, k_ref, v_ref, do_ref, du_in_ref, du_ref, g_ref, z_ref, dk_ref, dv_ref):
        i = pl.program_id(1)

        @pl.when(i == 0)
        def _():
            dk_ref[...] = jnp.zeros_like(dk_ref)
            dv_ref[...] = jnp.zeros_like(dv_ref)

        row = lax.broadcasted_iota(jnp.int32, (QB, KB), 0)
        col = lax.broadcasted_iota(jnp.int32, (QB, KB), 1)
        suffix = (row >= col).astype(BF16)
        prefix = (row <= col).astype(BF16)

        def more(carry):
            return jnp.logical_and(carry[0] <= i, carry[2] > 0)

        def down(n, afters, mask):
            ks = pl.multiple_of((i - n) * KB, KB)
            rows = pl.ds(ks, KB)
            new_afters = []
            scores = [_sb_scores(_head(q_ref, h), _head(k_ref, h, rows), scale, mask) for h in range(hg)]
            withins = _split_dot([log_not for _, log_not in scores], suffix)
            for h in range(hg):
                do = _head(do_ref, h).astype(BF16)
                z, log_not = scores[h]
                a = _sb_weights(z + withins[h] + afters[h], mask)
                da = lax.dot_general(do, _head(v_ref, h, rows), (((1,), (1,)), ((), ())),
                                     preferred_element_type=F32)
                g_ref[h, :, pl.ds(ks, KB)] = a * da
                z_ref[h, :, pl.ds(ks, KB)] = z
                dv_ref[rows, h * HEAD_DIM:(h + 1) * HEAD_DIM] += lax.dot_general(
                    a.astype(BF16), do, (((0,), (0,)), ((), ())), preferred_element_type=F32)
                new_afters.append(afters[h] + jnp.sum(log_not, axis=1, keepdims=True))
            return n + 1, tuple(new_afters), _weights_alive(functools.reduce(jnp.maximum, new_afters))

        diagonal = col < row
        first = down(jnp.int32(0), tuple(jnp.zeros((QB, 1), F32) for _ in range(hg)), diagonal)
        visited, _, _ = lax.while_loop(more, lambda carry: down(carry[0], carry[1], None), first)

        def up(kb, carry, mask):
            dqs, befores = carry
            ks = pl.multiple_of(kb * KB, KB)
            rows = pl.ds(ks, KB)
            new_dqs, new_befores = [], []
            gs = [g_ref[h, :, pl.ds(ks, KB)] for h in range(hg)]
            g_withins = _split_dot(gs, prefix)
            for h in range(hg):
                g = gs[h]
                z = z_ref[h, :, pl.ds(ks, KB)]
                g_upto = g_withins[h] + befores[h]
                dz = g - jax.nn.sigmoid(z) * g_upto
                dz = dz if mask is None else jnp.where(mask, dz, 0.0)
                dzs = (dz * scale).astype(BF16)
                new_dqs.append(dqs[h] + jnp.dot(dzs, _head(k_ref, h, rows), preferred_element_type=F32))
                dk_ref[rows, h * HEAD_DIM:(h + 1) * HEAD_DIM] += lax.dot_general(
                    dzs, _head(q_ref, h), (((0,), (0,)), ((), ())), preferred_element_type=F32)
                new_befores.append(befores[h] + jnp.sum(g, axis=1, keepdims=True))
            return tuple(new_dqs), tuple(new_befores)

        below = lax.fori_loop(i + 1 - visited, i, lambda kb, carry: up(kb, carry, None),
                              (tuple(jnp.zeros((QB, HEAD_DIM), F32) for _ in range(hg)),
                               tuple(jnp.zeros((QB, 1), F32) for _ in range(hg))))
        dqs, _ = up(i, below, diagonal)
        q_rows = pl.ds(pl.multiple_of(i * QB, QB), QB)
        for h in range(hg):
            du_ref[0, q_rows, h * HEAD_DIM:(h + 1) * HEAD_DIM] = dqs[h].astype(BF16)

        @pl.when(i == t // QB - 1)
        def _():
            du_ref[1] = dk_ref[...].astype(BF16)
            du_ref[2] = dv_ref[...].astype(BF16)

    wide = hg * HEAD_DIM
    tile = lambda off: pl.BlockSpec((QB, wide), lambda g, i: (i, off + g))
    strip = lambda off: pl.BlockSpec((t, wide), lambda g, i: (0, off + g))
    outs, phase_outs = _call(
        body, "attn_bwd", (ng, t // QB), [tile(ng), strip(2 * ng), strip(3 * ng), tile(ng), ANY],
        [pl.BlockSpec((3, t, wide), lambda g, i: (0, 0, g))], [jax.ShapeDtypeStruct(du.shape, du.dtype)],
        [pltpu.VMEM((hg, QB, t), F32), pltpu.VMEM((hg, QB, t), F32), pltpu.VMEM((t, wide), F32),
         pltpu.VMEM((t, wide), F32)], ("parallel", "arbitrary"), (ub, ub, ub, dmix, du), phases, in_place={4: 0})
    return outs[0], phase_outs


def _row_tile(rows, cols, pref_bytes=2 * 1024 * 1024):
    tr = max(8, pref_bytes // (4 * cols))
    while rows % tr:
        tr //= 2
    return max(tr, 1)


def _pair_sum(name, g, s, c_idx):
    _, _, r2, cols = g.shape
    tr = _row_tile(r2, cols)

    def body(c_ref, g_ref, s_ref, o_ref):
        o_ref[...] = (g_ref[...] + s_ref[...]).astype(BF16)

    return pl.pallas_call(
        body, name=name,
        grid_spec=pltpu.PrefetchScalarGridSpec(
            num_scalar_prefetch=1, grid=(N_CHIPS - 1, r2 // tr),
            in_specs=[pl.BlockSpec((None, None, tr, cols), lambda p, i, c: (c[1 + p], c[0], i, 0)),
                      pl.BlockSpec((None, tr, cols), lambda p, i, c: (c[1 + p], i, 0))],
            out_specs=pl.BlockSpec((None, tr, cols), lambda p, i, c: (c[1 + p], i, 0))),
        out_shape=jax.ShapeDtypeStruct((N_CHIPS, r2, cols), BF16),
        compiler_params=_params(("parallel", "parallel")),
    )(c_idx, g, s)


def _chip_sum(name, g, s, received, place):
    _, _, r2, cols = g.shape
    tr = _row_tile(r2, cols)
    counts = [r.shape[0] for r in received]

    def body(place_ref, g_ref, s_ref, *refs):
        total = g_ref[...] + s_ref[...]
        for r_ref, n in zip(refs[:-1], counts):
            for k in range(n):
                total = total + r_ref[k].astype(F32)
        refs[-1][...] = total

    return pl.pallas_call(
        body, name=name,
        grid_spec=pltpu.PrefetchScalarGridSpec(
            num_scalar_prefetch=1, grid=(r2 // tr,),
            in_specs=[pl.BlockSpec((None, None, tr, cols), lambda i, p: (p[0], p[1], i, 0)),
                      pl.BlockSpec((None, tr, cols), lambda i, p: (p[0], i, 0)),
                      *[pl.BlockSpec((n, tr, cols), lambda i, p: (0, i, 0)) for n in counts]],
            out_specs=pl.BlockSpec((None, tr, cols), lambda i, p: (p[1], i, 0))),
        out_shape=jax.ShapeDtypeStruct((2, r2, cols), F32),
        compiler_params=_params(("parallel",)),
    )(place, g, s, *received)


def _cast_into_slot(name, w, place):
    rows, cols = w.shape
    r2 = rows // 2
    tr = _row_tile(r2, cols)
    nb = r2 // tr

    def body(place_ref, w_ref, o_ref):
        o_ref[...] = w_ref[...].astype(BF16)

    return pl.pallas_call(
        body, name=name,
        grid_spec=pltpu.PrefetchScalarGridSpec(
            num_scalar_prefetch=1, grid=(2, nb),
            in_specs=[pl.BlockSpec((tr, cols), lambda h, i, s: (h * nb + i, 0))],
            out_specs=pl.BlockSpec((None, None, tr, cols), lambda h, i, s: (s[0], h, i, 0))),
        out_shape=jax.ShapeDtypeStruct((N_CHIPS, 2, r2, cols), BF16),
        compiler_params=_params(("parallel", "parallel")),
    )(place, w)


def _colsum(name, a):
    def body(a_ref, o_ref):
        o_ref[...] = jnp.sum(a_ref[...], axis=0, keepdims=True)

    whole = lambda shape: pl.BlockSpec(shape, lambda i: (0, 0))
    return pl.pallas_call(
        body, name=name, grid=(1,), in_specs=[whole(a.shape)], out_specs=whole((1, a.shape[1])),
        out_shape=jax.ShapeDtypeStruct((1, a.shape[1]), F32), compiler_params=_params(("arbitrary",)),
    )(a)


def _adamw(name, w, g, m, v):
    rows, cols = w.shape
    tr = _row_tile(rows, cols, 1024 * 1024)

    def body(w_ref, g_ref, m_ref, v_ref, g_out_ref, d_ref, nm_ref, nv_ref):
        g_t = g_ref[...]
        m_t = ADAM_B1 * m_ref[...] + (1.0 - ADAM_B1) * g_t
        v_t = ADAM_B2 * v_ref[...] + (1.0 - ADAM_B2) * jnp.square(g_t)
        m_hat = m_t / (1.0 - ADAM_B1 ** ADAM_STEP)
        v_hat = v_t / (1.0 - ADAM_B2 ** ADAM_STEP)
        g_out_ref[...] = g_t
        d_ref[...] = -ADAM_LR * (m_hat / (jnp.sqrt(v_hat) + ADAM_EPS) + ADAM_WD * w_ref[...])
        nm_ref[...] = m_t
        nv_ref[...] = v_t

    spec = pl.BlockSpec((tr, cols), lambda i: (i, 0))
    shape = jax.ShapeDtypeStruct((rows, cols), F32)
    return pl.pallas_call(
        body, name=name, grid=(rows // tr,), in_specs=[spec] * 4, out_specs=[spec] * 4, out_shape=[shape] * 4,
        compiler_params=_params(("parallel",)),
    )(w, g, m, v)


def _all_reduce_small(packed, phases=()):
    rows, cols = packed.shape

    def body(in_ref, out_ref, all_ref, send_sems, recv_sems):
        x, y, c = _pos()
        me = 4 * x + 2 * y + c
        all_ref[me] = in_ref[...]
        cps = []
        for r in range(1, N_DEV):
            bx, by, bc = (r >> 2) & 1, (r >> 1) & 1, r & 1
            peer = (1 - x if bx else x, 1 - y if by else y, 1 - c if bc else c)
            cp = pltpu.make_async_remote_copy(
                src_ref=in_ref, dst_ref=all_ref.at[me], send_sem=send_sems.at[r - 1], recv_sem=recv_sems.at[r - 1],
                device_id=peer, device_id_type=MESH)
            cp.start()
            cps.append(cp)
        for cp in cps:
            cp.wait()
        total = all_ref[0]
        for d in range(1, N_DEV):
            total = total + all_ref[d]
        out_ref[...] = total

    vmem = pl.BlockSpec(memory_space=pltpu.VMEM)
    outs, phase_outs = _call(
        body, "all_reduce_small", (1,), [vmem], [vmem], [jax.ShapeDtypeStruct((rows, cols), F32)],
        [pltpu.VMEM((N_DEV, rows, cols), F32), pltpu.SemaphoreType.DMA((N_DEV - 1,)),
         pltpu.SemaphoreType.DMA((N_DEV - 1,))], ("arbitrary",), (packed,), phases)
    return outs[0], phase_outs


def kernel(x, ln_in_g, ln_in_b, w_in, w_pool, pool_scale, w_out, ln1_g, ln1_b, w_ff1, b_ff1, w_ff2, b_ff2, ln2_g, ln2_b, loss_target, m_ln_in_g, m_ln_in_b, m_w_in, m_w_pool, m_pool_scale, m_w_out, m_ln1_g, m_ln1_b, m_w_ff1, m_b_ff1, m_w_ff2, m_b_ff2, m_ln2_g, m_ln2_b, v_ln_in_g, v_ln_in_b, v_w_in, v_w_pool, v_pool_scale, v_w_out, v_ln1_g, v_ln1_b, v_w_ff1, v_b_ff1, v_w_ff2, v_b_ff2, v_ln2_g, v_ln2_b):
    t, d = x.shape[1], x.shape[2]
    pw = d // 2
    n_grp = len(POOL_WINDOWS)
    gw = pw // n_grp
    gwc = gw // N_CHIPS
    nh = pw // HEAD_DIM
    ff = w_ff1.shape[2] * N_CHIPS
    assert w_in.shape[0] == 1 and w_in.shape[2] * N_CHIPS == 2 * d and gwc <= 128

    x_idx, y_idx, c_idx = _pos()
    chip_arr = jnp.reshape(2 * x_idx + y_idx, (1,)).astype(jnp.int32)
    c_arr = jnp.reshape(c_idx, (1,)).astype(jnp.int32)
    place = jnp.concatenate([chip_arr, c_arr])
    core_and_others = jnp.stack([c_idx, *[2 * cx + cy for cx, cy in _other_chips(x_idx, y_idx)]]).astype(jnp.int32)

    xs = x.reshape(t, d)
    target = loss_target.reshape(t, d)
    row = lambda vec: vec.reshape(1, -1)

    scale_tile = jnp.zeros((1, 8, 128), F32).at[0, :n_grp, :gwc].set(pool_scale[0])
    scale_slots = lax.dynamic_update_slice(jnp.zeros((N_CHIPS, 8, 128), F32), scale_tile, (chip_arr[0], 0, 0))
    shards = dict(w_in=w_in[0], w_out=w_out[0], w_ff1=w_ff1[0], w_ff2=w_ff2[0], w_pool=w_pool[0].reshape(gw, gw))
    slot = {nm: _cast_into_slot("cast_" + nm, w, place) for nm, w in shards.items()}
    unsplit = lambda s: s.reshape(N_CHIPS, 2 * s.shape[2], s.shape[3])

    (h0, h0b), ((win_s,), (wpool_s,), (scale_g,)) = _ln_fwd(
        "ln_in_fwd", xs, row(ln_in_g), row(ln_in_b),
        phases=[_gather_ici_phase(slot["w_in"]), _gather_ici_phase(slot["w_pool"]), _gather_whole_phase(scale_slots)])
    (win_s,), (wpool_s,) = _comm_call("gather_d2d_first", [_gather_d2d_phase(win_s), _gather_d2d_phase(wpool_s)])
    win_g = unsplit(win_s)
    wpool_full = unsplit(wpool_s).reshape(N_CHIPS, n_grp, gwc, gw).transpose(1, 0, 2, 3).reshape(n_grp, gw, gw)
    scale_full = scale_g[:, :n_grp, :gwc].transpose(1, 0, 2).reshape(n_grp, 1, gw)

    def store_f32(acc, extra_refs, out_refs):
        out_refs[0][...] = acc

    def pool_f32_all_bf16(acc, extra_refs, out_refs):
        @pl.when(pl.program_id(1) == 0)
        def _():
            out_refs[0][...] = acc

        out_refs[1][...] = acc.astype(BF16)

    assert w_in.shape[2] == pw
    (u, ub), ((wout_s,), (wff1_s,)) = _mm_nn(
        "in_proj", h0b, win_g, (), lambda tm, tn: [],
        [jax.ShapeDtypeStruct((t, pw), F32), jax.ShapeDtypeStruct((t, 2 * d), BF16)],
        lambda tm, tn: [pl.BlockSpec((tm, tn), lambda i, j, kk: (i, 0)), _tile_spec(tm, tn)],
        pool_f32_all_bf16, b_chips=True, tn=pw,
        phases=[_gather_ici_phase(slot["w_out"]), _gather_ici_phase(slot["w_ff1"], rows=(0, 1, 8))])
    mix_in = _pool_fwd(u, wpool_full, scale_full, t, pw)
    mix_in, ((wff1_s,), (wout_s,)) = _attn_fwd(
        ub, mix_in, t, nh, min(nh, 4),
        phases=[_gather_ici_phase(wff1_s, rows=(1, 6, 8)), _gather_d2d_phase(wout_s)])
    same_part = lambda block: block
    wout_full = unsplit(wout_s).reshape(d, d)

    def residual(acc, extra_refs, out_refs):
        out_refs[0][...] = ALPHA * extra_refs[0][...] + acc

    (r1,), ((wff1_s,), (wff2_s,)) = _mm_nn(
        "out_proj", mix_in, wout_full, (h0,), lambda tm, tn: [_tile_spec(tm, tn)],
        [jax.ShapeDtypeStruct((t, d), F32)], lambda tm, tn: [_tile_spec(tm, tn)], residual, a_parts=same_part,
        phases=[_gather_ici_phase(wff1_s, rows=(7, 1, 8)), _gather_ici_phase(slot["w_ff2"], rows=(0, 1, 8))])
    (h1, h1b), ((wff1_s,),) = _ln_fwd("ln1_fwd", r1, ln1_g, ln1_b, phases=[_gather_d2d_phase(wff1_s)])
    wff1_g = unsplit(wff1_s)

    def relu_sq(acc, extra_refs, out_refs):
        p = jnp.maximum(acc + extra_refs[0][...], 0.0)
        out_refs[0][...] = p
        out_refs[1][...] = jnp.square(p).astype(BF16)

    (relu_z, act_b), ((wff2_s,),) = _mm_nn(
        "ff1", h1b, wff1_g, (b_ff1,), lambda tm, tn: [_row_spec(tn)],
        [jax.ShapeDtypeStruct((t, ff), F32), jax.ShapeDtypeStruct((t, ff), BF16)],
        lambda tm, tn: [_tile_spec(tm, tn)] * 2, relu_sq, b_chips=True,
        phases=[_gather_ici_phase(wff2_s, rows=(1, 7, 8))])
    ((wff2_s,),) = _comm_call("gather_d2d_last", [_gather_d2d_phase(wff2_s)])
    wff2_full = unsplit(wff2_s).reshape(ff, d)

    def residual_bias(acc, extra_refs, out_refs):
        out_refs[0][...] = ALPHA * extra_refs[0][...] + (acc + extra_refs[1][...])

    r2 = _mm_nn("ff2", act_b, wff2_full, (h1, b_ff2), lambda tm, tn: [_tile_spec(tm, tn), _row_spec(tn)],
                [jax.ShapeDtypeStruct((t, d), F32)], lambda tm, tn: [_tile_spec(tm, tn)], residual_bias)[0]

    dr2, dr2b, loss_tile, g_ln2_g, g_ln2_b, g_b_ff2 = _ln2_loss_bwd(r2, target, ln2_g, ln2_b)

    halves = lambda g: g.reshape(N_CHIPS, 2, g.shape[1] // 2, g.shape[2])
    g_ff2 = halves(_mm_tn("grad_w_ff2", act_b, dr2b).reshape(N_CHIPS, ff // N_CHIPS, d))

    def relu_sq_bwd(acc, extra_refs, out_refs):
        dz = acc * (2.0 * extra_refs[0][...])
        out_refs[0][...] = dz.astype(BF16)
        rows = lax.broadcasted_iota(jnp.int32, out_refs[1].shape, 0)
        out_refs[1][...] = jnp.where(rows == 0, jnp.sum(dz, axis=0, keepdims=True), 0.0)

    tm_ff = _tile(t, 1024)
    (dz1b, g_b_ff1_parts), ((s_ff2,),) = _mm_nt(
        "ff2_bwd", dr2b, wff2_full, (relu_z,), lambda tm, tn: [_tile_spec(tm, tn)],
        [jax.ShapeDtypeStruct((t, ff), BF16), jax.ShapeDtypeStruct((8 * (t // tm_ff), ff), F32)],
        lambda tm, tn: [_tile_spec(tm, tn), pl.BlockSpec((8, tn), lambda i, j, kk: (i, j))], relu_sq_bwd,
        phases=[_swap_phase(g_ff2)])
    p_ff2 = _pair_sum("pair_sum_w_ff2", g_ff2, s_ff2, core_and_others)
    g_ff1, ((r_ff2_a,),) = _mm_tn("grad_w_ff1", h1b, dz1b, out_chips=True,
                                  phases=[_scatter_phase(p_ff2, others=(0, 1))])
    g_ff1 = halves(g_ff1)

    def plus_alpha(acc, extra_refs, out_refs):
        out_refs[0][...] = ALPHA * extra_refs[0][...] + acc

    (dh1,), ((s_ff1,), (r_ff2_b,)) = _mm_nt(
        "ff1_bwd", dz1b, wff1_g, (dr2,), lambda tm, tn: [_tile_spec(tm, tn)],
        [jax.ShapeDtypeStruct((t, d), F32)], lambda tm, tn: [_tile_spec(tm, tn)], plus_alpha, b_chips=True,
        phases=[_swap_phase(g_ff1), _scatter_phase(p_ff2, others=(2,))])
    q_ff2 = _chip_sum("chip_sum_w_ff2", g_ff2, s_ff2, [r_ff2_a, r_ff2_b], place)
    p_ff1 = _pair_sum("pair_sum_w_ff1", g_ff1, s_ff1, core_and_others)
    (dr1, dr1b, g_ln1_g, g_ln1_b), ((q_ff2,),) = _ln_bwd("ln1_bwd", dh1, r1, ln1_g,
                                                         phases=[_assemble_phase(q_ff2)])

    g_out = halves(_mm_tn("grad_w_out", mix_in, dr1b, a_parts=same_part).reshape(N_CHIPS, d // N_CHIPS, d))
    (dmix,), ((s_out,),) = _mm_nt(
        "out_proj_bwd", dr1b, wout_full, (), lambda tm, tn: [], [jax.ShapeDtypeStruct((t, d), F32)],
        lambda tm, tn: [_tile_spec(tm, tn)], store_f32, phases=[_swap_phase(g_out)])
    p_out = _pair_sum("pair_sum_w_out", g_out, s_out, core_and_others)
    du, g_w_pool_full, g_scale_full = _pool_bwd(u, dmix, wpool_full, scale_full, t, pw)
    du, ((r_ff1,), (r_out,)) = _attn_bwd(ub, dmix, du, t, nh, min(nh, 2),
                                         phases=[_scatter_phase(p_ff1), _scatter_phase(p_out)])
    q_ff1 = _chip_sum("chip_sum_w_ff1", g_ff1, s_ff1, [r_ff1], place)
    q_out = _chip_sum("chip_sum_w_out", g_out, s_out, [r_out], place)
    g_in, ((q_ff1,), (q_out,)) = _mm_tn("grad_w_in", h0b, du, out_chips=True, b_parts=_du_part,
                                        phases=[_assemble_phase(q_ff1), _assemble_phase(q_out)])
    g_in = halves(g_in)
    g_pool = halves(g_w_pool_full.reshape(n_grp, N_CHIPS, gwc, gw).transpose(1, 0, 2, 3).reshape(N_CHIPS, gw, gw))
    (s_in,), (s_pool,) = _comm_call("rs_swap_last", [_swap_phase(g_in), _swap_phase(g_pool)])
    p_in = _pair_sum("pair_sum_w_in", g_in, s_in, core_and_others)
    p_pool = _pair_sum("pair_sum_w_pool", g_pool, s_pool, core_and_others)
    (dh0,), ((r_in,), (r_pool,)) = _mm_nt(
        "in_proj_bwd", du, win_g, (dr1,), lambda tm, tn: [_tile_spec(tm, tn)],
        [jax.ShapeDtypeStruct((t, d), F32)], lambda tm, tn: [_tile_spec(tm, tn)], plus_alpha, b_chips=True,
        a_parts=_du_part, phases=[_scatter_phase(p_in), _scatter_phase(p_pool)])
    q_in = _chip_sum("chip_sum_w_in", g_in, s_in, [r_in], place)
    q_pool = _chip_sum("chip_sum_w_pool", g_pool, s_pool, [r_pool], place)
    dx, _, g_ln_in_g, g_ln_in_b = _ln_bwd("ln_in_bwd", dh0, xs, row(ln_in_g))

    lane = 2048 if d % 2048 == 0 else d
    small_names = ["ln_in_g", "ln_in_b", "ln1_g", "ln1_b", "b_ff1", "b_ff2", "ln2_g", "ln2_b"]
    small_w = dict(ln_in_g=ln_in_g, ln_in_b=ln_in_b, ln1_g=ln1_g, ln1_b=ln1_b, b_ff1=b_ff1, b_ff2=b_ff2, ln2_g=ln2_g,
                   ln2_b=ln2_b)
    small_m = dict(ln_in_g=m_ln_in_g, ln_in_b=m_ln_in_b, ln1_g=m_ln1_g, ln1_b=m_ln1_b, b_ff1=m_b_ff1, b_ff2=m_b_ff2,
                   ln2_g=m_ln2_g, ln2_b=m_ln2_b)
    small_v = dict(ln_in_g=v_ln_in_g, ln_in_b=v_ln_in_b, ln1_g=v_ln1_g, ln1_b=v_ln1_b, b_ff1=v_b_ff1, b_ff2=v_b_ff2,
                   ln2_g=v_ln2_g, ln2_b=v_ln2_b)
    small_g = dict(ln_in_g=g_ln_in_g, ln_in_b=g_ln_in_b, ln1_g=g_ln1_g, ln1_b=g_ln1_b, b_ff2=g_b_ff2, ln2_g=g_ln2_g,
                   ln2_b=g_ln2_b)

    def pack(parts):
        flat = jnp.concatenate([p.reshape(-1) for p in parts])
        n_rows = -(-flat.shape[0] // lane)
        n_rows = -(-n_rows // 8) * 8
        return jnp.pad(flat, (0, n_rows * lane - flat.shape[0])).reshape(n_rows, lane)

    small_g["b_ff1"] = _colsum("b_ff1_colsum", g_b_ff1_parts)
    summed, ((q_in,), (q_pool,)) = _all_reduce_small(
        pack([small_g[nm] for nm in small_names] + [g_scale_full, loss_tile[0, :1]]),
        phases=[_assemble_phase(q_in), _assemble_phase(q_pool)])
    summed = summed.reshape(-1)

    big = {}
    for nm, q, w, m, v in [("w_in", q_in, w_in, m_w_in, v_w_in), ("w_out", q_out, w_out, m_w_out, v_w_out),
                           ("w_ff1", q_ff1, w_ff1, m_w_ff1, v_w_ff1), ("w_ff2", q_ff2, w_ff2, m_w_ff2, v_w_ff2),
                           ("w_pool", q_pool, w_pool, m_w_pool, v_w_pool)]:
        g = q.reshape(2 * q.shape[1], q.shape[2])
        flat = lambda arr: arr.reshape(g.shape)
        big[nm] = tuple(arr.reshape(w.shape) for arr in _adamw("adamw_" + nm, flat(w), g, flat(m), flat(v)))

    g_small, off = {}, 0
    for nm in small_names:
        g_small[nm] = summed[off:off + small_w[nm].size]
        off += small_w[nm].size
    g_scale_all = summed[off:off + n_grp * gw].reshape(n_grp, N_CHIPS, gwc)
    loss = summed[off + n_grp * gw]
    g_scale = lax.dynamic_index_in_dim(g_scale_all, chip_arr[0], axis=1, keepdims=False)

    order = small_names + ["pool_scale"]
    small_w["pool_scale"], small_m["pool_scale"], small_v["pool_scale"] = pool_scale, m_pool_scale, v_pool_scale
    g_small["pool_scale"] = g_scale
    _, delta_s, new_m_s, new_v_s = _adamw("adamw_small", pack([small_w[nm] for nm in order]),
                                          pack([g_small[nm] for nm in order]), pack([small_m[nm] for nm in order]),
                                          pack([small_v[nm] for nm in order]))
    small = {}
    off = 0
    for nm in order:
        size, shape = small_w[nm].size, small_w[nm].shape
        cut = lambda arr: arr.reshape(-1)[off:off + size].reshape(shape)
        small[nm] = (g_small[nm].reshape(shape), cut(delta_s), cut(new_m_s), cut(new_v_s))
        off += size

    every = {**big, **small}
    weight_order = ["ln_in_g", "ln_in_b", "w_in", "w_pool", "pool_scale", "w_out", "ln1_g", "ln1_b", "w_ff1", "b_ff1",
                    "w_ff2", "b_ff2", "ln2_g", "ln2_b"]
    grads = [every[nm][0] for nm in weight_order]
    deltas = [every[nm][1] for nm in weight_order]
    new_ms = [every[nm][2] for nm in weight_order]
    new_vs = [every[nm][3] for nm in weight_order]
    return (loss, dx.reshape(x.shape), *grads, *deltas, *new_ms, *new_vs)
```

```python
import functools

import jax
import jax.numpy as jnp
from jax import lax
from jax.experimental import pallas as pl
from jax.experimental.pallas import tpu as pltpu

F32 = jnp.float32
BF16 = jnp.bfloat16
MESH = pl.DeviceIdType.MESH

HEAD_DIM = 128
POOL_WINDOWS = (2, 4, 8, 16)
POOL_HALO = 16
LN_EPS = 1e-5
ALPHA = 2.0 ** 0.25
ADAM_LR, ADAM_B1, ADAM_B2, ADAM_EPS, ADAM_WD, ADAM_STEP = 0.001, 0.9, 0.999, 1e-08, 0.01, 10

QB = 256
KB = 256
VMEM_LIMIT = 56 * 1024 * 1024
N_CHIPS = 4
N_DEV = 8


def _params(sem=None, collective_id=None):
    return pltpu.CompilerParams(dimension_semantics=sem, vmem_limit_bytes=VMEM_LIMIT, collective_id=collective_id)


def _tile(dim, pref):
    return pref if dim % pref == 0 else dim


def _pos():
    return lax.axis_index("x"), lax.axis_index("y"), lax.axis_index("c")


def _other_chips(x, y):
    return [(1 - x, y), (x, 1 - y), (1 - x, 1 - y)]


ANY = pl.BlockSpec(memory_space=pl.ANY)


SIBLING = "sibling"


class _Phase:
    def __init__(self, ins, out_shapes, aliases, n_sems, build, peers):
        self.ins, self.out_shapes, self.aliases, self.n_sems, self.build = ins, out_shapes, aliases, n_sems, build
        self.peers = tuple(peers)


def _entry_barrier(phases):
    x, y, c = _pos()
    chips = _other_chips(x, y)
    relations = sorted({r for ph in phases for r in ph.peers}, key=str)
    barrier = pltpu.get_barrier_semaphore()
    for r in relations:
        peer = (x, y, 1 - c) if r == SIBLING else (*chips[r], c)
        pl.semaphore_signal(barrier, inc=1, device_id=peer, device_id_type=MESH)
    pl.semaphore_wait(barrier, len(relations))


def _barrier_id(phases):
    return sum(1 << (3 if r == SIBLING else r) for r in {r for ph in phases for r in ph.peers})


def _remote(src, dst, send_sems, recv_sems, k, to):
    return pltpu.make_async_remote_copy(src_ref=src, dst_ref=dst, send_sem=send_sems.at[k], recv_sem=recv_sems.at[k],
                                        device_id=to, device_id_type=MESH)


def _swap_phase(g):
    def build(ins, outs, ss, rs):
        x, y, c = _pos()
        cp = _remote(ins[0].at[:, 1 - c], outs[0], ss, rs, 0, (x, y, 1 - c))
        return [cp], [cp]

    return _Phase([g], [jax.ShapeDtypeStruct((N_CHIPS, g.shape[2], g.shape[3]), g.dtype)], {}, 1, build, [SIBLING])


ALL_OTHERS = (0, 1, 2)


def _scatter_phase(p, others=ALL_OTHERS):
    def build(ins, outs, ss, rs):
        x, y, c = _pos()
        chips = _other_chips(x, y)
        cps = [_remote(ins[0].at[2 * chips[j][0] + chips[j][1]], outs[0].at[k], ss, rs, k, (*chips[j], c))
               for k, j in enumerate(others)]
        return cps, cps

    return _Phase([p], [jax.ShapeDtypeStruct((len(others), p.shape[1], p.shape[2]), p.dtype)], {}, len(others), build,
                  others)


def _assemble_phase(q):
    def build(ins, outs, ss, rs):
        x, y, c = _pos()
        mine, other = outs[0].at[c], outs[0].at[1 - c]
        return [_remote(mine, mine, ss, rs, 0, (x, y, 1 - c))], [_remote(other, other, ss, rs, 0, (x, y, c))]

    return _Phase([q], [jax.ShapeDtypeStruct(q.shape, q.dtype)], {0: 0}, 1, build, [SIBLING])


def _gather_ici_phase(slot, others=ALL_OTHERS, rows=(0, 1, 1)):
    chunk = slot.shape[2] // rows[2]
    span = pl.ds(rows[0] * chunk, rows[1] * chunk)

    def build(ins, outs, ss, rs):
        x, y, c = _pos()
        chips = _other_chips(x, y)
        mine = outs[0].at[2 * x + y, c, span]
        sends, recvs = [], []
        for k, j in enumerate(others):
            theirs = outs[0].at[2 * chips[j][0] + chips[j][1], c, span]
            sends.append(_remote(mine, mine, ss, rs, k, (*chips[j], c)))
            recvs.append(_remote(theirs, theirs, ss, rs, k, (x, y, c)))
        return sends, recvs

    return _Phase([slot], [jax.ShapeDtypeStruct(slot.shape, slot.dtype)], {0: 0}, len(others), build, others)


def _gather_d2d_phase(slot):
    def build(ins, outs, ss, rs):
        x, y, c = _pos()
        sends, recvs = [], []
        for j, chip in enumerate(_other_chips(x, y)):
            landed = outs[0].at[2 * chip[0] + chip[1], c]
            coming = outs[0].at[2 * chip[0] + chip[1], 1 - c]
            sends.append(_remote(landed, landed, ss, rs, j, (x, y, 1 - c)))
            recvs.append(_remote(coming, coming, ss, rs, j, (x, y, c)))
        return sends, recvs

    return _Phase([slot], [jax.ShapeDtypeStruct(slot.shape, slot.dtype)], {0: 0}, 3, build, [SIBLING])


def _gather_whole_phase(slots):
    def build(ins, outs, ss, rs):
        x, y, c = _pos()
        mine = outs[0].at[2 * x + y]
        sends, recvs = [], []
        for j, chip in enumerate(_other_chips(x, y)):
            theirs = outs[0].at[2 * chip[0] + chip[1]]
            sends.append(_remote(mine, mine, ss, rs, j, (*chip, c)))
            recvs.append(_remote(theirs, theirs, ss, rs, j, (x, y, c)))
        return sends, recvs

    return _Phase([slots], [jax.ShapeDtypeStruct(slots.shape, slots.dtype)], {0: 0}, 3, build, ALL_OTHERS)


def _split_refs(refs, n_in, n_out, n_scratch, phases):
    n_pin = sum(len(ph.ins) for ph in phases)
    n_pout = sum(len(ph.out_shapes) for ph in phases)
    cuts = [n_in, n_pin, n_out, n_pout, n_scratch]
    parts, at = [], 0
    for n in cuts:
        parts.append(refs[at:at + n])
        at += n
    parts.append(refs[at:])
    return parts


def _build_phases(phases, pin, pout, sems):
    built, i, o = [], 0, 0
    for k, ph in enumerate(phases):
        built.append(ph.build(pin[i:i + len(ph.ins)], pout[o:o + len(ph.out_shapes)], sems[2 * k], sems[2 * k + 1]))
        i += len(ph.ins)
        o += len(ph.out_shapes)
    return built


def _finish_phases(built):
    for _, recvs in built:
        for cp in recvs:
            cp.wait_recv()
    for sends, _ in built:
        for cp in sends:
            cp.wait_send()


def _call(body, name, grid, in_specs, out_specs, out_shape, scratch_shapes, semantics, args, phases=(),
          in_place=None, body_communicates=False):
    own_barrier = bool(phases) and not body_communicates
    n_in, n_out, n_scratch = len(args), len(out_shape), len(scratch_shapes)
    aliases, in_at, out_at = dict(in_place or {}), n_in, n_out
    for ph in phases:
        aliases.update({in_at + i: out_at + o for i, o in ph.aliases.items()})
        in_at += len(ph.ins)
        out_at += len(ph.out_shapes)

    def hosted(*refs):
        ins, pin, outs, pout, scratch, sems = _split_refs(refs, n_in, n_out, n_scratch, phases)
        ids = [pl.program_id(a) for a in range(len(grid))]
        first = functools.reduce(jnp.logical_and, [i == 0 for i in ids])
        last = functools.reduce(jnp.logical_and, [i == g - 1 for i, g in zip(ids, grid)])

        @pl.when(first)
        def _():
            if own_barrier:
                _entry_barrier(phases)
            for sends, _ in _build_phases(phases, pin, pout, sems):
                for cp in sends:
                    cp.start()

        body(*ins, *outs, *scratch)

        @pl.when(last)
        def _():
            _finish_phases(_build_phases(phases, pin, pout, sems))

    p_args = [a for ph in phases for a in ph.ins]
    p_shapes = [s for ph in phases for s in ph.out_shapes]
    sem_shapes = [pltpu.SemaphoreType.DMA((ph.n_sems,)) for ph in phases for _ in range(2)]
    outs = pl.pallas_call(
        hosted if phases else body, name=name, grid=grid, in_specs=[*in_specs, *[ANY] * len(p_args)],
        out_specs=[*out_specs, *[ANY] * len(p_shapes)], out_shape=[*out_shape, *p_shapes],
        input_output_aliases=aliases, scratch_shapes=[*scratch_shapes, *sem_shapes],
        compiler_params=_params(("arbitrary",) * len(grid) if phases else semantics,
                                _barrier_id(phases) if own_barrier else None),
    )(*args, *p_args)
    phase_outs, at = [], n_out
    for ph in phases:
        phase_outs.append(list(outs[at:at + len(ph.out_shapes)]))
        at += len(ph.out_shapes)
    return list(outs[:n_out]), phase_outs


def _comm_call(name, phases):
    def body(*refs):
        _, pin, _, pout, _, sems = _split_refs(refs, 0, 0, 0, phases)
        _entry_barrier(phases)
        built = _build_phases(phases, pin, pout, sems)
        for sends, _ in built:
            for cp in sends:
                cp.start()
        _finish_phases(built)

    aliases, in_at, out_at = {}, 0, 0
    for ph in phases:
        aliases.update({in_at + i: out_at + o for i, o in ph.aliases.items()})
        in_at += len(ph.ins)
        out_at += len(ph.out_shapes)
    p_args = [a for ph in phases for a in ph.ins]
    p_shapes = [s for ph in phases for s in ph.out_shapes]
    outs = pl.pallas_call(
        body, name=name, in_specs=[ANY] * len(p_args), out_specs=[ANY] * len(p_shapes), out_shape=p_shapes,
        input_output_aliases=aliases,
        scratch_shapes=[pltpu.SemaphoreType.DMA((ph.n_sems,)) for ph in phases for _ in range(2)],
        compiler_params=pltpu.CompilerParams(collective_id=_barrier_id(phases)),
    )(*p_args)
    phase_outs, at = [], 0
    for ph in phases:
        phase_outs.append(list(outs[at:at + len(ph.out_shapes)]))
        at += len(ph.out_shapes)
    return phase_outs


def _matmul(name, a, b, grid, a_spec, b_spec, contract, acc_shape, extras, extra_specs, out_shape, out_specs,
            epilogue, phases=()):
    n_extra, n_out, gk = len(extras), len(out_shape), grid[2]

    def product(a_ref, b_ref):
        return lax.dot_general(a_ref[...], b_ref[...], (contract, ((), ())), preferred_element_type=F32)

    def body_one_step(*refs):
        epilogue(product(refs[0], refs[1]), refs[2:2 + n_extra], refs[2 + n_extra:])

    def body(*refs):
        a_ref, b_ref = refs[0], refs[1]
        extra_refs = refs[2:2 + n_extra]
        out_refs = refs[2 + n_extra:2 + n_extra + n_out]
        acc_ref = refs[-1]
        kk = pl.program_id(2)

        @pl.when(kk == 0)
        def _():
            acc_ref[...] = product(a_ref, b_ref)

        @pl.when(kk > 0)
        def _():
            acc_ref[...] += product(a_ref, b_ref)

        @pl.when(kk == gk - 1)
        def _():
            epilogue(acc_ref[...], extra_refs, out_refs)

    outs, phase_outs = _call(
        body_one_step if gk == 1 else body, name, grid, [a_spec, b_spec, *extra_specs], out_specs, out_shape,
        [] if gk == 1 else [pltpu.VMEM(acc_shape, F32)], ("parallel", "arbitrary", "arbitrary"), (a, b, *extras),
        phases)
    return (outs, phase_outs) if phases else outs


def _mm_nn(name, a, b, extras, extra_specs, out_shape, out_specs, epilogue, b_chips=False, tm=1024, tn=1024,
           tk=2048, phases=(), a_parts=None):
    m, k, tm, tk, a_spec = _lhs_rows_by_k(a, tm, tk, a_parts)
    n = b.shape[1] if not b_chips else b.shape[2] * N_CHIPS
    if b_chips:
        tn = _tile(b.shape[2], tn)
        nb = b.shape[2] // tn
        b_spec = pl.BlockSpec((None, tk, tn), lambda i, j, kk: (j // nb, kk, j % nb))
    else:
        tn = _tile(n, tn)
        b_spec = pl.BlockSpec((tk, tn), lambda i, j, kk: (kk, j))
    return _matmul(name, a, b, (m // tm, n // tn, k // tk), a_spec, b_spec, ((1,), (0,)), (tm, tn), extras,
                   extra_specs(tm, tn), out_shape, out_specs(tm, tn), epilogue, phases)


def _lhs_rows_by_k(a, tm, tk, a_parts, tk_max=None):
    if a_parts is None:
        m, k = a.shape
        tm, tk = _tile(m, tm), _tile(k if tk_max is None else tk_max, tk)
        return m, k, tm, tk, pl.BlockSpec((tm, tk), lambda i, j, kk: (i, kk))
    n_parts, m, kp = a.shape
    tm, tk = _tile(m, tm), _tile(kp if tk_max is None else min(kp, tk_max), tk)
    nb = kp // tk
    return m, n_parts * kp, tm, tk, pl.BlockSpec((None, tm, tk), lambda i, j, kk: (a_parts(kk // nb), i, kk % nb))


def _mm_nt(name, a, b, extras, extra_specs, out_shape, out_specs, epilogue, b_chips=False, tm=1024, tn=1024,
           tk=2048, phases=(), a_parts=None):
    m, k, tm, tk, a_spec = _lhs_rows_by_k(a, tm, tk, a_parts, tk_max=b.shape[2] if b_chips else None)
    n = b.shape[0] if not b_chips else b.shape[1]
    tn = _tile(n, tn)
    if b_chips:
        nb = b.shape[2] // tk
        b_spec = pl.BlockSpec((None, tn, tk), lambda i, j, kk: (kk // nb, j, kk % nb))
    else:
        b_spec = pl.BlockSpec((tn, tk), lambda i, j, kk: (j, kk))
    return _matmul(name, a, b, (m // tm, n // tn, k // tk), a_spec, b_spec, ((1,), (1,)), (tm, tn), extras,
                   extra_specs(tm, tn), out_shape, out_specs(tm, tn), epilogue, phases)


def _mm_tn(name, a, b, out_chips=False, tm=1024, tn=1024, tk=2048, phases=(), a_parts=None, b_parts=None):
    if a_parts is None:
        k, m = a.shape
        tm = _tile(m, tm)
        a_spec = pl.BlockSpec((_tile(k, tk), tm), lambda i, j, kk: (kk, i))
    else:
        n_parts, k, mp = a.shape
        m, tm = n_parts * mp, _tile(mp, tm)
        nbm = mp // tm
        a_spec = pl.BlockSpec((None, _tile(k, tk), tm), lambda i, j, kk: (a_parts(i // nbm), kk, i % nbm))
    tk = _tile(k, tk)
    n = b.shape[1] if b_parts is None else b.shape[0] * b.shape[2]
    if out_chips:
        nc = n // N_CHIPS
        tn = _tile(nc, tn)
        nb = nc // tn
        out_shape = [jax.ShapeDtypeStruct((N_CHIPS, m, nc), F32)]
        out_specs = [pl.BlockSpec((None, tm, tn), lambda i, j, kk: (j // nb, i, j % nb))]
    else:
        tn = _tile(n, tn)
        out_shape = [jax.ShapeDtypeStruct((m, n), F32)]
        out_specs = [pl.BlockSpec((tm, tn), lambda i, j, kk: (i, j))]
    if b_parts is None:
        b_spec = pl.BlockSpec((tk, tn), lambda i, j, kk: (kk, j))
    else:
        nbn = b.shape[2] // tn
        b_spec = pl.BlockSpec((None, tk, tn), lambda i, j, kk: (b_parts(j // nbn), kk, j % nbn))

    def epilogue(acc, extra_refs, out_refs):
        out_refs[0][...] = acc

    res = _matmul(name, a, b, (m // tm, n // tn, k // tk), a_spec, b_spec, ((0,), (0,)), (tm, tn), (), [],
                  out_shape, out_specs, epilogue, phases)
    return (res[0][0], res[1]) if phases else res[0]


def _tile_spec(tm, tn):
    return pl.BlockSpec((tm, tn), lambda i, j, kk: (i, j))


def _row_spec(tn):
    return pl.BlockSpec((1, tn), lambda i, j, kk: (0, j))


def _ln_stats(r):
    mu = jnp.mean(r, axis=-1, keepdims=True)
    var = jnp.mean(jnp.square(r - mu), axis=-1, keepdims=True)
    rstd = lax.rsqrt(var + LN_EPS)
    return (r - mu) * rstd, rstd


def _ln_fwd(name, r, g, b, tr=256, phases=()):
    t, d = r.shape
    tr = _tile(t, tr)

    def body(r_ref, g_ref, b_ref, y_ref, yb_ref):
        xhat, _ = _ln_stats(r_ref[...])
        y = xhat * g_ref[...] + b_ref[...]
        y_ref[...] = y
        yb_ref[...] = y.astype(BF16)

    row = pl.BlockSpec((tr, d), lambda i: (i, 0))
    vec = pl.BlockSpec((1, d), lambda i: (0, 0))
    outs, phase_outs = _call(
        body, name, (t // tr,), [row, vec, vec], [row, row],
        [jax.ShapeDtypeStruct((t, d), F32), jax.ShapeDtypeStruct((t, d), BF16)], [], ("parallel",), (r, g, b), phases)
    return (outs, phase_outs) if phases else outs


def _ln_bwd_rows(dy, xhat, rstd, g):
    dxhat = dy * g
    m1 = jnp.mean(dxhat, axis=-1, keepdims=True)
    m2 = jnp.mean(dxhat * xhat, axis=-1, keepdims=True)
    return rstd * (dxhat - m1 - xhat * m2)


def _ln_bwd(name, dy, r, g, tr=256, phases=()):
    t, d = dy.shape
    tr = _tile(t, tr)

    def body(dy_ref, r_ref, g_ref, dr_ref, drb_ref, dg_ref, db_ref):
        @pl.when(pl.program_id(0) == 0)
        def _():
            dg_ref[...] = jnp.zeros_like(dg_ref)
            db_ref[...] = jnp.zeros_like(db_ref)

        dy_t = dy_ref[...]
        xhat_t, rstd = _ln_stats(r_ref[...])
        dr = _ln_bwd_rows(dy_t, xhat_t, rstd, g_ref[...])
        dr_ref[...] = dr
        drb_ref[...] = dr.astype(BF16)
        dg_ref[...] += jnp.sum(dy_t * xhat_t, axis=0, keepdims=True)
        db_ref[...] += jnp.sum(dy_t, axis=0, keepdims=True)

    row = pl.BlockSpec((tr, d), lambda i: (i, 0))
    vec = pl.BlockSpec((1, d), lambda i: (0, 0))
    outs, phase_outs = _call(
        body, name, (t // tr,), [row, row, vec], [row, row, vec, vec],
        [jax.ShapeDtypeStruct((t, d), F32), jax.ShapeDtypeStruct((t, d), BF16),
         jax.ShapeDtypeStruct((1, d), F32), jax.ShapeDtypeStruct((1, d), F32)], [], ("arbitrary",),
        (dy, r, g), phases)
    return (outs, phase_outs) if phases else outs


def _ln2_loss_bwd(r2, target, g, b, tr=256):
    t, d = r2.shape
    tr = _tile(t, tr)

    def body(r_ref, t_ref, g_ref, b_ref, dr_ref, drb_ref, loss_ref, dg_ref, db_ref, dsum_ref):
        @pl.when(pl.program_id(0) == 0)
        def _():
            loss_ref[...] = jnp.zeros_like(loss_ref)
            dg_ref[...] = jnp.zeros_like(dg_ref)
            db_ref[...] = jnp.zeros_like(db_ref)
            dsum_ref[...] = jnp.zeros_like(dsum_ref)

        xhat, rstd = _ln_stats(r_ref[...])
        g_t = g_ref[...]
        err = xhat * g_t + b_ref[...] - t_ref[...]
        loss_ref[...] += 0.5 * jnp.sum(jnp.mean(jnp.square(err), axis=-1, keepdims=True), axis=0, keepdims=True)
        dy = err * (1.0 / d)
        dr = _ln_bwd_rows(dy, xhat, rstd, g_t)
        dr_ref[...] = dr
        drb_ref[...] = dr.astype(BF16)
        dg_ref[...] += jnp.sum(dy * xhat, axis=0, keepdims=True)
        db_ref[...] += jnp.sum(dy, axis=0, keepdims=True)
        dsum_ref[...] += jnp.sum(dr, axis=0, keepdims=True)

    row = pl.BlockSpec((tr, d), lambda i: (i, 0))
    vec = pl.BlockSpec((1, d), lambda i: (0, 0))
    return pl.pallas_call(
        body, name="ln2_loss_bwd", grid=(t // tr,), in_specs=[row, row, vec, vec],
        out_specs=[row, row, pl.BlockSpec((8, 128), lambda i: (0, 0)), vec, vec, vec],
        out_shape=[jax.ShapeDtypeStruct((t, d), F32), jax.ShapeDtypeStruct((t, d), BF16),
                   jax.ShapeDtypeStruct((8, 128), F32), jax.ShapeDtypeStruct((1, d), F32),
                   jax.ShapeDtypeStruct((1, d), F32), jax.ShapeDtypeStruct((1, d), F32)],
        compiler_params=_params(("arbitrary",)),
    )(r2, target, g, b)


POOL_ROWS = 512

DU_POOL = 3


def _du_part(block):
    return (block + DU_POOL) % 4


def _pool_mean_minus_token(u_ref, r0, rows, grp, first):
    width = u_ref.shape[1]
    body = u_ref[pl.ds(r0, rows), :]
    halo = u_ref[pl.ds(pl.multiple_of(jnp.maximum(r0 - POOL_HALO, 0), POOL_HALO), POOL_HALO), :]
    halo = jnp.where(first, 0.0, halo)
    full = jnp.concatenate([halo, body], axis=0)
    s = full
    for step in range(len(POOL_WINDOWS)):
        shifted = pltpu.roll(s, 1 << step, axis=0)
        s = s + jnp.where(grp >= step, shifted, 0.0)
    s = s[POOL_HALO:, :]
    tpos = r0 + lax.broadcasted_iota(jnp.int32, (rows, width), 0)
    count = jnp.minimum(tpos + 1, 2 << grp).astype(F32)
    return s / count - body, count


def _pool_fwd(u, w_pool, pool_scale, t, pw):
    gw = pw // len(POOL_WINDOWS)
    rows = _tile(t, POOL_ROWS)

    def body(u_ref, w_ref, s_ref, o_ref):
        grp = pl.program_id(0)

        def chunk(ci, carry):
            r0 = pl.multiple_of(ci * rows, rows)
            y, _ = _pool_mean_minus_token(u_ref, r0, rows, grp, ci == 0)
            yw = jnp.dot(y.astype(BF16), w_ref[...], preferred_element_type=F32)
            o_ref[pl.ds(r0, rows), :] = (yw * s_ref[...]).astype(BF16)
            return carry

        lax.fori_loop(0, t // rows, chunk, 0)

    return pl.pallas_call(
        body, name="pool_fwd", grid=(len(POOL_WINDOWS),),
        in_specs=[pl.BlockSpec((t, gw), lambda g: (0, g)), pl.BlockSpec((None, gw, gw), lambda g: (g, 0, 0)),
                  pl.BlockSpec((None, 1, gw), lambda g: (g, 0, 0))],
        out_specs=pl.BlockSpec((None, t, gw), lambda g: (0, 0, g)),
        out_shape=jax.ShapeDtypeStruct((2, t, pw), BF16),
        compiler_params=_params(("parallel",)),
    )(u, w_pool, pool_scale)


def _pool_bwd(u, dmix, w_pool, pool_scale, t, pw):
    n_grp = len(POOL_WINDOWS)
    gw = pw // n_grp
    rows = _tile(t, POOL_ROWS)

    def body(u_ref, dm_ref, w_ref, s_ref, du_ref, dw_ref, ds_ref, e_ref):
        grp = pl.program_id(0)
        dw_ref[...] = jnp.zeros_like(dw_ref)
        ds_ref[...] = jnp.zeros_like(ds_ref)
        e_ref[pl.ds(t, POOL_HALO), :] = jnp.zeros((POOL_HALO, gw), F32)

        def chunk(ci, carry):
            r0 = pl.multiple_of(ci * rows, rows)
            y, count = _pool_mean_minus_token(u_ref, r0, rows, grp, ci == 0)
            yb = y.astype(BF16)
            yw = jnp.dot(yb, w_ref[...], preferred_element_type=F32)
            dy2 = dm_ref[pl.ds(r0, rows), :]
            ds_ref[...] += jnp.sum(dy2 * yw, axis=0, keepdims=True)
            dyw = (dy2 * s_ref[...]).astype(BF16)
            dw_ref[...] += lax.dot_general(yb, dyw, (((0,), (0,)), ((), ())), preferred_element_type=F32)
            dy = lax.dot_general(dyw, w_ref[...], (((1,), (1,)), ((), ())), preferred_element_type=F32)
            e_ref[pl.ds(r0, rows), :] = dy / count
            return carry

        lax.fori_loop(0, t // rows, chunk, 0)

        def chunk2(ci, carry):
            r0 = pl.multiple_of(ci * rows, rows)
            full = e_ref[pl.ds(r0, rows + POOL_HALO), :]
            s = full
            for step in range(n_grp):
                shifted = pltpu.roll(s, rows + POOL_HALO - (1 << step), axis=0)
                s = s + jnp.where(grp >= step, shifted, 0.0)
            e = full[:rows, :]
            tpos = r0 + lax.broadcasted_iota(jnp.int32, (rows, gw), 0)
            count = jnp.minimum(tpos + 1, 2 << grp).astype(F32)
            du_ref[pl.ds(r0, rows), :] = (s[:rows, :] - e * count).astype(BF16)
            return carry

        lax.fori_loop(0, t // rows, chunk2, 0)

    return pl.pallas_call(
        body, name="pool_bwd", grid=(n_grp,),
        in_specs=[pl.BlockSpec((t, gw), lambda g: (0, g)), pl.BlockSpec((t, gw), lambda g: (0, g)),
                  pl.BlockSpec((None, gw, gw), lambda g: (g, 0, 0)),
                  pl.BlockSpec((None, 1, gw), lambda g: (g, 0, 0))],
        out_specs=[pl.BlockSpec((None, t, gw), lambda g: (DU_POOL, 0, g)),
                   pl.BlockSpec((None, gw, gw), lambda g: (g, 0, 0)), pl.BlockSpec((None, 1, gw), lambda g: (g, 0, 0))],
        out_shape=[jax.ShapeDtypeStruct((4, t, pw), BF16), jax.ShapeDtypeStruct((n_grp, gw, gw), F32),
                   jax.ShapeDtypeStruct((n_grp, 1, gw), F32)],
        scratch_shapes=[pltpu.VMEM((t + POOL_HALO, gw), F32)],
        compiler_params=_params(("parallel",)),
    )(u, dmix, w_pool, pool_scale)


def _sb_scores(q, k_blk, scale, mask):
    z = lax.dot_general(q, k_blk, (((1,), (1,)), ((), ())), preferred_element_type=F32) * scale
    log_not = jnp.minimum(-z, 0.0) - jnp.log(1.0 + jnp.exp(-jnp.abs(z)))
    return z, (log_not if mask is None else jnp.where(mask, log_not, 0.0))


def _sb_weights(e, mask):
    a = jnp.exp(e)
    return a if mask is None else jnp.where(mask, a, 0.0)


EXP_IS_ZERO_BELOW = -104.0


def _weights_alive(after):
    return (jnp.max(after) >= EXP_IS_ZERO_BELOW).astype(jnp.int32)


def _split_dot(vs, tri):
    parts = []
    for v in vs:
        hi = v.astype(BF16)
        parts += [hi, (v - hi.astype(F32)).astype(BF16)]
    prod = jnp.dot(jnp.concatenate(parts, axis=0), tri, preferred_element_type=F32)
    m = vs[0].shape[0]
    return [prod[2 * k * m:(2 * k + 1) * m] + prod[(2 * k + 1) * m:(2 * k + 2) * m] for k in range(len(vs))]


def _head(ref, h, rows=None):
    cols = slice(h * HEAD_DIM, (h + 1) * HEAD_DIM)
    return ref[:, cols] if rows is None else ref[rows, cols]


def _attn_fwd(ub, mix, t, nh, hg, phases=()):
    scale = float(1.0 / (HEAD_DIM ** 0.5))
    ng = nh // hg

    def body(q_ref, k_ref, v_ref, o_ref):
        i = pl.program_id(1)
        row = lax.broadcasted_iota(jnp.int32, (QB, KB), 0)
        col = lax.broadcasted_iota(jnp.int32, (QB, KB), 1)
        suffix = (row >= col).astype(BF16)

        def more(carry):
            return jnp.logical_and(carry[0] <= i, carry[3] > 0)

        def block(n, accs, afters, mask):
            rows = pl.ds(pl.multiple_of((i - n) * KB, KB), KB)
            new_accs, new_afters = [], []
            scores = [_sb_scores(_head(q_ref, h), _head(k_ref, h, rows), scale, mask) for h in range(hg)]
            withins = _split_dot([log_not for _, log_not in scores], suffix)
            for h in range(hg):
                z, log_not = scores[h]
                a = _sb_weights(z + withins[h] + afters[h], mask)
                new_accs.append(accs[h] + jnp.dot(a.astype(BF16), _head(v_ref, h, rows),
                                                  preferred_element_type=F32))
                new_afters.append(afters[h] + jnp.sum(log_not, axis=1, keepdims=True))
            return n + 1, tuple(new_accs), tuple(new_afters), _weights_alive(functools.reduce(jnp.maximum, new_afters))

        first = block(jnp.int32(0), tuple(jnp.zeros((QB, HEAD_DIM), F32) for _ in range(hg)),
                      tuple(jnp.zeros((QB, 1), F32) for _ in range(hg)), col < row)
        _, accs, _, _ = lax.while_loop(more, lambda carry: block(carry[0], carry[1], carry[2], None), first)
        for h in range(hg):
            o_ref[:, h * HEAD_DIM:(h + 1) * HEAD_DIM] = accs[h].astype(BF16)

    wide = hg * HEAD_DIM
    outs, phase_outs = _call(
        lambda q_ref, k_ref, v_ref, mix_ref, o_ref: body(q_ref, k_ref, v_ref, o_ref), "attn_fwd", (ng, t // QB),
        [pl.BlockSpec((QB, wide), lambda g, i: (i, ng + g)), pl.BlockSpec((t, wide), lambda g, i: (0, 2 * ng + g)),
         pl.BlockSpec((t, wide), lambda g, i: (0, 3 * ng + g)), ANY],
        [pl.BlockSpec((None, QB, wide), lambda g, i: (1, i, g))], [jax.ShapeDtypeStruct(mix.shape, mix.dtype)], [],
        ("parallel", "arbitrary"), (ub, ub, ub, mix), phases, in_place={3: 0})
    return outs[0], phase_outs


def _attn_bwd(ub, dmix, du, t, nh, hg, phases=()):
    scale = float(1.0 / (HEAD_DIM ** 0.5))
    ng = nh // hg

    def body(q_ref, k_ref, v_ref, do_ref, du_in_ref, du_ref, g_ref, z_ref, dk_ref, dv_ref):
        i = pl.program_id(1)

        @pl.when(i == 0)
        def _():
            dk_ref[...] = jnp.zeros_like(dk_ref)
            dv_ref[...] = jnp.zeros_like(dv_ref)

        row = lax.broadcasted_iota(jnp.int32, (QB, KB), 0)
        col = lax.broadcasted_iota(jnp.int32, (QB, KB), 1)
        suffix = (row >= col).astype(BF16)
        prefix = (row <= col).astype(BF16)

        def more(carry):
            return jnp.logical_and(carry[0] <= i, carry[2] > 0)

        def down(n, afters, mask):
            ks = pl.multiple_of((i - n) * KB, KB)
            rows = pl.ds(ks, KB)
            new_afters = []
            scores = [_sb_scores(_head(q_ref, h), _head(k_ref, h, rows), scale, mask) for h in range(hg)]
            withins = _split_dot([log_not for _, log_not in scores], suffix)
            for h in range(hg):
                do = _head(do_ref, h).astype(BF16)
                z, log_not = scores[h]
                a = _sb_weights(z + withins[h] + afters[h], mask)
                da = lax.dot_general(do, _head(v_ref, h, rows), (((1,), (1,)), ((), ())),
                                     preferred_element_type=F32)
                g_ref[h, :, pl.ds(ks, KB)] = a * da
                z_ref[h, :, pl.ds(ks, KB)] = z
                dv_ref[rows, h * HEAD_DIM:(h + 1) * HEAD_DIM] += lax.dot_general(
                    a.astype(BF16), do, (((0,), (0,)), ((), ())), preferred_element_type=F32)
                new_afters.append(afters[h] + jnp.sum(log_not, axis=1, keepdims=True))
            return n + 1, tuple(new_afters), _weights_alive(functools.reduce(jnp.maximum, new_afters))

        diagonal = col < row
        first = down(jnp.int32(0), tuple(jnp.zeros((QB, 1), F32) for _ in range(hg)), diagonal)
        visited, _, _ = lax.while_loop(more, lambda carry: down(carry[0], carry[1], None), first)

        def up(kb, carry, mask):
            dqs, befores = carry
            ks = pl.multiple_of(kb * KB, KB)
            rows = pl.ds(ks, KB)
            new_dqs, new_befores = [], []
            gs = [g_ref[h, :, pl.ds(ks, KB)] for h in range(hg)]
            g_withins = _split_dot(gs, prefix)
            for h in range(hg):
                g = gs[h]
                z = z_ref[h, :, pl.ds(ks, KB)]
                g_upto = g_withins[h] + befores[h]
                dz = g - jax.nn.sigmoid(z) * g_upto
                dz = dz if mask is None else jnp.where(mask, dz, 0.0)
                dzs = (dz * scale).astype(BF16)
                new_dqs.append(dqs[h] + jnp.dot(dzs, _head(k_ref, h, rows), preferred_element_type=F32))
                dk_ref[rows, h * HEAD_DIM:(h + 1) * HEAD_DIM] += lax.dot_general(
                    dzs, _head(q_ref, h), (((0,), (0,)), ((), ())), preferred_element_type=F32)
                new_befores.append(befores[h] + jnp.sum(g, axis=1, keepdims=True))
            return tuple(new_dqs), tuple(new_befores)

        below = lax.fori_loop(i + 1 - visited, i, lambda kb, carry: up(kb, carry, None),
                              (tuple(jnp.zeros((QB, HEAD_DIM), F32) for _ in range(hg)),
                               tuple(jnp.zeros((QB, 1), F32) for _ in range(hg))))
        dqs, _ = up(i, below, diagonal)
        q_rows = pl.ds(pl.multiple_of(i * QB, QB), QB)
        for h in range(hg):
            du_ref[0, q_rows, h * HEAD_DIM:(h + 1) * HEAD_DIM] = dqs[h].astype(BF16)

        @pl.when(i == t // QB - 1)
        def _():
            du_ref[1] = dk_ref[...].astype(BF16)
            du_ref[2] = dv_ref[...].astype(BF16)

    wide = hg * HEAD_DIM
    tile = lambda off: pl.BlockSpec((QB, wide), lambda g, i: (i, off + g))
    strip = lambda off: pl.BlockSpec((t, wide), lambda g, i: (0, off + g))
    outs, phase_outs = _call(
        body, "attn_bwd", (ng, t // QB), [tile(ng), strip(2 * ng), strip(3 * ng), tile(ng), ANY],
        [pl.BlockSpec((3, t, wide), lambda g, i: (0, 0, g))], [jax.ShapeDtypeStruct(du.shape, du.dtype)],
        [pltpu.VMEM((hg, QB, t), F32), pltpu.VMEM((hg, QB, t), F32), pltpu.VMEM((t, wide), F32),
         pltpu.VMEM((t, wide), F32)], ("parallel", "arbitrary"), (ub, ub, ub, dmix, du), phases, in_place={4: 0})
    return outs[0], phase_outs


def _row_tile(rows, cols, pref_bytes=2 * 1024 * 1024):
    tr = max(8, pref_bytes // (4 * cols))
    while rows % tr:
        tr //= 2
    return max(tr, 1)


def _pair_sum(name, g, s, c_idx):
    _, _, r2, cols = g.shape
    tr = _row_tile(r2, cols)

    def body(c_ref, g_ref, s_ref, o_ref):
        o_ref[...] = (g_ref[...] + s_ref[...]).astype(BF16)

    return pl.pallas_call(
        body, name=name,
        grid_spec=pltpu.PrefetchScalarGridSpec(
            num_scalar_prefetch=1, grid=(N_CHIPS - 1, r2 // tr),
            in_specs=[pl.BlockSpec((None, None, tr, cols), lambda p, i, c: (c[1 + p], c[0], i, 0)),
                      pl.BlockSpec((None, tr, cols), lambda p, i, c: (c[1 + p], i, 0))],
            out_specs=pl.BlockSpec((None, tr, cols), lambda p, i, c: (c[1 + p], i, 0))),
        out_shape=jax.ShapeDtypeStruct((N_CHIPS, r2, cols), BF16),
        compiler_params=_params(("parallel", "parallel")),
    )(c_idx, g, s)


def _chip_sum(name, g, s, received, place):
    _, _, r2, cols = g.shape
    tr = _row_tile(r2, cols)
    counts = [r.shape[0] for r in received]

    def body(place_ref, g_ref, s_ref, *refs):
        total = g_ref[...] + s_ref[...]
        for r_ref, n in zip(refs[:-1], counts):
            for k in range(n):
                total = total + r_ref[k].astype(F32)
        refs[-1][...] = total

    return pl.pallas_call(
        body, name=name,
        grid_spec=pltpu.PrefetchScalarGridSpec(
            num_scalar_prefetch=1, grid=(r2 // tr,),
            in_specs=[pl.BlockSpec((None, None, tr, cols), lambda i, p: (p[0], p[1], i, 0)),
                      pl.BlockSpec((None, tr, cols), lambda i, p: (p[0], i, 0)),
                      *[pl.BlockSpec((n, tr, cols), lambda i, p: (0, i, 0)) for n in counts]],
            out_specs=pl.BlockSpec((None, tr, cols), lambda i, p: (p[1], i, 0))),
        out_shape=jax.ShapeDtypeStruct((2, r2, cols), F32),
        compiler_params=_params(("parallel",)),
    )(place, g, s, *received)


def _cast_into_slot(name, w, place):
    rows, cols = w.shape
    r2 = rows // 2
    tr = _row_tile(r2, cols)
    nb = r2 // tr

    def body(place_ref, w_ref, o_ref):
        o_ref[...] = w_ref[...].astype(BF16)

    return pl.pallas_call(
        body, name=name,
        grid_spec=pltpu.PrefetchScalarGridSpec(
            num_scalar_prefetch=1, grid=(2, nb),
            in_specs=[pl.BlockSpec((tr, cols), lambda h, i, s: (h * nb + i, 0))],
            out_specs=pl.BlockSpec((None, None, tr, cols), lambda h, i, s: (s[0], h, i, 0))),
        out_shape=jax.ShapeDtypeStruct((N_CHIPS, 2, r2, cols), BF16),
        compiler_params=_params(("parallel", "parallel")),
    )(place, w)


def _colsum(name, a):
    def body(a_ref, o_ref):
        o_ref[...] = jnp.sum(a_ref[...], axis=0, keepdims=True)

    whole = lambda shape: pl.BlockSpec(shape, lambda i: (0, 0))
    return pl.pallas_call(
        body, name=name, grid=(1,), in_specs=[whole(a.shape)], out_specs=whole((1, a.shape[1])),
        out_shape=jax.ShapeDtypeStruct((1, a.shape[1]), F32), compiler_params=_params(("arbitrary",)),
    )(a)


def _adamw(name, w, g, m, v):
    rows, cols = w.shape
    tr = _row_tile(rows, cols, 1024 * 1024)

    def body(w_ref, g_ref, m_ref, v_ref, g_out_ref, d_ref, nm_ref, nv_ref):
        g_t = g_ref[...]
        m_t = ADAM_B1 * m_ref[...] + (1.0 - ADAM_B1) * g_t
        v_t = ADAM_B2 * v_ref[...] + (1.0 - ADAM_B2) * jnp.square(g_t)
        m_hat = m_t / (1.0 - ADAM_B1 ** ADAM_STEP)
        v_hat = v_t / (1.0 - ADAM_B2 ** ADAM_STEP)
        g_out_ref[...] = g_t
        d_ref[...] = -ADAM_LR * (m_hat / (jnp.sqrt(v_hat) + ADAM_EPS) + ADAM_WD * w_ref[...])
        nm_ref[...] = m_t
        nv_ref[...] = v_t

    spec = pl.BlockSpec((tr, cols), lambda i: (i, 0))
    shape = jax.ShapeDtypeStruct((rows, cols), F32)
    return pl.pallas_call(
        body, name=name, grid=(rows // tr,), in_specs=[spec] * 4, out_specs=[spec] * 4, out_shape=[shape] * 4,
        compiler_params=_params(("parallel",)),
    )(w, g, m, v)


def _all_reduce_small(packed, phases=()):
    rows, cols = packed.shape

    def body(in_ref, out_ref, all_ref, send_sems, recv_sems):
        x, y, c = _pos()
        me = 4 * x + 2 * y + c
        all_ref[me] = in_ref[...]
        cps = []
        for r in range(1, N_DEV):
            bx, by, bc = (r >> 2) & 1, (r >> 1) & 1, r & 1
            peer = (1 - x if bx else x, 1 - y if by else y, 1 - c if bc else c)
            cp = pltpu.make_async_remote_copy(
                src_ref=in_ref, dst_ref=all_ref.at[me], send_sem=send_sems.at[r - 1], recv_sem=recv_sems.at[r - 1],
                device_id=peer, device_id_type=MESH)
            cp.start()
            cps.append(cp)
        for cp in cps:
            cp.wait()
        total = all_ref[0]
        for d in range(1, N_DEV):
            total = total + all_ref[d]
        out_ref[...] = total

    vmem = pl.BlockSpec(memory_space=pltpu.VMEM)
    outs, phase_outs = _call(
        body, "all_reduce_small", (1,), [vmem], [vmem], [jax.ShapeDtypeStruct((rows, cols), F32)],
        [pltpu.VMEM((N_DEV, rows, cols), F32), pltpu.SemaphoreType.DMA((N_DEV - 1,)),
         pltpu.SemaphoreType.DMA((N_DEV - 1,))], ("arbitrary",), (packed,), phases, body_communicates=True)
    return outs[0], phase_outs


def kernel(x, ln_in_g, ln_in_b, w_in, w_pool, pool_scale, w_out, ln1_g, ln1_b, w_ff1, b_ff1, w_ff2, b_ff2, ln2_g, ln2_b, loss_target, m_ln_in_g, m_ln_in_b, m_w_in, m_w_pool, m_pool_scale, m_w_out, m_ln1_g, m_ln1_b, m_w_ff1, m_b_ff1, m_w_ff2, m_b_ff2, m_ln2_g, m_ln2_b, v_ln_in_g, v_ln_in_b, v_w_in, v_w_pool, v_pool_scale, v_w_out, v_ln1_g, v_ln1_b, v_w_ff1, v_b_ff1, v_w_ff2, v_b_ff2, v_ln2_g, v_ln2_b):
    t, d = x.shape[1], x.shape[2]
    pw = d // 2
    n_grp = len(POOL_WINDOWS)
    gw = pw // n_grp
    gwc = gw // N_CHIPS
    nh = pw // HEAD_DIM
    ff = w_ff1.shape[2] * N_CHIPS
    assert w_in.shape[0] == 1 and w_in.shape[2] * N_CHIPS == 2 * d and gwc <= 128

    x_idx, y_idx, c_idx = _pos()
    chip_arr = jnp.reshape(2 * x_idx + y_idx, (1,)).astype(jnp.int32)
    c_arr = jnp.reshape(c_idx, (1,)).astype(jnp.int32)
    place = jnp.concatenate([chip_arr, c_arr])
    core_and_others = jnp.stack([c_idx, *[2 * cx + cy for cx, cy in _other_chips(x_idx, y_idx)]]).astype(jnp.int32)

    xs = x.reshape(t, d)
    target = loss_target.reshape(t, d)
    row = lambda vec: vec.reshape(1, -1)

    scale_tile = jnp.zeros((1, 8, 128), F32).at[0, :n_grp, :gwc].set(pool_scale[0])
    scale_slots = lax.dynamic_update_slice(jnp.zeros((N_CHIPS, 8, 128), F32), scale_tile, (chip_arr[0], 0, 0))
    shards = dict(w_in=w_in[0], w_out=w_out[0], w_ff1=w_ff1[0], w_ff2=w_ff2[0], w_pool=w_pool[0].reshape(gw, gw))
    slot = {nm: _cast_into_slot("cast_" + nm, w, place) for nm, w in shards.items()}
    unsplit = lambda s: s.reshape(N_CHIPS, 2 * s.shape[2], s.shape[3])

    (h0, h0b), ((win_s,), (wpool_s,), (scale_g,)) = _ln_fwd(
        "ln_in_fwd", xs, row(ln_in_g), row(ln_in_b),
        phases=[_gather_ici_phase(slot["w_in"]), _gather_ici_phase(slot["w_pool"]), _gather_whole_phase(scale_slots)])
    (win_s,), (wpool_s,) = _comm_call("gather_d2d_first", [_gather_d2d_phase(win_s), _gather_d2d_phase(wpool_s)])
    win_g = unsplit(win_s)
    wpool_full = unsplit(wpool_s).reshape(N_CHIPS, n_grp, gwc, gw).transpose(1, 0, 2, 3).reshape(n_grp, gw, gw)
    scale_full = scale_g[:, :n_grp, :gwc].transpose(1, 0, 2).reshape(n_grp, 1, gw)

    def store_f32(acc, extra_refs, out_refs):
        out_refs[0][...] = acc

    def pool_f32_all_bf16(acc, extra_refs, out_refs):
        @pl.when(pl.program_id(1) == 0)
        def _():
            out_refs[0][...] = acc

        out_refs[1][...] = acc.astype(BF16)

    assert w_in.shape[2] == pw
    (u, ub), ((wout_s,), (wff1_s,)) = _mm_nn(
        "in_proj", h0b, win_g, (), lambda tm, tn: [],
        [jax.ShapeDtypeStruct((t, pw), F32), jax.ShapeDtypeStruct((t, 2 * d), BF16)],
        lambda tm, tn: [pl.BlockSpec((tm, tn), lambda i, j, kk: (i, 0)), _tile_spec(tm, tn)],
        pool_f32_all_bf16, b_chips=True, tn=pw,
        phases=[_gather_ici_phase(slot["w_out"]), _gather_ici_phase(slot["w_ff1"], rows=(0, 1, 8))])
    mix_in = _pool_fwd(u, wpool_full, scale_full, t, pw)
    mix_in, ((wff1_s,), (wout_s,)) = _attn_fwd(
        ub, mix_in, t, nh, min(nh, 4),
        phases=[_gather_ici_phase(wff1_s, rows=(1, 6, 8)), _gather_d2d_phase(wout_s)])
    same_part = lambda block: block
    wout_full = unsplit(wout_s).reshape(d, d)

    def residual(acc, extra_refs, out_refs):
        out_refs[0][...] = ALPHA * extra_refs[0][...] + acc

    (r1,), ((wff1_s,), (wff2_s,)) = _mm_nn(
        "out_proj", mix_in, wout_full, (h0,), lambda tm, tn: [_tile_spec(tm, tn)],
        [jax.ShapeDtypeStruct((t, d), F32)], lambda tm, tn: [_tile_spec(tm, tn)], residual, a_parts=same_part,
        phases=[_gather_ici_phase(wff1_s, rows=(7, 1, 8)), _gather_ici_phase(slot["w_ff2"], rows=(0, 1, 8))])
    (h1, h1b), ((wff1_s,),) = _ln_fwd("ln1_fwd", r1, ln1_g, ln1_b, phases=[_gather_d2d_phase(wff1_s)])
    wff1_g = unsplit(wff1_s)

    def relu_sq(acc, extra_refs, out_refs):
        p = jnp.maximum(acc + extra_refs[0][...], 0.0)
        out_refs[0][...] = p
        out_refs[1][...] = jnp.square(p).astype(BF16)

    (relu_z, act_b), ((wff2_s,),) = _mm_nn(
        "ff1", h1b, wff1_g, (b_ff1,), lambda tm, tn: [_row_spec(tn)],
        [jax.ShapeDtypeStruct((t, ff), F32), jax.ShapeDtypeStruct((t, ff), BF16)],
        lambda tm, tn: [_tile_spec(tm, tn)] * 2, relu_sq, b_chips=True,
        phases=[_gather_ici_phase(wff2_s, rows=(1, 7, 8))])
    ((wff2_s,),) = _comm_call("gather_d2d_last", [_gather_d2d_phase(wff2_s)])
    wff2_full = unsplit(wff2_s).reshape(ff, d)

    def residual_bias(acc, extra_refs, out_refs):
        out_refs[0][...] = ALPHA * extra_refs[0][...] + (acc + extra_refs[1][...])

    r2 = _mm_nn("ff2", act_b, wff2_full, (h1, b_ff2), lambda tm, tn: [_tile_spec(tm, tn), _row_spec(tn)],
                [jax.ShapeDtypeStruct((t, d), F32)], lambda tm, tn: [_tile_spec(tm, tn)], residual_bias)[0]

    dr2, dr2b, loss_tile, g_ln2_g, g_ln2_b, g_b_ff2 = _ln2_loss_bwd(r2, target, ln2_g, ln2_b)

    halves = lambda g: g.reshape(N_CHIPS, 2, g.shape[1] // 2, g.shape[2])
    g_ff2 = halves(_mm_tn("grad_w_ff2", act_b, dr2b).reshape(N_CHIPS, ff // N_CHIPS, d))

    def relu_sq_bwd(acc, extra_refs, out_refs):
        dz = acc * (2.0 * extra_refs[0][...])
        out_refs[0][...] = dz.astype(BF16)
        rows = lax.broadcasted_iota(jnp.int32, out_refs[1].shape, 0)
        out_refs[1][...] = jnp.where(rows == 0, jnp.sum(dz, axis=0, keepdims=True), 0.0)

    tm_ff = _tile(t, 1024)
    (dz1b, g_b_ff1_parts), ((s_ff2,),) = _mm_nt(
        "ff2_bwd", dr2b, wff2_full, (relu_z,), lambda tm, tn: [_tile_spec(tm, tn)],
        [jax.ShapeDtypeStruct((t, ff), BF16), jax.ShapeDtypeStruct((8 * (t // tm_ff), ff), F32)],
        lambda tm, tn: [_tile_spec(tm, tn), pl.BlockSpec((8, tn), lambda i, j, kk: (i, j))], relu_sq_bwd,
        phases=[_swap_phase(g_ff2)])
    p_ff2 = _pair_sum("pair_sum_w_ff2", g_ff2, s_ff2, core_and_others)
    g_ff1, ((r_ff2_a,),) = _mm_tn("grad_w_ff1", h1b, dz1b, out_chips=True,
                                  phases=[_scatter_phase(p_ff2, others=(0, 1))])
    g_ff1 = halves(g_ff1)

    def plus_alpha(acc, extra_refs, out_refs):
        out_refs[0][...] = ALPHA * extra_refs[0][...] + acc

    (dh1,), ((s_ff1,), (r_ff2_b,)) = _mm_nt(
        "ff1_bwd", dz1b, wff1_g, (dr2,), lambda tm, tn: [_tile_spec(tm, tn)],
        [jax.ShapeDtypeStruct((t, d), F32)], lambda tm, tn: [_tile_spec(tm, tn)], plus_alpha, b_chips=True,
        phases=[_swap_phase(g_ff1), _scatter_phase(p_ff2, others=(2,))])
    q_ff2 = _chip_sum("chip_sum_w_ff2", g_ff2, s_ff2, [r_ff2_a, r_ff2_b], place)
    p_ff1 = _pair_sum("pair_sum_w_ff1", g_ff1, s_ff1, core_and_others)
    (dr1, dr1b, g_ln1_g, g_ln1_b), ((q_ff2,),) = _ln_bwd("ln1_bwd", dh1, r1, ln1_g,
                                                         phases=[_assemble_phase(q_ff2)])

    g_out = halves(_mm_tn("grad_w_out", mix_in, dr1b, a_parts=same_part).reshape(N_CHIPS, d // N_CHIPS, d))
    (dmix,), ((s_out,),) = _mm_nt(
        "out_proj_bwd", dr1b, wout_full, (), lambda tm, tn: [], [jax.ShapeDtypeStruct((t, d), F32)],
        lambda tm, tn: [_tile_spec(tm, tn)], store_f32, phases=[_swap_phase(g_out)])
    p_out = _pair_sum("pair_sum_w_out", g_out, s_out, core_and_others)
    du, g_w_pool_full, g_scale_full = _pool_bwd(u, dmix, wpool_full, scale_full, t, pw)
    du, ((r_ff1,), (r_out,)) = _attn_bwd(ub, dmix, du, t, nh, min(nh, 2),
                                         phases=[_scatter_phase(p_ff1), _scatter_phase(p_out)])
    q_ff1 = _chip_sum("chip_sum_w_ff1", g_ff1, s_ff1, [r_ff1], place)
    q_out = _chip_sum("chip_sum_w_out", g_out, s_out, [r_out], place)
    g_in, ((q_ff1,), (q_out,)) = _mm_tn("grad_w_in", h0b, du, out_chips=True, b_parts=_du_part,
                                        phases=[_assemble_phase(q_ff1), _assemble_phase(q_out)])
    g_in = halves(g_in)
    g_pool = halves(g_w_pool_full.reshape(n_grp, N_CHIPS, gwc, gw).transpose(1, 0, 2, 3).reshape(N_CHIPS, gw, gw))
    (s_in,), (s_pool,) = _comm_call("rs_swap_last", [_swap_phase(g_in), _swap_phase(g_pool)])
    p_in = _pair_sum("pair_sum_w_in", g_in, s_in, core_and_others)
    p_pool = _pair_sum("pair_sum_w_pool", g_pool, s_pool, core_and_others)
    (dh0,), ((r_in,), (r_pool,)) = _mm_nt(
        "in_proj_bwd", du, win_g, (dr1,), lambda tm, tn: [_tile_spec(tm, tn)],
        [jax.ShapeDtypeStruct((t, d), F32)], lambda tm, tn: [_tile_spec(tm, tn)], plus_alpha, b_chips=True,
        a_parts=_du_part, phases=[_scatter_phase(p_in), _scatter_phase(p_pool)])
    q_in = _chip_sum("chip_sum_w_in", g_in, s_in, [r_in], place)
    q_pool = _chip_sum("chip_sum_w_pool", g_pool, s_pool, [r_pool], place)
    dx, _, g_ln_in_g, g_ln_in_b = _ln_bwd("ln_in_bwd", dh0, xs, row(ln_in_g))

    lane = 2048 if d % 2048 == 0 else d
    small_names = ["ln_in_g", "ln_in_b", "ln1_g", "ln1_b", "b_ff1", "b_ff2", "ln2_g", "ln2_b"]
    small_w = dict(ln_in_g=ln_in_g, ln_in_b=ln_in_b, ln1_g=ln1_g, ln1_b=ln1_b, b_ff1=b_ff1, b_ff2=b_ff2, ln2_g=ln2_g,
                   ln2_b=ln2_b)
    small_m = dict(ln_in_g=m_ln_in_g, ln_in_b=m_ln_in_b, ln1_g=m_ln1_g, ln1_b=m_ln1_b, b_ff1=m_b_ff1, b_ff2=m_b_ff2,
                   ln2_g=m_ln2_g, ln2_b=m_ln2_b)
    small_v = dict(ln_in_g=v_ln_in_g, ln_in_b=v_ln_in_b, ln1_g=v_ln1_g, ln1_b=v_ln1_b, b_ff1=v_b_ff1, b_ff2=v_b_ff2,
                   ln2_g=v_ln2_g, ln2_b=v_ln2_b)
    small_g = dict(ln_in_g=g_ln_in_g, ln_in_b=g_ln_in_b, ln1_g=g_ln1_g, ln1_b=g_ln1_b, b_ff2=g_b_ff2, ln2_g=g_ln2_g,
                   ln2_b=g_ln2_b)

    def pack(parts):
        flat = jnp.concatenate([p.reshape(-1) for p in parts])
        n_rows = -(-flat.shape[0] // lane)
        n_rows = -(-n_rows // 8) * 8
        return jnp.pad(flat, (0, n_rows * lane - flat.shape[0])).reshape(n_rows, lane)

    small_g["b_ff1"] = _colsum("b_ff1_colsum", g_b_ff1_parts)
    summed, ((q_in,), (q_pool,)) = _all_reduce_small(
        pack([small_g[nm] for nm in small_names] + [g_scale_full, loss_tile[0, :1]]),
        phases=[_assemble_phase(q_in), _assemble_phase(q_pool)])
    summed = summed.reshape(-1)

    big = {}
    for nm, q, w, m, v in [("w_in", q_in, w_in, m_w_in, v_w_in), ("w_out", q_out, w_out, m_w_out, v_w_out),
                           ("w_ff1", q_ff1, w_ff1, m_w_ff1, v_w_ff1), ("w_ff2", q_ff2, w_ff2, m_w_ff2, v_w_ff2),
                           ("w_pool", q_pool, w_pool, m_w_pool, v_w_pool)]:
        g = q.reshape(2 * q.shape[1], q.shape[2])
        flat = lambda arr: arr.reshape(g.shape)
        big[nm] = tuple(arr.reshape(w.shape) for arr in _adamw("adamw_" + nm, flat(w), g, flat(m), flat(v)))

    g_small, off = {}, 0
    for nm in small_names:
        g_small[nm] = summed[off:off + small_w[nm].size]
        off += small_w[nm].size
    g_scale_all = summed[off:off + n_grp * gw].reshape(n_grp, N_CHIPS, gwc)
    loss = summed[off + n_grp * gw]
    g_scale = lax.dynamic_index_in_dim(g_scale_all, chip_arr[0], axis=1, keepdims=False)

    order = small_names + ["pool_scale"]
    small_w["pool_scale"], small_m["pool_scale"], small_v["pool_scale"] = pool_scale, m_pool_scale, v_pool_scale
    g_small["pool_scale"] = g_scale
    _, delta_s, new_m_s, new_v_s = _adamw("adamw_small", pack([small_w[nm] for nm in order]),
                                          pack([g_small[nm] for nm in order]), pack([small_m[nm] for nm in order]),
                                          pack([small_v[nm] for nm in order]))
    small = {}
    off = 0
    for nm in order:
        size, shape = small_w[nm].size, small_w[nm].shape
        cut = lambda arr: arr.reshape(-1)[off:off + size].reshape(shape)
        small[nm] = (g_small[nm].reshape(shape), cut(delta_s), cut(new_m_s), cut(new_v_s))
        off += size

    every = {**big, **small}
    weight_order = ["ln_in_g", "ln_in_b", "w_in", "w_pool", "pool_scale", "w_out", "ln1_g", "ln1_b", "w_ff1", "b_ff1",
                    "w_ff2", "b_ff2", "ln2_g", "ln2_b"]
    grads = [every[nm][0] for nm in weight_order]
    deltas = [every[nm][1] for nm in weight_order]
    new_ms = [every[nm][2] for nm in weight_order]
    new_vs = [every[nm][3] for nm in weight_order]
    return (loss, dx.reshape(x.shape), *grads, *deltas, *new_ms, *new_vs)
```

```python
import functools

import jax
import jax.numpy as jnp
from jax import lax
from jax.experimental import pallas as pl
from jax.experimental.pallas import tpu as pltpu

F32 = jnp.float32
BF16 = jnp.bfloat16
MESH = pl.DeviceIdType.MESH

HEAD_DIM = 128
POOL_WINDOWS = (2, 4, 8, 16)
POOL_HALO = 16
LN_EPS = 1e-5
ALPHA = 2.0 ** 0.25
ADAM_LR, ADAM_B1, ADAM_B2, ADAM_EPS, ADAM_WD, ADAM_STEP = 0.001, 0.9, 0.999, 1e-08, 0.01, 10

QB = 256
KB = 256
VMEM_LIMIT = 56 * 1024 * 1024
N_CHIPS = 4
N_DEV = 8


def _params(sem=None, collective_id=None):
    return pltpu.CompilerParams(dimension_semantics=sem, vmem_limit_bytes=VMEM_LIMIT, collective_id=collective_id)


def _tile(dim, pref):
    return pref if dim % pref == 0 else dim


def _pos():
    return lax.axis_index("x"), lax.axis_index("y"), lax.axis_index("c")


def _other_chips(x, y):
    return [(1 - x, y), (x, 1 - y), (1 - x, 1 - y)]


ANY = pl.BlockSpec(memory_space=pl.ANY)


SIBLING = "sibling"


class _Phase:
    def __init__(self, ins, out_shapes, aliases, n_sems, build, peers):
        self.ins, self.out_shapes, self.aliases, self.n_sems, self.build = ins, out_shapes, aliases, n_sems, build
        self.peers = tuple(peers)


def _entry_barrier(phases):
    x, y, c = _pos()
    chips = _other_chips(x, y)
    relations = sorted({r for ph in phases for r in ph.peers}, key=str)
    barrier = pltpu.get_barrier_semaphore()
    for r in relations:
        peer = (x, y, 1 - c) if r == SIBLING else (*chips[r], c)
        pl.semaphore_signal(barrier, inc=1, device_id=peer, device_id_type=MESH)
    pl.semaphore_wait(barrier, len(relations))


def _barrier_id(phases):
    return sum(1 << (3 if r == SIBLING else r) for r in {r for ph in phases for r in ph.peers})


def _remote(src, dst, send_sems, recv_sems, k, to):
    return pltpu.make_async_remote_copy(src_ref=src, dst_ref=dst, send_sem=send_sems.at[k], recv_sem=recv_sems.at[k],
                                        device_id=to, device_id_type=MESH)


def _swap_phase(g):
    def build(ins, outs, ss, rs):
        x, y, c = _pos()
        cp = _remote(ins[0].at[:, 1 - c], outs[0], ss, rs, 0, (x, y, 1 - c))
        return [cp], [cp]

    return _Phase([g], [jax.ShapeDtypeStruct((N_CHIPS, g.shape[2], g.shape[3]), g.dtype)], {}, 1, build, [SIBLING])


ALL_OTHERS = (0, 1, 2)


def _scatter_phase(p, others=ALL_OTHERS):
    def build(ins, outs, ss, rs):
        x, y, c = _pos()
        chips = _other_chips(x, y)
        cps = [_remote(ins[0].at[2 * chips[j][0] + chips[j][1]], outs[0].at[k], ss, rs, k, (*chips[j], c))
               for k, j in enumerate(others)]
        return cps, cps

    return _Phase([p], [jax.ShapeDtypeStruct((len(others), p.shape[1], p.shape[2]), p.dtype)], {}, len(others), build,
                  others)


def _assemble_phase(q):
    def build(ins, outs, ss, rs):
        x, y, c = _pos()
        mine, other = outs[0].at[c], outs[0].at[1 - c]
        return [_remote(mine, mine, ss, rs, 0, (x, y, 1 - c))], [_remote(other, other, ss, rs, 0, (x, y, c))]

    return _Phase([q], [jax.ShapeDtypeStruct(q.shape, q.dtype)], {0: 0}, 1, build, [SIBLING])


def _gather_ici_phase(slot, others=ALL_OTHERS, rows=(0, 1, 1)):
    chunk = slot.shape[2] // rows[2]
    span = pl.ds(rows[0] * chunk, rows[1] * chunk)

    def build(ins, outs, ss, rs):
        x, y, c = _pos()
        chips = _other_chips(x, y)
        mine = outs[0].at[2 * x + y, c, span]
        sends, recvs = [], []
        for k, j in enumerate(others):
            theirs = outs[0].at[2 * chips[j][0] + chips[j][1], c, span]
            sends.append(_remote(mine, mine, ss, rs, k, (*chips[j], c)))
            recvs.append(_remote(theirs, theirs, ss, rs, k, (x, y, c)))
        return sends, recvs

    return _Phase([slot], [jax.ShapeDtypeStruct(slot.shape, slot.dtype)], {0: 0}, len(others), build, others)


def _gather_d2d_phase(slot):
    def build(ins, outs, ss, rs):
        x, y, c = _pos()
        sends, recvs = [], []
        for j, chip in enumerate(_other_chips(x, y)):
            landed = outs[0].at[2 * chip[0] + chip[1], c]
            coming = outs[0].at[2 * chip[0] + chip[1], 1 - c]
            sends.append(_remote(landed, landed, ss, rs, j, (x, y, 1 - c)))
            recvs.append(_remote(coming, coming, ss, rs, j, (x, y, c)))
        return sends, recvs

    return _Phase([slot], [jax.ShapeDtypeStruct(slot.shape, slot.dtype)], {0: 0}, 3, build, [SIBLING])


def _gather_whole_phase(slots):
    def build(ins, outs, ss, rs):
        x, y, c = _pos()
        mine = outs[0].at[2 * x + y]
        sends, recvs = [], []
        for j, chip in enumerate(_other_chips(x, y)):
            theirs = outs[0].at[2 * chip[0] + chip[1]]
            sends.append(_remote(mine, mine, ss, rs, j, (*chip, c)))
            recvs.append(_remote(theirs, theirs, ss, rs, j, (x, y, c)))
        return sends, recvs

    return _Phase([slots], [jax.ShapeDtypeStruct(slots.shape, slots.dtype)], {0: 0}, 3, build, ALL_OTHERS)


def _split_refs(refs, n_in, n_out, n_scratch, phases):
    n_pin = sum(len(ph.ins) for ph in phases)
    n_pout = sum(len(ph.out_shapes) for ph in phases)
    cuts = [n_in, n_pin, n_out, n_pout, n_scratch]
    parts, at = [], 0
    for n in cuts:
        parts.append(refs[at:at + n])
        at += n
    parts.append(refs[at:])
    return parts


def _build_phases(phases, pin, pout, sems):
    built, i, o = [], 0, 0
    for k, ph in enumerate(phases):
        built.append(ph.build(pin[i:i + len(ph.ins)], pout[o:o + len(ph.out_shapes)], sems[2 * k], sems[2 * k + 1]))
        i += len(ph.ins)
        o += len(ph.out_shapes)
    return built


def _finish_phases(built):
    for _, recvs in built:
        for cp in recvs:
            cp.wait_recv()
    for sends, _ in built:
        for cp in sends:
            cp.wait_send()


def _call(body, name, grid, in_specs, out_specs, out_shape, scratch_shapes, semantics, args, phases=(),
          in_place=None, body_communicates=False):
    own_barrier = bool(phases) and not body_communicates
    n_in, n_out, n_scratch = len(args), len(out_shape), len(scratch_shapes)
    aliases, in_at, out_at = dict(in_place or {}), n_in, n_out
    for ph in phases:
        aliases.update({in_at + i: out_at + o for i, o in ph.aliases.items()})
        in_at += len(ph.ins)
        out_at += len(ph.out_shapes)

    def hosted(*refs):
        ins, pin, outs, pout, scratch, sems = _split_refs(refs, n_in, n_out, n_scratch, phases)
        ids = [pl.program_id(a) for a in range(len(grid))]
        first = functools.reduce(jnp.logical_and, [i == 0 for i in ids])
        last = functools.reduce(jnp.logical_and, [i == g - 1 for i, g in zip(ids, grid)])

        @pl.when(first)
        def _():
            if own_barrier:
                _entry_barrier(phases)
            for sends, _ in _build_phases(phases, pin, pout, sems):
                for cp in sends:
                    cp.start()

        body(*ins, *outs, *scratch)

        @pl.when(last)
        def _():
            _finish_phases(_build_phases(phases, pin, pout, sems))

    p_args = [a for ph in phases for a in ph.ins]
    p_shapes = [s for ph in phases for s in ph.out_shapes]
    sem_shapes = [pltpu.SemaphoreType.DMA((ph.n_sems,)) for ph in phases for _ in range(2)]
    outs = pl.pallas_call(
        hosted if phases else body, name=name, grid=grid, in_specs=[*in_specs, *[ANY] * len(p_args)],
        out_specs=[*out_specs, *[ANY] * len(p_shapes)], out_shape=[*out_shape, *p_shapes],
        input_output_aliases=aliases, scratch_shapes=[*scratch_shapes, *sem_shapes],
        compiler_params=_params(("arbitrary",) * len(grid) if phases else semantics,
                                _barrier_id(phases) if own_barrier else None),
    )(*args, *p_args)
    phase_outs, at = [], n_out
    for ph in phases:
        phase_outs.append(list(outs[at:at + len(ph.out_shapes)]))
        at += len(ph.out_shapes)
    return list(outs[:n_out]), phase_outs


def _comm_call(name, phases):
    def body(*refs):
        _, pin, _, pout, _, sems = _split_refs(refs, 0, 0, 0, phases)
        _entry_barrier(phases)
        built = _build_phases(phases, pin, pout, sems)
        for sends, _ in built:
            for cp in sends:
                cp.start()
        _finish_phases(built)

    aliases, in_at, out_at = {}, 0, 0
    for ph in phases:
        aliases.update({in_at + i: out_at + o for i, o in ph.aliases.items()})
        in_at += len(ph.ins)
        out_at += len(ph.out_shapes)
    p_args = [a for ph in phases for a in ph.ins]
    p_shapes = [s for ph in phases for s in ph.out_shapes]
    outs = pl.pallas_call(
        body, name=name, in_specs=[ANY] * len(p_args), out_specs=[ANY] * len(p_shapes), out_shape=p_shapes,
        input_output_aliases=aliases,
        scratch_shapes=[pltpu.SemaphoreType.DMA((ph.n_sems,)) for ph in phases for _ in range(2)],
        compiler_params=pltpu.CompilerParams(collective_id=_barrier_id(phases)),
    )(*p_args)
    phase_outs, at = [], 0
    for ph in phases:
        phase_outs.append(list(outs[at:at + len(ph.out_shapes)]))
        at += len(ph.out_shapes)
    return phase_outs


def _matmul(name, a, b, grid, a_spec, b_spec, contract, acc_shape, extras, extra_specs, out_shape, out_specs,
            epilogue, phases=()):
    n_extra, n_out, gk = len(extras), len(out_shape), grid[2]

    def product(a_ref, b_ref):
        return lax.dot_general(a_ref[...], b_ref[...], (contract, ((), ())), preferred_element_type=F32)

    def body_one_step(*refs):
        epilogue(product(refs[0], refs[1]), refs[2:2 + n_extra], refs[2 + n_extra:])

    def body(*refs):
        a_ref, b_ref = refs[0], refs[1]
        extra_refs = refs[2:2 + n_extra]
        out_refs = refs[2 + n_extra:2 + n_extra + n_out]
        acc_ref = refs[-1]
        kk = pl.program_id(2)

        @pl.when(kk == 0)
        def _():
            acc_ref[...] = product(a_ref, b_ref)

        @pl.when(kk > 0)
        def _():
            acc_ref[...] += product(a_ref, b_ref)

        @pl.when(kk == gk - 1)
        def _():
            epilogue(acc_ref[...], extra_refs, out_refs)

    outs, phase_outs = _call(
        body_one_step if gk == 1 else body, name, grid, [a_spec, b_spec, *extra_specs], out_specs, out_shape,
        [] if gk == 1 else [pltpu.VMEM(acc_shape, F32)], ("parallel", "arbitrary", "arbitrary"), (a, b, *extras),
        phases)
    return (outs, phase_outs) if phases else outs


def _mm_nn(name, a, b, extras, extra_specs, out_shape, out_specs, epilogue, b_chips=False, tm=1024, tn=1024,
           tk=2048, phases=(), a_parts=None):
    m, k, tm, tk, a_spec = _lhs_rows_by_k(a, tm, tk, a_parts)
    n = b.shape[1] if not b_chips else b.shape[2] * N_CHIPS
    if b_chips:
        tn = _tile(b.shape[2], tn)
        nb = b.shape[2] // tn
        b_spec = pl.BlockSpec((None, tk, tn), lambda i, j, kk: (j // nb, kk, j % nb))
    else:
        tn = _tile(n, tn)
        b_spec = pl.BlockSpec((tk, tn), lambda i, j, kk: (kk, j))
    return _matmul(name, a, b, (m // tm, n // tn, k // tk), a_spec, b_spec, ((1,), (0,)), (tm, tn), extras,
                   extra_specs(tm, tn), out_shape, out_specs(tm, tn), epilogue, phases)


def _lhs_rows_by_k(a, tm, tk, a_parts, tk_max=None):
    if a_parts is None:
        m, k = a.shape
        tm, tk = _tile(m, tm), _tile(k if tk_max is None else tk_max, tk)
        return m, k, tm, tk, pl.BlockSpec((tm, tk), lambda i, j, kk: (i, kk))
    n_parts, m, kp = a.shape
    tm, tk = _tile(m, tm), _tile(kp if tk_max is None else min(kp, tk_max), tk)
    nb = kp // tk
    return m, n_parts * kp, tm, tk, pl.BlockSpec((None, tm, tk), lambda i, j, kk: (a_parts(kk // nb), i, kk % nb))


def _mm_nt(name, a, b, extras, extra_specs, out_shape, out_specs, epilogue, b_chips=False, tm=1024, tn=1024,
           tk=2048, phases=(), a_parts=None):
    m, k, tm, tk, a_spec = _lhs_rows_by_k(a, tm, tk, a_parts, tk_max=b.shape[2] if b_chips else None)
    n = b.shape[0] if not b_chips else b.shape[1]
    tn = _tile(n, tn)
    if b_chips:
        nb = b.shape[2] // tk
        b_spec = pl.BlockSpec((None, tn, tk), lambda i, j, kk: (kk // nb, j, kk % nb))
    else:
        b_spec = pl.BlockSpec((tn, tk), lambda i, j, kk: (j, kk))
    return _matmul(name, a, b, (m // tm, n // tn, k // tk), a_spec, b_spec, ((1,), (1,)), (tm, tn), extras,
                   extra_specs(tm, tn), out_shape, out_specs(tm, tn), epilogue, phases)


def _mm_tn(name, a, b, out_chips=False, tm=1024, tn=1024, tk=2048, phases=(), a_parts=None, b_parts=None):
    if a_parts is None:
        k, m = a.shape
        tm = _tile(m, tm)
        a_spec = pl.BlockSpec((_tile(k, tk), tm), lambda i, j, kk: (kk, i))
    else:
        n_parts, k, mp = a.shape
        m, tm = n_parts * mp, _tile(mp, tm)
        nbm = mp // tm
        a_spec = pl.BlockSpec((None, _tile(k, tk), tm), lambda i, j, kk: (a_parts(i // nbm), kk, i % nbm))
    tk = _tile(k, tk)
    n = b.shape[1] if b_parts is None else b.shape[0] * b.shape[2]
    if out_chips:
        nc = n // N_CHIPS
        tn = _tile(nc, tn)
        nb = nc // tn
        out_shape = [jax.ShapeDtypeStruct((N_CHIPS, m, nc), F32)]
        out_specs = [pl.BlockSpec((None, tm, tn), lambda i, j, kk: (j // nb, i, j % nb))]
    else:
        tn = _tile(n, tn)
        out_shape = [jax.ShapeDtypeStruct((m, n), F32)]
        out_specs = [pl.BlockSpec((tm, tn), lambda i, j, kk: (i, j))]
    if b_parts is None:
        b_spec = pl.BlockSpec((tk, tn), lambda i, j, kk: (kk, j))
    else:
        nbn = b.shape[2] // tn
        b_spec = pl.BlockSpec((None, tk, tn), lambda i, j, kk: (b_parts(j // nbn), kk, j % nbn))

    def epilogue(acc, extra_refs, out_refs):
        out_refs[0][...] = acc

    res = _matmul(name, a, b, (m // tm, n // tn, k // tk), a_spec, b_spec, ((0,), (0,)), (tm, tn), (), [],
                  out_shape, out_specs, epilogue, phases)
    return (res[0][0], res[1]) if phases else res[0]


def _tile_spec(tm, tn):
    return pl.BlockSpec((tm, tn), lambda i, j, kk: (i, j))


def _row_spec(tn):
    return pl.BlockSpec((1, tn), lambda i, j, kk: (0, j))


def _ln_stats(r):
    mu = jnp.mean(r, axis=-1, keepdims=True)
    var = jnp.mean(jnp.square(r - mu), axis=-1, keepdims=True)
    rstd = lax.rsqrt(var + LN_EPS)
    return (r - mu) * rstd, rstd


def _ln_fwd(name, r, g, b, tr=256, phases=()):
    t, d = r.shape
    tr = _tile(t, tr)

    def body(r_ref, g_ref, b_ref, y_ref, yb_ref):
        xhat, _ = _ln_stats(r_ref[...])
        y = xhat * g_ref[...] + b_ref[...]
        y_ref[...] = y
        yb_ref[...] = y.astype(BF16)

    row = pl.BlockSpec((tr, d), lambda i: (i, 0))
    vec = pl.BlockSpec((1, d), lambda i: (0, 0))
    outs, phase_outs = _call(
        body, name, (t // tr,), [row, vec, vec], [row, row],
        [jax.ShapeDtypeStruct((t, d), F32), jax.ShapeDtypeStruct((t, d), BF16)], [], ("parallel",), (r, g, b), phases)
    return (outs, phase_outs) if phases else outs


def _ln_bwd_rows(dy, xhat, rstd, g):
    dxhat = dy * g
    m1 = jnp.mean(dxhat, axis=-1, keepdims=True)
    m2 = jnp.mean(dxhat * xhat, axis=-1, keepdims=True)
    return rstd * (dxhat - m1 - xhat * m2)


def _ln_bwd(name, dy, r, g, tr=256, phases=()):
    t, d = dy.shape
    tr = _tile(t, tr)

    def body(dy_ref, r_ref, g_ref, dr_ref, drb_ref, dg_ref, db_ref):
        @pl.when(pl.program_id(0) == 0)
        def _():
            dg_ref[...] = jnp.zeros_like(dg_ref)
            db_ref[...] = jnp.zeros_like(db_ref)

        dy_t = dy_ref[...]
        xhat_t, rstd = _ln_stats(r_ref[...])
        dr = _ln_bwd_rows(dy_t, xhat_t, rstd, g_ref[...])
        dr_ref[...] = dr
        drb_ref[...] = dr.astype(BF16)
        dg_ref[...] += jnp.sum(dy_t * xhat_t, axis=0, keepdims=True)
        db_ref[...] += jnp.sum(dy_t, axis=0, keepdims=True)

    row = pl.BlockSpec((tr, d), lambda i: (i, 0))
    vec = pl.BlockSpec((1, d), lambda i: (0, 0))
    outs, phase_outs = _call(
        body, name, (t // tr,), [row, row, vec], [row, row, vec, vec],
        [jax.ShapeDtypeStruct((t, d), F32), jax.ShapeDtypeStruct((t, d), BF16),
         jax.ShapeDtypeStruct((1, d), F32), jax.ShapeDtypeStruct((1, d), F32)], [], ("arbitrary",),
        (dy, r, g), phases)
    return (outs, phase_outs) if phases else outs


def _ln2_loss_bwd(r2, target, g, b, tr=256):
    t, d = r2.shape
    tr = _tile(t, tr)

    def body(r_ref, t_ref, g_ref, b_ref, dr_ref, drb_ref, loss_ref, dg_ref, db_ref, dsum_ref):
        @pl.when(pl.program_id(0) == 0)
        def _():
            loss_ref[...] = jnp.zeros_like(loss_ref)
            dg_ref[...] = jnp.zeros_like(dg_ref)
            db_ref[...] = jnp.zeros_like(db_ref)
            dsum_ref[...] = jnp.zeros_like(dsum_ref)

        xhat, rstd = _ln_stats(r_ref[...])
        g_t = g_ref[...]
        err = xhat * g_t + b_ref[...] - t_ref[...]
        loss_ref[...] += 0.5 * jnp.sum(jnp.mean(jnp.square(err), axis=-1, keepdims=True), axis=0, keepdims=True)
        dy = err * (1.0 / d)
        dr = _ln_bwd_rows(dy, xhat, rstd, g_t)
        dr_ref[...] = dr
        drb_ref[...] = dr.astype(BF16)
        dg_ref[...] += jnp.sum(dy * xhat, axis=0, keepdims=True)
        db_ref[...] += jnp.sum(dy, axis=0, keepdims=True)
        dsum_ref[...] += jnp.sum(dr, axis=0, keepdims=True)

    row = pl.BlockSpec((tr, d), lambda i: (i, 0))
    vec = pl.BlockSpec((1, d), lambda i: (0, 0))
    return pl.pallas_call(
        body, name="ln2_loss_bwd", grid=(t // tr,), in_specs=[row, row, vec, vec],
        out_specs=[row, row, pl.BlockSpec((8, 128), lambda i: (0, 0)), vec, vec, vec],
        out_shape=[jax.ShapeDtypeStruct((t, d), F32), jax.ShapeDtypeStruct((t, d), BF16),
                   jax.ShapeDtypeStruct((8, 128), F32), jax.ShapeDtypeStruct((1, d), F32),
                   jax.ShapeDtypeStruct((1, d), F32), jax.ShapeDtypeStruct((1, d), F32)],
        compiler_params=_params(("arbitrary",)),
    )(r2, target, g, b)


POOL_ROWS = 512

DU_POOL = 3


def _du_part(block):
    return (block + DU_POOL) % 4


def _pool_mean_minus_token(u_ref, r0, rows, grp, first):
    width = u_ref.shape[1]
    body = u_ref[pl.ds(r0, rows), :]
    halo = u_ref[pl.ds(pl.multiple_of(jnp.maximum(r0 - POOL_HALO, 0), POOL_HALO), POOL_HALO), :]
    halo = jnp.where(first, 0.0, halo)
    full = jnp.concatenate([halo, body], axis=0)
    s = full
    for step in range(len(POOL_WINDOWS)):
        shifted = pltpu.roll(s, 1 << step, axis=0)
        s = s + jnp.where(grp >= step, shifted, 0.0)
    s = s[POOL_HALO:, :]
    tpos = r0 + lax.broadcasted_iota(jnp.int32, (rows, width), 0)
    count = jnp.minimum(tpos + 1, 2 << grp).astype(F32)
    return s / count - body, count


def _pool_fwd(u, w_pool, pool_scale, t, pw):
    gw = pw // len(POOL_WINDOWS)
    rows = _tile(t, POOL_ROWS)

    def body(u_ref, w_ref, s_ref, o_ref):
        grp = pl.program_id(0)

        def chunk(ci, carry):
            r0 = pl.multiple_of(ci * rows, rows)
            y, _ = _pool_mean_minus_token(u_ref, r0, rows, grp, ci == 0)
            yw = jnp.dot(y.astype(BF16), w_ref[...], preferred_element_type=F32)
            o_ref[pl.ds(r0, rows), :] = (yw * s_ref[...]).astype(BF16)
            return carry

        lax.fori_loop(0, t // rows, chunk, 0)

    return pl.pallas_call(
        body, name="pool_fwd", grid=(len(POOL_WINDOWS),),
        in_specs=[pl.BlockSpec((t, gw), lambda g: (0, g)), pl.BlockSpec((None, gw, gw), lambda g: (g, 0, 0)),
                  pl.BlockSpec((None, 1, gw), lambda g: (g, 0, 0))],
        out_specs=pl.BlockSpec((None, t, gw), lambda g: (0, 0, g)),
        out_shape=jax.ShapeDtypeStruct((2, t, pw), BF16),
        compiler_params=_params(("parallel",)),
    )(u, w_pool, pool_scale)


def _pool_bwd(u, dmix, w_pool, pool_scale, t, pw):
    n_grp = len(POOL_WINDOWS)
    gw = pw // n_grp
    rows = _tile(t, POOL_ROWS)

    def body(u_ref, dm_ref, w_ref, s_ref, du_ref, dw_ref, ds_ref, e_ref):
        grp = pl.program_id(0)
        dw_ref[...] = jnp.zeros_like(dw_ref)
        ds_ref[...] = jnp.zeros_like(ds_ref)
        e_ref[pl.ds(t, POOL_HALO), :] = jnp.zeros((POOL_HALO, gw), F32)

        def chunk(ci, carry):
            r0 = pl.multiple_of(ci * rows, rows)
            y, count = _pool_mean_minus_token(u_ref, r0, rows, grp, ci == 0)
            yb = y.astype(BF16)
            yw = jnp.dot(yb, w_ref[...], preferred_element_type=F32)
            dy2 = dm_ref[pl.ds(r0, rows), :]
            ds_ref[...] += jnp.sum(dy2 * yw, axis=0, keepdims=True)
            dyw = (dy2 * s_ref[...]).astype(BF16)
            dw_ref[...] += lax.dot_general(yb, dyw, (((0,), (0,)), ((), ())), preferred_element_type=F32)
            dy = lax.dot_general(dyw, w_ref[...], (((1,), (1,)), ((), ())), preferred_element_type=F32)
            e_ref[pl.ds(r0, rows), :] = dy / count
            return carry

        lax.fori_loop(0, t // rows, chunk, 0)

        def chunk2(ci, carry):
            r0 = pl.multiple_of(ci * rows, rows)
            full = e_ref[pl.ds(r0, rows + POOL_HALO), :]
            s = full
            for step in range(n_grp):
                shifted = pltpu.roll(s, rows + POOL_HALO - (1 << step), axis=0)
                s = s + jnp.where(grp >= step, shifted, 0.0)
            e = full[:rows, :]
            tpos = r0 + lax.broadcasted_iota(jnp.int32, (rows, gw), 0)
            count = jnp.minimum(tpos + 1, 2 << grp).astype(F32)
            du_ref[pl.ds(r0, rows), :] = (s[:rows, :] - e * count).astype(BF16)
            return carry

        lax.fori_loop(0, t // rows, chunk2, 0)

    return pl.pallas_call(
        body, name="pool_bwd", grid=(n_grp,),
        in_specs=[pl.BlockSpec((t, gw), lambda g: (0, g)), pl.BlockSpec((t, gw), lambda g: (0, g)),
                  pl.BlockSpec((None, gw, gw), lambda g: (g, 0, 0)),
                  pl.BlockSpec((None, 1, gw), lambda g: (g, 0, 0))],
        out_specs=[pl.BlockSpec((None, t, gw), lambda g: (DU_POOL, 0, g)),
                   pl.BlockSpec((None, gw, gw), lambda g: (g, 0, 0)), pl.BlockSpec((None, 1, gw), lambda g: (g, 0, 0))],
        out_shape=[jax.ShapeDtypeStruct((4, t, pw), BF16), jax.ShapeDtypeStruct((n_grp, gw, gw), F32),
                   jax.ShapeDtypeStruct((n_grp, 1, gw), F32)],
        scratch_shapes=[pltpu.VMEM((t + POOL_HALO, gw), F32)],
        compiler_params=_params(("parallel",)),
    )(u, dmix, w_pool, pool_scale)


def _sb_scores(q, k_blk, scale, mask):
    z = lax.dot_general(q, k_blk, (((1,), (1,)), ((), ())), preferred_element_type=F32) * scale
    log_not = jnp.minimum(-z, 0.0) - jnp.log(1.0 + jnp.exp(-jnp.abs(z)))
    return z, (log_not if mask is None else jnp.where(mask, log_not, 0.0))


def _sb_weights(e, mask):
    a = jnp.exp(e)
    return a if mask is None else jnp.where(mask, a, 0.0)


EXP_IS_ZERO_BELOW = -104.0


def _weights_alive(after):
    return (jnp.max(after) >= EXP_IS_ZERO_BELOW).astype(jnp.int32)


def _split_dot(vs, tri):
    parts = []
    for v in vs:
        hi = v.astype(BF16)
        parts += [hi, (v - hi.astype(F32)).astype(BF16)]
    prod = jnp.dot(jnp.concatenate(parts, axis=0), tri, preferred_element_type=F32)
    m = vs[0].shape[0]
    return [prod[2 * k * m:(2 * k + 1) * m] + prod[(2 * k + 1) * m:(2 * k + 2) * m] for k in range(len(vs))]


def _head(ref, h, rows=None):
    cols = slice(h * HEAD_DIM, (h + 1) * HEAD_DIM)
    return ref[:, cols] if rows is None else ref[rows, cols]


def _attn_fwd(ub, mix, t, nh, hg, phases=()):
    scale = float(1.0 / (HEAD_DIM ** 0.5))
    ng = nh // hg

    def body(q_ref, k_ref, v_ref, o_ref):
        i = pl.program_id(1)
        row = lax.broadcasted_iota(jnp.int32, (QB, KB), 0)
        col = lax.broadcasted_iota(jnp.int32, (QB, KB), 1)
        suffix = (row >= col).astype(BF16)

        def more(carry):
            return jnp.logical_and(carry[0] <= i, carry[3] > 0)

        def block(n, accs, afters, mask):
            rows = pl.ds(pl.multiple_of((i - n) * KB, KB), KB)
            new_accs, new_afters = [], []
            scores = [_sb_scores(_head(q_ref, h), _head(k_ref, h, rows), scale, mask) for h in range(hg)]
            withins = _split_dot([log_not for _, log_not in scores], suffix)
            for h in range(hg):
                z, log_not = scores[h]
                a = _sb_weights(z + withins[h] + afters[h], mask)
                new_accs.append(accs[h] + jnp.dot(a.astype(BF16), _head(v_ref, h, rows),
                                                  preferred_element_type=F32))
                new_afters.append(afters[h] + jnp.sum(log_not, axis=1, keepdims=True))
            return n + 1, tuple(new_accs), tuple(new_afters), _weights_alive(functools.reduce(jnp.maximum, new_afters))

        first = block(jnp.int32(0), tuple(jnp.zeros((QB, HEAD_DIM), F32) for _ in range(hg)),
                      tuple(jnp.zeros((QB, 1), F32) for _ in range(hg)), col < row)
        _, accs, _, _ = lax.while_loop(more, lambda carry: block(carry[0], carry[1], carry[2], None), first)
        for h in range(hg):
            o_ref[:, h * HEAD_DIM:(h + 1) * HEAD_DIM] = accs[h].astype(BF16)

    wide = hg * HEAD_DIM
    outs, phase_outs = _call(
        lambda q_ref, k_ref, v_ref, mix_ref, o_ref: body(q_ref, k_ref, v_ref, o_ref), "attn_fwd", (ng, t // QB),
        [pl.BlockSpec((QB, wide), lambda g, i: (i, ng + g)), pl.BlockSpec((t, wide), lambda g, i: (0, 2 * ng + g)),
         pl.BlockSpec((t, wide), lambda g, i: (0, 3 * ng + g)), ANY],
        [pl.BlockSpec((None, QB, wide), lambda g, i: (1, i, g))], [jax.ShapeDtypeStruct(mix.shape, mix.dtype)], [],
        ("parallel", "arbitrary"), (ub, ub, ub, mix), phases, in_place={3: 0})
    return outs[0], phase_outs


def _attn_bwd(ub, dmix, du, t, nh, hg, phases=()):
    scale = float(1.0 / (HEAD_DIM ** 0.5))
    ng = nh // hg

    def body(q_ref, k_ref, v_ref, do_ref, du_in_ref, du_ref, g_ref, dk_ref, dv_ref):
        i = pl.program_id(1)

        @pl.when(i == 0)
        def _():
            dk_ref[...] = jnp.zeros_like(dk_ref)
            dv_ref[...] = jnp.zeros_like(dv_ref)

        row = lax.broadcasted_iota(jnp.int32, (QB, KB), 0)
        col = lax.broadcasted_iota(jnp.int32, (QB, KB), 1)
        suffix = (row >= col).astype(BF16)
        prefix = (row <= col).astype(BF16)

        def more(carry):
            return jnp.logical_and(carry[0] <= i, carry[2] > 0)

        def down(n, afters, mask):
            ks = pl.multiple_of((i - n) * KB, KB)
            rows = pl.ds(ks, KB)
            new_afters = []
            scores = [_sb_scores(_head(q_ref, h), _head(k_ref, h, rows), scale, mask) for h in range(hg)]
            withins = _split_dot([log_not for _, log_not in scores], suffix)
            for h in range(hg):
                do = _head(do_ref, h).astype(BF16)
                z, log_not = scores[h]
                a = _sb_weights(z + withins[h] + afters[h], mask)
                da = lax.dot_general(do, _head(v_ref, h, rows), (((1,), (1,)), ((), ())),
                                     preferred_element_type=F32)
                g_ref[h, :, pl.ds(ks, KB)] = a * da
                dv_ref[rows, h * HEAD_DIM:(h + 1) * HEAD_DIM] += lax.dot_general(
                    a.astype(BF16), do, (((0,), (0,)), ((), ())), preferred_element_type=F32)
                new_afters.append(afters[h] + jnp.sum(log_not, axis=1, keepdims=True))
            return n + 1, tuple(new_afters), _weights_alive(functools.reduce(jnp.maximum, new_afters))

        diagonal = col < row
        first = down(jnp.int32(0), tuple(jnp.zeros((QB, 1), F32) for _ in range(hg)), diagonal)
        visited, _, _ = lax.while_loop(more, lambda carry: down(carry[0], carry[1], None), first)

        def up(kb, carry, mask):
            dqs, befores = carry
            ks = pl.multiple_of(kb * KB, KB)
            rows = pl.ds(ks, KB)
            new_dqs, new_befores = [], []
            gs = [g_ref[h, :, pl.ds(ks, KB)] for h in range(hg)]
            g_withins = _split_dot(gs, prefix)
            for h in range(hg):
                g = gs[h]
                z = lax.dot_general(_head(q_ref, h), _head(k_ref, h, rows), (((1,), (1,)), ((), ())),
                                    preferred_element_type=F32) * scale
                g_upto = g_withins[h] + befores[h]
                dz = g - jax.nn.sigmoid(z) * g_upto
                dz = dz if mask is None else jnp.where(mask, dz, 0.0)
                dzs = (dz * scale).astype(BF16)
                new_dqs.append(dqs[h] + jnp.dot(dzs, _head(k_ref, h, rows), preferred_element_type=F32))
                dk_ref[rows, h * HEAD_DIM:(h + 1) * HEAD_DIM] += lax.dot_general(
                    dzs, _head(q_ref, h), (((0,), (0,)), ((), ())), preferred_element_type=F32)
                new_befores.append(befores[h] + jnp.sum(g, axis=1, keepdims=True))
            return tuple(new_dqs), tuple(new_befores)

        below = lax.fori_loop(i + 1 - visited, i, lambda kb, carry: up(kb, carry, None),
                              (tuple(jnp.zeros((QB, HEAD_DIM), F32) for _ in range(hg)),
                               tuple(jnp.zeros((QB, 1), F32) for _ in range(hg))))
        dqs, _ = up(i, below, diagonal)
        q_rows = pl.ds(pl.multiple_of(i * QB, QB), QB)
        for h in range(hg):
            du_ref[0, q_rows, h * HEAD_DIM:(h + 1) * HEAD_DIM] = dqs[h].astype(BF16)

        @pl.when(i == t // QB - 1)
        def _():
            du_ref[1] = dk_ref[...].astype(BF16)
            du_ref[2] = dv_ref[...].astype(BF16)

    wide = hg * HEAD_DIM
    tile = lambda off: pl.BlockSpec((QB, wide), lambda g, i: (i, off + g))
    strip = lambda off: pl.BlockSpec((t, wide), lambda g, i: (0, off + g))
    outs, phase_outs = _call(
        body, "attn_bwd", (ng, t // QB), [tile(ng), strip(2 * ng), strip(3 * ng), tile(ng), ANY],
        [pl.BlockSpec((3, t, wide), lambda g, i: (0, 0, g))], [jax.ShapeDtypeStruct(du.shape, du.dtype)],
        [pltpu.VMEM((hg, QB, t), F32), pltpu.VMEM((t, wide), F32), pltpu.VMEM((t, wide), F32)], ("parallel", "arbitrary"), (ub, ub, ub, dmix, du), phases, in_place={4: 0})
    return outs[0], phase_outs


def _row_tile(rows, cols, pref_bytes=2 * 1024 * 1024):
    tr = max(8, pref_bytes // (4 * cols))
    while rows % tr:
        tr //= 2
    return max(tr, 1)


def _pair_sum(name, g, s, c_idx):
    _, _, r2, cols = g.shape
    tr = _row_tile(r2, cols)

    def body(c_ref, g_ref, s_ref, o_ref):
        o_ref[...] = (g_ref[...] + s_ref[...]).astype(BF16)

    return pl.pallas_call(
        body, name=name,
        grid_spec=pltpu.PrefetchScalarGridSpec(
            num_scalar_prefetch=1, grid=(N_CHIPS - 1, r2 // tr),
            in_specs=[pl.BlockSpec((None, None, tr, cols), lambda p, i, c: (c[1 + p], c[0], i, 0)),
                      pl.BlockSpec((None, tr, cols), lambda p, i, c: (c[1 + p], i, 0))],
            out_specs=pl.BlockSpec((None, tr, cols), lambda p, i, c: (c[1 + p], i, 0))),
        out_shape=jax.ShapeDtypeStruct((N_CHIPS, r2, cols), BF16),
        compiler_params=_params(("parallel", "parallel")),
    )(c_idx, g, s)


def _chip_sum(name, g, s, received, place):
    _, _, r2, cols = g.shape
    tr = _row_tile(r2, cols)
    counts = [r.shape[0] for r in received]

    def body(place_ref, g_ref, s_ref, *refs):
        total = g_ref[...] + s_ref[...]
        for r_ref, n in zip(refs[:-1], counts):
            for k in range(n):
                total = total + r_ref[k].astype(F32)
        refs[-1][...] = total

    return pl.pallas_call(
        body, name=name,
        grid_spec=pltpu.PrefetchScalarGridSpec(
            num_scalar_prefetch=1, grid=(r2 // tr,),
            in_specs=[pl.BlockSpec((None, None, tr, cols), lambda i, p: (p[0], p[1], i, 0)),
                      pl.BlockSpec((None, tr, cols), lambda i, p: (p[0], i, 0)),
                      *[pl.BlockSpec((n, tr, cols), lambda i, p: (0, i, 0)) for n in counts]],
            out_specs=pl.BlockSpec((None, tr, cols), lambda i, p: (p[1], i, 0))),
        out_shape=jax.ShapeDtypeStruct((2, r2, cols), F32),
        compiler_params=_params(("parallel",)),
    )(place, g, s, *received)


def _cast_into_slot(name, w, place):
    rows, cols = w.shape
    r2 = rows // 2
    tr = _row_tile(r2, cols)
    nb = r2 // tr

    def body(place_ref, w_ref, o_ref):
        o_ref[...] = w_ref[...].astype(BF16)

    return pl.pallas_call(
        body, name=name,
        grid_spec=pltpu.PrefetchScalarGridSpec(
            num_scalar_prefetch=1, grid=(2, nb),
            in_specs=[pl.BlockSpec((tr, cols), lambda h, i, s: (h * nb + i, 0))],
            out_specs=pl.BlockSpec((None, None, tr, cols), lambda h, i, s: (s[0], h, i, 0))),
        out_shape=jax.ShapeDtypeStruct((N_CHIPS, 2, r2, cols), BF16),
        compiler_params=_params(("parallel", "parallel")),
    )(place, w)


def _colsum(name, a):
    def body(a_ref, o_ref):
        o_ref[...] = jnp.sum(a_ref[...], axis=0, keepdims=True)

    whole = lambda shape: pl.BlockSpec(shape, lambda i: (0, 0))
    return pl.pallas_call(
        body, name=name, grid=(1,), in_specs=[whole(a.shape)], out_specs=whole((1, a.shape[1])),
        out_shape=jax.ShapeDtypeStruct((1, a.shape[1]), F32), compiler_params=_params(("arbitrary",)),
    )(a)


def _adamw(name, w, g, m, v):
    rows, cols = w.shape
    tr = _row_tile(rows, cols, 1024 * 1024)

    def body(w_ref, g_ref, m_ref, v_ref, g_out_ref, d_ref, nm_ref, nv_ref):
        g_t = g_ref[...]
        m_t = ADAM_B1 * m_ref[...] + (1.0 - ADAM_B1) * g_t
        v_t = ADAM_B2 * v_ref[...] + (1.0 - ADAM_B2) * jnp.square(g_t)
        m_hat = m_t / (1.0 - ADAM_B1 ** ADAM_STEP)
        v_hat = v_t / (1.0 - ADAM_B2 ** ADAM_STEP)
        g_out_ref[...] = g_t
        d_ref[...] = -ADAM_LR * (m_hat / (jnp.sqrt(v_hat) + ADAM_EPS) + ADAM_WD * w_ref[...])
        nm_ref[...] = m_t
        nv_ref[...] = v_t

    spec = pl.BlockSpec((tr, cols), lambda i: (i, 0))
    shape = jax.ShapeDtypeStruct((rows, cols), F32)
    return pl.pallas_call(
        body, name=name, grid=(rows // tr,), in_specs=[spec] * 4, out_specs=[spec] * 4, out_shape=[shape] * 4,
        compiler_params=_params(("parallel",)),
    )(w, g, m, v)


def _all_reduce_small(packed, phases=()):
    rows, cols = packed.shape

    def body(in_ref, out_ref, all_ref, send_sems, recv_sems):
        x, y, c = _pos()
        me = 4 * x + 2 * y + c
        all_ref[me] = in_ref[...]
        cps = []
        for r in range(1, N_DEV):
            bx, by, bc = (r >> 2) & 1, (r >> 1) & 1, r & 1
            peer = (1 - x if bx else x, 1 - y if by else y, 1 - c if bc else c)
            cp = pltpu.make_async_remote_copy(
                src_ref=in_ref, dst_ref=all_ref.at[me], send_sem=send_sems.at[r - 1], recv_sem=recv_sems.at[r - 1],
                device_id=peer, device_id_type=MESH)
            cp.start()
            cps.append(cp)
        for cp in cps:
            cp.wait()
        total = all_ref[0]
        for d in range(1, N_DEV):
            total = total + all_ref[d]
        out_ref[...] = total

    vmem = pl.BlockSpec(memory_space=pltpu.VMEM)
    outs, phase_outs = _call(
        body, "all_reduce_small", (1,), [vmem], [vmem], [jax.ShapeDtypeStruct((rows, cols), F32)],
        [pltpu.VMEM((N_DEV, rows, cols), F32), pltpu.SemaphoreType.DMA((N_DEV - 1,)),
         pltpu.SemaphoreType.DMA((N_DEV - 1,))], ("arbitrary",), (packed,), phases, body_communicates=True)
    return outs[0], phase_outs


def kernel(x, ln_in_g, ln_in_b, w_in, w_pool, pool_scale, w_out, ln1_g, ln1_b, w_ff1, b_ff1, w_ff2, b_ff2, ln2_g, ln2_b, loss_target, m_ln_in_g, m_ln_in_b, m_w_in, m_w_pool, m_pool_scale, m_w_out, m_ln1_g, m_ln1_b, m_w_ff1, m_b_ff1, m_w_ff2, m_b_ff2, m_ln2_g, m_ln2_b, v_ln_in_g, v_ln_in_b, v_w_in, v_w_pool, v_pool_scale, v_w_out, v_ln1_g, v_ln1_b, v_w_ff1, v_b_ff1, v_w_ff2, v_b_ff2, v_ln2_g, v_ln2_b):
    t, d = x.shape[1], x.shape[2]
    pw = d // 2
    n_grp = len(POOL_WINDOWS)
    gw = pw // n_grp
    gwc = gw // N_CHIPS
    nh = pw // HEAD_DIM
    ff = w_ff1.shape[2] * N_CHIPS
    assert w_in.shape[0] == 1 and w_in.shape[2] * N_CHIPS == 2 * d and gwc <= 128

    x_idx, y_idx, c_idx = _pos()
    chip_arr = jnp.reshape(2 * x_idx + y_idx, (1,)).astype(jnp.int32)
    c_arr = jnp.reshape(c_idx, (1,)).astype(jnp.int32)
    place = jnp.concatenate([chip_arr, c_arr])
    core_and_others = jnp.stack([c_idx, *[2 * cx + cy for cx, cy in _other_chips(x_idx, y_idx)]]).astype(jnp.int32)

    xs = x.reshape(t, d)
    target = loss_target.reshape(t, d)
    row = lambda vec: vec.reshape(1, -1)

    scale_tile = jnp.zeros((1, 8, 128), F32).at[0, :n_grp, :gwc].set(pool_scale[0])
    scale_slots = lax.dynamic_update_slice(jnp.zeros((N_CHIPS, 8, 128), F32), scale_tile, (chip_arr[0], 0, 0))
    shards = dict(w_in=w_in[0], w_out=w_out[0], w_ff1=w_ff1[0], w_ff2=w_ff2[0], w_pool=w_pool[0].reshape(gw, gw))
    slot = {nm: _cast_into_slot("cast_" + nm, w, place) for nm, w in shards.items()}
    unsplit = lambda s: s.reshape(N_CHIPS, 2 * s.shape[2], s.shape[3])

    (h0, h0b), ((win_s,), (wpool_s,), (scale_g,)) = _ln_fwd(
        "ln_in_fwd", xs, row(ln_in_g), row(ln_in_b),
        phases=[_gather_ici_phase(slot["w_in"]), _gather_ici_phase(slot["w_pool"]), _gather_whole_phase(scale_slots)])
    (win_s,), (wpool_s,) = _comm_call("gather_d2d_first", [_gather_d2d_phase(win_s), _gather_d2d_phase(wpool_s)])
    win_g = unsplit(win_s)
    wpool_full = unsplit(wpool_s).reshape(N_CHIPS, n_grp, gwc, gw).transpose(1, 0, 2, 3).reshape(n_grp, gw, gw)
    scale_full = scale_g[:, :n_grp, :gwc].transpose(1, 0, 2).reshape(n_grp, 1, gw)

    def store_f32(acc, extra_refs, out_refs):
        out_refs[0][...] = acc

    def pool_f32_all_bf16(acc, extra_refs, out_refs):
        @pl.when(pl.program_id(1) == 0)
        def _():
            out_refs[0][...] = acc

        out_refs[1][...] = acc.astype(BF16)

    assert w_in.shape[2] == pw
    (u, ub), ((wout_s,), (wff1_s,)) = _mm_nn(
        "in_proj", h0b, win_g, (), lambda tm, tn: [],
        [jax.ShapeDtypeStruct((t, pw), F32), jax.ShapeDtypeStruct((t, 2 * d), BF16)],
        lambda tm, tn: [pl.BlockSpec((tm, tn), lambda i, j, kk: (i, 0)), _tile_spec(tm, tn)],
        pool_f32_all_bf16, b_chips=True, tn=pw,
        phases=[_gather_ici_phase(slot["w_out"]), _gather_ici_phase(slot["w_ff1"], rows=(0, 1, 8))])
    mix_in = _pool_fwd(u, wpool_full, scale_full, t, pw)
    mix_in, ((wff1_s,), (wout_s,)) = _attn_fwd(
        ub, mix_in, t, nh, min(nh, 4),
        phases=[_gather_ici_phase(wff1_s, rows=(1, 6, 8)), _gather_d2d_phase(wout_s)])
    same_part = lambda block: block
    wout_full = unsplit(wout_s).reshape(d, d)

    def residual(acc, extra_refs, out_refs):
        out_refs[0][...] = ALPHA * extra_refs[0][...] + acc

    (r1,), ((wff1_s,), (wff2_s,)) = _mm_nn(
        "out_proj", mix_in, wout_full, (h0,), lambda tm, tn: [_tile_spec(tm, tn)],
        [jax.ShapeDtypeStruct((t, d), F32)], lambda tm, tn: [_tile_spec(tm, tn)], residual, a_parts=same_part,
        phases=[_gather_ici_phase(wff1_s, rows=(7, 1, 8)), _gather_ici_phase(slot["w_ff2"], rows=(0, 1, 8))])
    (h1, h1b), ((wff1_s,),) = _ln_fwd("ln1_fwd", r1, ln1_g, ln1_b, phases=[_gather_d2d_phase(wff1_s)])
    wff1_g = unsplit(wff1_s)

    def relu_sq(acc, extra_refs, out_refs):
        p = jnp.maximum(acc + extra_refs[0][...], 0.0)
        out_refs[0][...] = p
        out_refs[1][...] = jnp.square(p).astype(BF16)

    (relu_z, act_b), ((wff2_s,),) = _mm_nn(
        "ff1", h1b, wff1_g, (b_ff1,), lambda tm, tn: [_row_spec(tn)],
        [jax.ShapeDtypeStruct((t, ff), F32), jax.ShapeDtypeStruct((t, ff), BF16)],
        lambda tm, tn: [_tile_spec(tm, tn)] * 2, relu_sq, b_chips=True,
        phases=[_gather_ici_phase(wff2_s, rows=(1, 7, 8))])
    ((wff2_s,),) = _comm_call("gather_d2d_last", [_gather_d2d_phase(wff2_s)])
    wff2_full = unsplit(wff2_s).reshape(ff, d)

    def residual_bias(acc, extra_refs, out_refs):
        out_refs[0][...] = ALPHA * extra_refs[0][...] + (acc + extra_refs[1][...])

    r2 = _mm_nn("ff2", act_b, wff2_full, (h1, b_ff2), lambda tm, tn: [_tile_spec(tm, tn), _row_spec(tn)],
                [jax.ShapeDtypeStruct((t, d), F32)], lambda tm, tn: [_tile_spec(tm, tn)], residual_bias)[0]

    dr2, dr2b, loss_tile, g_ln2_g, g_ln2_b, g_b_ff2 = _ln2_loss_bwd(r2, target, ln2_g, ln2_b)

    halves = lambda g: g.reshape(N_CHIPS, 2, g.shape[1] // 2, g.shape[2])
    g_ff2 = halves(_mm_tn("grad_w_ff2", act_b, dr2b).reshape(N_CHIPS, ff // N_CHIPS, d))

    def relu_sq_bwd(acc, extra_refs, out_refs):
        dz = acc * (2.0 * extra_refs[0][...])
        out_refs[0][...] = dz.astype(BF16)
        rows = lax.broadcasted_iota(jnp.int32, out_refs[1].shape, 0)
        out_refs[1][...] = jnp.where(rows == 0, jnp.sum(dz, axis=0, keepdims=True), 0.0)

    tm_ff = _tile(t, 1024)
    (dz1b, g_b_ff1_parts), ((s_ff2,),) = _mm_nt(
        "ff2_bwd", dr2b, wff2_full, (relu_z,), lambda tm, tn: [_tile_spec(tm, tn)],
        [jax.ShapeDtypeStruct((t, ff), BF16), jax.ShapeDtypeStruct((8 * (t // tm_ff), ff), F32)],
        lambda tm, tn: [_tile_spec(tm, tn), pl.BlockSpec((8, tn), lambda i, j, kk: (i, j))], relu_sq_bwd,
        phases=[_swap_phase(g_ff2)])
    p_ff2 = _pair_sum("pair_sum_w_ff2", g_ff2, s_ff2, core_and_others)
    g_ff1, ((r_ff2_a,),) = _mm_tn("grad_w_ff1", h1b, dz1b, out_chips=True,
                                  phases=[_scatter_phase(p_ff2, others=(0, 1))])
    g_ff1 = halves(g_ff1)

    def plus_alpha(acc, extra_refs, out_refs):
        out_refs[0][...] = ALPHA * extra_refs[0][...] + acc

    (dh1,), ((s_ff1,), (r_ff2_b,)) = _mm_nt(
        "ff1_bwd", dz1b, wff1_g, (dr2,), lambda tm, tn: [_tile_spec(tm, tn)],
        [jax.ShapeDtypeStruct((t, d), F32)], lambda tm, tn: [_tile_spec(tm, tn)], plus_alpha, b_chips=True,
        phases=[_swap_phase(g_ff1), _scatter_phase(p_ff2, others=(2,))])
    q_ff2 = _chip_sum("chip_sum_w_ff2", g_ff2, s_ff2, [r_ff2_a, r_ff2_b], place)
    p_ff1 = _pair_sum("pair_sum_w_ff1", g_ff1, s_ff1, core_and_others)
    (dr1, dr1b, g_ln1_g, g_ln1_b), ((q_ff2,),) = _ln_bwd("ln1_bwd", dh1, r1, ln1_g,
                                                         phases=[_assemble_phase(q_ff2)])

    g_out = halves(_mm_tn("grad_w_out", mix_in, dr1b, a_parts=same_part).reshape(N_CHIPS, d // N_CHIPS, d))
    (dmix,), ((s_out,),) = _mm_nt(
        "out_proj_bwd", dr1b, wout_full, (), lambda tm, tn: [], [jax.ShapeDtypeStruct((t, d), F32)],
        lambda tm, tn: [_tile_spec(tm, tn)], store_f32, phases=[_swap_phase(g_out)])
    p_out = _pair_sum("pair_sum_w_out", g_out, s_out, core_and_others)
    du, g_w_pool_full, g_scale_full = _pool_bwd(u, dmix, wpool_full, scale_full, t, pw)
    du, ((r_ff1,), (r_out,)) = _attn_bwd(ub, dmix, du, t, nh, min(nh, 2),
                                         phases=[_scatter_phase(p_ff1), _scatter_phase(p_out)])
    q_ff1 = _chip_sum("chip_sum_w_ff1", g_ff1, s_ff1, [r_ff1], place)
    q_out = _chip_sum("chip_sum_w_out", g_out, s_out, [r_out], place)
    g_in, ((q_ff1,), (q_out,)) = _mm_tn("grad_w_in", h0b, du, out_chips=True, b_parts=_du_part,
                                        phases=[_assemble_phase(q_ff1), _assemble_phase(q_out)])
    g_in = halves(g_in)
    g_pool = halves(g_w_pool_full.reshape(n_grp, N_CHIPS, gwc, gw).transpose(1, 0, 2, 3).reshape(N_CHIPS, gw, gw))
    (s_in,), (s_pool,) = _comm_call("rs_swap_last", [_swap_phase(g_in), _swap_phase(g_pool)])
    p_in = _pair_sum("pair_sum_w_in", g_in, s_in, core_and_others)
    p_pool = _pair_sum("pair_sum_w_pool", g_pool, s_pool, core_and_others)
    (dh0,), ((r_in,), (r_pool,)) = _mm_nt(
        "in_proj_bwd", du, win_g, (dr1,), lambda tm, tn: [_tile_spec(tm, tn)],
        [jax.ShapeDtypeStruct((t, d), F32)], lambda tm, tn: [_tile_spec(tm, tn)], plus_alpha, b_chips=True,
        a_parts=_du_part, phases=[_scatter_phase(p_in), _scatter_phase(p_pool)])
    q_in = _chip_sum("chip_sum_w_in", g_in, s_in, [r_in], place)
    q_pool = _chip_sum("chip_sum_w_pool", g_pool, s_pool, [r_pool], place)
    dx, _, g_ln_in_g, g_ln_in_b = _ln_bwd("ln_in_bwd", dh0, xs, row(ln_in_g))

    lane = 2048 if d % 2048 == 0 else d
    small_names = ["ln_in_g", "ln_in_b", "ln1_g", "ln1_b", "b_ff1", "b_ff2", "ln2_g", "ln2_b"]
    small_w = dict(ln_in_g=ln_in_g, ln_in_b=ln_in_b, ln1_g=ln1_g, ln1_b=ln1_b, b_ff1=b_ff1, b_ff2=b_ff2, ln2_g=ln2_g,
                   ln2_b=ln2_b)
    small_m = dict(ln_in_g=m_ln_in_g, ln_in_b=m_ln_in_b, ln1_g=m_ln1_g, ln1_b=m_ln1_b, b_ff1=m_b_ff1, b_ff2=m_b_ff2,
                   ln2_g=m_ln2_g, ln2_b=m_ln2_b)
    small_v = dict(ln_in_g=v_ln_in_g, ln_in_b=v_ln_in_b, ln1_g=v_ln1_g, ln1_b=v_ln1_b, b_ff1=v_b_ff1, b_ff2=v_b_ff2,
                   ln2_g=v_ln2_g, ln2_b=v_ln2_b)
    small_g = dict(ln_in_g=g_ln_in_g, ln_in_b=g_ln_in_b, ln1_g=g_ln1_g, ln1_b=g_ln1_b, b_ff2=g_b_ff2, ln2_g=g_ln2_g,
                   ln2_b=g_ln2_b)

    def pack(parts):
        flat = jnp.concatenate([p.reshape(-1) for p in parts])
        n_rows = -(-flat.shape[0] // lane)
        n_rows = -(-n_rows // 8) * 8
        return jnp.pad(flat, (0, n_rows * lane - flat.shape[0])).reshape(n_rows, lane)

    small_g["b_ff1"] = _colsum("b_ff1_colsum", g_b_ff1_parts)
    summed, ((q_in,), (q_pool,)) = _all_reduce_small(
        pack([small_g[nm] for nm in small_names] + [g_scale_full, loss_tile[0, :1]]),
        phases=[_assemble_phase(q_in), _assemble_phase(q_pool)])
    summed = summed.reshape(-1)

    big = {}
    for nm, q, w, m, v in [("w_in", q_in, w_in, m_w_in, v_w_in), ("w_out", q_out, w_out, m_w_out, v_w_out),
                           ("w_ff1", q_ff1, w_ff1, m_w_ff1, v_w_ff1), ("w_ff2", q_ff2, w_ff2, m_w_ff2, v_w_ff2),
                           ("w_pool", q_pool, w_pool, m_w_pool, v_w_pool)]:
        g = q.reshape(2 * q.shape[1], q.shape[2])
        flat = lambda arr: arr.reshape(g.shape)
        big[nm] = tuple(arr.reshape(w.shape) for arr in _adamw("adamw_" + nm, flat(w), g, flat(m), flat(v)))

    g_small, off = {}, 0
    for nm in small_names:
        g_small[nm] = summed[off:off + small_w[nm].size]
        off += small_w[nm].size
    g_scale_all = summed[off:off + n_grp * gw].reshape(n_grp, N_CHIPS, gwc)
    loss = summed[off + n_grp * gw]
    g_scale = lax.dynamic_index_in_dim(g_scale_all, chip_arr[0], axis=1, keepdims=False)

    order = small_names + ["pool_scale"]
    small_w["pool_scale"], small_m["pool_scale"], small_v["pool_scale"] = pool_scale, m_pool_scale, v_pool_scale
    g_small["pool_scale"] = g_scale
    _, delta_s, new_m_s, new_v_s = _adamw("adamw_small", pack([small_w[nm] for nm in order]),
                                          pack([g_small[nm] for nm in order]), pack([small_m[nm] for nm in order]),
                                          pack([small_v[nm] for nm in order]))
    small = {}
    off = 0
    for nm in order:
        size, shape = small_w[nm].size, small_w[nm].shape
        cut = lambda arr: arr.reshape(-1)[off:off + size].reshape(shape)
        small[nm] = (g_small[nm].reshape(shape), cut(delta_s), cut(new_m_s), cut(new_v_s))
        off += size

    every = {**big, **small}
    weight_order = ["ln_in_g", "ln_in_b", "w_in", "w_pool", "pool_scale", "w_out", "ln1_g", "ln1_b", "w_ff1", "b_ff1",
                    "w_ff2", "b_ff2", "ln2_g", "ln2_b"]
    grads = [every[nm][0] for nm in weight_order]
    deltas = [every[nm][1] for nm in weight_order]
    new_ms = [every[nm][2] for nm in weight_order]
    new_vs = [every[nm][3] for nm in weight_order]
    return (loss, dx.reshape(x.shape), *grads, *deltas, *new_ms, *new_vs)
```
